```python
import math
import jax
import jax.numpy as jnp
from jax import lax
import numpy as np

D_MODEL = 1024
BATCH = 8
SEQ = 2048
DEPTH = 2
DEC_BATCH = 128
DEC_SEQ = 8
PAST_LEN = 16384
PAGE_SIZE = 128

N_EVEN = (DEPTH + 1) // 2
N_ODD = DEPTH // 2
HALF = D_MODEL // 2
MLSTM_HEADS = 4
MLSTM_HD = HALF // MLSTM_HEADS
RWKV_HD = 64
RWKV_HEADS = HALF // RWKV_HD
RWKV_W_LORA = 64
RWKV_A_LORA = 64
RWKV_G_LORA = 128
GLA_HEADS = 4
GLA_DK = HALF // 2 // GLA_HEADS
GLA_DV = HALF // GLA_HEADS
GLA_QK = GLA_HEADS * GLA_DK
GLA_GATE_RANK = 16
GLA_TAU = 16.0
S5_CH = HALF
S5_GROUP = 16
S5_GROUPS = S5_CH // S5_GROUP
S5_P = 64
D_FF = ((8 * D_MODEL // 3 + 255) // 256) * 256
CONV_W = 3
CHUNK = 64
EPS = 1e-6
RWKV_LN_EPS = 64e-5
M_INIT = -1e30
MLSTM_COLS = 4 * HALF + 2 * MLSTM_HEADS
RWKV_COLS = 3 * HALF + RWKV_W_LORA + RWKV_A_LORA + RWKV_G_LORA
EVEN_COLS = MLSTM_COLS + RWKV_COLS
GLA_COLS = 2 * GLA_QK + 2 * HALF + GLA_GATE_RANK
ODD_COLS = GLA_COLS + S5_CH

kernel_name = 'hybrid_mlstm_rwkv7_gla_s5_step'


def rms_norm(x, g):
    xf = x.astype(jnp.float32)
    y = xf * lax.rsqrt(jnp.mean(xf * xf, -1, keepdims=True) + EPS)
    return (y * g.astype(jnp.float32)).astype(x.dtype)


def head_layer_norm(x, g, b, eps):
    mu = jnp.mean(x, -1, keepdims=True)
    xc = x - mu
    y = xc * lax.rsqrt(jnp.mean(xc * xc, -1, keepdims=True) + eps)
    y = y * g.astype(jnp.float32).reshape(x.shape[-2:])
    if b is not None:
        y = y + b.astype(jnp.float32).reshape(x.shape[-2:])
    return y


def head_rms_norm(x, g):
    y = x * lax.rsqrt(jnp.mean(x * x, -1, keepdims=True) + EPS)
    return y * g.astype(jnp.float32).reshape(x.shape[-2:])


def to_heads(t, n_heads):
    b, s, _ = t.shape
    return t.reshape(b, s, n_heads, -1).transpose(0, 2, 1, 3)


def to_chunks(a, L):
    s = a.shape
    a = a.reshape(s[:2] + (s[2] // L, L) + s[3:])
    return jnp.moveaxis(a, 2, 0)


def from_chunks(a):
    a = jnp.moveaxis(a, 0, 2)
    s = a.shape
    return a.reshape(s[:2] + (s[2] * s[3],) + s[4:])


def mlstm_chunked(q, k, v, i_log, f_log, C0, n0, m0):
    T = q.shape[2]
    L = math.gcd(T, CHUNK)
    causal = jnp.tril(jnp.ones((L, L), dtype=bool))

    def step(carry, xs):
        C, n, m = carry
        qc, kc, vc, ic, fc = xs
        b = jnp.cumsum(fc, -1)
        dlog = jnp.where(causal, b[..., :, None] - b[..., None, :] + ic[..., None, :], -jnp.inf)
        inter = b + m[..., None]
        m_t = jnp.maximum(inter, jnp.max(dlog, -1))
        w = jnp.exp(dlog - m_t[..., None])
        s_inter = jnp.exp(inter - m_t)
        qk = jnp.einsum('bhtd,bhsd->bhts', qc, kc) * w
        num = s_inter[..., None] * jnp.einsum('bhvd,bhtd->bhtv', C, qc) + jnp.einsum('bhts,bhsv->bhtv', qk, vc)
        den = s_inter * jnp.einsum('bhd,bhtd->bht', n, qc) + jnp.sum(qk, -1)
        h = num / jnp.maximum(jnp.abs(den), jnp.exp(-m_t))[..., None]
        m_new = m_t[..., -1]
        wl = jnp.exp(b[..., -1:] - b + ic - m_new[..., None])
        dec = jnp.exp(b[..., -1] + m - m_new)
        C_new = dec[..., None, None] * C + jnp.einsum('bhs,bhsv,bhsd->bhvd', wl, vc, kc)
        n_new = dec[..., None] * n + jnp.einsum('bhs,bhsd->bhd', wl, kc)
        return (C_new, n_new, m_new), h

    xs = (to_chunks(q, L), to_chunks(k, L), to_chunks(v, L), to_chunks(i_log, L), to_chunks(f_log, L))
    (C, n, m), h = lax.scan(step, (C0, n0, m0), xs)
    return from_chunks(h), C, n, m


def rwkv7_recurrence(r, w, k, v, kk, a, S0):
    def step(S, xs):
        rt, wt, kt, vt, kkt, at = xs
        sa = jnp.einsum('bhvk,bhk->bhv', S, -kkt)
        S = S * wt[:, :, None, :] + sa[..., None] * (kkt * at)[:, :, None, :] + vt[..., None] * kt[:, :, None, :]
        return S, jnp.einsum('bhvk,bhk->bhv', S, rt)

    xs = (jnp.moveaxis(r, 1, 0), jnp.moveaxis(w, 1, 0), jnp.moveaxis(k, 1, 0),
          jnp.moveaxis(v, 1, 0), jnp.moveaxis(kk, 1, 0), jnp.moveaxis(a, 1, 0))
    S, y = lax.scan(step, S0, xs)
    return jnp.moveaxis(y, 0, 1), S


def gla_chunked(q, k, v, la, S0):
    T = q.shape[2]
    L = math.gcd(T, CHUNK)
    causal = jnp.tril(jnp.ones((L, L), dtype=bool))[:, :, None]

    def step(S, xs):
        qc, kc, vc, lc = xs
        bc = jnp.cumsum(lc, axis=2)
        inter = jnp.einsum('bhtd,bhdv->bhtv', qc * jnp.exp(bc), S)
        diff = jnp.where(causal, bc[:, :, :, None, :] - bc[:, :, None, :, :], -jnp.inf)
        att = jnp.einsum('bhtd,bhsd,bhtsd->bhts', qc, kc, jnp.exp(diff))
        o = inter + jnp.einsum('bhts,bhsv->bhtv', att, vc)
        bl = bc[:, :, -1]
        S = jnp.exp(bl)[..., None] * S + jnp.einsum('bhsd,bhsv->bhdv', kc * jnp.exp(bl[:, :, None] - bc), vc)
        return S, o

    S, o = lax.scan(step, S0, (to_chunks(q, L), to_chunks(k, L), to_chunks(v, L), to_chunks(la, L)))
    return from_chunks(o), S


def s5_scan(u, lam_re, lam_im, log_step, b_re, b_im, c_re, c_im, d_skip, x_re0, x_im0):
    f32 = jnp.float32
    bsz, T, _ = u.shape
    ug = u.reshape(bsz, T, S5_GROUPS, S5_GROUP).astype(jnp.complex64)
    lam = lax.complex(lam_re.astype(f32), lam_im.astype(f32))
    lam_bar = jnp.exp(lam * jnp.exp(log_step.astype(f32))[:, None])
    b_bar = ((lam_bar - 1.0) / lam)[..., None] * lax.complex(b_re.astype(f32), b_im.astype(f32))
    bu = jnp.einsum('gpc,btgc->btgp', b_bar, ug)
    x0 = lax.complex(x_re0.astype(f32), x_im0.astype(f32))
    bu = bu.at[:, 0].add(lam_bar * x0)
    a = jnp.broadcast_to(lam_bar, bu.shape)

    def combine(e1, e2):
        a1, b1 = e1
        a2, b2 = e2
        return a1 * a2, a2 * b1 + b2

    _, xs = lax.associative_scan(combine, (a, bu), axis=1)
    cmat = lax.complex(c_re.astype(f32), c_im.astype(f32))
    y = jnp.real(jnp.einsum('gcp,btgp->btgc', cmat, xs)).reshape(bsz, T, S5_CH) + d_skip.astype(f32) * u
    x_last = xs[:, -1]
    return y, jnp.real(x_last), jnp.imag(x_last)


def even_mixer(h, state, p):
    C0, n0, m0, S0, shift0 = state
    (w_in, b_if, mu, w0, w_up, a0, a_up, g_up, k_k, k_a, r_k, ln_m_g, ln_r_g, ln_r_b, w_out) = p
    f32 = jnp.float32
    bsz, T, _ = h.shape
    z = jnp.matmul(h, w_in).astype(f32)
    zm, zr = z[..., :MLSTM_COLS], z[..., MLSTM_COLS:]
    q = to_heads(zm[..., :HALF], MLSTM_HEADS)
    k = to_heads(zm[..., HALF:2 * HALF], MLSTM_HEADS) * (MLSTM_HD ** -0.5)
    v = to_heads(zm[..., 2 * HALF:3 * HALF], MLSTM_HEADS)
    o_gate = jax.nn.sigmoid(zm[..., 3 * HALF:4 * HALF])
    gates = zm[..., 4 * HALF:] + b_if.astype(f32)
    i_log = jnp.swapaxes(gates[..., :MLSTM_HEADS], 1, 2)
    f_log = jnp.swapaxes(jax.nn.log_sigmoid(gates[..., MLSTM_HEADS:]), 1, 2)
    hm, C, n, m = mlstm_chunked(q, k, v, i_log, f_log, C0.astype(f32), n0.astype(f32), m0.astype(f32))
    hm = head_layer_norm(jnp.swapaxes(hm, 1, 2), ln_m_g, None, EPS)
    y_m = o_gate * hm.reshape(bsz, T, HALF)
    prev = jnp.concatenate([shift0.astype(f32)[:, None], zr[:, :-1]], axis=1)
    zs = zr + (prev - zr) * mu.astype(f32)
    new_shift = zr[:, -1]
    o1 = 3 * HALF
    o2 = o1 + RWKV_W_LORA
    o3 = o2 + RWKV_A_LORA
    r = zs[..., :HALF]
    kr = zs[..., HALF:2 * HALF]
    vr = zs[..., 2 * HALF:3 * HALF]
    w_log = -jax.nn.softplus(-(w0 + jnp.matmul(jnp.tanh(zs[..., o1:o2]), w_up))) - 0.5
    decay = jnp.exp(-jnp.exp(w_log))
    a = jax.nn.sigmoid(a0 + jnp.matmul(zs[..., o2:o3], a_up))
    g = jnp.matmul(jax.nn.sigmoid(zs[..., o3:]), g_up)
    kk = (kr * k_k).reshape(bsz, T, RWKV_HEADS, RWKV_HD)
    kk = kk * lax.rsqrt(jnp.maximum(jnp.sum(kk * kk, -1, keepdims=True), 1e-24))
    kr = kr * (1.0 + (a - 1.0) * k_a)
    shp = (bsz, T, RWKV_HEADS, RWKV_HD)
    r4, k4, v4 = r.reshape(shp), kr.reshape(shp), vr.reshape(shp)
    yr, S = rwkv7_recurrence(r4, decay.reshape(shp), k4, v4, kk, a.reshape(shp), S0.astype(f32))
    yr = head_layer_norm(yr, ln_r_g, ln_r_b, RWKV_LN_EPS)
    yr = yr + jnp.sum(r4 * k4 * r_k.astype(f32).reshape(RWKV_HEADS, RWKV_HD), -1, keepdims=True) * v4
    y_r = yr.reshape(bsz, T, HALF) * g
    y = jnp.matmul(jnp.concatenate([y_m, y_r], -1), w_out)
    return y, (C, n, m, S, new_shift)


def odd_mixer(h, state, p):
    S0, x_re0, x_im0 = state
    (w_in, a_up, a_b, ln_g, lam_re, lam_im, log_step, b_re, b_im, c_re, c_im, d_skip, w_glu, w_out) = p
    f32 = jnp.float32
    bsz, T, _ = h.shape
    z = jnp.matmul(h, w_in).astype(f32)
    q = to_heads(z[..., :GLA_QK], GLA_HEADS) * (GLA_DK ** -0.5)
    k = to_heads(z[..., GLA_QK:2 * GLA_QK], GLA_HEADS)
    v = to_heads(z[..., 2 * GLA_QK:2 * GLA_QK + HALF], GLA_HEADS)
    g = z[..., 2 * GLA_QK + HALF:2 * GLA_QK + 2 * HALF]
    a_down = z[..., 2 * GLA_QK + 2 * HALF:GLA_COLS]
    la = to_heads(jax.nn.log_sigmoid(jnp.matmul(a_down, a_up) + a_b) / GLA_TAU, GLA_HEADS)
    o, S = gla_chunked(q, k, v, la, S0.astype(f32))
    y_g = head_rms_norm(jnp.swapaxes(o, 1, 2), ln_g).reshape(bsz, T, HALF) * jax.nn.silu(g)
    u = z[..., GLA_COLS:]
    ys, x_re, x_im = s5_scan(u, lam_re, lam_im, log_step, b_re, b_im, c_re, c_im, d_skip, x_re0, x_im0)
    ys = jax.nn.gelu(ys)
    y_s = ys * jax.nn.sigmoid(jnp.matmul(ys, w_glu))
    y = jnp.matmul(jnp.concatenate([y_g, y_s], -1), w_out)
    return y, (S, x_re, x_im)


def conv_ffn(h, conv0, w_up, conv_w, conv_b, w_down):
    T = h.shape[1]
    u = jnp.matmul(h, w_up)
    ext = jnp.concatenate([conv0.astype(u.dtype), u], axis=1)
    c = conv_b + sum(conv_w[j] * ext[:, j:j + T] for j in range(CONV_W))
    val, gate = c[..., :D_FF], c[..., D_FF:]
    return jnp.matmul(val * jax.nn.silu(gate), w_down), ext[:, T:]


def trunk(x, st, prm):
    (st_C, st_n, st_m, st_S, st_shift, st_gla, st_re, st_im, st_conv) = st
    (norm_mix, norm_ffn, norm_final,
     e_w_in, e_b_if, e_mu, e_w0, e_w_up, e_a0, e_a_up, e_g_up, e_k_k, e_k_a, e_r_k, e_ln_m_g, e_ln_r_g, e_ln_r_b, e_w_out,
     o_w_in, o_a_up, o_a_b, o_ln_g, o_lam_re, o_lam_im, o_log_step, o_b_re, o_b_im, o_c_re, o_c_im, o_d, o_w_glu, o_w_out,
     f_w_up, f_conv_w, f_conv_b, f_w_down) = prm
    ev, od, cv = [], [], []
    for l in range(DEPTH):
        i = l // 2
        h = rms_norm(x, norm_mix[l])
        if l % 2 == 0:
            y, s = even_mixer(h, (st_C[i], st_n[i], st_m[i], st_S[i], st_shift[i]),
                              (e_w_in[i], e_b_if[i], e_mu[i], e_w0[i], e_w_up[i], e_a0[i], e_a_up[i], e_g_up[i],
                               e_k_k[i], e_k_a[i], e_r_k[i], e_ln_m_g[i], e_ln_r_g[i], e_ln_r_b[i], e_w_out[i]))
            ev.append(s)
        else:
            y, s = odd_mixer(h, (st_gla[i], st_re[i], st_im[i]),
                             (o_w_in[i], o_a_up[i], o_a_b[i], o_ln_g[i], o_lam_re[i], o_lam_im[i], o_log_step[i],
                              o_b_re[i], o_b_im[i], o_c_re[i], o_c_im[i], o_d[i], o_w_glu[i], o_w_out[i]))
            od.append(s)
        x = x + y.astype(x.dtype)
        h = rms_norm(x, norm_ffn[l])
        y, c = conv_ffn(h, st_conv[l], f_w_up[l], f_conv_w[l], f_conv_b[l], f_w_down[l])
        x = x + y.astype(x.dtype)
        cv.append(c)
    out = rms_norm(x, norm_final)
    new_state = (jnp.stack([s[0] for s in ev]), jnp.stack([s[1] for s in ev]), jnp.stack([s[2] for s in ev]),
                 jnp.stack([s[3] for s in ev]), jnp.stack([s[4] for s in ev]),
                 jnp.stack([s[0] for s in od]), jnp.stack([s[1] for s in od]), jnp.stack([s[2] for s in od]),
                 jnp.stack(cv))
    return out, new_state


def setup_inputs(seed: int = 0) -> dict:
    key = jax.random.key(seed)
    keys = jax.random.split(key, 96)
    counter = [0]

    def nk():
        counter[0] += 1
        return keys[counter[0] - 1]

    def nrm(shape, scale):
        return scale * jax.random.normal(nk(), shape, jnp.float32)

    def unif(shape, lo, hi):
        return jax.random.uniform(nk(), shape, jnp.float32, lo, hi)

    f2 = 2 * D_FF
    inp = {}
    inp['x_prompt'] = nrm((BATCH, SEQ, D_MODEL), 1.0)
    inp['x_sample'] = nrm((DEC_BATCH, DEC_SEQ, D_MODEL), 1.0)
    inp['state_mlstm_C'] = nrm((N_EVEN, DEC_BATCH, MLSTM_HEADS, MLSTM_HD, MLSTM_HD), 0.05)
    inp['state_mlstm_n'] = nrm((N_EVEN, DEC_BATCH, MLSTM_HEADS, MLSTM_HD), 0.1)
    inp['state_mlstm_m'] = nrm((N_EVEN, DEC_BATCH, MLSTM_HEADS), 1.0)
    inp['state_rwkv_S'] = nrm((N_EVEN, DEC_BATCH, RWKV_HEADS, RWKV_HD, RWKV_HD), 0.1)
    inp['state_rwkv_shift'] = nrm((N_EVEN, DEC_BATCH, RWKV_COLS), 1.0)
    inp['state_gla_S'] = nrm((N_ODD, DEC_BATCH, GLA_HEADS, GLA_DK, GLA_DV), 0.1)
    inp['state_s5_re'] = nrm((N_ODD, DEC_BATCH, S5_GROUPS, S5_P), 0.1)
    inp['state_s5_im'] = nrm((N_ODD, DEC_BATCH, S5_GROUPS, S5_P), 0.1)
    inp['state_ffn_conv'] = nrm((DEPTH, DEC_BATCH, CONV_W - 1, f2), 1.0)
    inp['norm_mix'] = 1.0 + nrm((DEPTH, D_MODEL), 0.02)
    inp['norm_ffn'] = 1.0 + nrm((DEPTH, D_MODEL), 0.02)
    inp['norm_final'] = 1.0 + nrm((D_MODEL,), 0.02)
    inp['e_w_in'] = nrm((N_EVEN, D_MODEL, EVEN_COLS), D_MODEL ** -0.5)
    inp['e_b_if'] = jnp.concatenate([nrm((N_EVEN, MLSTM_HEADS), 0.1),
                                     jnp.linspace(3.0, 6.0, MLSTM_HEADS, dtype=jnp.float32)[None] + nrm((N_EVEN, MLSTM_HEADS), 0.1)], axis=-1)
    inp['e_mu'] = unif((N_EVEN, RWKV_COLS), 0.0, 1.0)
    inp['e_w0'] = unif((N_EVEN, HALF), -6.0, -1.0)
    inp['e_w_up'] = nrm((N_EVEN, RWKV_W_LORA, HALF), 0.1)
    inp['e_a0'] = nrm((N_EVEN, HALF), 0.1)
    inp['e_a_up'] = nrm((N_EVEN, RWKV_A_LORA, HALF), RWKV_A_LORA ** -0.5)
    inp['e_g_up'] = nrm((N_EVEN, RWKV_G_LORA, HALF), RWKV_G_LORA ** -0.5)
    inp['e_k_k'] = 0.85 + nrm((N_EVEN, HALF), 0.05)
    inp['e_k_a'] = 1.0 + nrm((N_EVEN, HALF), 0.05)
    inp['e_r_k'] = nrm((N_EVEN, HALF), 0.1)
    inp['e_ln_m_g'] = 1.0 + nrm((N_EVEN, HALF), 0.02)
    inp['e_ln_r_g'] = 1.0 + nrm((N_EVEN, HALF), 0.02)
    inp['e_ln_r_b'] = nrm((N_EVEN, HALF), 0.02)
    inp['e_w_out'] = nrm((N_EVEN, 2 * HALF, D_MODEL), (2 * HALF) ** -0.5)
    inp['o_w_in'] = nrm((N_ODD, D_MODEL, ODD_COLS), D_MODEL ** -0.5)
    inp['o_a_up'] = nrm((N_ODD, GLA_GATE_RANK, GLA_QK), GLA_GATE_RANK ** -0.5)
    inp['o_a_b'] = nrm((N_ODD, GLA_QK), 0.1)
    inp['o_ln_g'] = 1.0 + nrm((N_ODD, HALF), 0.02)
    inp['o_lam_re'] = -0.5 + nrm((N_ODD, S5_GROUPS, S5_P), 0.01)
    inp['o_lam_im'] = math.pi * jnp.arange(S5_P, dtype=jnp.float32) + nrm((N_ODD, S5_GROUPS, S5_P), 0.01)
    inp['o_log_step'] = unif((N_ODD, S5_GROUPS), math.log(1e-3), math.log(1e-1))
    inp['o_b_re'] = nrm((N_ODD, S5_GROUPS, S5_P, S5_GROUP), (2 * S5_GROUP) ** -0.5)
    inp['o_b_im'] = nrm((N_ODD, S5_GROUPS, S5_P, S5_GROUP), (2 * S5_GROUP) ** -0.5)
    inp['o_c_re'] = nrm((N_ODD, S5_GROUPS, S5_GROUP, S5_P), 0.5)
    inp['o_c_im'] = nrm((N_ODD, S5_GROUPS, S5_GROUP, S5_P), 0.5)
    inp['o_d'] = nrm((N_ODD, S5_CH), 1.0)
    inp['o_w_glu'] = nrm((N_ODD, S5_CH, S5_CH), S5_CH ** -0.5)
    inp['o_w_out'] = nrm((N_ODD, 2 * HALF, D_MODEL), (2 * HALF) ** -0.5)
    inp['f_w_up'] = nrm((DEPTH, D_MODEL, f2), D_MODEL ** -0.5)
    inp['f_conv_w'] = nrm((DEPTH, CONV_W, f2), CONV_W ** -0.5)
    inp['f_conv_b'] = nrm((DEPTH, f2), 0.02)
    inp['f_w_down'] = nrm((DEPTH, D_FF, D_MODEL), D_FF ** -0.5)
    return inp


def reference(x_prompt, x_sample, state_mlstm_C, state_mlstm_n, state_mlstm_m, state_rwkv_S, state_rwkv_shift,
              state_gla_S, state_s5_re, state_s5_im, state_ffn_conv,
              norm_mix, norm_ffn, norm_final,
              e_w_in, e_b_if, e_mu, e_w0, e_w_up, e_a0, e_a_up, e_g_up, e_k_k, e_k_a, e_r_k,
              e_ln_m_g, e_ln_r_g, e_ln_r_b, e_w_out,
              o_w_in, o_a_up, o_a_b, o_ln_g, o_lam_re, o_lam_im, o_log_step, o_b_re, o_b_im, o_c_re, o_c_im,
              o_d, o_w_glu, o_w_out,
              f_w_up, f_conv_w, f_conv_b, f_w_down):
    params = (norm_mix, norm_ffn, norm_final,
              e_w_in, e_b_if, e_mu, e_w0, e_w_up, e_a0, e_a_up, e_g_up, e_k_k, e_k_a, e_r_k,
              e_ln_m_g, e_ln_r_g, e_ln_r_b, e_w_out,
              o_w_in, o_a_up, o_a_b, o_ln_g, o_lam_re, o_lam_im, o_log_step, o_b_re, o_b_im, o_c_re, o_c_im,
              o_d, o_w_glu, o_w_out,
              f_w_up, f_conv_w, f_conv_b, f_w_down)
    f32 = jnp.float32
    bp = x_prompt.shape[0]
    init = (jnp.zeros((N_EVEN, bp, MLSTM_HEADS, MLSTM_HD, MLSTM_HD), f32),
            jnp.zeros((N_EVEN, bp, MLSTM_HEADS, MLSTM_HD), f32),
            jnp.full((N_EVEN, bp, MLSTM_HEADS), M_INIT, f32),
            jnp.zeros((N_EVEN, bp, RWKV_HEADS, RWKV_HD, RWKV_HD), f32),
            jnp.zeros((N_EVEN, bp, RWKV_COLS), f32),
            jnp.zeros((N_ODD, bp, GLA_HEADS, GLA_DK, GLA_DV), f32),
            jnp.zeros((N_ODD, bp, S5_GROUPS, S5_P), f32),
            jnp.zeros((N_ODD, bp, S5_GROUPS, S5_P), f32),
            jnp.zeros((DEPTH, bp, CONV_W - 1, 2 * D_FF), x_prompt.dtype))
    y_prompt, (p_C, p_n, p_m, p_S, p_shift, p_gla, p_re, p_im, p_conv) = trunk(x_prompt, init, params)
    past = (state_mlstm_C, state_mlstm_n, state_mlstm_m, state_rwkv_S, state_rwkv_shift,
            state_gla_S, state_s5_re, state_s5_im, state_ffn_conv)
    y_sample, (s_C, s_n, s_m, s_S, s_shift, s_gla, s_re, s_im, s_conv) = trunk(x_sample, past, params)
    return (y_prompt, y_sample, p_C, p_n, p_m, p_S, p_shift, p_gla, p_re, p_im, p_conv,
            s_C, s_n, s_m, s_S, s_shift, s_gla, s_re, s_im, s_conv)
```

```python
import functools
import math

import jax
import jax.numpy as jnp
from jax import lax
from jax.experimental import pallas as pl
from jax.experimental.pallas import tpu as pltpu

F32 = jnp.float32
BF16 = jnp.bfloat16

D_MODEL = 1024
HALF = D_MODEL // 2
MLSTM_HEADS = 4
MLSTM_HD = HALF // MLSTM_HEADS
RWKV_HD = 64
RWKV_HEADS = HALF // RWKV_HD
RWKV_W_LORA = 64
RWKV_A_LORA = 64
RWKV_G_LORA = 128
RWKV_COLS = 3 * HALF + RWKV_W_LORA + RWKV_A_LORA + RWKV_G_LORA
GLA_HEADS = 4
GLA_DK = 64
GLA_DV = 128
GLA_QK = GLA_HEADS * GLA_DK
GLA_GATE_RANK = 16
GLA_TAU = 16.0
S5_CH = HALF
S5_GROUP = 16
S5_GROUPS = S5_CH // S5_GROUP
S5_P = 64
S5_STATE = S5_GROUPS * S5_P
D_FF = 2816
CONV_W = 3
CHUNK = 64
EPS = 1e-6
RWKV_LN_EPS = 64e-5
M_INIT = -1e30

LANES = 128
SUBLANES = 8
VMEM_LIMIT_BYTES = 56 * 1024 * 1024

Z0_RWKV = 0
Z0_GATE = RWKV_COLS
Z0_Q = Z0_GATE + 2 * LANES
Z0_K, Z0_V, Z0_O = Z0_Q + HALF, Z0_Q + 2 * HALF, Z0_Q + 3 * HALF
Z0_COLS = Z0_Q + 4 * HALF
RW_LORA = 3 * HALF
RW_G = RW_LORA + RWKV_W_LORA + RWKV_A_LORA
Z1_Q, Z1_K, Z1_V, Z1_G, Z1_U = 0, GLA_QK, 2 * GLA_QK, 2 * GLA_QK + HALF, 2 * GLA_QK + 2 * HALF
Z1_AD = Z1_U + S5_CH
Z1_COLS = 2560


def _cparams(*sem):
    return pltpu.CompilerParams(dimension_semantics=sem, vmem_limit_bytes=VMEM_LIMIT_BYTES)


def _rms(x, g):
    return x * lax.rsqrt(jnp.mean(x * x, -1, keepdims=True) + EPS) * g


def _norm_matmul_kernel(x_ref, g_ref, w_ref, o_ref, h_scr):
    @pl.when(pl.program_id(1) == 0)
    def _():
        h_scr[...] = _rms(x_ref[...], g_ref[...]).astype(BF16)

    o_ref[...] = jnp.dot(h_scr[...], w_ref[...], preferred_element_type=F32)


def norm_matmul(x, g, w, *, tm, tn):
    m, d = x.shape
    n = w.shape[1]
    return pl.pallas_call(
        _norm_matmul_kernel,
        grid=(m // tm, n // tn),
        in_specs=[pl.BlockSpec((tm, d), lambda i, j: (i, 0)),
                  pl.BlockSpec((1, d), lambda i, j: (0, 0)),
                  pl.BlockSpec((d, tn), lambda i, j: (0, j))],
        out_specs=pl.BlockSpec((tm, tn), lambda i, j: (i, j)),
        out_shape=jax.ShapeDtypeStruct((m, n), F32),
        scratch_shapes=[pltpu.VMEM((tm, d), BF16)],
        compiler_params=_cparams("parallel", "arbitrary"),
        name="norm_matmul",
    )(x, g.reshape(1, d), w)


def _out_proj_kernel(x_ref, ya_ref, yb_ref, wa_ref, wb_ref, o_ref):
    acc = jnp.dot(ya_ref[...].astype(BF16), wa_ref[...], preferred_element_type=F32)
    acc += jnp.dot(yb_ref[...].astype(BF16), wb_ref[...], preferred_element_type=F32)
    o_ref[...] = x_ref[...] + acc


def out_proj(x, ya, yb, wa, wb, *, tm):
    m, d = x.shape
    k = ya.shape[1]
    if yb.ndim == 3:
        tiles_per_seq = yb.shape[0] // tm
        yb = yb.reshape(yb.shape[0], -1)
        yb_spec = pl.BlockSpec((tm, k), lambda i: (i % tiles_per_seq, i // tiles_per_seq))
    else:
        yb_spec = pl.BlockSpec((tm, k), lambda i: (i, 0))
    return pl.pallas_call(
        _out_proj_kernel,
        grid=(m // tm,),
        in_specs=[pl.BlockSpec((tm, d), lambda i: (i, 0)),
                  pl.BlockSpec((tm, k), lambda i: (i, 0)),
                  yb_spec,
                  pl.BlockSpec((k, d), lambda i: (0, 0)),
                  pl.BlockSpec((k, d), lambda i: (0, 0))],
        out_specs=pl.BlockSpec((tm, d), lambda i: (i, 0)),
        out_shape=jax.ShapeDtypeStruct((m, d), F32),
        compiler_params=_cparams("parallel"),
        name="out_proj",
    )(x, ya, yb, wa, wb)


FFN_SUB = 256


def _ffn_kernel(x_ref, g_ref, wup_ref, cw_ref, cb_ref, wdn_ref, st_ref, gf_ref,
                o_ref, tail_ref, h_scr, act_scr, car_scr, *, tm, seq_len, final_norm):
    long_seq = seq_len >= tm
    i = pl.program_id(0)
    h_scr[...] = _rms(x_ref[...], g_ref[...]).astype(BF16)
    row = lax.broadcasted_iota(jnp.int32, (tm, FFN_SUB), 0)
    if long_seq:
        @pl.when((i * tm) % seq_len == 0)
        def _():
            car_scr[...] = st_ref[...]
    else:
        t_in_seq = row % seq_len

    def conv_part(col0):
        u = jnp.dot(h_scr[...], wup_ref[:, col0:col0 + FFN_SUB], preferred_element_type=F32)
        p1 = pltpu.roll(u, 1, 0)
        p2 = pltpu.roll(u, 2, 0)
        if long_seq:
            halo = car_scr[:, col0:col0 + FFN_SUB]
            h6 = halo[SUBLANES - 2:SUBLANES - 1]
            h7 = halo[SUBLANES - 1:SUBLANES]
            p1 = jnp.where(row == 0, h7, p1)
            p2 = jnp.where(row == 0, h6, jnp.where(row == 1, h7, p2))
            car_scr[:, col0:col0 + FFN_SUB] = u[tm - SUBLANES:]
            tail_ref[:, col0:col0 + FFN_SUB] = u[tm - SUBLANES:]
        else:
            e = st_ref[:, col0:col0 + FFN_SUB]
            p1 = jnp.where(t_in_seq == 0, pltpu.roll(e, tm - 1, 0), p1)
            p2 = jnp.where(t_in_seq < 2, e, p2)
            tail_ref[:, col0:col0 + FFN_SUB] = u
        cw = cw_ref[:, col0:col0 + FFN_SUB]
        return (cb_ref[:, col0:col0 + FFN_SUB]
                + (cw[0:1] * p2 + cw[1:2] * p1 + cw[2:3] * u))

    for c in range(D_FF // FFN_SUB):
        val = conv_part(c * FFN_SUB)
        gate = conv_part(D_FF + c * FFN_SUB)
        act_scr[:, c * FFN_SUB:(c + 1) * FFN_SUB] = (val * (gate * jax.nn.sigmoid(gate))).astype(BF16)

    y = x_ref[...] + jnp.dot(act_scr[...], wdn_ref[...], preferred_element_type=F32)
    if final_norm:
        y = _rms(y, gf_ref[...])
    o_ref[...] = y


def conv_ffn(x, g, w_up, conv_w, conv_b, w_down, conv0, gf, *, batch, seq_len, tm, final_norm):
    m, d = x.shape
    f2 = 2 * D_FF
    long_seq = seq_len >= tm
    if long_seq:
        assert seq_len % tm == 0
        tiles_per_seq = seq_len // tm
        st = jnp.pad(conv0, ((0, 0), (SUBLANES - (CONV_W - 1), 0), (0, 0)))
        st_spec = pl.BlockSpec((None, SUBLANES, f2), lambda i: (i // tiles_per_seq, 0, 0))
        tail_shape = jax.ShapeDtypeStruct((batch, SUBLANES, f2), F32)
        tail_spec = pl.BlockSpec((None, SUBLANES, f2), lambda i: (i // tiles_per_seq, 0, 0))
        sem = "arbitrary"
    else:
        assert seq_len == SUBLANES and tm % seq_len == 0
        st = jnp.pad(conv0, ((0, 0), (0, seq_len - (CONV_W - 1)), (0, 0))).reshape(m, f2)
        st_spec = pl.BlockSpec((tm, f2), lambda i: (i, 0))
        tail_shape = jax.ShapeDtypeStruct((m, f2), F32)
        tail_spec = pl.BlockSpec((tm, f2), lambda i: (i, 0))
        sem = "parallel"
    const = lambda i: (0, 0)
    out, tail = pl.pallas_call(
        functools.partial(_ffn_kernel, tm=tm, seq_len=seq_len, final_norm=final_norm),
        grid=(m // tm,),
        in_specs=[pl.BlockSpec((tm, d), lambda i: (i, 0)),
                  pl.BlockSpec((1, d), const),
                  pl.BlockSpec((d, f2), const, pipeline_mode=pl.Buffered(1)),
                  pl.BlockSpec((CONV_W, f2), const),
                  pl.BlockSpec((1, f2), const),
                  pl.BlockSpec((D_FF, d), const, pipeline_mode=pl.Buffered(1)),
                  st_spec,
                  pl.BlockSpec((1, d), const)],
        out_specs=[pl.BlockSpec((tm, d), lambda i: (i, 0)), tail_spec],
        out_shape=[jax.ShapeDtypeStruct((m, d), F32), tail_shape],
        scratch_shapes=[pltpu.VMEM((tm, d), BF16),
                        pltpu.VMEM((tm, D_FF), BF16),
                        pltpu.VMEM((SUBLANES, f2), F32)],
        compiler_params=_cparams(sem),
        name="conv_ffn",
    )(x, g.reshape(1, d), w_up, conv_w, conv_b.reshape(1, f2), w_down, st, gf.reshape(1, d))
    new_state = tail.reshape(batch, -1, f2)[:, -(CONV_W - 1):]
    return out, new_state


def _log_sigmoid(x):
    return jnp.minimum(x, 0.0) - jnp.log1p(jnp.exp(-jnp.abs(x)))


def _dot_nt(a, b):
    return lax.dot_general(a, b, (((1,), (1,)), ((), ())), preferred_element_type=F32)


def _dot_tn(a, b):
    return lax.dot_general(a, b, (((0,), (0,)), ((), ())), preferred_element_type=F32)


def _mlstm_kernel(bif_ref, q_ref, k_ref, v_ref, og_ref, gz_ref, gt_ref, c0_ref, n0_ref, m0_ref,
                  lng_ref, y_ref, c_ref, n_ref, m_ref, *, chunk, n_chunks):
    L = chunk

    @pl.when(pl.program_id(1) == 0)
    def _():
        c_ref[...] = c0_ref[...]
        n_ref[...] = n0_ref[...]
        m_ref[...] = m0_ref[...]

    t_idx = lax.broadcasted_iota(jnp.int32, (L, L), 0)
    s_idx = lax.broadcasted_iota(jnp.int32, (L, L), 1)
    causal = s_idx <= t_idx
    scale = MLSTM_HD ** -0.5

    def chunk_body(c, carry):
        r0 = pl.multiple_of(c * L, L)
        rows = pl.ds(r0, L)
        for h in range(MLSTM_HEADS):
            cols = slice(h * MLSTM_HD, (h + 1) * MLSTM_HD)
            b_i = bif_ref[h]
            b_f = bif_ref[MLSTM_HEADS + h]
            i_col = gz_ref[rows, h:h + 1] + b_i
            f_col = _log_sigmoid(gz_ref[rows, MLSTM_HEADS + h:MLSTM_HEADS + h + 1] + b_f)
            i_row = gt_ref[h, pl.ds(c, 1), :] + b_i
            f_row = _log_sigmoid(gt_ref[MLSTM_HEADS + h, pl.ds(c, 1), :] + b_f)
            b_col = jnp.sum(jnp.where(causal, f_row, 0.0), axis=1, keepdims=True)
            b_row = jnp.sum(jnp.where(t_idx <= s_idx, f_col, 0.0), axis=0, keepdims=True)
            g_row = i_row - b_row
            g_col = i_col - b_col
            cm_col = jnp.max(jnp.where(causal, g_row, -jnp.inf), axis=1, keepdims=True)
            m_old = m_ref[h]
            mu_col = jnp.maximum(m_old, cm_col)
            w = jnp.where(causal, jnp.exp(g_row - mu_col), 0.0)
            s_inter = jnp.exp(m_old - mu_col)

            q = q_ref[rows, cols]
            ks = k_ref[rows, cols] * scale
            v = v_ref[rows, cols]
            qb, kb, vb = q.astype(BF16), ks.astype(BF16), v.astype(BF16)
            c_old = c_ref[h]
            n_old = n_ref[h]
            qk = _dot_nt(qb, kb) * w
            num = s_inter * _dot_nt(qb, c_old.astype(BF16)) + jnp.dot(
                qk.astype(BF16), vb, preferred_element_type=F32)
            den = (s_inter * jnp.sum(q * n_old, axis=1, keepdims=True)
                   + jnp.sum(qk, axis=1, keepdims=True))
            hh = num / jnp.maximum(jnp.abs(den), jnp.exp(-(b_col + mu_col)))

            mu_last = mu_col[L - 1:L]
            wl_col = jnp.exp(g_col - mu_last)
            dec = jnp.exp(m_old - mu_last)
            c_ref[h] = dec * c_old + _dot_tn((v * wl_col).astype(BF16), kb)
            n_ref[h] = dec * n_old + jnp.sum(wl_col * ks, axis=0, keepdims=True)
            m_ref[h] = b_col[L - 1:L] + mu_last

            mean = jnp.mean(hh, -1, keepdims=True)
            xc = hh - mean
            hn = xc * lax.rsqrt(jnp.mean(xc * xc, -1, keepdims=True) + EPS) * lng_ref[:, cols]
            y_ref[rows, cols] = jax.nn.sigmoid(og_ref[rows, cols]) * hn
        return carry

    lax.fori_loop(0, n_chunks, chunk_body, 0)


def mlstm_mixer(z, b_if, c0, n0, m0, ln_g, *, batch, seq_len, tb):
    m_rows = batch * seq_len
    L = math.gcd(seq_len, CHUNK)
    n_tb = seq_len // tb
    n_chunks = tb // L
    gates = z[:, Z0_GATE:Z0_GATE + 2 * MLSTM_HEADS].reshape(batch, seq_len, 2 * MLSTM_HEADS)
    gates_t = jnp.swapaxes(gates, 1, 2).reshape(batch, 2 * MLSTM_HEADS, seq_len // L, L)
    zcol = lambda off: pl.BlockSpec((tb, HALF), lambda b, t: (b * n_tb + t, off // HALF))
    state4 = lambda a, b_: pl.BlockSpec((None, MLSTM_HEADS, a, b_), lambda b, t: (b, 0, 0, 0))
    y, c, n, m = pl.pallas_call(
        functools.partial(_mlstm_kernel, chunk=L, n_chunks=n_chunks),
        grid=(batch, n_tb),
        in_specs=[pl.BlockSpec(memory_space=pltpu.SMEM),
                  zcol(Z0_Q), zcol(Z0_K), zcol(Z0_V), zcol(Z0_O),
                  pl.BlockSpec((tb, LANES), lambda b, t: (b * n_tb + t, Z0_GATE // LANES)),
                  pl.BlockSpec((None, 2 * MLSTM_HEADS, n_chunks, L), lambda b, t: (b, 0, t, 0)),
                  state4(MLSTM_HD, MLSTM_HD), state4(1, MLSTM_HD), state4(1, 1),
                  pl.BlockSpec((1, HALF), lambda b, t: (0, 0))],
        out_specs=[pl.BlockSpec((tb, HALF), lambda b, t: (b * n_tb + t, 0)),
                   state4(MLSTM_HD, MLSTM_HD), state4(1, MLSTM_HD), state4(1, 1)],
        out_shape=[jax.ShapeDtypeStruct((m_rows, HALF), F32),
                   jax.ShapeDtypeStruct((batch, MLSTM_HEADS, MLSTM_HD, MLSTM_HD), F32),
                   jax.ShapeDtypeStruct((batch, MLSTM_HEADS, 1, MLSTM_HD), F32),
                   jax.ShapeDtypeStruct((batch, MLSTM_HEADS, 1, 1), F32)],
        compiler_params=_cparams("parallel", "arbitrary"),
        name="mlstm",
    )(b_if, z, z, z, z, z, gates_t,
      c0, n0.reshape(batch, MLSTM_HEADS, 1, MLSTM_HD), m0.reshape(batch, MLSTM_HEADS, 1, 1),
      ln_g.reshape(1, HALF))
    return y, c, n.reshape(batch, MLSTM_HEADS, MLSTM_HD), m.reshape(batch, MLSTM_HEADS)


def _split2(x):
    hi = x.astype(BF16)
    lo = (x - hi.astype(F32)).astype(BF16)
    return hi, lo


def _seg_sum(x, ones_blk):
    hi, lo = _split2(x)
    return (jnp.dot(hi, ones_blk, preferred_element_type=F32)
            + jnp.dot(lo, ones_blk, preferred_element_type=F32))


def _softplus(x):
    return jnp.maximum(x, 0.0) + jnp.log1p(jnp.exp(-jnp.abs(x)))


def _block_ones(n, seg):
    idx = jnp.arange(n) // seg
    return (idx[:, None] == idx[None, :]).astype(BF16)


def _rwkv_prep_kernel(z_ref, st_ref, mu_ref, w0_ref, wup_ref, a0_ref, aup_ref, gup_ref, kk_ref, ka_ref,
                      rk_ref, ones_ref, r_out, w_out, k_out, v_out, kk_out, kka_out, bv_out, g_out,
                      car_scr, *, tm, seq_len):
    zr = z_ref[...]
    row = lax.broadcasted_iota(jnp.int32, zr.shape, 0)
    prev = pltpu.roll(zr, 1, 0)
    if seq_len >= tm:
        @pl.when((pl.program_id(0) * tm) % seq_len == 0)
        def _():
            car_scr[...] = st_ref[...]
        prev = jnp.where(row == 0, car_scr[SUBLANES - 1:SUBLANES], prev)
        car_scr[...] = zr[tm - SUBLANES:]
    else:
        prev = jnp.where(row % seq_len == 0, st_ref[...], prev)
    zs = zr + (prev - zr) * mu_ref[...]
    r = zs[:, :HALF]
    kr = zs[:, HALF:2 * HALF]
    vr = zs[:, 2 * HALF:3 * HALF]
    lora = zs[:, RW_LORA:RW_G]
    ones = ones_ref[...]
    w_log = -_softplus(-(w0_ref[...] + jnp.dot(jnp.tanh(lora).astype(BF16), wup_ref[...],
                                               preferred_element_type=F32))) - 0.5
    a = jax.nn.sigmoid(a0_ref[...] + jnp.dot(lora.astype(BF16), aup_ref[...], preferred_element_type=F32))
    g = jnp.dot(jax.nn.sigmoid(zs[:, RW_G:]).astype(BF16), gup_ref[...], preferred_element_type=F32)
    kk = kr * kk_ref[...]
    kk = kk * lax.rsqrt(jnp.maximum(_seg_sum(kk * kk, ones), 1e-24))
    k2 = kr * (1.0 + (a - 1.0) * ka_ref[...])
    r_out[...] = r
    w_out[...] = jnp.exp(-jnp.exp(w_log))
    k_out[...] = k2
    v_out[...] = vr
    kk_out[...] = kk
    kka_out[...] = kk * a
    bv_out[...] = _seg_sum(r * k2 * rk_ref[...], ones) * vr
    g_out[...] = g


def _rwkv_rec_kernel(r_ref, w_ref, k_ref, v_ref, kk_ref, kka_ref, bv_ref, g_ref, s0_ref, lng_ref, lnb_ref,
                     ones2_ref, ones_ref, y_ref, s_ref, yraw_scr, *, bb, tb):
    @pl.when(pl.program_id(1) == 0)
    def _():
        s_ref[...] = s0_ref[...]

    shape = (RWKV_HD, LANES)
    eye2 = (lax.broadcasted_iota(jnp.int32, shape, 0)
            == lax.broadcasted_iota(jnp.int32, shape, 1) % RWKV_HD)
    ones2 = ones2_ref[...]
    ones1 = ones2[:LANES]

    def step(t, carry):
        for b in range(bb):
            for p in range(RWKV_HEADS // 2):
                cols = slice(p * LANES, (p + 1) * LANES)
                row = lambda ref: ref[t, b:b + 1, cols]
                s = s_ref[b, p]
                hi, lo = _split2(s * row(kk_ref))
                sa = -jnp.dot(jnp.concatenate([hi, lo], axis=1), ones2, preferred_element_type=F32)
                v_row = row(v_ref)
                v_hi = v_row.astype(BF16).astype(F32)
                v_pieces = jnp.concatenate([jnp.where(eye2, v_hi, 0.0).astype(BF16),
                                            jnp.where(eye2, v_row - v_hi, 0.0).astype(BF16)], axis=1)
                v_tile = jnp.dot(v_pieces, ones2, preferred_element_type=F32)
                s = s * row(w_ref) + sa * row(kka_ref) + v_tile * row(k_ref)
                s_ref[b, p] = s
                y_tile = jnp.dot((s * row(r_ref)).astype(BF16), ones1, preferred_element_type=F32)
                yraw_scr[t, b:b + 1, cols] = jnp.sum(jnp.where(eye2, y_tile, 0.0), axis=0, keepdims=True)
        return carry

    lax.fori_loop(0, tb, step, 0)

    ones = ones_ref[...]
    y = yraw_scr[...].reshape(tb * bb, HALF)
    xc = y - _seg_sum(y, ones) * (1.0 / RWKV_HD)
    var = _seg_sum(xc * xc, ones) * (1.0 / RWKV_HD)
    yn = xc * lax.rsqrt(var + RWKV_LN_EPS) * lng_ref[...] + lnb_ref[...]
    out = (yn + bv_ref[...].reshape(tb * bb, HALF)) * g_ref[...].reshape(tb * bb, HALF)
    y_ref[...] = out.reshape(tb, bb, HALF)


def rwkv_mixer(z, s0, shift0, mu, w0, w_up, a0, a_up, g_up, k_k, k_a, r_k, ln_g, ln_b,
               *, batch, seq_len, tm, bb, tb):
    m_rows = batch * seq_len
    tm_shape = (seq_len, batch, HALF)
    if seq_len >= tm:
        tiles_per_seq = seq_len // tm
        st = jnp.pad(shift0[:, None], ((0, 0), (SUBLANES - 1, 0), (0, 0)))
        st_spec = pl.BlockSpec((None, SUBLANES, RWKV_COLS), lambda i: (i // tiles_per_seq, 0, 0))
        sem = "arbitrary"
        tok = pl.BlockSpec((tm, HALF), lambda i: (i % tiles_per_seq, i // tiles_per_seq))
        tok_shape = (seq_len, batch * HALF)
    else:
        st = jnp.pad(shift0[:, None], ((0, 0), (0, seq_len - 1), (0, 0))).reshape(m_rows, RWKV_COLS)
        st_spec = pl.BlockSpec((tm, RWKV_COLS), lambda i: (i, 0))
        sem = "parallel"
        tok = pl.BlockSpec((tm, HALF), lambda i: (i, 0))
        tok_shape = (m_rows, HALF)
    row = lambda a: a.reshape(1, -1)
    const = lambda i: (0, 0)
    vec = pl.BlockSpec((1, HALF), const)
    lora_rows = RWKV_W_LORA + RWKV_A_LORA
    w_up_p = jnp.pad(w_up, ((0, RWKV_A_LORA), (0, 0))).astype(BF16)
    a_up_p = jnp.pad(a_up, ((RWKV_W_LORA, 0), (0, 0))).astype(BF16)
    ones_half = _block_ones(HALF, RWKV_HD)
    prep = pl.pallas_call(
        functools.partial(_rwkv_prep_kernel, tm=tm, seq_len=seq_len),
        grid=(m_rows // tm,),
        in_specs=[pl.BlockSpec((tm, RWKV_COLS), lambda i: (i, Z0_RWKV // RWKV_COLS)),
                  st_spec,
                  pl.BlockSpec((1, RWKV_COLS), const),
                  vec, pl.BlockSpec((lora_rows, HALF), const),
                  vec, pl.BlockSpec((lora_rows, HALF), const),
                  pl.BlockSpec((RWKV_G_LORA, HALF), const),
                  vec, vec, vec,
                  pl.BlockSpec((HALF, HALF), const)],
        out_specs=[tok] * 8,
        out_shape=[jax.ShapeDtypeStruct(tok_shape, F32)] * 8,
        scratch_shapes=[pltpu.VMEM((SUBLANES, RWKV_COLS), F32)],
        compiler_params=_cparams(sem),
        name="rwkv_prep",
    )(z, st, row(mu), row(w0), w_up_p, row(a0), a_up_p, g_up.astype(BF16), row(k_k), row(k_a), row(r_k),
      ones_half)
    if seq_len >= tm:
        seqs = [a.reshape(tm_shape) for a in prep]
    else:
        seqs = [jnp.swapaxes(a.reshape(batch, seq_len, HALF), 0, 1) for a in prep]

    pairs = RWKV_HEADS // 2
    s_pairs = (s0.reshape(batch, pairs, 2, RWKV_HD, RWKV_HD).transpose(0, 1, 3, 2, 4)
               .reshape(batch, pairs, RWKV_HD, LANES))
    blk = pl.BlockSpec((tb, bb, HALF), lambda b, t: (t, b, 0))
    s_spec = pl.BlockSpec((bb, pairs, RWKV_HD, LANES), lambda b, t: (b, 0, 0, 0))
    const2 = lambda b, t: (0, 0)
    y, s_new = pl.pallas_call(
        functools.partial(_rwkv_rec_kernel, bb=bb, tb=tb),
        grid=(batch // bb, seq_len // tb),
        in_specs=[blk] * 8 + [s_spec,
                              pl.BlockSpec((1, HALF), const2), pl.BlockSpec((1, HALF), const2),
                              pl.BlockSpec((2 * LANES, LANES), const2),
                              pl.BlockSpec((HALF, HALF), const2)],
        out_specs=[blk, s_spec],
        out_shape=[jax.ShapeDtypeStruct(tm_shape, F32),
                   jax.ShapeDtypeStruct((batch, pairs, RWKV_HD, LANES), F32)],
        scratch_shapes=[pltpu.VMEM((tb, bb, HALF), F32)],
        compiler_params=_cparams("parallel", "arbitrary"),
        name="rwkv_rec",
    )(*seqs, s_pairs, row(ln_g), row(ln_b),
      jnp.concatenate([_block_ones(LANES, RWKV_HD)] * 2, axis=0), ones_half)
    s_new = (s_new.reshape(batch, pairs, RWKV_HD, 2, RWKV_HD).transpose(0, 1, 3, 2, 4)
             .reshape(batch, RWKV_HEADS, RWKV_HD, RWKV_HD))
    return y, s_new


GLA_SUB = 16


def _gla_kernel(q_ref, k_ref, v_ref, g_ref, ad_ref, aup_ref, ab_ref, lng_ref, s0_ref,
                y_ref, s_ref, *, chunk, n_chunks):
    L = chunk
    sub = min(GLA_SUB, L)

    @pl.when(pl.program_id(1) == 0)
    def _():
        s_ref[...] = s0_ref[...]

    lane = lax.broadcasted_iota(jnp.int32, (1, LANES), 1)
    head_mask = [(lane < GLA_DK).astype(F32), (lane >= GLA_DK).astype(F32)]
    tril = (lax.broadcasted_iota(jnp.int32, (L, L), 1)
            <= lax.broadcasted_iota(jnp.int32, (L, L), 0)).astype(F32)
    eye = (lax.broadcasted_iota(jnp.int32, (LANES, LANES), 0)
           == lax.broadcasted_iota(jnp.int32, (LANES, LANES), 1))

    def chunk_body(c, carry):
        rows = pl.ds(pl.multiple_of(c * L, L), L)
        pre = jnp.dot(ad_ref[rows, :].astype(BF16), aup_ref[...], preferred_element_type=F32) + ab_ref[...]
        la = _log_sigmoid(pre) * (1.0 / GLA_TAU)
        bc_all = jnp.dot(tril, la, preferred_element_type=F32, precision=lax.Precision.HIGHEST)
        for p in range(GLA_HEADS // 2):
            kcols = slice(p * LANES, (p + 1) * LANES)
            qp = q_ref[rows, kcols] * (GLA_DK ** -0.5)
            kp = k_ref[rows, kcols]
            bc = bc_all[:, kcols]
            v_pair = v_ref[rows, 2 * p * GLA_DV:2 * (p + 1) * GLA_DV].astype(BF16)
            s_old = s_ref[p]
            stack = lambda x: jnp.concatenate([x * head_mask[0], x * head_mask[1]], axis=0).astype(BF16)

            inter = jnp.dot(stack(qp * jnp.exp(bc)), s_old.astype(BF16), preferred_element_type=F32)
            intra = []
            for i in range(L // sub):
                lo, hi = i * sub, (i + 1) * sub
                c_i = bc[lo - 1:lo] if i > 0 else jnp.zeros((1, LANES), F32)
                qe = qp[lo:hi] * jnp.exp(bc[lo:hi] - c_i)
                ke = kp[:hi] * jnp.exp(c_i - bc[:hi])
                att = _dot_nt(stack(qe), ke.astype(BF16))
                t_idx = lo + lax.broadcasted_iota(jnp.int32, (2 * sub, hi), 0) % sub
                s_idx = lax.broadcasted_iota(jnp.int32, (2 * sub, hi), 1)
                att = jnp.where(s_idx <= t_idx, att, 0.0)
                intra.append(jnp.dot(att.astype(BF16), v_pair[:hi], preferred_element_type=F32))
            for j in range(2):
                h = 2 * p + j
                vcols = slice(h * GLA_DV, (h + 1) * GLA_DV)
                o = inter[j * L:(j + 1) * L] + jnp.concatenate(
                    [blk[j * sub:(j + 1) * sub, j * GLA_DV:(j + 1) * GLA_DV] for blk in intra], axis=0)
                on = o * lax.rsqrt(jnp.mean(o * o, -1, keepdims=True) + EPS) * lng_ref[:, vcols]
                gate = g_ref[rows, vcols]
                y_ref[rows, vcols] = on * (gate * jax.nn.sigmoid(gate))

            bl = bc[L - 1:L]
            kd = kp * jnp.exp(bl - bc)
            v_stack = jnp.concatenate([v_pair[:, :GLA_DV], v_pair[:, GLA_DV:]], axis=0)
            dec_col = jnp.sum(jnp.where(eye, jnp.exp(bl), 0.0), axis=1, keepdims=True)
            s_ref[p] = dec_col * s_old + _dot_tn(stack(kd), v_stack)
        return carry

    lax.fori_loop(0, n_chunks, chunk_body, 0)


def gla_mixer(z, s0, a_up, a_b, ln_g, *, batch, seq_len, tb):
    m_rows = batch * seq_len
    L = math.gcd(seq_len, CHUNK)
    n_tb = seq_len // tb
    pairs = GLA_HEADS // 2
    zblk = lambda width, off: pl.BlockSpec((tb, width), lambda b, t: (b * n_tb + t, off // width))
    s_spec = pl.BlockSpec((None, pairs, 2 * GLA_DK, GLA_DV), lambda b, t: (b, 0, 0, 0))
    const = lambda b, t: (0, 0)
    a_up_p = jnp.pad(a_up, ((0, LANES - GLA_GATE_RANK), (0, 0))).astype(BF16)
    y, s_new = pl.pallas_call(
        functools.partial(_gla_kernel, chunk=L, n_chunks=tb // L),
        grid=(batch, n_tb),
        in_specs=[zblk(GLA_QK, Z1_Q), zblk(GLA_QK, Z1_K), zblk(HALF, Z1_V), zblk(HALF, Z1_G),
                  zblk(LANES, Z1_AD),
                  pl.BlockSpec((LANES, GLA_QK), const), pl.BlockSpec((1, GLA_QK), const),
                  pl.BlockSpec((1, HALF), const), s_spec],
        out_specs=[pl.BlockSpec((tb, HALF), lambda b, t: (b * n_tb + t, 0)), s_spec],
        out_shape=[jax.ShapeDtypeStruct((m_rows, HALF), F32),
                   jax.ShapeDtypeStruct((batch, pairs, 2 * GLA_DK, GLA_DV), F32)],
        compiler_params=_cparams("parallel", "arbitrary"),
        name="gla",
    )(z, z, z, z, z, a_up_p, a_b.reshape(1, GLA_QK), ln_g.reshape(1, HALF),
      s0.reshape(batch, pairs, 2 * GLA_DK, GLA_DV))
    return y, s_new.reshape(batch, GLA_HEADS, GLA_DK, GLA_DV)


S5_KT = LANES // S5_GROUP


def _s5_param_kernel(lre_ref, lim_ref, step_ref, bre_ref, bim_ref, bbre_ref, bbim_ref, pre_ref, pim_ref):
    lre, lim = lre_ref[...], lim_ref[...]
    step = jnp.exp(step_ref[...])
    mag = jnp.exp(lre * step)
    bar_re = mag * jnp.cos(lim * step)
    bar_im = mag * jnp.sin(lim * step)
    inv = 1.0 / (lre * lre + lim * lim)
    cre = ((bar_re - 1.0) * lre + bar_im * lim) * inv
    cim = (bar_im * lre - (bar_re - 1.0) * lim) * inv
    for g in range(S5_GROUPS):
        cr, ci = cre[g:g + 1], cim[g:g + 1]
        bbre_ref[g] = cr * bre_ref[g] - ci * bim_ref[g]
        bbim_ref[g] = cr * bim_ref[g] + ci * bre_ref[g]
    p_re, p_im = bar_re, bar_im
    for k in range(SUBLANES):
        pre_ref[k] = p_re
        pim_ref[k] = p_im
        p_re, p_im = p_re * bar_re - p_im * bar_im, p_re * bar_im + p_im * bar_re


def _s5_kernel(u_ref, wbre_ref, wbim_ref, pre_ref, pim_ref, x0re_ref, x0im_ref, wcre_ref, wcim_ref,
               d_ref, wglu_ref, y_ref, xre_ref, xim_ref, sre_scr, sim_scr, *, tt):
    @pl.when(pl.program_id(1) == 0)
    def _():
        xre_ref[...] = x0re_ref[...]
        xim_ref[...] = x0im_ref[...]

    u = u_ref[...]
    ub = u.astype(BF16)
    nblk = S5_CH // LANES
    wide = S5_STATE // nblk
    row = lax.broadcasted_iota(jnp.int32, (tt, wide), 0) % SUBLANES
    for kt in range(nblk):
        cols = slice(kt * wide, (kt + 1) * wide)
        ukt = ub[:, kt * LANES:(kt + 1) * LANES]
        xr = jnp.dot(ukt, wbre_ref[kt], preferred_element_type=F32)
        xi = jnp.dot(ukt, wbim_ref[kt], preferred_element_type=F32)
        for lvl in range(3):
            s = 1 << lvl
            pr, pi = pre_ref[s - 1:s, cols], pim_ref[s - 1:s, cols]
            sr, si = pltpu.roll(xr, s, 0), pltpu.roll(xi, s, 0)
            keep = row >= s
            xr, xi = (xr + jnp.where(keep, pr * sr - pi * si, 0.0),
                      xi + jnp.where(keep, pr * si + pi * sr, 0.0))
        sre_scr[:, cols] = xr
        sim_scr[:, cols] = xi

    pw_re, pw_im = pre_ref[...], pim_ref[...]

    def group_body(j, carry):
        cr, ci = carry
        rows = pl.ds(pl.multiple_of(j * SUBLANES, SUBLANES), SUBLANES)
        gr = sre_scr[rows, :] + (pw_re * cr - pw_im * ci)
        gi = sim_scr[rows, :] + (pw_re * ci + pw_im * cr)
        sre_scr[rows, :] = gr
        sim_scr[rows, :] = gi
        return gr[SUBLANES - 1:], gi[SUBLANES - 1:]

    cr, ci = lax.fori_loop(0, tt // SUBLANES, group_body, (xre_ref[...], xim_ref[...]))
    xre_ref[...] = cr
    xim_ref[...] = ci

    parts = []
    for nt in range(nblk):
        cols = slice(nt * wide, (nt + 1) * wide)
        parts.append(jnp.dot(sre_scr[:, cols].astype(BF16), wcre_ref[nt], preferred_element_type=F32)
                     - jnp.dot(sim_scr[:, cols].astype(BF16), wcim_ref[nt], preferred_element_type=F32))
    y = jnp.concatenate(parts, axis=1) + d_ref[...] * u
    ys = y * (0.5 * (1.0 + jnp.tanh(math.sqrt(2.0 / math.pi) * (y + 0.044715 * (y * y * y)))))
    y_ref[...] = ys * jax.nn.sigmoid(jnp.dot(ys.astype(BF16), wglu_ref[...], preferred_element_type=F32))


def s5_mixer(z, x0_re, x0_im, lam_re, lam_im, log_step, b_re, b_im, c_re, c_im, d_skip, w_glu,
             *, batch, seq_len, tt):
    m_rows = batch * seq_len
    gpc = (S5_GROUPS, S5_GROUP, S5_P)
    bb_re, bb_im, p_re, p_im = pl.pallas_call(
        _s5_param_kernel,
        out_shape=[jax.ShapeDtypeStruct(gpc, F32), jax.ShapeDtypeStruct(gpc, F32),
                   jax.ShapeDtypeStruct((SUBLANES, S5_GROUPS, S5_P), F32),
                   jax.ShapeDtypeStruct((SUBLANES, S5_GROUPS, S5_P), F32)],
        name="s5_params",
    )(lam_re, lam_im, log_step.reshape(S5_GROUPS, 1), jnp.swapaxes(b_re, 1, 2), jnp.swapaxes(b_im, 1, 2))

    nblk = S5_CH // LANES
    eye = jnp.eye(S5_KT, dtype=F32)

    def in_blocks(bb):
        w = bb.reshape(nblk, S5_KT, S5_GROUP, 1, S5_P) * eye[None, :, None, :, None]
        return w.reshape(nblk, LANES, S5_KT * S5_P).astype(BF16)

    def out_blocks(c):
        w = jnp.swapaxes(c, 1, 2).reshape(nblk, S5_KT, S5_P, 1, S5_GROUP) * eye[None, :, None, :, None]
        return w.reshape(nblk, S5_KT * S5_P, LANES).astype(BF16)

    n_tt = seq_len // tt
    const2 = lambda b, t: (0, 0)
    const3 = lambda b, t: (0, 0, 0)
    st_spec = pl.BlockSpec((None, 1, S5_STATE), lambda b, t: (b, 0, 0))
    wb_spec = pl.BlockSpec((nblk, LANES, S5_KT * S5_P), const3)
    wc_spec = pl.BlockSpec((nblk, S5_KT * S5_P, LANES), const3)
    pw_spec = pl.BlockSpec((SUBLANES, S5_STATE), const2)
    y, x_re, x_im = pl.pallas_call(
        functools.partial(_s5_kernel, tt=tt),
        grid=(batch, n_tt),
        in_specs=[pl.BlockSpec((tt, S5_CH), lambda b, t: (b * n_tt + t, Z1_U // S5_CH)),
                  wb_spec, wb_spec, pw_spec, pw_spec, st_spec, st_spec, wc_spec, wc_spec,
                  pl.BlockSpec((1, S5_CH), const2), pl.BlockSpec((S5_CH, S5_CH), const2)],
        out_specs=[pl.BlockSpec((tt, S5_CH), lambda b, t: (b * n_tt + t, 0)), st_spec, st_spec],
        out_shape=[jax.ShapeDtypeStruct((m_rows, S5_CH), F32),
                   jax.ShapeDtypeStruct((batch, 1, S5_STATE), F32),
                   jax.ShapeDtypeStruct((batch, 1, S5_STATE), F32)],
        scratch_shapes=[pltpu.VMEM((tt, S5_STATE), F32), pltpu.VMEM((tt, S5_STATE), F32)],
        compiler_params=_cparams("parallel", "arbitrary"),
        name="s5",
    )(z, in_blocks(bb_re), in_blocks(bb_im),
      p_re.reshape(SUBLANES, S5_STATE), p_im.reshape(SUBLANES, S5_STATE),
      x0_re.reshape(batch, 1, S5_STATE), x0_im.reshape(batch, 1, S5_STATE),
      out_blocks(c_re), out_blocks(c_im), d_skip.reshape(1, S5_CH), w_glu.astype(BF16))
    shape = (batch, S5_GROUPS, S5_P)
    return y, x_re.reshape(shape), x_im.reshape(shape)


def _pack_even_w_in(w_in):
    d = w_in.shape[0]
    m_cols = 4 * HALF + 2 * MLSTM_HEADS
    parts = [w_in[:, m_cols:],
             w_in[:, 4 * HALF:m_cols], jnp.zeros((d, Z0_Q - Z0_GATE - 2 * MLSTM_HEADS), w_in.dtype),
             w_in[:, :4 * HALF]]
    return jnp.concatenate(parts, axis=1).astype(BF16)


def _pack_odd_w_in(w_in):
    d = w_in.shape[0]
    gla_main = 2 * GLA_QK + 2 * HALF
    gla_cols = gla_main + GLA_GATE_RANK
    parts = [w_in[:, :gla_main], w_in[:, gla_cols:], w_in[:, gla_main:gla_cols]]
    w = jnp.concatenate(parts, axis=1)
    return jnp.pad(w, ((0, 0), (0, Z1_COLS - w.shape[1]))).astype(BF16)


def _trunk(x, st, prm, *, batch, seq_len):
    (st_C, st_n, st_m, st_S, st_shift, st_gla, st_re, st_im, st_conv) = st
    (norm_mix, norm_ffn, norm_final,
     e_w_in, e_b_if, e_mu, e_w0, e_w_up, e_a0, e_a_up, e_g_up, e_k_k, e_k_a, e_r_k, e_ln_m_g, e_ln_r_g,
     e_ln_r_b, e_w_out,
     o_w_in, o_a_up, o_a_b, o_ln_g, o_lam_re, o_lam_im, o_log_step, o_b_re, o_b_im, o_c_re, o_c_im, o_d,
     o_w_glu, o_w_out,
     f_w_up, f_conv_w, f_conv_b, f_w_down) = prm
    assert len(e_w_in) == 1 and len(o_w_in) == 1 and len(f_w_up) == 2, "wired for depth 2"
    long_seq = seq_len >= 512
    assert long_seq or seq_len == SUBLANES
    m_rows = batch * seq_len
    tiles = dict(
        tm_proj=min(m_rows, 1024),
        tm_tok=512 if long_seq else 256,
        tm_ffn=512 if long_seq else 128,
        tb=512 if long_seq else seq_len,
        tb_rwkv=64 if long_seq else seq_len,
        tt_s5=256 if long_seq else seq_len,
    )
    x = x.reshape(m_rows, D_MODEL)
    ffn = functools.partial(conv_ffn, batch=batch, seq_len=seq_len, tm=tiles["tm_ffn"])

    z0 = norm_matmul(x, norm_mix[0], _pack_even_w_in(e_w_in[0]), tm=tiles["tm_proj"], tn=512)
    y_m, c_new, n_new, m_new = mlstm_mixer(z0, e_b_if[0], st_C[0], st_n[0], st_m[0], e_ln_m_g[0],
                                           batch=batch, seq_len=seq_len, tb=tiles["tb"])
    y_r, s_new = rwkv_mixer(z0, st_S[0], st_shift[0], e_mu[0], e_w0[0], e_w_up[0], e_a0[0], e_a_up[0],
                            e_g_up[0], e_k_k[0], e_k_a[0], e_r_k[0], e_ln_r_g[0], e_ln_r_b[0],
                            batch=batch, seq_len=seq_len, tm=tiles["tm_tok"], bb=SUBLANES,
                            tb=tiles["tb_rwkv"])
    if not long_seq:
        y_r = jnp.swapaxes(y_r, 0, 1).reshape(m_rows, HALF)
    shift_new = z0.reshape(batch, seq_len, Z0_COLS)[:, -1, Z0_RWKV:Z0_RWKV + RWKV_COLS]
    w_out = e_w_out[0].astype(BF16)
    x = out_proj(x, y_m, y_r, w_out[:HALF], w_out[HALF:], tm=tiles["tm_tok"])
    x, conv_new0 = ffn(x, norm_ffn[0], f_w_up[0].astype(BF16), f_conv_w[0], f_conv_b[0],
                       f_w_down[0].astype(BF16), st_conv[0], norm_final, final_norm=False)

    z1 = norm_matmul(x, norm_mix[1], _pack_odd_w_in(o_w_in[0]), tm=tiles["tm_proj"], tn=512)
    y_g, gla_new = gla_mixer(z1, st_gla[0], o_a_up[0], o_a_b[0], o_ln_g[0],
                             batch=batch, seq_len=seq_len, tb=tiles["tb"])
    y_s, re_new, im_new = s5_mixer(z1, st_re[0], st_im[0], o_lam_re[0], o_lam_im[0], o_log_step[0],
                                   o_b_re[0], o_b_im[0], o_c_re[0], o_c_im[0], o_d[0], o_w_glu[0],
                                   batch=batch, seq_len=seq_len, tt=tiles["tt_s5"])
    w_out = o_w_out[0].astype(BF16)
    x = out_proj(x, y_g, y_s, w_out[:HALF], w_out[HALF:], tm=tiles["tm_tok"])
    x, conv_new1 = ffn(x, norm_ffn[1], f_w_up[1].astype(BF16), f_conv_w[1], f_conv_b[1],
                       f_w_down[1].astype(BF16), st_conv[1], norm_final, final_norm=True)
    new_state = (c_new[None], n_new[None], m_new[None], s_new[None], shift_new[None],
                 gla_new[None], re_new[None], im_new[None], jnp.stack([conv_new0, conv_new1]))
    return x.reshape(batch, seq_len, D_MODEL), new_state


def kernel(x_prompt, x_sample, state_mlstm_C, state_mlstm_n, state_mlstm_m, state_rwkv_S, state_rwkv_shift,
           state_gla_S, state_s5_re, state_s5_im, state_ffn_conv,
           norm_mix, norm_ffn, norm_final,
           e_w_in, e_b_if, e_mu, e_w0, e_w_up, e_a0, e_a_up, e_g_up, e_k_k, e_k_a, e_r_k,
           e_ln_m_g, e_ln_r_g, e_ln_r_b, e_w_out,
           o_w_in, o_a_up, o_a_b, o_ln_g, o_lam_re, o_lam_im, o_log_step, o_b_re, o_b_im, o_c_re, o_c_im,
           o_d, o_w_glu, o_w_out,
           f_w_up, f_conv_w, f_conv_b, f_w_down):
    params = (norm_mix, norm_ffn, norm_final,
              e_w_in, e_b_if, e_mu, e_w0, e_w_up, e_a0, e_a_up, e_g_up, e_k_k, e_k_a, e_r_k,
              e_ln_m_g, e_ln_r_g, e_ln_r_b, e_w_out,
              o_w_in, o_a_up, o_a_b, o_ln_g, o_lam_re, o_lam_im, o_log_step, o_b_re, o_b_im, o_c_re, o_c_im,
              o_d, o_w_glu, o_w_out,
              f_w_up, f_conv_w, f_conv_b, f_w_down)
    bp, tp, _ = x_prompt.shape
    bs, ts, _ = x_sample.shape
    n_even, n_odd, depth = state_mlstm_C.shape[0], state_gla_S.shape[0], state_ffn_conv.shape[0]
    init = (jnp.zeros((n_even, bp, MLSTM_HEADS, MLSTM_HD, MLSTM_HD), F32),
            jnp.zeros((n_even, bp, MLSTM_HEADS, MLSTM_HD), F32),
            jnp.full((n_even, bp, MLSTM_HEADS), M_INIT, F32),
            jnp.zeros((n_even, bp, RWKV_HEADS, RWKV_HD, RWKV_HD), F32),
            jnp.zeros((n_even, bp, RWKV_COLS), F32),
            jnp.zeros((n_odd, bp, GLA_HEADS, GLA_DK, GLA_DV), F32),
            jnp.zeros((n_odd, bp, S5_GROUPS, S5_P), F32),
            jnp.zeros((n_odd, bp, S5_GROUPS, S5_P), F32),
            jnp.zeros((depth, bp, CONV_W - 1, 2 * D_FF), F32))
    past = (state_mlstm_C, state_mlstm_n, state_mlstm_m, state_rwkv_S, state_rwkv_shift,
            state_gla_S, state_s5_re, state_s5_im, state_ffn_conv)
    y_prompt, p_state = _trunk(x_prompt, init, params, batch=bp, seq_len=tp)
    y_sample, s_state = _trunk(x_sample, past, params, batch=bs, seq_len=ts)
    return (y_prompt, y_sample) + tuple(p_state) + tuple(s_state)
```

```python
import functools
import math

import jax
import jax.numpy as jnp
from jax import lax
from jax.experimental import pallas as pl
from jax.experimental.pallas import tpu as pltpu

F32 = jnp.float32
BF16 = jnp.bfloat16

D_MODEL = 1024
HALF = D_MODEL // 2
MLSTM_HEADS = 4
MLSTM_HD = HALF // MLSTM_HEADS
RWKV_HD = 64
RWKV_HEADS = HALF // RWKV_HD
RWKV_W_LORA = 64
RWKV_A_LORA = 64
RWKV_G_LORA = 128
RWKV_COLS = 3 * HALF + RWKV_W_LORA + RWKV_A_LORA + RWKV_G_LORA
GLA_HEADS = 4
GLA_DK = 64
GLA_DV = 128
GLA_QK = GLA_HEADS * GLA_DK
GLA_GATE_RANK = 16
GLA_TAU = 16.0
S5_CH = HALF
S5_GROUP = 16
S5_GROUPS = S5_CH // S5_GROUP
S5_P = 64
S5_STATE = S5_GROUPS * S5_P
D_FF = 2816
CONV_W = 3
CHUNK = 64
EPS = 1e-6
RWKV_LN_EPS = 64e-5
M_INIT = -1e30

LANES = 128
SUBLANES = 8
VMEM_LIMIT_BYTES = 56 * 1024 * 1024

Z0_RWKV = 0
Z0_GATE = RWKV_COLS
Z0_Q = Z0_GATE + 2 * LANES
Z0_K, Z0_V, Z0_O = Z0_Q + HALF, Z0_Q + 2 * HALF, Z0_Q + 3 * HALF
Z0_COLS = Z0_Q + 4 * HALF
RW_LORA = 3 * HALF
RW_G = RW_LORA + RWKV_W_LORA + RWKV_A_LORA
Z1_Q, Z1_K, Z1_V, Z1_G, Z1_U = 0, GLA_QK, 2 * GLA_QK, 2 * GLA_QK + HALF, 2 * GLA_QK + 2 * HALF
Z1_AD = Z1_U + S5_CH
Z1_COLS = 2560


def _cparams(*sem):
    return pltpu.CompilerParams(dimension_semantics=sem, vmem_limit_bytes=VMEM_LIMIT_BYTES)


def _rms(x, g):
    return x * lax.rsqrt(jnp.mean(x * x, -1, keepdims=True) + EPS) * g


def _norm_matmul_kernel(x_ref, g_ref, w_ref, o_ref, h_scr):
    @pl.when(pl.program_id(1) == 0)
    def _():
        h_scr[...] = _rms(x_ref[...], g_ref[...]).astype(BF16)

    o_ref[...] = jnp.dot(h_scr[...], w_ref[...], preferred_element_type=F32)


def norm_matmul(x, g, w, *, tm, tn):
    m, d = x.shape
    n = w.shape[1]
    return pl.pallas_call(
        _norm_matmul_kernel,
        grid=(m // tm, n // tn),
        in_specs=[pl.BlockSpec((tm, d), lambda i, j: (i, 0)),
                  pl.BlockSpec((1, d), lambda i, j: (0, 0)),
                  pl.BlockSpec((d, tn), lambda i, j: (0, j))],
        out_specs=pl.BlockSpec((tm, tn), lambda i, j: (i, j)),
        out_shape=jax.ShapeDtypeStruct((m, n), F32),
        scratch_shapes=[pltpu.VMEM((tm, d), BF16)],
        compiler_params=_cparams("parallel", "arbitrary"),
        name="norm_matmul",
    )(x, g.reshape(1, d), w)


def _out_proj_kernel(x_ref, ya_ref, yb_ref, wa_ref, wb_ref, o_ref):
    acc = jnp.dot(ya_ref[...].astype(BF16), wa_ref[...], preferred_element_type=F32)
    acc += jnp.dot(yb_ref[...].astype(BF16), wb_ref[...], preferred_element_type=F32)
    o_ref[...] = x_ref[...] + acc


def out_proj(x, ya, yb, wa, wb, *, tm):
    m, d = x.shape
    k = ya.shape[1]
    if yb.ndim == 3:
        tiles_per_seq = yb.shape[0] // tm
        yb = yb.reshape(yb.shape[0], -1)
        yb_spec = pl.BlockSpec((tm, k), lambda i: (i % tiles_per_seq, i // tiles_per_seq))
    else:
        yb_spec = pl.BlockSpec((tm, k), lambda i: (i, 0))
    return pl.pallas_call(
        _out_proj_kernel,
        grid=(m // tm,),
        in_specs=[pl.BlockSpec((tm, d), lambda i: (i, 0)),
                  pl.BlockSpec((tm, k), lambda i: (i, 0)),
                  yb_spec,
                  pl.BlockSpec((k, d), lambda i: (0, 0)),
                  pl.BlockSpec((k, d), lambda i: (0, 0))],
        out_specs=pl.BlockSpec((tm, d), lambda i: (i, 0)),
        out_shape=jax.ShapeDtypeStruct((m, d), F32),
        compiler_params=_cparams("parallel"),
        name="out_proj",
    )(x, ya, yb, wa, wb)


FFN_SUB = 256


def _ffn_kernel(x_ref, g_ref, wup_ref, cw_ref, cb_ref, wdn_ref, st_ref, gf_ref,
                o_ref, tail_ref, h_scr, act_scr, car_scr, *, tm, seq_len, final_norm):
    long_seq = seq_len >= tm
    i = pl.program_id(0)
    h_scr[...] = _rms(x_ref[...], g_ref[...]).astype(BF16)
    row = lax.broadcasted_iota(jnp.int32, (tm, FFN_SUB), 0)
    if long_seq:
        @pl.when((i * tm) % seq_len == 0)
        def _():
            car_scr[...] = st_ref[...]
    else:
        t_in_seq = row % seq_len

    def conv_part(col0):
        u = jnp.dot(h_scr[...], wup_ref[:, col0:col0 + FFN_SUB], preferred_element_type=F32)
        p1 = pltpu.roll(u, 1, 0)
        p2 = pltpu.roll(u, 2, 0)
        if long_seq:
            halo = car_scr[:, col0:col0 + FFN_SUB]
            h6 = halo[SUBLANES - 2:SUBLANES - 1]
            h7 = halo[SUBLANES - 1:SUBLANES]
            p1 = jnp.where(row == 0, h7, p1)
            p2 = jnp.where(row == 0, h6, jnp.where(row == 1, h7, p2))
            car_scr[:, col0:col0 + FFN_SUB] = u[tm - SUBLANES:]
            tail_ref[:, col0:col0 + FFN_SUB] = u[tm - SUBLANES:]
        else:
            e = st_ref[:, col0:col0 + FFN_SUB]
            p1 = jnp.where(t_in_seq == 0, pltpu.roll(e, tm - 1, 0), p1)
            p2 = jnp.where(t_in_seq < 2, e, p2)
            tail_ref[:, col0:col0 + FFN_SUB] = u
        cw = cw_ref[:, col0:col0 + FFN_SUB]
        return (cb_ref[:, col0:col0 + FFN_SUB]
                + (cw[0:1] * p2 + cw[1:2] * p1 + cw[2:3] * u))

    for c in range(D_FF // FFN_SUB):
        val = conv_part(c * FFN_SUB)
        gate = conv_part(D_FF + c * FFN_SUB)
        act_scr[:, c * FFN_SUB:(c + 1) * FFN_SUB] = (val * (gate * jax.nn.sigmoid(gate))).astype(BF16)

    y = x_ref[...] + jnp.dot(act_scr[...], wdn_ref[...], preferred_element_type=F32)
    if final_norm:
        y = _rms(y, gf_ref[...])
    o_ref[...] = y


def conv_ffn(x, g, w_up, conv_w, conv_b, w_down, conv0, gf, *, batch, seq_len, tm, final_norm):
    m, d = x.shape
    f2 = 2 * D_FF
    long_seq = seq_len >= tm
    if long_seq:
        assert seq_len % tm == 0
        tiles_per_seq = seq_len // tm
        st = jnp.pad(conv0, ((0, 0), (SUBLANES - (CONV_W - 1), 0), (0, 0)))
        st_spec = pl.BlockSpec((None, SUBLANES, f2), lambda i: (i // tiles_per_seq, 0, 0))
        tail_shape = jax.ShapeDtypeStruct((batch, SUBLANES, f2), F32)
        tail_spec = pl.BlockSpec((None, SUBLANES, f2), lambda i: (i // tiles_per_seq, 0, 0))
        sem = "arbitrary"
    else:
        assert seq_len == SUBLANES and tm % seq_len == 0
        st = jnp.pad(conv0, ((0, 0), (0, seq_len - (CONV_W - 1)), (0, 0))).reshape(m, f2)
        st_spec = pl.BlockSpec((tm, f2), lambda i: (i, 0))
        tail_shape = jax.ShapeDtypeStruct((m, f2), F32)
        tail_spec = pl.BlockSpec((tm, f2), lambda i: (i, 0))
        sem = "parallel"
    const = lambda i: (0, 0)
    out, tail = pl.pallas_call(
        functools.partial(_ffn_kernel, tm=tm, seq_len=seq_len, final_norm=final_norm),
        grid=(m // tm,),
        in_specs=[pl.BlockSpec((tm, d), lambda i: (i, 0)),
                  pl.BlockSpec((1, d), const),
                  pl.BlockSpec((d, f2), const, pipeline_mode=pl.Buffered(1)),
                  pl.BlockSpec((CONV_W, f2), const),
                  pl.BlockSpec((1, f2), const),
                  pl.BlockSpec((D_FF, d), const, pipeline_mode=pl.Buffered(1)),
                  st_spec,
                  pl.BlockSpec((1, d), const)],
        out_specs=[pl.BlockSpec((tm, d), lambda i: (i, 0)), tail_spec],
        out_shape=[jax.ShapeDtypeStruct((m, d), F32), tail_shape],
        scratch_shapes=[pltpu.VMEM((tm, d), BF16),
                        pltpu.VMEM((tm, D_FF), BF16),
                        pltpu.VMEM((SUBLANES, f2), F32)],
        compiler_params=_cparams(sem),
        name="conv_ffn",
    )(x, g.reshape(1, d), w_up, conv_w, conv_b.reshape(1, f2), w_down, st, gf.reshape(1, d))
    new_state = tail.reshape(batch, -1, f2)[:, -(CONV_W - 1):]
    return out, new_state


def _log_sigmoid(x):
    return jnp.minimum(x, 0.0) - jnp.log1p(jnp.exp(-jnp.abs(x)))


def _dot_nt(a, b):
    return lax.dot_general(a, b, (((1,), (1,)), ((), ())), preferred_element_type=F32)


def _dot_tn(a, b):
    return lax.dot_general(a, b, (((0,), (0,)), ((), ())), preferred_element_type=F32)


def _mlstm_kernel(bif_ref, q_ref, k_ref, v_ref, og_ref, gz_ref, gt_ref, c0_ref, n0_ref, m0_ref,
                  lng_ref, y_ref, c_ref, n_ref, m_ref, *, chunk, n_chunks):
    L = chunk

    @pl.when(pl.program_id(1) == 0)
    def _():
        c_ref[...] = c0_ref[...]
        n_ref[...] = n0_ref[...]
        m_ref[...] = m0_ref[...]

    t_idx = lax.broadcasted_iota(jnp.int32, (L, L), 0)
    s_idx = lax.broadcasted_iota(jnp.int32, (L, L), 1)
    causal = s_idx <= t_idx
    scale = MLSTM_HD ** -0.5

    def chunk_body(c, carry):
        r0 = pl.multiple_of(c * L, L)
        rows = pl.ds(r0, L)
        for h in range(MLSTM_HEADS):
            cols = slice(h * MLSTM_HD, (h + 1) * MLSTM_HD)
            b_i = bif_ref[h]
            b_f = bif_ref[MLSTM_HEADS + h]
            i_col = gz_ref[rows, h:h + 1] + b_i
            f_col = _log_sigmoid(gz_ref[rows, MLSTM_HEADS + h:MLSTM_HEADS + h + 1] + b_f)
            i_row = gt_ref[h, pl.ds(c, 1), :] + b_i
            f_row = _log_sigmoid(gt_ref[MLSTM_HEADS + h, pl.ds(c, 1), :] + b_f)
            b_col = jnp.sum(jnp.where(causal, f_row, 0.0), axis=1, keepdims=True)
            b_row = jnp.sum(jnp.where(t_idx <= s_idx, f_col, 0.0), axis=0, keepdims=True)
            g_row = i_row - b_row
            g_col = i_col - b_col
            cm_col = jnp.max(jnp.where(causal, g_row, -jnp.inf), axis=1, keepdims=True)
            m_old = m_ref[h]
            mu_col = jnp.maximum(m_old, cm_col)
            w = jnp.where(causal, jnp.exp(g_row - mu_col), 0.0)
            s_inter = jnp.exp(m_old - mu_col)

            q = q_ref[rows, cols]
            ks = k_ref[rows, cols] * scale
            v = v_ref[rows, cols]
            qb, kb, vb = q.astype(BF16), ks.astype(BF16), v.astype(BF16)
            c_old = c_ref[h]
            n_old = n_ref[h]
            qk = _dot_nt(qb, kb) * w
            num = s_inter * _dot_nt(qb, c_old.astype(BF16)) + jnp.dot(
                qk.astype(BF16), vb, preferred_element_type=F32)
            den = (s_inter * jnp.sum(q * n_old, axis=1, keepdims=True)
                   + jnp.sum(qk, axis=1, keepdims=True))
            hh = num / jnp.maximum(jnp.abs(den), jnp.exp(-(b_col + mu_col)))

            mu_last = mu_col[L - 1:L]
            wl_col = jnp.exp(g_col - mu_last)
            dec = jnp.exp(m_old - mu_last)
            c_ref[h] = dec * c_old + _dot_tn((v * wl_col).astype(BF16), kb)
            n_ref[h] = dec * n_old + jnp.sum(wl_col * ks, axis=0, keepdims=True)
            m_ref[h] = b_col[L - 1:L] + mu_last

            mean = jnp.mean(hh, -1, keepdims=True)
            xc = hh - mean
            hn = xc * lax.rsqrt(jnp.mean(xc * xc, -1, keepdims=True) + EPS) * lng_ref[:, cols]
            y_ref[rows, cols] = jax.nn.sigmoid(og_ref[rows, cols]) * hn
        return carry

    lax.fori_loop(0, n_chunks, chunk_body, 0)


def mlstm_mixer(z, b_if, c0, n0, m0, ln_g, *, batch, seq_len, tb):
    m_rows = batch * seq_len
    L = math.gcd(seq_len, CHUNK)
    n_tb = seq_len // tb
    n_chunks = tb // L
    gates = z[:, Z0_GATE:Z0_GATE + 2 * MLSTM_HEADS].reshape(batch, seq_len, 2 * MLSTM_HEADS)
    gates_t = jnp.swapaxes(gates, 1, 2).reshape(batch, 2 * MLSTM_HEADS, seq_len // L, L)
    zcol = lambda off: pl.BlockSpec((tb, HALF), lambda b, t: (b * n_tb + t, off // HALF))
    state4 = lambda a, b_: pl.BlockSpec((None, MLSTM_HEADS, a, b_), lambda b, t: (b, 0, 0, 0))
    y, c, n, m = pl.pallas_call(
        functools.partial(_mlstm_kernel, chunk=L, n_chunks=n_chunks),
        grid=(batch, n_tb),
        in_specs=[pl.BlockSpec(memory_space=pltpu.SMEM),
                  zcol(Z0_Q), zcol(Z0_K), zcol(Z0_V), zcol(Z0_O),
                  pl.BlockSpec((tb, LANES), lambda b, t: (b * n_tb + t, Z0_GATE // LANES)),
                  pl.BlockSpec((None, 2 * MLSTM_HEADS, n_chunks, L), lambda b, t: (b, 0, t, 0)),
                  state4(MLSTM_HD, MLSTM_HD), state4(1, MLSTM_HD), state4(1, 1),
                  pl.BlockSpec((1, HALF), lambda b, t: (0, 0))],
        out_specs=[pl.BlockSpec((tb, HALF), lambda b, t: (b * n_tb + t, 0)),
                   state4(MLSTM_HD, MLSTM_HD), state4(1, MLSTM_HD), state4(1, 1)],
        out_shape=[jax.ShapeDtypeStruct((m_rows, HALF), F32),
                   jax.ShapeDtypeStruct((batch, MLSTM_HEADS, MLSTM_HD, MLSTM_HD), F32),
                   jax.ShapeDtypeStruct((batch, MLSTM_HEADS, 1, MLSTM_HD), F32),
                   jax.ShapeDtypeStruct((batch, MLSTM_HEADS, 1, 1), F32)],
        compiler_params=_cparams("parallel", "arbitrary"),
        name="mlstm",
    )(b_if, z, z, z, z, z, gates_t,
      c0, n0.reshape(batch, MLSTM_HEADS, 1, MLSTM_HD), m0.reshape(batch, MLSTM_HEADS, 1, 1),
      ln_g.reshape(1, HALF))
    return y, c, n.reshape(batch, MLSTM_HEADS, MLSTM_HD), m.reshape(batch, MLSTM_HEADS)


def _split2(x):
    hi = x.astype(BF16)
    lo = (x - hi.astype(F32)).astype(BF16)
    return hi, lo


def _seg_sum(x, ones_blk):
    hi, lo = _split2(x)
    return (jnp.dot(hi, ones_blk, preferred_element_type=F32)
            + jnp.dot(lo, ones_blk, preferred_element_type=F32))


def _softplus(x):
    return jnp.maximum(x, 0.0) + jnp.log1p(jnp.exp(-jnp.abs(x)))


def _block_ones(n, seg):
    idx = jnp.arange(n) // seg
    return (idx[:, None] == idx[None, :]).astype(BF16)


def _rwkv_prep_kernel(z_ref, st_ref, mu_ref, w0_ref, wup_ref, a0_ref, aup_ref, gup_ref, kk_ref, ka_ref,
                      rk_ref, ones_ref, r_out, w_out, k_out, v_out, kk_out, kka_out, bv_out, g_out,
                      car_scr, *, tm, seq_len):
    zr = z_ref[...]
    row = lax.broadcasted_iota(jnp.int32, zr.shape, 0)
    prev = pltpu.roll(zr, 1, 0)
    if seq_len >= tm:
        @pl.when((pl.program_id(0) * tm) % seq_len == 0)
        def _():
            car_scr[...] = st_ref[...]
        prev = jnp.where(row == 0, car_scr[SUBLANES - 1:SUBLANES], prev)
        car_scr[...] = zr[tm - SUBLANES:]
    else:
        prev = jnp.where(row % seq_len == 0, st_ref[...], prev)
    zs = zr + (prev - zr) * mu_ref[...]
    r = zs[:, :HALF]
    kr = zs[:, HALF:2 * HALF]
    vr = zs[:, 2 * HALF:3 * HALF]
    lora = zs[:, RW_LORA:RW_G]
    ones = ones_ref[...]
    w_log = -_softplus(-(w0_ref[...] + jnp.dot(jnp.tanh(lora).astype(BF16), wup_ref[...],
                                               preferred_element_type=F32))) - 0.5
    a = jax.nn.sigmoid(a0_ref[...] + jnp.dot(lora.astype(BF16), aup_ref[...], preferred_element_type=F32))
    g = jnp.dot(jax.nn.sigmoid(zs[:, RW_G:]).astype(BF16), gup_ref[...], preferred_element_type=F32)
    kk = kr * kk_ref[...]
    kk = kk * lax.rsqrt(jnp.maximum(_seg_sum(kk * kk, ones), 1e-24))
    k2 = kr * (1.0 + (a - 1.0) * ka_ref[...])
    r_out[...] = r
    w_out[...] = jnp.exp(-jnp.exp(w_log))
    k_out[...] = k2
    v_out[...] = vr
    kk_out[...] = kk
    kka_out[...] = kk * a
    bv_out[...] = _seg_sum(r * k2 * rk_ref[...], ones) * vr
    g_out[...] = g


def _rwkv_rec_kernel(r_ref, w_ref, k_ref, v_ref, kk_ref, kka_ref, bv_ref, g_ref, s0_ref, lng_ref, lnb_ref,
                     ones2_ref, ones_ref, y_ref, s_ref, yraw_scr, lhs_scr, ylhs_scr, *, bb, tb):
    @pl.when(pl.program_id(1) == 0)
    def _():
        s_ref[...] = s0_ref[...]

    pairs = RWKV_HEADS // 2
    n_tiles = bb * pairs
    rows_all = n_tiles * RWKV_HD
    shape = (RWKV_HD, LANES)
    eye2 = (lax.broadcasted_iota(jnp.int32, shape, 0)
            == lax.broadcasted_iota(jnp.int32, shape, 1) % RWKV_HD)
    ones2 = ones2_ref[...]
    ones1 = ones2[:LANES]

    def step(t, carry):
        def tile(idx):
            b, p = divmod(idx, pairs)
            cols = slice(p * LANES, (p + 1) * LANES)
            return b, p, (lambda ref: ref[t, b:b + 1, cols]), slice(idx * RWKV_HD, (idx + 1) * RWKV_HD)

        for idx in range(n_tiles):
            b, p, row, rows = tile(idx)
            hi, lo = _split2(s_ref[b, p] * row(kk_ref))
            lhs_scr[rows, :LANES] = hi
            lhs_scr[rows, LANES:] = lo
            v_row = row(v_ref)
            v_hi = v_row.astype(BF16).astype(F32)
            v_rows = slice(rows_all + idx * RWKV_HD, rows_all + (idx + 1) * RWKV_HD)
            lhs_scr[v_rows, :LANES] = jnp.where(eye2, v_hi, 0.0).astype(BF16)
            lhs_scr[v_rows, LANES:] = jnp.where(eye2, v_row - v_hi, 0.0).astype(BF16)
        sums = jnp.dot(lhs_scr[...], ones2, preferred_element_type=F32)
        for idx in range(n_tiles):
            b, p, row, rows = tile(idx)
            v_tile = sums[rows_all + idx * RWKV_HD:rows_all + (idx + 1) * RWKV_HD]
            s = s_ref[b, p] * row(w_ref) - sums[rows] * row(kka_ref) + v_tile * row(k_ref)
            s_ref[b, p] = s
            ylhs_scr[rows, :] = (s * row(r_ref)).astype(BF16)
        ysum = jnp.dot(ylhs_scr[...], ones1, preferred_element_type=F32)
        for idx in range(n_tiles):
            b, p, row, rows = tile(idx)
            yraw_scr[t, b:b + 1, p * LANES:(p + 1) * LANES] = jnp.sum(
                jnp.where(eye2, ysum[rows], 0.0), axis=0, keepdims=True)
        return carry

    lax.fori_loop(0, tb, step, 0)

    ones = ones_ref[...]
    y = yraw_scr[...].reshape(tb * bb, HALF)
    xc = y - _seg_sum(y, ones) * (1.0 / RWKV_HD)
    var = _seg_sum(xc * xc, ones) * (1.0 / RWKV_HD)
    yn = xc * lax.rsqrt(var + RWKV_LN_EPS) * lng_ref[...] + lnb_ref[...]
    out = (yn + bv_ref[...].reshape(tb * bb, HALF)) * g_ref[...].reshape(tb * bb, HALF)
    y_ref[...] = out.reshape(tb, bb, HALF)


def rwkv_mixer(z, s0, shift0, mu, w0, w_up, a0, a_up, g_up, k_k, k_a, r_k, ln_g, ln_b,
               *, batch, seq_len, tm, bb, tb):
    m_rows = batch * seq_len
    tm_shape = (seq_len, batch, HALF)
    if seq_len >= tm:
        tiles_per_seq = seq_len // tm
        st = jnp.pad(shift0[:, None], ((0, 0), (SUBLANES - 1, 0), (0, 0)))
        st_spec = pl.BlockSpec((None, SUBLANES, RWKV_COLS), lambda i: (i // tiles_per_seq, 0, 0))
        sem = "arbitrary"
        tok = pl.BlockSpec((tm, HALF), lambda i: (i % tiles_per_seq, i // tiles_per_seq))
        tok_shape = (seq_len, batch * HALF)
    else:
        st = jnp.pad(shift0[:, None], ((0, 0), (0, seq_len - 1), (0, 0))).reshape(m_rows, RWKV_COLS)
        st_spec = pl.BlockSpec((tm, RWKV_COLS), lambda i: (i, 0))
        sem = "parallel"
        tok = pl.BlockSpec((tm, HALF), lambda i: (i, 0))
        tok_shape = (m_rows, HALF)
    row = lambda a: a.reshape(1, -1)
    const = lambda i: (0, 0)
    vec = pl.BlockSpec((1, HALF), const)
    lora_rows = RWKV_W_LORA + RWKV_A_LORA
    w_up_p = jnp.pad(w_up, ((0, RWKV_A_LORA), (0, 0))).astype(BF16)
    a_up_p = jnp.pad(a_up, ((RWKV_W_LORA, 0), (0, 0))).astype(BF16)
    ones_half = _block_ones(HALF, RWKV_HD)
    prep = pl.pallas_call(
        functools.partial(_rwkv_prep_kernel, tm=tm, seq_len=seq_len),
        grid=(m_rows // tm,),
        in_specs=[pl.BlockSpec((tm, RWKV_COLS), lambda i: (i, Z0_RWKV // RWKV_COLS)),
                  st_spec,
                  pl.BlockSpec((1, RWKV_COLS), const),
                  vec, pl.BlockSpec((lora_rows, HALF), const),
                  vec, pl.BlockSpec((lora_rows, HALF), const),
                  pl.BlockSpec((RWKV_G_LORA, HALF), const),
                  vec, vec, vec,
                  pl.BlockSpec((HALF, HALF), const)],
        out_specs=[tok] * 8,
        out_shape=[jax.ShapeDtypeStruct(tok_shape, F32)] * 8,
        scratch_shapes=[pltpu.VMEM((SUBLANES, RWKV_COLS), F32)],
        compiler_params=_cparams(sem),
        name="rwkv_prep",
    )(z, st, row(mu), row(w0), w_up_p, row(a0), a_up_p, g_up.astype(BF16), row(k_k), row(k_a), row(r_k),
      ones_half)
    if seq_len >= tm:
        seqs = [a.reshape(tm_shape) for a in prep]
    else:
        seqs = [jnp.swapaxes(a.reshape(batch, seq_len, HALF), 0, 1) for a in prep]

    pairs = RWKV_HEADS // 2
    s_pairs = (s0.reshape(batch, pairs, 2, RWKV_HD, RWKV_HD).transpose(0, 1, 3, 2, 4)
               .reshape(batch, pairs, RWKV_HD, LANES))
    blk = pl.BlockSpec((tb, bb, HALF), lambda b, t: (t, b, 0))
    s_spec = pl.BlockSpec((bb, pairs, RWKV_HD, LANES), lambda b, t: (b, 0, 0, 0))
    const2 = lambda b, t: (0, 0)
    y, s_new = pl.pallas_call(
        functools.partial(_rwkv_rec_kernel, bb=bb, tb=tb),
        grid=(batch // bb, seq_len // tb),
        in_specs=[blk] * 8 + [s_spec,
                              pl.BlockSpec((1, HALF), const2), pl.BlockSpec((1, HALF), const2),
                              pl.BlockSpec((2 * LANES, LANES), const2),
                              pl.BlockSpec((HALF, HALF), const2)],
        out_specs=[blk, s_spec],
        out_shape=[jax.ShapeDtypeStruct(tm_shape, F32),
                   jax.ShapeDtypeStruct((batch, pairs, RWKV_HD, LANES), F32)],
        scratch_shapes=[pltpu.VMEM((tb, bb, HALF), F32),
                        pltpu.VMEM((2 * bb * pairs * RWKV_HD, 2 * LANES), BF16),
                        pltpu.VMEM((bb * pairs * RWKV_HD, LANES), BF16)],
        compiler_params=_cparams("parallel", "arbitrary"),
        name="rwkv_rec",
    )(*seqs, s_pairs, row(ln_g), row(ln_b),
      jnp.concatenate([_block_ones(LANES, RWKV_HD)] * 2, axis=0), ones_half)
    s_new = (s_new.reshape(batch, pairs, RWKV_HD, 2, RWKV_HD).transpose(0, 1, 3, 2, 4)
             .reshape(batch, RWKV_HEADS, RWKV_HD, RWKV_HD))
    return y, s_new


GLA_SUB = 16


def _gla_kernel(q_ref, k_ref, v_ref, g_ref, ad_ref, aup_ref, ab_ref, lng_ref, s0_ref,
                y_ref, s_ref, *, chunk, n_chunks):
    L = chunk
    sub = min(GLA_SUB, L)

    @pl.when(pl.program_id(1) == 0)
    def _():
        s_ref[...] = s0_ref[...]

    lane = lax.broadcasted_iota(jnp.int32, (1, LANES), 1)
    head_mask = [(lane < GLA_DK).astype(F32), (lane >= GLA_DK).astype(F32)]
    tril = (lax.broadcasted_iota(jnp.int32, (L, L), 1)
            <= lax.broadcasted_iota(jnp.int32, (L, L), 0)).astype(F32)
    eye = (lax.broadcasted_iota(jnp.int32, (LANES, LANES), 0)
           == lax.broadcasted_iota(jnp.int32, (LANES, LANES), 1))

    def chunk_body(c, carry):
        rows = pl.ds(pl.multiple_of(c * L, L), L)
        pre = jnp.dot(ad_ref[rows, :].astype(BF16), aup_ref[...], preferred_element_type=F32) + ab_ref[...]
        la = _log_sigmoid(pre) * (1.0 / GLA_TAU)
        bc_all = jnp.dot(tril, la, preferred_element_type=F32, precision=lax.Precision.HIGHEST)
        for p in range(GLA_HEADS // 2):
            kcols = slice(p * LANES, (p + 1) * LANES)
            qp = q_ref[rows, kcols] * (GLA_DK ** -0.5)
            kp = k_ref[rows, kcols]
            bc = bc_all[:, kcols]
            v_pair = v_ref[rows, 2 * p * GLA_DV:2 * (p + 1) * GLA_DV].astype(BF16)
            s_old = s_ref[p]
            stack = lambda x: jnp.concatenate([x * head_mask[0], x * head_mask[1]], axis=0).astype(BF16)

            inter = jnp.dot(stack(qp * jnp.exp(bc)), s_old.astype(BF16), preferred_element_type=F32)
            intra = []
            for i in range(L // sub):
                lo, hi = i * sub, (i + 1) * sub
                c_i = bc[lo - 1:lo] if i > 0 else jnp.zeros((1, LANES), F32)
                qe = qp[lo:hi] * jnp.exp(bc[lo:hi] - c_i)
                ke = kp[:hi] * jnp.exp(c_i - bc[:hi])
                att = _dot_nt(stack(qe), ke.astype(BF16))
                t_idx = lo + lax.broadcasted_iota(jnp.int32, (2 * sub, hi), 0) % sub
                s_idx = lax.broadcasted_iota(jnp.int32, (2 * sub, hi), 1)
                att = jnp.where(s_idx <= t_idx, att, 0.0)
                intra.append(jnp.dot(att.astype(BF16), v_pair[:hi], preferred_element_type=F32))
            for j in range(2):
                h = 2 * p + j
                vcols = slice(h * GLA_DV, (h + 1) * GLA_DV)
                o = inter[j * L:(j + 1) * L] + jnp.concatenate(
                    [blk[j * sub:(j + 1) * sub, j * GLA_DV:(j + 1) * GLA_DV] for blk in intra], axis=0)
                on = o * lax.rsqrt(jnp.mean(o * o, -1, keepdims=True) + EPS) * lng_ref[:, vcols]
                gate = g_ref[rows, vcols]
                y_ref[rows, vcols] = on * (gate * jax.nn.sigmoid(gate))

            bl = bc[L - 1:L]
            kd = kp * jnp.exp(bl - bc)
            v_stack = jnp.concatenate([v_pair[:, :GLA_DV], v_pair[:, GLA_DV:]], axis=0)
            dec_col = jnp.sum(jnp.where(eye, jnp.exp(bl), 0.0), axis=1, keepdims=True)
            s_ref[p] = dec_col * s_old + _dot_tn(stack(kd), v_stack)
        return carry

    lax.fori_loop(0, n_chunks, chunk_body, 0)


def gla_mixer(z, s0, a_up, a_b, ln_g, *, batch, seq_len, tb):
    m_rows = batch * seq_len
    L = math.gcd(seq_len, CHUNK)
    n_tb = seq_len // tb
    pairs = GLA_HEADS // 2
    zblk = lambda width, off: pl.BlockSpec((tb, width), lambda b, t: (b * n_tb + t, off // width))
    s_spec = pl.BlockSpec((None, pairs, 2 * GLA_DK, GLA_DV), lambda b, t: (b, 0, 0, 0))
    const = lambda b, t: (0, 0)
    a_up_p = jnp.pad(a_up, ((0, LANES - GLA_GATE_RANK), (0, 0))).astype(BF16)
    y, s_new = pl.pallas_call(
        functools.partial(_gla_kernel, chunk=L, n_chunks=tb // L),
        grid=(batch, n_tb),
        in_specs=[zblk(GLA_QK, Z1_Q), zblk(GLA_QK, Z1_K), zblk(HALF, Z1_V), zblk(HALF, Z1_G),
                  zblk(LANES, Z1_AD),
                  pl.BlockSpec((LANES, GLA_QK), const), pl.BlockSpec((1, GLA_QK), const),
                  pl.BlockSpec((1, HALF), const), s_spec],
        out_specs=[pl.BlockSpec((tb, HALF), lambda b, t: (b * n_tb + t, 0)), s_spec],
        out_shape=[jax.ShapeDtypeStruct((m_rows, HALF), F32),
                   jax.ShapeDtypeStruct((batch, pairs, 2 * GLA_DK, GLA_DV), F32)],
        compiler_params=_cparams("parallel", "arbitrary"),
        name="gla",
    )(z, z, z, z, z, a_up_p, a_b.reshape(1, GLA_QK), ln_g.reshape(1, HALF),
      s0.reshape(batch, pairs, 2 * GLA_DK, GLA_DV))
    return y, s_new.reshape(batch, GLA_HEADS, GLA_DK, GLA_DV)


S5_KT = LANES // S5_GROUP


def _s5_param_kernel(lre_ref, lim_ref, step_ref, bre_ref, bim_ref, bbre_ref, bbim_ref, pre_ref, pim_ref):
    lre, lim = lre_ref[...], lim_ref[...]
    step = jnp.exp(step_ref[...])
    mag = jnp.exp(lre * step)
    bar_re = mag * jnp.cos(lim * step)
    bar_im = mag * jnp.sin(lim * step)
    inv = 1.0 / (lre * lre + lim * lim)
    cre = ((bar_re - 1.0) * lre + bar_im * lim) * inv
    cim = (bar_im * lre - (bar_re - 1.0) * lim) * inv
    for g in range(S5_GROUPS):
        cr, ci = cre[g:g + 1], cim[g:g + 1]
        bbre_ref[g] = cr * bre_ref[g] - ci * bim_ref[g]
        bbim_ref[g] = cr * bim_ref[g] + ci * bre_ref[g]
    p_re, p_im = bar_re, bar_im
    for k in range(SUBLANES):
        pre_ref[k] = p_re
        pim_ref[k] = p_im
        p_re, p_im = p_re * bar_re - p_im * bar_im, p_re * bar_im + p_im * bar_re


def _s5_kernel(u_ref, wbre_ref, wbim_ref, pre_ref, pim_ref, x0re_ref, x0im_ref, wcre_ref, wcim_ref,
               d_ref, wglu_ref, y_ref, xre_ref, xim_ref, sre_scr, sim_scr, *, tt):
    @pl.when(pl.program_id(1) == 0)
    def _():
        xre_ref[...] = x0re_ref[...]
        xim_ref[...] = x0im_ref[...]

    u = u_ref[...]
    ub = u.astype(BF16)
    nblk = S5_CH // LANES
    wide = S5_STATE // nblk
    row = lax.broadcasted_iota(jnp.int32, (tt, wide), 0) % SUBLANES
    for kt in range(nblk):
        cols = slice(kt * wide, (kt + 1) * wide)
        ukt = ub[:, kt * LANES:(kt + 1) * LANES]
        xr = jnp.dot(ukt, wbre_ref[kt], preferred_element_type=F32)
        xi = jnp.dot(ukt, wbim_ref[kt], preferred_element_type=F32)
        for lvl in range(3):
            s = 1 << lvl
            pr, pi = pre_ref[s - 1:s, cols], pim_ref[s - 1:s, cols]
            sr, si = pltpu.roll(xr, s, 0), pltpu.roll(xi, s, 0)
            keep = row >= s
            xr, xi = (xr + jnp.where(keep, pr * sr - pi * si, 0.0),
                      xi + jnp.where(keep, pr * si + pi * sr, 0.0))
        sre_scr[:, cols] = xr
        sim_scr[:, cols] = xi

    pw_re, pw_im = pre_ref[...], pim_ref[...]

    def group_body(j, carry):
        cr, ci = carry
        rows = pl.ds(pl.multiple_of(j * SUBLANES, SUBLANES), SUBLANES)
        gr = sre_scr[rows, :] + (pw_re * cr - pw_im * ci)
        gi = sim_scr[rows, :] + (pw_re * ci + pw_im * cr)
        sre_scr[rows, :] = gr
        sim_scr[rows, :] = gi
        return gr[SUBLANES - 1:], gi[SUBLANES - 1:]

    cr, ci = lax.fori_loop(0, tt // SUBLANES, group_body, (xre_ref[...], xim_ref[...]))
    xre_ref[...] = cr
    xim_ref[...] = ci

    parts = []
    for nt in range(nblk):
        cols = slice(nt * wide, (nt + 1) * wide)
        parts.append(jnp.dot(sre_scr[:, cols].astype(BF16), wcre_ref[nt], preferred_element_type=F32)
                     - jnp.dot(sim_scr[:, cols].astype(BF16), wcim_ref[nt], preferred_element_type=F32))
    y = jnp.concatenate(parts, axis=1) + d_ref[...] * u
    ys = y * (0.5 * (1.0 + jnp.tanh(math.sqrt(2.0 / math.pi) * (y + 0.044715 * (y * y * y)))))
    y_ref[...] = ys * jax.nn.sigmoid(jnp.dot(ys.astype(BF16), wglu_ref[...], preferred_element_type=F32))


def s5_mixer(z, x0_re, x0_im, lam_re, lam_im, log_step, b_re, b_im, c_re, c_im, d_skip, w_glu,
             *, batch, seq_len, tt):
    m_rows = batch * seq_len
    gpc = (S5_GROUPS, S5_GROUP, S5_P)
    bb_re, bb_im, p_re, p_im = pl.pallas_call(
        _s5_param_kernel,
        out_shape=[jax.ShapeDtypeStruct(gpc, F32), jax.ShapeDtypeStruct(gpc, F32),
                   jax.ShapeDtypeStruct((SUBLANES, S5_GROUPS, S5_P), F32),
                   jax.ShapeDtypeStruct((SUBLANES, S5_GROUPS, S5_P), F32)],
        name="s5_params",
    )(lam_re, lam_im, log_step.reshape(S5_GROUPS, 1), jnp.swapaxes(b_re, 1, 2), jnp.swapaxes(b_im, 1, 2))

    nblk = S5_CH // LANES
    eye = jnp.eye(S5_KT, dtype=F32)

    def in_blocks(bb):
        w = bb.reshape(nblk, S5_KT, S5_GROUP, 1, S5_P) * eye[None, :, None, :, None]
        return w.reshape(nblk, LANES, S5_KT * S5_P).astype(BF16)

    def out_blocks(c):
        w = jnp.swapaxes(c, 1, 2).reshape(nblk, S5_KT, S5_P, 1, S5_GROUP) * eye[None, :, None, :, None]
        return w.reshape(nblk, S5_KT * S5_P, LANES).astype(BF16)

    n_tt = seq_len // tt
    const2 = lambda b, t: (0, 0)
    const3 = lambda b, t: (0, 0, 0)
    st_spec = pl.BlockSpec((None, 1, S5_STATE), lambda b, t: (b, 0, 0))
    wb_spec = pl.BlockSpec((nblk, LANES, S5_KT * S5_P), const3)
    wc_spec = pl.BlockSpec((nblk, S5_KT * S5_P, LANES), const3)
    pw_spec = pl.BlockSpec((SUBLANES, S5_STATE), const2)
    y, x_re, x_im = pl.pallas_call(
        functools.partial(_s5_kernel, tt=tt),
        grid=(batch, n_tt),
        in_specs=[pl.BlockSpec((tt, S5_CH), lambda b, t: (b * n_tt + t, Z1_U // S5_CH)),
                  wb_spec, wb_spec, pw_spec, pw_spec, st_spec, st_spec, wc_spec, wc_spec,
                  pl.BlockSpec((1, S5_CH), const2), pl.BlockSpec((S5_CH, S5_CH), const2)],
        out_specs=[pl.BlockSpec((tt, S5_CH), lambda b, t: (b * n_tt + t, 0)), st_spec, st_spec],
        out_shape=[jax.ShapeDtypeStruct((m_rows, S5_CH), F32),
                   jax.ShapeDtypeStruct((batch, 1, S5_STATE), F32),
                   jax.ShapeDtypeStruct((batch, 1, S5_STATE), F32)],
        scratch_shapes=[pltpu.VMEM((tt, S5_STATE), F32), pltpu.VMEM((tt, S5_STATE), F32)],
        compiler_params=_cparams("parallel", "arbitrary"),
        name="s5",
    )(z, in_blocks(bb_re), in_blocks(bb_im),
      p_re.reshape(SUBLANES, S5_STATE), p_im.reshape(SUBLANES, S5_STATE),
      x0_re.reshape(batch, 1, S5_STATE), x0_im.reshape(batch, 1, S5_STATE),
      out_blocks(c_re), out_blocks(c_im), d_skip.reshape(1, S5_CH), w_glu.astype(BF16))
    shape = (batch, S5_GROUPS, S5_P)
    return y, x_re.reshape(shape), x_im.reshape(shape)


def _pack_even_w_in(w_in):
    d = w_in.shape[0]
    m_cols = 4 * HALF + 2 * MLSTM_HEADS
    parts = [w_in[:, m_cols:],
             w_in[:, 4 * HALF:m_cols], jnp.zeros((d, Z0_Q - Z0_GATE - 2 * MLSTM_HEADS), w_in.dtype),
             w_in[:, :4 * HALF]]
    return jnp.concatenate(parts, axis=1).astype(BF16)


def _pack_odd_w_in(w_in):
    d = w_in.shape[0]
    gla_main = 2 * GLA_QK + 2 * HALF
    gla_cols = gla_main + GLA_GATE_RANK
    parts = [w_in[:, :gla_main], w_in[:, gla_cols:], w_in[:, gla_main:gla_cols]]
    w = jnp.concatenate(parts, axis=1)
    return jnp.pad(w, ((0, 0), (0, Z1_COLS - w.shape[1]))).astype(BF16)


def _trunk(x, st, prm, *, batch, seq_len):
    (st_C, st_n, st_m, st_S, st_shift, st_gla, st_re, st_im, st_conv) = st
    (norm_mix, norm_ffn, norm_final,
     e_w_in, e_b_if, e_mu, e_w0, e_w_up, e_a0, e_a_up, e_g_up, e_k_k, e_k_a, e_r_k, e_ln_m_g, e_ln_r_g,
     e_ln_r_b, e_w_out,
     o_w_in, o_a_up, o_a_b, o_ln_g, o_lam_re, o_lam_im, o_log_step, o_b_re, o_b_im, o_c_re, o_c_im, o_d,
     o_w_glu, o_w_out,
     f_w_up, f_conv_w, f_conv_b, f_w_down) = prm
    assert len(e_w_in) == 1 and len(o_w_in) == 1 and len(f_w_up) == 2, "wired for depth 2"
    long_seq = seq_len >= 512
    assert long_seq or seq_len == SUBLANES
    m_rows = batch * seq_len
    tiles = dict(
        tm_proj=min(m_rows, 1024),
        tm_tok=512 if long_seq else 256,
        tm_ffn=512 if long_seq else 128,
        tb=512 if long_seq else seq_len,
        tb_rwkv=64 if long_seq else seq_len,
        tt_s5=256 if long_seq else seq_len,
    )
    x = x.reshape(m_rows, D_MODEL)
    ffn = functools.partial(conv_ffn, batch=batch, seq_len=seq_len, tm=tiles["tm_ffn"])

    z0 = norm_matmul(x, norm_mix[0], _pack_even_w_in(e_w_in[0]), tm=tiles["tm_proj"], tn=512)
    y_m, c_new, n_new, m_new = mlstm_mixer(z0, e_b_if[0], st_C[0], st_n[0], st_m[0], e_ln_m_g[0],
                                           batch=batch, seq_len=seq_len, tb=tiles["tb"])
    y_r, s_new = rwkv_mixer(z0, st_S[0], st_shift[0], e_mu[0], e_w0[0], e_w_up[0], e_a0[0], e_a_up[0],
                            e_g_up[0], e_k_k[0], e_k_a[0], e_r_k[0], e_ln_r_g[0], e_ln_r_b[0],
                            batch=batch, seq_len=seq_len, tm=tiles["tm_tok"], bb=SUBLANES,
                            tb=tiles["tb_rwkv"])
    if not long_seq:
        y_r = jnp.swapaxes(y_r, 0, 1).reshape(m_rows, HALF)
    shift_new = z0.reshape(batch, seq_len, Z0_COLS)[:, -1, Z0_RWKV:Z0_RWKV + RWKV_COLS]
    w_out = e_w_out[0].astype(BF16)
    x = out_proj(x, y_m, y_r, w_out[:HALF], w_out[HALF:], tm=tiles["tm_tok"])
    x, conv_new0 = ffn(x, norm_ffn[0], f_w_up[0].astype(BF16), f_conv_w[0], f_conv_b[0],
                       f_w_down[0].astype(BF16), st_conv[0], norm_final, final_norm=False)

    z1 = norm_matmul(x, norm_mix[1], _pack_odd_w_in(o_w_in[0]), tm=tiles["tm_proj"], tn=512)
    y_g, gla_new = gla_mixer(z1, st_gla[0], o_a_up[0], o_a_b[0], o_ln_g[0],
                             batch=batch, seq_len=seq_len, tb=tiles["tb"])
    y_s, re_new, im_new = s5_mixer(z1, st_re[0], st_im[0], o_lam_re[0], o_lam_im[0], o_log_step[0],
                                   o_b_re[0], o_b_im[0], o_c_re[0], o_c_im[0], o_d[0], o_w_glu[0],
                                   batch=batch, seq_len=seq_len, tt=tiles["tt_s5"])
    w_out = o_w_out[0].astype(BF16)
    x = out_proj(x, y_g, y_s, w_out[:HALF], w_out[HALF:], tm=tiles["tm_tok"])
    x, conv_new1 = ffn(x, norm_ffn[1], f_w_up[1].astype(BF16), f_conv_w[1], f_conv_b[1],
                       f_w_down[1].astype(BF16), st_conv[1], norm_final, final_norm=True)
    new_state = (c_new[None], n_new[None], m_new[None], s_new[None], shift_new[None],
                 gla_new[None], re_new[None], im_new[None], jnp.stack([conv_new0, conv_new1]))
    return x.reshape(batch, seq_len, D_MODEL), new_state


def kernel(x_prompt, x_sample, state_mlstm_C, state_mlstm_n, state_mlstm_m, state_rwkv_S, state_rwkv_shift,
           state_gla_S, state_s5_re, state_s5_im, state_ffn_conv,
           norm_mix, norm_ffn, norm_final,
           e_w_in, e_b_if, e_mu, e_w0, e_w_up, e_a0, e_a_up, e_g_up, e_k_k, e_k_a, e_r_k,
           e_ln_m_g, e_ln_r_g, e_ln_r_b, e_w_out,
           o_w_in, o_a_up, o_a_b, o_ln_g, o_lam_re, o_lam_im, o_log_step, o_b_re, o_b_im, o_c_re, o_c_im,
           o_d, o_w_glu, o_w_out,
           f_w_up, f_conv_w, f_conv_b, f_w_down):
    params = (norm_mix, norm_ffn, norm_final,
              e_w_in, e_b_if, e_mu, e_w0, e_w_up, e_a0, e_a_up, e_g_up, e_k_k, e_k_a, e_r_k,
              e_ln_m_g, e_ln_r_g, e_ln_r_b, e_w_out,
              o_w_in, o_a_up, o_a_b, o_ln_g, o_lam_re, o_lam_im, o_log_step, o_b_re, o_b_im, o_c_re, o_c_im,
              o_d, o_w_glu, o_w_out,
              f_w_up, f_conv_w, f_conv_b, f_w_down)
    bp, tp, _ = x_prompt.shape
    bs, ts, _ = x_sample.shape
    n_even, n_odd, depth = state_mlstm_C.shape[0], state_gla_S.shape[0], state_ffn_conv.shape[0]
    init = (jnp.zeros((n_even, bp, MLSTM_HEADS, MLSTM_HD, MLSTM_HD), F32),
            jnp.zeros((n_even, bp, MLSTM_HEADS, MLSTM_HD), F32),
            jnp.full((n_even, bp, MLSTM_HEADS), M_INIT, F32),
            jnp.zeros((n_even, bp, RWKV_HEADS, RWKV_HD, RWKV_HD), F32),
            jnp.zeros((n_even, bp, RWKV_COLS), F32),
            jnp.zeros((n_odd, bp, GLA_HEADS, GLA_DK, GLA_DV), F32),
            jnp.zeros((n_odd, bp, S5_GROUPS, S5_P), F32),
            jnp.zeros((n_odd, bp, S5_GROUPS, S5_P), F32),
            jnp.zeros((depth, bp, CONV_W - 1, 2 * D_FF), F32))
    past = (state_mlstm_C, state_mlstm_n, state_mlstm_m, state_rwkv_S, state_rwkv_shift,
            state_gla_S, state_s5_re, state_s5_im, state_ffn_conv)
    y_prompt, p_state = _trunk(x_prompt, init, params, batch=bp, seq_len=tp)
    y_sample, s_state = _trunk(x_sample, past, params, batch=bs, seq_len=ts)
    return (y_prompt, y_sample) + tuple(p_state) + tuple(s_state)
```

```python
import functools
import math

import jax
import jax.numpy as jnp
from jax import lax
from jax.experimental import pallas as pl
from jax.experimental.pallas import tpu as pltpu

F32 = jnp.float32
BF16 = jnp.bfloat16

D_MODEL = 1024
HALF = D_MODEL // 2
MLSTM_HEADS = 4
MLSTM_HD = HALF // MLSTM_HEADS
RWKV_HD = 64
RWKV_HEADS = HALF // RWKV_HD
RWKV_W_LORA = 64
RWKV_A_LORA = 64
RWKV_G_LORA = 128
RWKV_COLS = 3 * HALF + RWKV_W_LORA + RWKV_A_LORA + RWKV_G_LORA
GLA_HEADS = 4
GLA_DK = 64
GLA_DV = 128
GLA_QK = GLA_HEADS * GLA_DK
GLA_GATE_RANK = 16
GLA_TAU = 16.0
S5_CH = HALF
S5_GROUP = 16
S5_GROUPS = S5_CH // S5_GROUP
S5_P = 64
S5_STATE = S5_GROUPS * S5_P
D_FF = 2816
CONV_W = 3
CHUNK = 64
EPS = 1e-6
RWKV_LN_EPS = 64e-5
M_INIT = -1e30

LANES = 128
SUBLANES = 8
VMEM_LIMIT_BYTES = 56 * 1024 * 1024

Z0_RWKV = 0
Z0_GATE = RWKV_COLS
Z0_Q = Z0_GATE + 2 * LANES
Z0_K, Z0_V, Z0_O = Z0_Q + HALF, Z0_Q + 2 * HALF, Z0_Q + 3 * HALF
Z0_COLS = Z0_Q + 4 * HALF
RW_LORA = 3 * HALF
RW_G = RW_LORA + RWKV_W_LORA + RWKV_A_LORA
Z1_Q, Z1_K, Z1_V, Z1_G, Z1_U = 0, GLA_QK, 2 * GLA_QK, 2 * GLA_QK + HALF, 2 * GLA_QK + 2 * HALF
Z1_AD = Z1_U + S5_CH
Z1_COLS = 2560


def _cparams(*sem):
    return pltpu.CompilerParams(dimension_semantics=sem, vmem_limit_bytes=VMEM_LIMIT_BYTES)


def _rms(x, g):
    return x * lax.rsqrt(jnp.mean(x * x, -1, keepdims=True) + EPS) * g


def _norm_matmul_kernel(x_ref, g_ref, w_ref, o_ref, h_scr):
    @pl.when(pl.program_id(1) == 0)
    def _():
        h_scr[...] = _rms(x_ref[...], g_ref[...]).astype(BF16)

    o_ref[...] = jnp.dot(h_scr[...], w_ref[...], preferred_element_type=F32)


def norm_matmul(x, g, w, *, tm, tn, time_major_seq_len=None):
    m, d = x.shape
    n = w.shape[1]
    if time_major_seq_len is None:
        out_shape = (m, n)
        out_spec = pl.BlockSpec((tm, tn), lambda i, j: (i, j))
    else:
        tiles_per_seq = time_major_seq_len // tm
        n_col = n // tn
        out_shape = (time_major_seq_len, (m // time_major_seq_len) * n)
        out_spec = pl.BlockSpec((tm, tn), lambda i, j: (i % tiles_per_seq, (i // tiles_per_seq) * n_col + j))
    return pl.pallas_call(
        _norm_matmul_kernel,
        grid=(m // tm, n // tn),
        in_specs=[pl.BlockSpec((tm, d), lambda i, j: (i, 0)),
                  pl.BlockSpec((1, d), lambda i, j: (0, 0)),
                  pl.BlockSpec((d, tn), lambda i, j: (0, j))],
        out_specs=out_spec,
        out_shape=jax.ShapeDtypeStruct(out_shape, F32),
        scratch_shapes=[pltpu.VMEM((tm, d), BF16)],
        compiler_params=_cparams("parallel", "arbitrary"),
        name="norm_matmul",
    )(x, g.reshape(1, d), w)


def _out_proj_kernel(x_ref, ya_ref, yb_ref, wa_ref, wb_ref, o_ref):
    acc = jnp.dot(ya_ref[...].astype(BF16), wa_ref[...], preferred_element_type=F32)
    acc += jnp.dot(yb_ref[...].astype(BF16), wb_ref[...], preferred_element_type=F32)
    o_ref[...] = x_ref[...] + acc


def out_proj(x, ya, yb, wa, wb, *, tm):
    m, d = x.shape
    k = ya.shape[1]
    if yb.ndim == 3:
        tiles_per_seq = yb.shape[0] // tm
        yb = yb.reshape(yb.shape[0], -1)
        yb_spec = pl.BlockSpec((tm, k), lambda i: (i % tiles_per_seq, i // tiles_per_seq))
    else:
        yb_spec = pl.BlockSpec((tm, k), lambda i: (i, 0))
    return pl.pallas_call(
        _out_proj_kernel,
        grid=(m // tm,),
        in_specs=[pl.BlockSpec((tm, d), lambda i: (i, 0)),
                  pl.BlockSpec((tm, k), lambda i: (i, 0)),
                  yb_spec,
                  pl.BlockSpec((k, d), lambda i: (0, 0)),
                  pl.BlockSpec((k, d), lambda i: (0, 0))],
        out_specs=pl.BlockSpec((tm, d), lambda i: (i, 0)),
        out_shape=jax.ShapeDtypeStruct((m, d), F32),
        compiler_params=_cparams("parallel"),
        name="out_proj",
    )(x, ya, yb, wa, wb)


FFN_SUB = 256


def _ffn_kernel(x_ref, g_ref, wup_ref, cw_ref, cb_ref, wdn_ref, st_ref, gf_ref,
                o_ref, tail_ref, h_scr, act_scr, car_scr, *, tm, seq_len, final_norm):
    long_seq = seq_len >= tm
    i = pl.program_id(0)
    h_scr[...] = _rms(x_ref[...], g_ref[...]).astype(BF16)
    row = lax.broadcasted_iota(jnp.int32, (tm, FFN_SUB), 0)
    if long_seq:
        @pl.when((i * tm) % seq_len == 0)
        def _():
            car_scr[...] = st_ref[...]
    else:
        t_in_seq = row % seq_len

    row8 = lax.broadcasted_iota(jnp.int32, (SUBLANES, FFN_SUB), 0)

    def conv_part(col0):
        u = jnp.dot(h_scr[...], wup_ref[:, col0:col0 + FFN_SUB], preferred_element_type=F32)
        p1 = pltpu.roll(u, 1, 0)
        p2 = pltpu.roll(u, 2, 0)
        if long_seq:
            halo = car_scr[:, col0:col0 + FFN_SUB]
            h6 = halo[SUBLANES - 2:SUBLANES - 1]
            h7 = halo[SUBLANES - 1:SUBLANES]
            p1 = jnp.concatenate([jnp.where(row8 == 0, h7, p1[:SUBLANES]), p1[SUBLANES:]], axis=0)
            p2 = jnp.concatenate([jnp.where(row8 == 0, h6, jnp.where(row8 == 1, h7, p2[:SUBLANES])),
                                  p2[SUBLANES:]], axis=0)
            car_scr[:, col0:col0 + FFN_SUB] = u[tm - SUBLANES:]
            tail_ref[:, col0:col0 + FFN_SUB] = u[tm - SUBLANES:]
        else:
            e = st_ref[:, col0:col0 + FFN_SUB]
            p1 = jnp.where(t_in_seq == 0, pltpu.roll(e, tm - 1, 0), p1)
            p2 = jnp.where(t_in_seq < 2, e, p2)
            tail_ref[:, col0:col0 + FFN_SUB] = u
        cw = cw_ref[:, col0:col0 + FFN_SUB]
        return (cb_ref[:, col0:col0 + FFN_SUB]
                + (cw[0:1] * p2 + cw[1:2] * p1 + cw[2:3] * u))

    for c in range(D_FF // FFN_SUB):
        val = conv_part(c * FFN_SUB)
        gate = conv_part(D_FF + c * FFN_SUB)
        act_scr[:, c * FFN_SUB:(c + 1) * FFN_SUB] = (val * (gate * jax.nn.sigmoid(gate))).astype(BF16)

    y = x_ref[...] + jnp.dot(act_scr[...], wdn_ref[...], preferred_element_type=F32)
    if final_norm:
        y = _rms(y, gf_ref[...])
    o_ref[...] = y


def conv_ffn(x, g, w_up, conv_w, conv_b, w_down, conv0, gf, *, batch, seq_len, tm, final_norm):
    m, d = x.shape
    f2 = 2 * D_FF
    long_seq = seq_len >= tm
    if long_seq:
        assert seq_len % tm == 0
        tiles_per_seq = seq_len // tm
        st = jnp.pad(conv0, ((0, 0), (SUBLANES - (CONV_W - 1), 0), (0, 0)))
        st_spec = pl.BlockSpec((None, SUBLANES, f2), lambda i: (i // tiles_per_seq, 0, 0))
        tail_shape = jax.ShapeDtypeStruct((batch, SUBLANES, f2), F32)
        tail_spec = pl.BlockSpec((None, SUBLANES, f2), lambda i: (i // tiles_per_seq, 0, 0))
        sem = "arbitrary"
    else:
        assert seq_len == SUBLANES and tm % seq_len == 0
        st = jnp.pad(conv0, ((0, 0), (0, seq_len - (CONV_W - 1)), (0, 0))).reshape(m, f2)
        st_spec = pl.BlockSpec((tm, f2), lambda i: (i, 0))
        tail_shape = jax.ShapeDtypeStruct((m, f2), F32)
        tail_spec = pl.BlockSpec((tm, f2), lambda i: (i, 0))
        sem = "parallel"
    const = lambda i: (0, 0)
    out, tail = pl.pallas_call(
        functools.partial(_ffn_kernel, tm=tm, seq_len=seq_len, final_norm=final_norm),
        grid=(m // tm,),
        in_specs=[pl.BlockSpec((tm, d), lambda i: (i, 0)),
                  pl.BlockSpec((1, d), const),
                  pl.BlockSpec((d, f2), const, pipeline_mode=pl.Buffered(1)),
                  pl.BlockSpec((CONV_W, f2), const),
                  pl.BlockSpec((1, f2), const),
                  pl.BlockSpec((D_FF, d), const, pipeline_mode=pl.Buffered(1)),
                  st_spec,
                  pl.BlockSpec((1, d), const)],
        out_specs=[pl.BlockSpec((tm, d), lambda i: (i, 0)), tail_spec],
        out_shape=[jax.ShapeDtypeStruct((m, d), F32), tail_shape],
        scratch_shapes=[pltpu.VMEM((tm, d), BF16),
                        pltpu.VMEM((tm, D_FF), BF16),
                        pltpu.VMEM((SUBLANES, f2), F32)],
        compiler_params=_cparams(sem),
        name="conv_ffn",
    )(x, g.reshape(1, d), w_up, conv_w, conv_b.reshape(1, f2), w_down, st, gf.reshape(1, d))
    new_state = tail.reshape(batch, -1, f2)[:, -(CONV_W - 1):]
    return out, new_state


def _log_sigmoid(x):
    return jnp.minimum(x, 0.0) - jnp.log1p(jnp.exp(-jnp.abs(x)))


def _dot_nt(a, b):
    return lax.dot_general(a, b, (((1,), (1,)), ((), ())), preferred_element_type=F32)


def _dot_tn(a, b):
    return lax.dot_general(a, b, (((0,), (0,)), ((), ())), preferred_element_type=F32)


def _mlstm_kernel(bif_ref, q_ref, k_ref, v_ref, og_ref, gz_ref, gt_ref, c0_ref, n0_ref, m0_ref,
                  lng_ref, y_ref, c_ref, n_ref, m_ref, *, chunk, n_chunks, bb):
    L = chunk

    @pl.when(pl.program_id(1) == 0)
    def _():
        c_ref[...] = c0_ref[...]
        n_ref[...] = n0_ref[...]
        m_ref[...] = m0_ref[...]

    t_idx = lax.broadcasted_iota(jnp.int32, (L, L), 0)
    s_idx = lax.broadcasted_iota(jnp.int32, (L, L), 1)
    causal = s_idx <= t_idx
    scale = MLSTM_HD ** -0.5

    def compute(items):
        ph = []
        for j, c in items:
            start = (j * n_chunks + c) * L
            rows = pl.ds(start if isinstance(start, int) else pl.multiple_of(start, L), L)
            for h in range(MLSTM_HEADS):
                ph.append(gate_phase(j, c, h, rows))
        for d in ph:
            d["qk_raw"] = _dot_nt(d["qb"], d["kb"])
            d["qc"] = _dot_nt(d["qb"], d["c_old"].astype(BF16))
            d["c_upd"] = _dot_tn((d["v"] * d["wl_col"]).astype(BF16), d["kb"])
        for d in ph:
            d["qk"] = d["qk_raw"] * d["w"]
            d["pv"] = jnp.dot(d["qk"].astype(BF16), d["v"].astype(BF16), preferred_element_type=F32)
        stores = []
        for d in ph:
            stores += finish_phase(d)
        return stores

    def gate_phase(j, c, h, rows):
        if True:
            cols = slice(h * MLSTM_HD, (h + 1) * MLSTM_HD)
            b_i = bif_ref[h]
            b_f = bif_ref[MLSTM_HEADS + h]
            i_col = gz_ref[rows, h:h + 1] + b_i
            f_col = _log_sigmoid(gz_ref[rows, MLSTM_HEADS + h:MLSTM_HEADS + h + 1] + b_f)
            i_row = gt_ref[j, h, pl.ds(c, 1), :] + b_i
            f_row = _log_sigmoid(gt_ref[j, MLSTM_HEADS + h, pl.ds(c, 1), :] + b_f)
            b_col = jnp.sum(jnp.where(causal, f_row, 0.0), axis=1, keepdims=True)
            b_row = jnp.sum(jnp.where(t_idx <= s_idx, f_col, 0.0), axis=0, keepdims=True)
            g_row = i_row - b_row
            g_col = i_col - b_col
            cm_col = jnp.max(jnp.where(causal, g_row, -jnp.inf), axis=1, keepdims=True)
            m_old = m_ref[j, h]
            mu_col = jnp.maximum(m_old, cm_col)
            w = jnp.where(causal, jnp.exp(g_row - mu_col), 0.0)
            s_inter = jnp.exp(m_old - mu_col)

            q = q_ref[rows, cols]
            ks = k_ref[rows, cols] * scale
            mu_last = mu_col[L - 1:L]
            return dict(j=j, h=h, rows=rows, cols=cols, q=q, ks=ks, v=v_ref[rows, cols],
                        qb=q.astype(BF16), kb=ks.astype(BF16), c_old=c_ref[j, h], n_old=n_ref[j, h],
                        w=w, s_inter=s_inter, m_t=b_col + mu_col, wl_col=jnp.exp(g_col - mu_last),
                        dec=jnp.exp(m_old - mu_last), m_new=b_col[L - 1:L] + mu_last)

    def finish_phase(d):
        j, h, rows, cols = d["j"], d["h"], d["rows"], d["cols"]
        num = d["s_inter"] * d["qc"] + d["pv"]
        den = (d["s_inter"] * jnp.sum(d["q"] * d["n_old"], axis=1, keepdims=True)
               + jnp.sum(d["qk"], axis=1, keepdims=True))
        hh = num / jnp.maximum(jnp.abs(den), jnp.exp(-d["m_t"]))
        mean = jnp.mean(hh, -1, keepdims=True)
        xc = hh - mean
        hn = xc * lax.rsqrt(jnp.mean(xc * xc, -1, keepdims=True) + EPS) * lng_ref[:, cols]
        return [(c_ref, (j, h), d["dec"] * d["c_old"] + d["c_upd"]),
                (n_ref, (j, h), d["dec"] * d["n_old"] + jnp.sum(d["wl_col"] * d["ks"], axis=0, keepdims=True)),
                (m_ref, (j, h), d["m_new"]),
                (y_ref, (rows, cols), jax.nn.sigmoid(og_ref[rows, cols]) * hn)]

    def commit(stores):
        for ref, idx, val in stores:
            ref[idx] = val

    def chunk_body(j, c, carry):
        commit(compute([(j, c)]))
        return carry

    if n_chunks == 1:
        commit(compute([(j, 0) for j in range(bb)]))
    else:
        for j in range(bb):
            lax.fori_loop(0, n_chunks, functools.partial(chunk_body, j), 0)


def mlstm_mixer(z, b_if, c0, n0, m0, ln_g, *, batch, seq_len, tb, bb):
    m_rows = batch * seq_len
    L = math.gcd(seq_len, CHUNK)
    n_tb = seq_len // tb
    n_chunks = tb // L
    assert bb == 1 or n_tb == 1
    rows = bb * tb
    gates = z[:, Z0_GATE:Z0_GATE + 2 * MLSTM_HEADS].reshape(batch, seq_len, 2 * MLSTM_HEADS)
    gates_t = jnp.swapaxes(gates, 1, 2).reshape(batch, 2 * MLSTM_HEADS, seq_len // L, L)
    zcol = lambda off: pl.BlockSpec((rows, HALF), lambda b, t: (b * n_tb + t, off // HALF))
    state4 = lambda a, b_: pl.BlockSpec((bb, MLSTM_HEADS, a, b_), lambda b, t: (b, 0, 0, 0))
    y, c, n, m = pl.pallas_call(
        functools.partial(_mlstm_kernel, chunk=L, n_chunks=n_chunks, bb=bb),
        grid=(batch // bb, n_tb),
        in_specs=[pl.BlockSpec(memory_space=pltpu.SMEM),
                  zcol(Z0_Q), zcol(Z0_K), zcol(Z0_V), zcol(Z0_O),
                  pl.BlockSpec((rows, LANES), lambda b, t: (b * n_tb + t, Z0_GATE // LANES)),
                  pl.BlockSpec((bb, 2 * MLSTM_HEADS, n_chunks, L), lambda b, t: (b, 0, t, 0)),
                  state4(MLSTM_HD, MLSTM_HD), state4(1, MLSTM_HD), state4(1, 1),
                  pl.BlockSpec((1, HALF), lambda b, t: (0, 0))],
        out_specs=[pl.BlockSpec((rows, HALF), lambda b, t: (b * n_tb + t, 0)),
                   state4(MLSTM_HD, MLSTM_HD), state4(1, MLSTM_HD), state4(1, 1)],
        out_shape=[jax.ShapeDtypeStruct((m_rows, HALF), F32),
                   jax.ShapeDtypeStruct((batch, MLSTM_HEADS, MLSTM_HD, MLSTM_HD), F32),
                   jax.ShapeDtypeStruct((batch, MLSTM_HEADS, 1, MLSTM_HD), F32),
                   jax.ShapeDtypeStruct((batch, MLSTM_HEADS, 1, 1), F32)],
        compiler_params=_cparams("parallel", "arbitrary"),
        name="mlstm",
    )(b_if, z, z, z, z, z, gates_t,
      c0, n0.reshape(batch, MLSTM_HEADS, 1, MLSTM_HD), m0.reshape(batch, MLSTM_HEADS, 1, 1),
      ln_g.reshape(1, HALF))
    return y, c, n.reshape(batch, MLSTM_HEADS, MLSTM_HD), m.reshape(batch, MLSTM_HEADS)


def _split2(x):
    hi = x.astype(BF16)
    lo = (x - hi.astype(F32)).astype(BF16)
    return hi, lo


def _seg_sum(x, ones_blk):
    hi, lo = _split2(x)
    return (jnp.dot(hi, ones_blk, preferred_element_type=F32)
            + jnp.dot(lo, ones_blk, preferred_element_type=F32))


def _softplus(x):
    return jnp.maximum(x, 0.0) + jnp.log1p(jnp.exp(-jnp.abs(x)))


def _block_ones(n, seg):
    idx = jnp.arange(n) // seg
    return (idx[:, None] == idx[None, :]).astype(BF16)


def _rwkv_prep_kernel(z_ref, st_ref, mu_ref, w0_ref, wup_ref, a0_ref, aup_ref, gup_ref, kk_ref, ka_ref,
                      rk_ref, ones_ref, r_out, w_out, k_out, v_out, kk_out, kka_out, bv_out, g_out,
                      car_scr, *, tm, seq_len):
    zr = z_ref[...]
    row = lax.broadcasted_iota(jnp.int32, zr.shape, 0)
    prev = pltpu.roll(zr, 1, 0)
    if seq_len >= tm:
        @pl.when((pl.program_id(0) * tm) % seq_len == 0)
        def _():
            car_scr[...] = st_ref[...]
        prev = jnp.where(row == 0, car_scr[SUBLANES - 1:SUBLANES], prev)
        car_scr[...] = zr[tm - SUBLANES:]
    else:
        prev = jnp.where(row % seq_len == 0, st_ref[...], prev)
    zs = zr + (prev - zr) * mu_ref[...]
    r = zs[:, :HALF]
    kr = zs[:, HALF:2 * HALF]
    vr = zs[:, 2 * HALF:3 * HALF]
    lora = zs[:, RW_LORA:RW_G]
    ones = ones_ref[...]
    w_log = -_softplus(-(w0_ref[...] + jnp.dot(jnp.tanh(lora).astype(BF16), wup_ref[...],
                                               preferred_element_type=F32))) - 0.5
    a = jax.nn.sigmoid(a0_ref[...] + jnp.dot(lora.astype(BF16), aup_ref[...], preferred_element_type=F32))
    g = jnp.dot(jax.nn.sigmoid(zs[:, RW_G:]).astype(BF16), gup_ref[...], preferred_element_type=F32)
    kk = kr * kk_ref[...]
    kk = kk * lax.rsqrt(jnp.maximum(_seg_sum(kk * kk, ones), 1e-24))
    k2 = kr * (1.0 + (a - 1.0) * ka_ref[...])
    r_out[...] = r
    w_out[...] = jnp.exp(-jnp.exp(w_log))
    k_out[...] = k2
    v_out[...] = vr
    kk_out[...] = kk
    kka_out[...] = kk * a
    bv_out[...] = _seg_sum(r * k2 * rk_ref[...], ones) * vr
    g_out[...] = g


def _rwkv_rec_kernel(r_ref, w_ref, k_ref, v_ref, kk_ref, kka_ref, bv_ref, g_ref, s0_ref, lng_ref, lnb_ref,
                     ones2_ref, onesy_ref, ones_ref, y_ref, s_ref, yraw_scr, lhs_scr, ylhs_scr, *, bb, tb):
    @pl.when(pl.program_id(1) == 0)
    def _():
        s_ref[...] = s0_ref[...]

    pairs = RWKV_HEADS // 2
    n_tiles = bb * pairs
    rows_all = n_tiles * RWKV_HD
    shape = (RWKV_HD, LANES)
    eye2 = (lax.broadcasted_iota(jnp.int32, shape, 0)
            == lax.broadcasted_iota(jnp.int32, shape, 1) % RWKV_HD)
    ones2 = ones2_ref[...]
    y_rows = lambda idx: slice((idx // 2) * RWKV_HD, (idx // 2 + 1) * RWKV_HD)
    y_cols = lambda idx: slice((idx % 2) * LANES, (idx % 2 + 1) * LANES)

    def step(t, carry):
        def tile(idx):
            b, p = divmod(idx, pairs)
            cols = slice(p * LANES, (p + 1) * LANES)
            return b, p, (lambda ref: ref[t, b:b + 1, cols]), slice(idx * RWKV_HD, (idx + 1) * RWKV_HD)

        for idx in range(n_tiles):
            b, p, row, rows = tile(idx)
            hi, lo = _split2(s_ref[b, p] * row(kk_ref))
            lhs_scr[rows, :LANES] = hi
            lhs_scr[rows, LANES:] = lo
            v_row = row(v_ref)
            v_hi = v_row.astype(BF16).astype(F32)
            v_rows = slice(rows_all + idx * RWKV_HD, rows_all + (idx + 1) * RWKV_HD)
            lhs_scr[v_rows, :LANES] = jnp.where(eye2, v_hi, 0.0).astype(BF16)
            lhs_scr[v_rows, LANES:] = jnp.where(eye2, v_row - v_hi, 0.0).astype(BF16)
        sums = jnp.dot(lhs_scr[...], ones2, preferred_element_type=F32)
        for idx in range(n_tiles):
            b, p, row, rows = tile(idx)
            v_tile = sums[rows_all + idx * RWKV_HD:rows_all + (idx + 1) * RWKV_HD]
            s = s_ref[b, p] * row(w_ref) - sums[rows] * row(kka_ref) + v_tile * row(k_ref)
            s_ref[b, p] = s
            ylhs_scr[y_rows(idx), y_cols(idx)] = (s * row(r_ref)).astype(BF16)
        ysum = jnp.dot(ylhs_scr[...], onesy_ref[...], preferred_element_type=F32)
        for idx in range(n_tiles):
            b, p, row, rows = tile(idx)
            yraw_scr[t, b:b + 1, p * LANES:(p + 1) * LANES] = jnp.sum(
                jnp.where(eye2, ysum[y_rows(idx), y_cols(idx)], 0.0), axis=0, keepdims=True)
        return carry

    lax.fori_loop(0, tb, step, 0)

    ones = ones_ref[...]
    y = yraw_scr[...].reshape(tb * bb, HALF)
    xc = y - _seg_sum(y, ones) * (1.0 / RWKV_HD)
    var = _seg_sum(xc * xc, ones) * (1.0 / RWKV_HD)
    yn = xc * lax.rsqrt(var + RWKV_LN_EPS) * lng_ref[...] + lnb_ref[...]
    out = (yn + bv_ref[...].reshape(tb * bb, HALF)) * g_ref[...].reshape(tb * bb, HALF)
    y_ref[...] = out.reshape(tb, bb, HALF)


def rwkv_mixer(z, s0, shift0, mu, w0, w_up, a0, a_up, g_up, k_k, k_a, r_k, ln_g, ln_b,
               *, batch, seq_len, tm, bb, tb):
    m_rows = batch * seq_len
    tm_shape = (seq_len, batch, HALF)
    if seq_len >= tm:
        tiles_per_seq = seq_len // tm
        st = jnp.pad(shift0[:, None], ((0, 0), (SUBLANES - 1, 0), (0, 0)))
        st_spec = pl.BlockSpec((None, SUBLANES, RWKV_COLS), lambda i: (i // tiles_per_seq, 0, 0))
        sem = "arbitrary"
        tok = pl.BlockSpec((tm, HALF), lambda i: (i % tiles_per_seq, i // tiles_per_seq))
        tok_shape = (seq_len, batch * HALF)
    else:
        st = jnp.pad(shift0[:, None], ((0, 0), (0, seq_len - 1), (0, 0))).reshape(m_rows, RWKV_COLS)
        st_spec = pl.BlockSpec((tm, RWKV_COLS), lambda i: (i, 0))
        sem = "parallel"
        tok = pl.BlockSpec((tm, HALF), lambda i: (i, 0))
        tok_shape = (m_rows, HALF)
    row = lambda a: a.reshape(1, -1)
    const = lambda i: (0, 0)
    vec = pl.BlockSpec((1, HALF), const)
    lora_rows = RWKV_W_LORA + RWKV_A_LORA
    w_up_p = jnp.pad(w_up, ((0, RWKV_A_LORA), (0, 0))).astype(BF16)
    a_up_p = jnp.pad(a_up, ((RWKV_W_LORA, 0), (0, 0))).astype(BF16)
    ones_half = _block_ones(HALF, RWKV_HD)
    prep = pl.pallas_call(
        functools.partial(_rwkv_prep_kernel, tm=tm, seq_len=seq_len),
        grid=(m_rows // tm,),
        in_specs=[pl.BlockSpec((tm, RWKV_COLS), lambda i: (i, Z0_RWKV // RWKV_COLS)),
                  st_spec,
                  pl.BlockSpec((1, RWKV_COLS), const),
                  vec, pl.BlockSpec((lora_rows, HALF), const),
                  vec, pl.BlockSpec((lora_rows, HALF), const),
                  pl.BlockSpec((RWKV_G_LORA, HALF), const),
                  vec, vec, vec,
                  pl.BlockSpec((HALF, HALF), const)],
        out_specs=[tok] * 8,
        out_shape=[jax.ShapeDtypeStruct(tok_shape, F32)] * 8,
        scratch_shapes=[pltpu.VMEM((SUBLANES, RWKV_COLS), F32)],
        compiler_params=_cparams(sem),
        name="rwkv_prep",
    )(z, st, row(mu), row(w0), w_up_p, row(a0), a_up_p, g_up.astype(BF16), row(k_k), row(k_a), row(r_k),
      ones_half)
    if seq_len >= tm:
        seqs = [a.reshape(tm_shape) for a in prep]
    else:
        seqs = [jnp.swapaxes(a.reshape(batch, seq_len, HALF), 0, 1) for a in prep]

    pairs = RWKV_HEADS // 2
    s_pairs = (s0.reshape(batch, pairs, 2, RWKV_HD, RWKV_HD).transpose(0, 1, 3, 2, 4)
               .reshape(batch, pairs, RWKV_HD, LANES))
    blk = pl.BlockSpec((tb, bb, HALF), lambda b, t: (t, b, 0))
    s_spec = pl.BlockSpec((bb, pairs, RWKV_HD, LANES), lambda b, t: (b, 0, 0, 0))
    const2 = lambda b, t: (0, 0)
    y, s_new = pl.pallas_call(
        functools.partial(_rwkv_rec_kernel, bb=bb, tb=tb),
        grid=(batch // bb, seq_len // tb),
        in_specs=[blk] * 8 + [s_spec,
                              pl.BlockSpec((1, HALF), const2), pl.BlockSpec((1, HALF), const2),
                              pl.BlockSpec((2 * LANES, LANES), const2),
                              pl.BlockSpec((2 * LANES, 2 * LANES), const2),
                              pl.BlockSpec((HALF, HALF), const2)],
        out_specs=[blk, s_spec],
        out_shape=[jax.ShapeDtypeStruct(tm_shape, F32),
                   jax.ShapeDtypeStruct((batch, pairs, RWKV_HD, LANES), F32)],
        scratch_shapes=[pltpu.VMEM((tb, bb, HALF), F32),
                        pltpu.VMEM((2 * bb * pairs * RWKV_HD, 2 * LANES), BF16),
                        pltpu.VMEM((bb * pairs * RWKV_HD // 2, 2 * LANES), BF16)],
        compiler_params=_cparams("parallel", "arbitrary"),
        name="rwkv_rec",
    )(*seqs, s_pairs, row(ln_g), row(ln_b),
      jnp.concatenate([_block_ones(LANES, RWKV_HD)] * 2, axis=0), _block_ones(2 * LANES, RWKV_HD),
      ones_half)
    s_new = (s_new.reshape(batch, pairs, RWKV_HD, 2, RWKV_HD).transpose(0, 1, 3, 2, 4)
             .reshape(batch, RWKV_HEADS, RWKV_HD, RWKV_HD))
    return y, s_new


GLA_SUB = 16


def _gla_kernel(q_ref, k_ref, v_ref, g_ref, ad_ref, aup_ref, ab_ref, lng_ref, s0_ref,
                y_ref, s_ref, *, chunk, n_chunks, bb):
    L = chunk
    sub = min(GLA_SUB, L)

    @pl.when(pl.program_id(1) == 0)
    def _():
        s_ref[...] = s0_ref[...]

    lane = lax.broadcasted_iota(jnp.int32, (1, LANES), 1)
    head_mask = [(lane < GLA_DK).astype(F32), (lane >= GLA_DK).astype(F32)]
    tril = (lax.broadcasted_iota(jnp.int32, (L, L), 1)
            <= lax.broadcasted_iota(jnp.int32, (L, L), 0)).astype(F32)
    eye = (lax.broadcasted_iota(jnp.int32, (LANES, LANES), 0)
           == lax.broadcasted_iota(jnp.int32, (LANES, LANES), 1))

    def chunk_compute(j, c):
        stores = []
        start = (j * n_chunks + c) * L
        rows = pl.ds(start if isinstance(start, int) else pl.multiple_of(start, L), L)
        pre = jnp.dot(ad_ref[rows, :].astype(BF16), aup_ref[...], preferred_element_type=F32) + ab_ref[...]
        la = _log_sigmoid(pre) * (1.0 / GLA_TAU)
        bc_all = jnp.dot(tril, la, preferred_element_type=F32, precision=lax.Precision.HIGHEST)
        for p in range(GLA_HEADS // 2):
            kcols = slice(p * LANES, (p + 1) * LANES)
            qp = q_ref[rows, kcols] * (GLA_DK ** -0.5)
            kp = k_ref[rows, kcols]
            bc = bc_all[:, kcols]
            v_pair = v_ref[rows, 2 * p * GLA_DV:2 * (p + 1) * GLA_DV].astype(BF16)
            s_old = s_ref[j, p]
            stack = lambda x: jnp.concatenate([x * head_mask[0], x * head_mask[1]], axis=0).astype(BF16)

            inter = jnp.dot(stack(qp * jnp.exp(bc)), s_old.astype(BF16), preferred_element_type=F32)
            intra = []
            for i in range(L // sub):
                lo, hi = i * sub, (i + 1) * sub
                c_i = bc[lo - 1:lo] if i > 0 else jnp.zeros((1, LANES), F32)
                qe = qp[lo:hi] * jnp.exp(bc[lo:hi] - c_i)
                ke = kp[:hi] * jnp.exp(c_i - bc[:hi])
                att = _dot_nt(stack(qe), ke.astype(BF16))
                t_idx = lo + lax.broadcasted_iota(jnp.int32, (2 * sub, hi), 0) % sub
                s_idx = lax.broadcasted_iota(jnp.int32, (2 * sub, hi), 1)
                att = jnp.where(s_idx <= t_idx, att, 0.0)
                intra.append(jnp.dot(att.astype(BF16), v_pair[:hi], preferred_element_type=F32))
            for e in range(2):
                h = 2 * p + e
                vcols = slice(h * GLA_DV, (h + 1) * GLA_DV)
                o = inter[e * L:(e + 1) * L] + jnp.concatenate(
                    [blk[e * sub:(e + 1) * sub, e * GLA_DV:(e + 1) * GLA_DV] for blk in intra], axis=0)
                on = o * lax.rsqrt(jnp.mean(o * o, -1, keepdims=True) + EPS) * lng_ref[:, vcols]
                gate = g_ref[rows, vcols]
                stores.append((y_ref, (rows, vcols), on * (gate * jax.nn.sigmoid(gate))))

            bl = bc[L - 1:L]
            kd = kp * jnp.exp(bl - bc)
            v_stack = jnp.concatenate([v_pair[:, :GLA_DV], v_pair[:, GLA_DV:]], axis=0)
            dec_col = jnp.sum(jnp.where(eye, jnp.exp(bl), 0.0), axis=1, keepdims=True)
            stores.append((s_ref, (j, p), dec_col * s_old + _dot_tn(stack(kd), v_stack)))
        return stores

    def commit(stores):
        for ref, idx, val in stores:
            ref[idx] = val

    def chunk_body(j, c, carry):
        commit(chunk_compute(j, c))
        return carry

    if n_chunks == 1:
        commit([s for j in range(bb) for s in chunk_compute(j, 0)])
    else:
        for j in range(bb):
            lax.fori_loop(0, n_chunks, functools.partial(chunk_body, j), 0)


def gla_mixer(z, s0, a_up, a_b, ln_g, *, batch, seq_len, tb, bb, time_major):
    m_rows = batch * seq_len
    L = math.gcd(seq_len, CHUNK)
    n_tb = seq_len // tb
    pairs = GLA_HEADS // 2
    assert bb == 1 or (n_tb == 1 and not time_major)
    rows = bb * tb
    if time_major:
        zblk = lambda width, off: pl.BlockSpec((tb, width), lambda b, t: (t, (b * Z1_COLS + off) // width))
    else:
        zblk = lambda width, off: pl.BlockSpec((rows, width), lambda b, t: (b * n_tb + t, off // width))
    s_spec = pl.BlockSpec((bb, pairs, 2 * GLA_DK, GLA_DV), lambda b, t: (b, 0, 0, 0))
    const = lambda b, t: (0, 0)
    a_up_p = jnp.pad(a_up, ((0, LANES - GLA_GATE_RANK), (0, 0))).astype(BF16)
    y, s_new = pl.pallas_call(
        functools.partial(_gla_kernel, chunk=L, n_chunks=tb // L, bb=bb),
        grid=(batch // bb, n_tb),
        in_specs=[zblk(GLA_QK, Z1_Q), zblk(GLA_QK, Z1_K), zblk(HALF, Z1_V), zblk(HALF, Z1_G),
                  zblk(LANES, Z1_AD),
                  pl.BlockSpec((LANES, GLA_QK), const), pl.BlockSpec((1, GLA_QK), const),
                  pl.BlockSpec((1, HALF), const), s_spec],
        out_specs=[pl.BlockSpec((rows, HALF), lambda b, t: (b * n_tb + t, 0)), s_spec],
        out_shape=[jax.ShapeDtypeStruct((m_rows, HALF), F32),
                   jax.ShapeDtypeStruct((batch, pairs, 2 * GLA_DK, GLA_DV), F32)],
        compiler_params=_cparams("parallel", "arbitrary"),
        name="gla",
    )(z, z, z, z, z, a_up_p, a_b.reshape(1, GLA_QK), ln_g.reshape(1, HALF),
      s0.reshape(batch, pairs, 2 * GLA_DK, GLA_DV))
    return y, s_new.reshape(batch, GLA_HEADS, GLA_DK, GLA_DV)


S5_KT = LANES // S5_GROUP
S5_SCAN_SPLIT = 2


def _s5_param_kernel(lre_ref, lim_ref, step_ref, bre_ref, bim_ref, bbre_ref, bbim_ref, pre_ref, pim_ref):
    lre, lim = lre_ref[...], lim_ref[...]
    step = jnp.exp(step_ref[...])
    mag = jnp.exp(lre * step)
    bar_re = mag * jnp.cos(lim * step)
    bar_im = mag * jnp.sin(lim * step)
    inv = 1.0 / (lre * lre + lim * lim)
    cre = ((bar_re - 1.0) * lre + bar_im * lim) * inv
    cim = (bar_im * lre - (bar_re - 1.0) * lim) * inv
    for g in range(S5_GROUPS):
        cr, ci = cre[g:g + 1], cim[g:g + 1]
        bbre_ref[g] = cr * bre_ref[g] - ci * bim_ref[g]
        bbim_ref[g] = cr * bim_ref[g] + ci * bre_ref[g]
    pre_ref[0] = bar_re
    pim_ref[0] = bar_im


def _s5_kernel(u_ref, wbre_ref, wbim_ref, lre_ref, lim_ref, x0re_ref, x0im_ref, wcre_ref, wcim_ref,
               d_ref, wglu_ref, y_ref, xre_ref, xim_ref, sre_scr, sim_scr, *, tt, bb):
    @pl.when(pl.program_id(1) == 0)
    def _():
        xre_ref[...] = x0re_ref[...]
        xim_ref[...] = x0im_ref[...]

    rows_all = tt * bb
    u = u_ref[...].reshape(rows_all, S5_CH)
    ub = u.astype(BF16)
    nblk = S5_CH // LANES
    wide = S5_STATE // nblk
    for kt in range(nblk):
        cols = slice(kt * wide, (kt + 1) * wide)
        ukt = ub[:, kt * LANES:(kt + 1) * LANES]
        sre_scr[:, cols] = jnp.dot(ukt, wbre_ref[kt], preferred_element_type=F32)
        sim_scr[:, cols] = jnp.dot(ukt, wbim_ref[kt], preferred_element_type=F32)

    for part in range(S5_SCAN_SPLIT):
        width = S5_STATE // S5_SCAN_SPLIT
        cols = slice(part * width, (part + 1) * width)
        lr, li = lre_ref[:, cols], lim_ref[:, cols]

        def step(t, carry, cols=cols, lr=lr, li=li):
            cr, ci = carry
            rows = pl.ds(pl.multiple_of(t * bb, bb), bb)
            nr = sre_scr[rows, cols] + (lr * cr - li * ci)
            ni = sim_scr[rows, cols] + (lr * ci + li * cr)
            sre_scr[rows, cols] = nr
            sim_scr[rows, cols] = ni
            return nr, ni

        cr, ci = lax.fori_loop(0, tt, step, (xre_ref[:, cols], xim_ref[:, cols]))
        xre_ref[:, cols] = cr
        xim_ref[:, cols] = ci

    parts = []
    for nt in range(nblk):
        cols = slice(nt * wide, (nt + 1) * wide)
        parts.append(jnp.dot(sre_scr[:, cols].astype(BF16), wcre_ref[nt], preferred_element_type=F32)
                     - jnp.dot(sim_scr[:, cols].astype(BF16), wcim_ref[nt], preferred_element_type=F32))
    y = jnp.concatenate(parts, axis=1) + d_ref[...] * u
    ys = y * (0.5 * (1.0 + jnp.tanh(math.sqrt(2.0 / math.pi) * (y + 0.044715 * (y * y * y)))))
    out = ys * jax.nn.sigmoid(jnp.dot(ys.astype(BF16), wglu_ref[...], preferred_element_type=F32))
    y_ref[...] = out.reshape(tt, bb, S5_CH)


def s5_mixer(u_tm, u_col_block, x0_re, x0_im, lam_re, lam_im, log_step, b_re, b_im, c_re, c_im, d_skip,
             w_glu, *, tt, bb):
    seq_len, batch, _ = u_tm.shape
    gpc = (S5_GROUPS, S5_GROUP, S5_P)
    bb_re, bb_im, p_re, p_im = pl.pallas_call(
        _s5_param_kernel,
        out_shape=[jax.ShapeDtypeStruct(gpc, F32), jax.ShapeDtypeStruct(gpc, F32),
                   jax.ShapeDtypeStruct((1, S5_GROUPS, S5_P), F32),
                   jax.ShapeDtypeStruct((1, S5_GROUPS, S5_P), F32)],
        name="s5_params",
    )(lam_re, lam_im, log_step.reshape(S5_GROUPS, 1), jnp.swapaxes(b_re, 1, 2), jnp.swapaxes(b_im, 1, 2))

    nblk = S5_CH // LANES
    eye = jnp.eye(S5_KT, dtype=F32)

    def in_blocks(w):
        w = w.reshape(nblk, S5_KT, S5_GROUP, 1, S5_P) * eye[None, :, None, :, None]
        return w.reshape(nblk, LANES, S5_KT * S5_P).astype(BF16)

    def out_blocks(c):
        w = jnp.swapaxes(c, 1, 2).reshape(nblk, S5_KT, S5_P, 1, S5_GROUP) * eye[None, :, None, :, None]
        return w.reshape(nblk, S5_KT * S5_P, LANES).astype(BF16)

    lam_rows = lambda p: jnp.broadcast_to(p.reshape(1, S5_STATE), (bb, S5_STATE))
    const2 = lambda b, t: (0, 0)
    const3 = lambda b, t: (0, 0, 0)
    st_spec = pl.BlockSpec((bb, S5_STATE), lambda b, t: (b, 0))
    wb_spec = pl.BlockSpec((nblk, LANES, S5_KT * S5_P), const3)
    wc_spec = pl.BlockSpec((nblk, S5_KT * S5_P, LANES), const3)
    lam_spec = pl.BlockSpec((bb, S5_STATE), const2)
    y, x_re, x_im = pl.pallas_call(
        functools.partial(_s5_kernel, tt=tt, bb=bb),
        grid=(batch // bb, seq_len // tt),
        in_specs=[pl.BlockSpec((tt, bb, S5_CH), lambda b, t: (t, b, u_col_block)),
                  wb_spec, wb_spec, lam_spec, lam_spec, st_spec, st_spec, wc_spec, wc_spec,
                  pl.BlockSpec((1, S5_CH), const2), pl.BlockSpec((S5_CH, S5_CH), const2)],
        out_specs=[pl.BlockSpec((tt, bb, S5_CH), lambda b, t: (t, b, 0)), st_spec, st_spec],
        out_shape=[jax.ShapeDtypeStruct((seq_len, batch, S5_CH), F32),
                   jax.ShapeDtypeStruct((batch, S5_STATE), F32),
                   jax.ShapeDtypeStruct((batch, S5_STATE), F32)],
        scratch_shapes=[pltpu.VMEM((tt * bb, S5_STATE), F32), pltpu.VMEM((tt * bb, S5_STATE), F32)],
        compiler_params=_cparams("parallel", "arbitrary"),
        name="s5",
    )(u_tm, in_blocks(bb_re), in_blocks(bb_im), lam_rows(p_re), lam_rows(p_im),
      x0_re.reshape(batch, S5_STATE), x0_im.reshape(batch, S5_STATE),
      out_blocks(c_re), out_blocks(c_im), d_skip.reshape(1, S5_CH), w_glu.astype(BF16))
    shape = (batch, S5_GROUPS, S5_P)
    return y, x_re.reshape(shape), x_im.reshape(shape)


def _pack_even_w_in(w_in):
    d = w_in.shape[0]
    m_cols = 4 * HALF + 2 * MLSTM_HEADS
    parts = [w_in[:, m_cols:],
             w_in[:, 4 * HALF:m_cols], jnp.zeros((d, Z0_Q - Z0_GATE - 2 * MLSTM_HEADS), w_in.dtype),
             w_in[:, :4 * HALF]]
    return jnp.concatenate(parts, axis=1).astype(BF16)


def _pack_odd_w_in(w_in):
    d = w_in.shape[0]
    gla_main = 2 * GLA_QK + 2 * HALF
    gla_cols = gla_main + GLA_GATE_RANK
    parts = [w_in[:, :gla_main], w_in[:, gla_cols:], w_in[:, gla_main:gla_cols]]
    w = jnp.concatenate(parts, axis=1)
    return jnp.pad(w, ((0, 0), (0, Z1_COLS - w.shape[1]))).astype(BF16)


def _trunk(x, st, prm, *, batch, seq_len):
    (st_C, st_n, st_m, st_S, st_shift, st_gla, st_re, st_im, st_conv) = st
    (norm_mix, norm_ffn, norm_final,
     e_w_in, e_b_if, e_mu, e_w0, e_w_up, e_a0, e_a_up, e_g_up, e_k_k, e_k_a, e_r_k, e_ln_m_g, e_ln_r_g,
     e_ln_r_b, e_w_out,
     o_w_in, o_a_up, o_a_b, o_ln_g, o_lam_re, o_lam_im, o_log_step, o_b_re, o_b_im, o_c_re, o_c_im, o_d,
     o_w_glu, o_w_out,
     f_w_up, f_conv_w, f_conv_b, f_w_down) = prm
    assert len(e_w_in) == 1 and len(o_w_in) == 1 and len(f_w_up) == 2, "wired for depth 2"
    long_seq = seq_len >= 512
    assert long_seq or seq_len == SUBLANES
    m_rows = batch * seq_len
    tiles = dict(
        tm_proj=min(m_rows, 1024),
        tm_tok=512 if long_seq else 256,
        tm_ffn=512 if long_seq else 128,
        tb=512 if long_seq else seq_len,
        bb_chunked=1 if long_seq else 4,
        tb_rwkv=64 if long_seq else seq_len,
        tt_s5=128 if long_seq else seq_len,
    )
    x = x.reshape(m_rows, D_MODEL)
    ffn = functools.partial(conv_ffn, batch=batch, seq_len=seq_len, tm=tiles["tm_ffn"])

    z0 = norm_matmul(x, norm_mix[0], _pack_even_w_in(e_w_in[0]), tm=tiles["tm_proj"], tn=512)
    y_m, c_new, n_new, m_new = mlstm_mixer(z0, e_b_if[0], st_C[0], st_n[0], st_m[0], e_ln_m_g[0],
                                           batch=batch, seq_len=seq_len, tb=tiles["tb"], bb=tiles["bb_chunked"])
    y_r, s_new = rwkv_mixer(z0, st_S[0], st_shift[0], e_mu[0], e_w0[0], e_w_up[0], e_a0[0], e_a_up[0],
                            e_g_up[0], e_k_k[0], e_k_a[0], e_r_k[0], e_ln_r_g[0], e_ln_r_b[0],
                            batch=batch, seq_len=seq_len, tm=tiles["tm_tok"], bb=SUBLANES,
                            tb=tiles["tb_rwkv"])
    if not long_seq:
        y_r = jnp.swapaxes(y_r, 0, 1).reshape(m_rows, HALF)
    shift_new = z0.reshape(batch, seq_len, Z0_COLS)[:, -1, Z0_RWKV:Z0_RWKV + RWKV_COLS]
    w_out = e_w_out[0].astype(BF16)
    x = out_proj(x, y_m, y_r, w_out[:HALF], w_out[HALF:], tm=tiles["tm_tok"])
    x, conv_new0 = ffn(x, norm_ffn[0], f_w_up[0].astype(BF16), f_conv_w[0], f_conv_b[0],
                       f_w_down[0].astype(BF16), st_conv[0], norm_final, final_norm=False)

    z1 = norm_matmul(x, norm_mix[1], _pack_odd_w_in(o_w_in[0]), tm=tiles["tm_proj"], tn=512,
                     time_major_seq_len=seq_len if long_seq else None)
    y_g, gla_new = gla_mixer(z1, st_gla[0], o_a_up[0], o_a_b[0], o_ln_g[0],
                             batch=batch, seq_len=seq_len, tb=tiles["tb"], bb=tiles["bb_chunked"],
                             time_major=long_seq)
    if long_seq:
        u_tm, u_col_block = z1.reshape(seq_len, batch, Z1_COLS), Z1_U // S5_CH
    else:
        u_tm = jnp.swapaxes(z1[:, Z1_U:Z1_U + S5_CH].reshape(batch, seq_len, S5_CH), 0, 1)
        u_col_block = 0
    y_s, re_new, im_new = s5_mixer(u_tm, u_col_block, st_re[0], st_im[0], o_lam_re[0], o_lam_im[0],
                                   o_log_step[0], o_b_re[0], o_b_im[0], o_c_re[0], o_c_im[0], o_d[0],
                                   o_w_glu[0], tt=tiles["tt_s5"], bb=SUBLANES)
    if not long_seq:
        y_s = jnp.swapaxes(y_s, 0, 1).reshape(m_rows, S5_CH)
    w_out = o_w_out[0].astype(BF16)
    x = out_proj(x, y_g, y_s, w_out[:HALF], w_out[HALF:], tm=tiles["tm_tok"])
    x, conv_new1 = ffn(x, norm_ffn[1], f_w_up[1].astype(BF16), f_conv_w[1], f_conv_b[1],
                       f_w_down[1].astype(BF16), st_conv[1], norm_final, final_norm=True)
    new_state = (c_new[None], n_new[None], m_new[None], s_new[None], shift_new[None],
                 gla_new[None], re_new[None], im_new[None], jnp.stack([conv_new0, conv_new1]))
    return x.reshape(batch, seq_len, D_MODEL), new_state


def kernel(x_prompt, x_sample, state_mlstm_C, state_mlstm_n, state_mlstm_m, state_rwkv_S, state_rwkv_shift,
           state_gla_S, state_s5_re, state_s5_im, state_ffn_conv,
           norm_mix, norm_ffn, norm_final,
           e_w_in, e_b_if, e_mu, e_w0, e_w_up, e_a0, e_a_up, e_g_up, e_k_k, e_k_a, e_r_k,
           e_ln_m_g, e_ln_r_g, e_ln_r_b, e_w_out,
           o_w_in, o_a_up, o_a_b, o_ln_g, o_lam_re, o_lam_im, o_log_step, o_b_re, o_b_im, o_c_re, o_c_im,
           o_d, o_w_glu, o_w_out,
           f_w_up, f_conv_w, f_conv_b, f_w_down):
    params = (norm_mix, norm_ffn, norm_final,
              e_w_in, e_b_if, e_mu, e_w0, e_w_up, e_a0, e_a_up, e_g_up, e_k_k, e_k_a, e_r_k,
              e_ln_m_g, e_ln_r_g, e_ln_r_b, e_w_out,
              o_w_in, o_a_up, o_a_b, o_ln_g, o_lam_re, o_lam_im, o_log_step, o_b_re, o_b_im, o_c_re, o_c_im,
              o_d, o_w_glu, o_w_out,
              f_w_up, f_conv_w, f_conv_b, f_w_down)
    bp, tp, _ = x_prompt.shape
    bs, ts, _ = x_sample.shape
    n_even, n_odd, depth = state_mlstm_C.shape[0], state_gla_S.shape[0], state_ffn_conv.shape[0]
    init = (jnp.zeros((n_even, bp, MLSTM_HEADS, MLSTM_HD, MLSTM_HD), F32),
            jnp.zeros((n_even, bp, MLSTM_HEADS, MLSTM_HD), F32),
            jnp.full((n_even, bp, MLSTM_HEADS), M_INIT, F32),
            jnp.zeros((n_even, bp, RWKV_HEADS, RWKV_HD, RWKV_HD), F32),
            jnp.zeros((n_even, bp, RWKV_COLS), F32),
            jnp.zeros((n_odd, bp, GLA_HEADS, GLA_DK, GLA_DV), F32),
            jnp.zeros((n_odd, bp, S5_GROUPS, S5_P), F32),
            jnp.zeros((n_odd, bp, S5_GROUPS, S5_P), F32),
            jnp.zeros((depth, bp, CONV_W - 1, 2 * D_FF), F32))
    past = (state_mlstm_C, state_mlstm_n, state_mlstm_m, state_rwkv_S, state_rwkv_shift,
            state_gla_S, state_s5_re, state_s5_im, state_ffn_conv)
    y_prompt, p_state = _trunk(x_prompt, init, params, batch=bp, seq_len=tp)
    y_sample, s_state = _trunk(x_sample, past, params, batch=bs, seq_len=ts)
    return (y_prompt, y_sample) + tuple(p_state) + tuple(s_state)
```

```python
import functools
import math

import jax
import jax.numpy as jnp
from jax import lax
from jax.experimental import pallas as pl
from jax.experimental.pallas import tpu as pltpu

F32 = jnp.float32
BF16 = jnp.bfloat16

D_MODEL = 1024
HALF = D_MODEL // 2
MLSTM_HEADS = 4
MLSTM_HD = HALF // MLSTM_HEADS
RWKV_HD = 64
RWKV_HEADS = HALF // RWKV_HD
RWKV_W_LORA = 64
RWKV_A_LORA = 64
RWKV_G_LORA = 128
RWKV_COLS = 3 * HALF + RWKV_W_LORA + RWKV_A_LORA + RWKV_G_LORA
GLA_HEADS = 4
GLA_DK = 64
GLA_DV = 128
GLA_QK = GLA_HEADS * GLA_DK
GLA_GATE_RANK = 16
GLA_TAU = 16.0
S5_CH = HALF
S5_GROUP = 16
S5_GROUPS = S5_CH // S5_GROUP
S5_P = 64
S5_STATE = S5_GROUPS * S5_P
D_FF = 2816
CONV_W = 3
CHUNK = 64
EPS = 1e-6
RWKV_LN_EPS = 64e-5
M_INIT = -1e30

LANES = 128
SUBLANES = 8
VMEM_LIMIT_BYTES = 56 * 1024 * 1024

Z0_RWKV = 0
Z0_GATE = RWKV_COLS
Z0_Q = Z0_GATE + 2 * LANES
Z0_K, Z0_V, Z0_O = Z0_Q + HALF, Z0_Q + 2 * HALF, Z0_Q + 3 * HALF
Z0_COLS = Z0_Q + 4 * HALF
RW_LORA = 3 * HALF
RW_G = RW_LORA + RWKV_W_LORA + RWKV_A_LORA
Z1_Q, Z1_K, Z1_V, Z1_G, Z1_U = 0, GLA_QK, 2 * GLA_QK, 2 * GLA_QK + HALF, 2 * GLA_QK + 2 * HALF
Z1_AD = Z1_U + S5_CH
Z1_COLS = 2560


def _cparams(*sem):
    return pltpu.CompilerParams(dimension_semantics=sem, vmem_limit_bytes=VMEM_LIMIT_BYTES)


def _rms(x, g):
    return x * lax.rsqrt(jnp.mean(x * x, -1, keepdims=True) + EPS) * g


def _norm_matmul_kernel(x_ref, g_ref, w_ref, o_ref, h_scr):
    @pl.when(pl.program_id(1) == 0)
    def _():
        h_scr[...] = _rms(x_ref[...], g_ref[...]).astype(BF16)

    o_ref[...] = jnp.dot(h_scr[...], w_ref[...], preferred_element_type=F32)


def norm_matmul(x, g, w, *, tm, tn, time_major_seq_len=None):
    m, d = x.shape
    n = w.shape[1]
    if time_major_seq_len is None:
        out_shape = (m, n)
        out_spec = pl.BlockSpec((tm, tn), lambda i, j: (i, j))
    else:
        tiles_per_seq = time_major_seq_len // tm
        n_col = n // tn
        out_shape = (time_major_seq_len, (m // time_major_seq_len) * n)
        out_spec = pl.BlockSpec((tm, tn), lambda i, j: (i % tiles_per_seq, (i // tiles_per_seq) * n_col + j))
    return pl.pallas_call(
        _norm_matmul_kernel,
        grid=(m // tm, n // tn),
        in_specs=[pl.BlockSpec((tm, d), lambda i, j: (i, 0)),
                  pl.BlockSpec((1, d), lambda i, j: (0, 0)),
                  pl.BlockSpec((d, tn), lambda i, j: (0, j))],
        out_specs=out_spec,
        out_shape=jax.ShapeDtypeStruct(out_shape, F32),
        scratch_shapes=[pltpu.VMEM((tm, d), BF16)],
        compiler_params=_cparams("parallel", "arbitrary"),
        name="norm_matmul",
    )(x, g.reshape(1, d), w)


def _out_proj_kernel(x_ref, ya_ref, yb_ref, wa_ref, wb_ref, o_ref):
    acc = jnp.dot(ya_ref[...].astype(BF16), wa_ref[...], preferred_element_type=F32)
    acc += jnp.dot(yb_ref[...].astype(BF16), wb_ref[...], preferred_element_type=F32)
    o_ref[...] = x_ref[...] + acc


def out_proj(x, ya, yb, wa, wb, *, tm):
    m, d = x.shape
    k = ya.shape[1]
    if yb.ndim == 3:
        tiles_per_seq = yb.shape[0] // tm
        yb = yb.reshape(yb.shape[0], -1)
        yb_spec = pl.BlockSpec((tm, k), lambda i: (i % tiles_per_seq, i // tiles_per_seq))
    else:
        yb_spec = pl.BlockSpec((tm, k), lambda i: (i, 0))
    return pl.pallas_call(
        _out_proj_kernel,
        grid=(m // tm,),
        in_specs=[pl.BlockSpec((tm, d), lambda i: (i, 0)),
                  pl.BlockSpec((tm, k), lambda i: (i, 0)),
                  yb_spec,
                  pl.BlockSpec((k, d), lambda i: (0, 0)),
                  pl.BlockSpec((k, d), lambda i: (0, 0))],
        out_specs=pl.BlockSpec((tm, d), lambda i: (i, 0)),
        out_shape=jax.ShapeDtypeStruct((m, d), F32),
        compiler_params=_cparams("parallel"),
        name="out_proj",
    )(x, ya, yb, wa, wb)


FFN_SUB = 256


def _ffn_kernel(x_ref, g_ref, wup_ref, cw_ref, cb_ref, wdn_ref, st_ref, gf_ref,
                o_ref, tail_ref, h_scr, act_scr, car_scr, *, tm, seq_len, final_norm):
    long_seq = seq_len >= tm
    i = pl.program_id(0)
    h_scr[...] = _rms(x_ref[...], g_ref[...]).astype(BF16)
    row = lax.broadcasted_iota(jnp.int32, (tm, FFN_SUB), 0)
    if long_seq:
        @pl.when((i * tm) % seq_len == 0)
        def _():
            car_scr[...] = st_ref[...]
    else:
        t_in_seq = row % seq_len

    row8 = lax.broadcasted_iota(jnp.int32, (SUBLANES, FFN_SUB), 0)

    def conv_part(col0):
        u = jnp.dot(h_scr[...], wup_ref[:, col0:col0 + FFN_SUB], preferred_element_type=F32)
        p1 = pltpu.roll(u, 1, 0)
        p2 = pltpu.roll(u, 2, 0)
        if long_seq:
            halo = car_scr[:, col0:col0 + FFN_SUB]
            h6 = halo[SUBLANES - 2:SUBLANES - 1]
            h7 = halo[SUBLANES - 1:SUBLANES]
            p1 = jnp.concatenate([jnp.where(row8 == 0, h7, p1[:SUBLANES]), p1[SUBLANES:]], axis=0)
            p2 = jnp.concatenate([jnp.where(row8 == 0, h6, jnp.where(row8 == 1, h7, p2[:SUBLANES])),
                                  p2[SUBLANES:]], axis=0)
            car_scr[:, col0:col0 + FFN_SUB] = u[tm - SUBLANES:]
            tail_ref[:, col0:col0 + FFN_SUB] = u[tm - SUBLANES:]
        else:
            e = st_ref[:, col0:col0 + FFN_SUB]
            p1 = jnp.where(t_in_seq == 0, pltpu.roll(e, tm - 1, 0), p1)
            p2 = jnp.where(t_in_seq < 2, e, p2)
            tail_ref[:, col0:col0 + FFN_SUB] = u
        cw = cw_ref[:, col0:col0 + FFN_SUB]
        return (cb_ref[:, col0:col0 + FFN_SUB]
                + (cw[0:1] * p2 + cw[1:2] * p1 + cw[2:3] * u))

    for c in range(D_FF // FFN_SUB):
        val = conv_part(c * FFN_SUB)
        gate = conv_part(D_FF + c * FFN_SUB)
        act_scr[:, c * FFN_SUB:(c + 1) * FFN_SUB] = (val * (gate * jax.nn.sigmoid(gate))).astype(BF16)

    y = x_ref[...] + jnp.dot(act_scr[...], wdn_ref[...], preferred_element_type=F32)
    if final_norm:
        y = _rms(y, gf_ref[...])
    o_ref[...] = y


def conv_ffn(x, g, w_up, conv_w, conv_b, w_down, conv0, gf, *, batch, seq_len, tm, final_norm):
    m, d = x.shape
    f2 = 2 * D_FF
    long_seq = seq_len >= tm
    if long_seq:
        assert seq_len % tm == 0
        tiles_per_seq = seq_len // tm
        st = jnp.pad(conv0, ((0, 0), (SUBLANES - (CONV_W - 1), 0), (0, 0)))
        st_spec = pl.BlockSpec((None, SUBLANES, f2), lambda i: (i // tiles_per_seq, 0, 0))
        tail_shape = jax.ShapeDtypeStruct((batch, SUBLANES, f2), F32)
        tail_spec = pl.BlockSpec((None, SUBLANES, f2), lambda i: (i // tiles_per_seq, 0, 0))
        sem = "arbitrary"
    else:
        assert seq_len == SUBLANES and tm % seq_len == 0
        st = jnp.pad(conv0, ((0, 0), (0, seq_len - (CONV_W - 1)), (0, 0))).reshape(m, f2)
        st_spec = pl.BlockSpec((tm, f2), lambda i: (i, 0))
        tail_shape = jax.ShapeDtypeStruct((m, f2), F32)
        tail_spec = pl.BlockSpec((tm, f2), lambda i: (i, 0))
        sem = "parallel"
    const = lambda i: (0, 0)
    out, tail = pl.pallas_call(
        functools.partial(_ffn_kernel, tm=tm, seq_len=seq_len, final_norm=final_norm),
        grid=(m // tm,),
        in_specs=[pl.BlockSpec((tm, d), lambda i: (i, 0)),
                  pl.BlockSpec((1, d), const),
                  pl.BlockSpec((d, f2), const, pipeline_mode=pl.Buffered(1)),
                  pl.BlockSpec((CONV_W, f2), const),
                  pl.BlockSpec((1, f2), const),
                  pl.BlockSpec((D_FF, d), const, pipeline_mode=pl.Buffered(1)),
                  st_spec,
                  pl.BlockSpec((1, d), const)],
        out_specs=[pl.BlockSpec((tm, d), lambda i: (i, 0)), tail_spec],
        out_shape=[jax.ShapeDtypeStruct((m, d), F32), tail_shape],
        scratch_shapes=[pltpu.VMEM((tm, d), BF16),
                        pltpu.VMEM((tm, D_FF), BF16),
                        pltpu.VMEM((SUBLANES, f2), F32)],
        compiler_params=_cparams(sem),
        name="conv_ffn",
    )(x, g.reshape(1, d), w_up, conv_w, conv_b.reshape(1, f2), w_down, st, gf.reshape(1, d))
    new_state = tail.reshape(batch, -1, f2)[:, -(CONV_W - 1):]
    return out, new_state


def _log_sigmoid(x):
    return jnp.minimum(x, 0.0) - jnp.log1p(jnp.exp(-jnp.abs(x)))


def _dot_nt(a, b):
    return lax.dot_general(a, b, (((1,), (1,)), ((), ())), preferred_element_type=F32)


def _dot_tn(a, b):
    return lax.dot_general(a, b, (((0,), (0,)), ((), ())), preferred_element_type=F32)


def _mlstm_kernel(bif_ref, q_ref, k_ref, v_ref, og_ref, gz_ref, gt_ref, c0_ref, n0_ref, m0_ref,
                  lng_ref, y_ref, c_ref, n_ref, m_ref, grow_scr, gcol_scr, bcol_scr, cmax_scr,
                  *, chunk, n_chunks, bb):
    L = chunk

    @pl.when(pl.program_id(1) == 0)
    def _():
        c_ref[...] = c0_ref[...]
        n_ref[...] = n0_ref[...]
        m_ref[...] = m0_ref[...]

    t_idx = lax.broadcasted_iota(jnp.int32, (L, L), 0)
    s_idx = lax.broadcasted_iota(jnp.int32, (L, L), 1)
    causal = s_idx <= t_idx
    scale = MLSTM_HD ** -0.5

    def compute(items):
        ph = []
        for j, c in items:
            start = (j * n_chunks + c) * L
            rows = pl.ds(start if isinstance(start, int) else pl.multiple_of(start, L), L)
            for h in range(MLSTM_HEADS):
                ph.append(gate_phase(j, c, h, rows))
        for d in ph:
            d["qk_raw"] = _dot_nt(d["qb"], d["kb"])
            d["qc"] = _dot_nt(d["qb"], d["c_old"].astype(BF16))
            d["c_upd"] = _dot_tn((d["v"] * d["wl_col"]).astype(BF16), d["kb"])
        for d in ph:
            d["qk"] = d["qk_raw"] * d["w"]
            d["pv"] = jnp.dot(d["qk"].astype(BF16), d["v"].astype(BF16), preferred_element_type=F32)
        stores = []
        for d in ph:
            stores += finish_phase(d)
        return stores

    def entry(j, c, h):
        return (j * n_chunks + c) * MLSTM_HEADS + h

    for j in range(bb):
        for c in range(n_chunks):
            rows = slice((j * n_chunks + c) * L, (j * n_chunks + c + 1) * L)
            for h in range(MLSTM_HEADS):
                b_i = bif_ref[h]
                b_f = bif_ref[MLSTM_HEADS + h]
                i_col = gz_ref[rows, h:h + 1] + b_i
                f_col = _log_sigmoid(gz_ref[rows, MLSTM_HEADS + h:MLSTM_HEADS + h + 1] + b_f)
                i_row = gt_ref[j, h, c:c + 1, :] + b_i
                f_row = _log_sigmoid(gt_ref[j, MLSTM_HEADS + h, c:c + 1, :] + b_f)
                b_col = jnp.sum(jnp.where(causal, f_row, 0.0), axis=1, keepdims=True)
                b_row = jnp.sum(jnp.where(t_idx <= s_idx, f_col, 0.0), axis=0, keepdims=True)
                g_row = i_row - b_row
                e = entry(j, c, h)
                grow_scr[e, 0:1, 0:L] = g_row
                gcol_scr[e, :, 0:1] = i_col - b_col
                bcol_scr[e, :, 0:1] = b_col
                cmax_scr[e, :, 0:1] = jnp.max(jnp.where(causal, g_row, -jnp.inf), axis=1, keepdims=True)

    def gate_phase(j, c, h, rows):
        cols = slice(h * MLSTM_HD, (h + 1) * MLSTM_HD)
        e = entry(j, c, h)
        g_row = grow_scr[e, 0:1, 0:L]
        b_col = bcol_scr[e, :, 0:1]
        m_old = m_ref[j, h]
        mu_col = jnp.maximum(m_old, cmax_scr[e, :, 0:1])
        w = jnp.where(causal, jnp.exp(g_row - mu_col), 0.0)
        s_inter = jnp.exp(m_old - mu_col)
        q = q_ref[rows, cols]
        ks = k_ref[rows, cols] * scale
        mu_last = mu_col[L - 1:L]
        return dict(j=j, h=h, rows=rows, cols=cols, q=q, ks=ks, v=v_ref[rows, cols],
                    qb=q.astype(BF16), kb=ks.astype(BF16), c_old=c_ref[j, h], n_old=n_ref[j, h],
                    w=w, s_inter=s_inter, m_t=b_col + mu_col,
                    wl_col=jnp.exp(gcol_scr[e, :, 0:1] - mu_last),
                    dec=jnp.exp(m_old - mu_last), m_new=b_col[L - 1:L] + mu_last)

    def finish_phase(d):
        j, h, rows, cols = d["j"], d["h"], d["rows"], d["cols"]
        num = d["s_inter"] * d["qc"] + d["pv"]
        den = (d["s_inter"] * jnp.sum(d["q"] * d["n_old"], axis=1, keepdims=True)
               + jnp.sum(d["qk"], axis=1, keepdims=True))
        hh = num / jnp.maximum(jnp.abs(den), jnp.exp(-d["m_t"]))
        return [(c_ref, (j, h), d["dec"] * d["c_old"] + d["c_upd"]),
                (n_ref, (j, h), d["dec"] * d["n_old"] + jnp.sum(d["wl_col"] * d["ks"], axis=0, keepdims=True)),
                (m_ref, (j, h), d["m_new"]),
                (y_ref, (rows, cols), hh)]

    def commit(stores):
        for ref, idx, val in stores:
            ref[idx] = val

    def chunk_body(j, c, carry):
        commit(compute([(j, c)]))
        return carry

    if n_chunks == 1:
        commit(compute([(j, 0) for j in range(bb)]))
    else:
        for j in range(bb):
            lax.fori_loop(0, n_chunks, functools.partial(chunk_body, j), 0)

    for h in range(MLSTM_HEADS):
        cols = slice(h * MLSTM_HD, (h + 1) * MLSTM_HD)
        hh = y_ref[:, cols]
        xc = hh - jnp.mean(hh, -1, keepdims=True)
        hn = xc * lax.rsqrt(jnp.mean(xc * xc, -1, keepdims=True) + EPS) * lng_ref[:, cols]
        y_ref[:, cols] = jax.nn.sigmoid(og_ref[:, cols]) * hn


def mlstm_mixer(z, b_if, c0, n0, m0, ln_g, *, batch, seq_len, tb, bb):
    m_rows = batch * seq_len
    L = math.gcd(seq_len, CHUNK)
    n_tb = seq_len // tb
    n_chunks = tb // L
    assert bb == 1 or n_tb == 1
    rows = bb * tb
    n_entries = bb * n_chunks * MLSTM_HEADS
    gates = z[:, Z0_GATE:Z0_GATE + 2 * MLSTM_HEADS].reshape(batch, seq_len, 2 * MLSTM_HEADS)
    gates_t = jnp.swapaxes(gates, 1, 2).reshape(batch, 2 * MLSTM_HEADS, seq_len // L, L)
    zcol = lambda off: pl.BlockSpec((rows, HALF), lambda b, t: (b * n_tb + t, off // HALF))
    state4 = lambda a, b_: pl.BlockSpec((bb, MLSTM_HEADS, a, b_), lambda b, t: (b, 0, 0, 0))
    y, c, n, m = pl.pallas_call(
        functools.partial(_mlstm_kernel, chunk=L, n_chunks=n_chunks, bb=bb),
        grid=(batch // bb, n_tb),
        in_specs=[pl.BlockSpec(memory_space=pltpu.SMEM),
                  zcol(Z0_Q), zcol(Z0_K), zcol(Z0_V), zcol(Z0_O),
                  pl.BlockSpec((rows, LANES), lambda b, t: (b * n_tb + t, Z0_GATE // LANES)),
                  pl.BlockSpec((bb, 2 * MLSTM_HEADS, n_chunks, L), lambda b, t: (b, 0, t, 0)),
                  state4(MLSTM_HD, MLSTM_HD), state4(1, MLSTM_HD), state4(1, 1),
                  pl.BlockSpec((1, HALF), lambda b, t: (0, 0))],
        out_specs=[pl.BlockSpec((rows, HALF), lambda b, t: (b * n_tb + t, 0)),
                   state4(MLSTM_HD, MLSTM_HD), state4(1, MLSTM_HD), state4(1, 1)],
        out_shape=[jax.ShapeDtypeStruct((m_rows, HALF), F32),
                   jax.ShapeDtypeStruct((batch, MLSTM_HEADS, MLSTM_HD, MLSTM_HD), F32),
                   jax.ShapeDtypeStruct((batch, MLSTM_HEADS, 1, MLSTM_HD), F32),
                   jax.ShapeDtypeStruct((batch, MLSTM_HEADS, 1, 1), F32)],
        scratch_shapes=[pltpu.VMEM((n_entries, SUBLANES, LANES), F32)]
        + [pltpu.VMEM((n_entries, L, LANES), F32)] * 3,
        compiler_params=_cparams("parallel", "arbitrary"),
        name="mlstm",
    )(b_if, z, z, z, z, z, gates_t,
      c0, n0.reshape(batch, MLSTM_HEADS, 1, MLSTM_HD), m0.reshape(batch, MLSTM_HEADS, 1, 1),
      ln_g.reshape(1, HALF))
    return y, c, n.reshape(batch, MLSTM_HEADS, MLSTM_HD), m.reshape(batch, MLSTM_HEADS)


def _split2(x):
    hi = x.astype(BF16)
    lo = (x - hi.astype(F32)).astype(BF16)
    return hi, lo


def _seg_sum(x, ones_blk):
    hi, lo = _split2(x)
    return (jnp.dot(hi, ones_blk, preferred_element_type=F32)
            + jnp.dot(lo, ones_blk, preferred_element_type=F32))


def _softplus(x):
    return jnp.maximum(x, 0.0) + jnp.log1p(jnp.exp(-jnp.abs(x)))


def _block_ones(n, seg):
    idx = jnp.arange(n) // seg
    return (idx[:, None] == idx[None, :]).astype(BF16)


def _rwkv_prep_kernel(z_ref, st_ref, mu_ref, w0_ref, wup_ref, a0_ref, aup_ref, gup_ref, kk_ref, ka_ref,
                      rk_ref, ones_ref, r_out, w_out, k_out, v_out, kk_out, kka_out, bv_out, g_out,
                      car_scr, *, tm, seq_len):
    zr = z_ref[...]
    row = lax.broadcasted_iota(jnp.int32, zr.shape, 0)
    prev = pltpu.roll(zr, 1, 0)
    if seq_len >= tm:
        @pl.when((pl.program_id(0) * tm) % seq_len == 0)
        def _():
            car_scr[...] = st_ref[...]
        prev = jnp.where(row == 0, car_scr[SUBLANES - 1:SUBLANES], prev)
        car_scr[...] = zr[tm - SUBLANES:]
    else:
        prev = jnp.where(row % seq_len == 0, st_ref[...], prev)
    zs = zr + (prev - zr) * mu_ref[...]
    r = zs[:, :HALF]
    kr = zs[:, HALF:2 * HALF]
    vr = zs[:, 2 * HALF:3 * HALF]
    lora = zs[:, RW_LORA:RW_G]
    ones = ones_ref[...]
    w_log = -_softplus(-(w0_ref[...] + jnp.dot(jnp.tanh(lora).astype(BF16), wup_ref[...],
                                               preferred_element_type=F32))) - 0.5
    a = jax.nn.sigmoid(a0_ref[...] + jnp.dot(lora.astype(BF16), aup_ref[...], preferred_element_type=F32))
    g = jnp.dot(jax.nn.sigmoid(zs[:, RW_G:]).astype(BF16), gup_ref[...], preferred_element_type=F32)
    kk = kr * kk_ref[...]
    kk = kk * lax.rsqrt(jnp.maximum(_seg_sum(kk * kk, ones), 1e-24))
    k2 = kr * (1.0 + (a - 1.0) * ka_ref[...])
    r_out[...] = r
    w_out[...] = jnp.exp(-jnp.exp(w_log))
    k_out[...] = k2
    v_out[...] = vr
    kk_out[...] = kk
    kka_out[...] = kk * a
    bv_out[...] = _seg_sum(r * k2 * rk_ref[...], ones) * vr
    g_out[...] = g


RWKV_GROUPS = 2


def _rwkv_rec_kernel(r_ref, w_ref, k_ref, v_ref, kk_ref, kka_ref, bv_ref, g_ref, s0_ref, lng_ref, lnb_ref,
                     onesy_ref, ones_ref, y_ref, s_ref, yraw_scr, lhs_scr, ylhs_scr, *, bb, tb):
    @pl.when(pl.program_id(1) == 0)
    def _():
        s_ref[...] = s0_ref[...]

    pairs = RWKV_HEADS // 2
    n_tiles = bb * pairs
    rows_all = n_tiles * RWKV_HD
    shape = (RWKV_HD, LANES)
    eye2 = (lax.broadcasted_iota(jnp.int32, shape, 0)
            == lax.broadcasted_iota(jnp.int32, shape, 1) % RWKV_HD)
    eye2_swapped = ((lax.broadcasted_iota(jnp.int32, shape, 0) ^ 1)
                    == lax.broadcasted_iota(jnp.int32, shape, 1) % RWKV_HD)
    lane_even = lax.broadcasted_iota(jnp.int32, (1, LANES), 1) % 2 == 0
    y_rows = lambda idx: slice((idx // 2) * RWKV_HD, (idx // 2 + 1) * RWKV_HD)
    y_cols = lambda idx: slice((idx % 2) * LANES, (idx % 2 + 1) * LANES)

    per_group = n_tiles // RWKV_GROUPS

    def step(t, carry):
        def tile(idx):
            b, p = divmod(idx, pairs)
            cols = slice(p * LANES, (p + 1) * LANES)
            return b, p, (lambda ref: ref[t, b:b + 1, cols])

        groups = [range(g * per_group, (g + 1) * per_group) for g in range(RWKV_GROUPS)]
        g_rows = per_group * RWKV_HD * 3 // 2
        sums = []
        for g, grp in enumerate(groups):
            for idx in grp:
                b, p, row = tile(idx)
                rel = idx - grp[0]
                base = g * g_rows + rel * RWKV_HD
                hi, lo = _split2(s_ref[b, p] * row(kk_ref))
                lhs_scr[base:base + RWKV_HD, :LANES] = hi
                lhs_scr[base:base + RWKV_HD, LANES:] = lo
                v_row = row(v_ref)
                v_hi = v_row.astype(BF16).astype(F32)
                v_lo = v_row - v_hi
                v_lo_swapped = jnp.where(lane_even, pltpu.roll(v_lo, LANES - 1, 1), pltpu.roll(v_lo, 1, 1))
                v_base = g * g_rows + per_group * RWKV_HD + (rel // 2) * RWKV_HD
                lhs_scr[v_base:v_base + RWKV_HD, y_cols(rel)] = jnp.where(
                    eye2, v_hi, jnp.where(eye2_swapped, v_lo_swapped, 0.0)).astype(BF16)
            sums.append(jnp.dot(lhs_scr[g * g_rows:(g + 1) * g_rows, :], onesy_ref[...],
                                preferred_element_type=F32))
        ysums = []
        for grp, sm in zip(groups, sums):
            for idx in grp:
                b, p, row = tile(idx)
                rel = idx - grp[0]
                sa_rows = slice(rel * RWKV_HD, (rel + 1) * RWKV_HD)
                sa = sm[sa_rows, :LANES] + sm[sa_rows, LANES:]
                v_base = per_group * RWKV_HD + (rel // 2) * RWKV_HD
                v_tile = sm[v_base:v_base + RWKV_HD, y_cols(rel)]
                s = s_ref[b, p] * row(w_ref) - sa * row(kka_ref) + v_tile * row(k_ref)
                s_ref[b, p] = s
                ylhs_scr[y_rows(idx), y_cols(idx)] = (s * row(r_ref)).astype(BF16)
            g_rows = slice(y_rows(grp[0]).start, y_rows(grp[-1]).stop)
            ysums.append(jnp.dot(ylhs_scr[g_rows, :], onesy_ref[...], preferred_element_type=F32))
        for grp, ys in zip(groups, ysums):
            for idx in grp:
                b, p, row = tile(idx)
                rel = slice(y_rows(idx).start - y_rows(grp[0]).start, y_rows(idx).stop - y_rows(grp[0]).start)
                yraw_scr[t, b:b + 1, p * LANES:(p + 1) * LANES] = jnp.sum(
                    jnp.where(eye2, ys[rel, y_cols(idx)], 0.0), axis=0, keepdims=True)
        return carry

    lax.fori_loop(0, tb, step, 0)

    ones = ones_ref[...]
    y = yraw_scr[...].reshape(tb * bb, HALF)
    xc = y - _seg_sum(y, ones) * (1.0 / RWKV_HD)
    var = _seg_sum(xc * xc, ones) * (1.0 / RWKV_HD)
    yn = xc * lax.rsqrt(var + RWKV_LN_EPS) * lng_ref[...] + lnb_ref[...]
    out = (yn + bv_ref[...].reshape(tb * bb, HALF)) * g_ref[...].reshape(tb * bb, HALF)
    y_ref[...] = out.reshape(tb, bb, HALF)


def rwkv_mixer(z, s0, shift0, mu, w0, w_up, a0, a_up, g_up, k_k, k_a, r_k, ln_g, ln_b,
               *, batch, seq_len, tm, bb, tb):
    m_rows = batch * seq_len
    tm_shape = (seq_len, batch, HALF)
    if seq_len >= tm:
        tiles_per_seq = seq_len // tm
        st = jnp.pad(shift0[:, None], ((0, 0), (SUBLANES - 1, 0), (0, 0)))
        st_spec = pl.BlockSpec((None, SUBLANES, RWKV_COLS), lambda i: (i // tiles_per_seq, 0, 0))
        sem = "arbitrary"
        tok = pl.BlockSpec((tm, HALF), lambda i: (i % tiles_per_seq, i // tiles_per_seq))
        tok_shape = (seq_len, batch * HALF)
    else:
        st = jnp.pad(shift0[:, None], ((0, 0), (0, seq_len - 1), (0, 0))).reshape(m_rows, RWKV_COLS)
        st_spec = pl.BlockSpec((tm, RWKV_COLS), lambda i: (i, 0))
        sem = "parallel"
        tok = pl.BlockSpec((tm, HALF), lambda i: (i, 0))
        tok_shape = (m_rows, HALF)
    row = lambda a: a.reshape(1, -1)
    const = lambda i: (0, 0)
    vec = pl.BlockSpec((1, HALF), const)
    lora_rows = RWKV_W_LORA + RWKV_A_LORA
    w_up_p = jnp.pad(w_up, ((0, RWKV_A_LORA), (0, 0))).astype(BF16)
    a_up_p = jnp.pad(a_up, ((RWKV_W_LORA, 0), (0, 0))).astype(BF16)
    ones_half = _block_ones(HALF, RWKV_HD)
    prep = pl.pallas_call(
        functools.partial(_rwkv_prep_kernel, tm=tm, seq_len=seq_len),
        grid=(m_rows // tm,),
        in_specs=[pl.BlockSpec((tm, RWKV_COLS), lambda i: (i, Z0_RWKV // RWKV_COLS)),
                  st_spec,
                  pl.BlockSpec((1, RWKV_COLS), const),
                  vec, pl.BlockSpec((lora_rows, HALF), const),
                  vec, pl.BlockSpec((lora_rows, HALF), const),
                  pl.BlockSpec((RWKV_G_LORA, HALF), const),
                  vec, vec, vec,
                  pl.BlockSpec((HALF, HALF), const)],
        out_specs=[tok] * 8,
        out_shape=[jax.ShapeDtypeStruct(tok_shape, F32)] * 8,
        scratch_shapes=[pltpu.VMEM((SUBLANES, RWKV_COLS), F32)],
        compiler_params=_cparams(sem),
        name="rwkv_prep",
    )(z, st, row(mu), row(w0), w_up_p, row(a0), a_up_p, g_up.astype(BF16), row(k_k), row(k_a), row(r_k),
      ones_half)
    if seq_len >= tm:
        seqs = [a.reshape(tm_shape) for a in prep]
    else:
        seqs = [jnp.swapaxes(a.reshape(batch, seq_len, HALF), 0, 1) for a in prep]

    pairs = RWKV_HEADS // 2
    s_pairs = (s0.reshape(batch, pairs, 2, RWKV_HD, RWKV_HD).transpose(0, 1, 3, 2, 4)
               .reshape(batch, pairs, RWKV_HD, LANES))
    blk = pl.BlockSpec((tb, bb, HALF), lambda b, t: (t, b, 0))
    s_spec = pl.BlockSpec((bb, pairs, RWKV_HD, LANES), lambda b, t: (b, 0, 0, 0))
    const2 = lambda b, t: (0, 0)
    y, s_new = pl.pallas_call(
        functools.partial(_rwkv_rec_kernel, bb=bb, tb=tb),
        grid=(batch // bb, seq_len // tb),
        in_specs=[blk] * 8 + [s_spec,
                              pl.BlockSpec((1, HALF), const2), pl.BlockSpec((1, HALF), const2),
                              pl.BlockSpec((2 * LANES, 2 * LANES), const2),
                              pl.BlockSpec((HALF, HALF), const2)],
        out_specs=[blk, s_spec],
        out_shape=[jax.ShapeDtypeStruct(tm_shape, F32),
                   jax.ShapeDtypeStruct((batch, pairs, RWKV_HD, LANES), F32)],
        scratch_shapes=[pltpu.VMEM((tb, bb, HALF), F32),
                        pltpu.VMEM((bb * pairs * RWKV_HD * 3 // 2, 2 * LANES), BF16),
                        pltpu.VMEM((bb * pairs * RWKV_HD // 2, 2 * LANES), BF16)],
        compiler_params=_cparams("parallel", "arbitrary"),
        name="rwkv_rec",
    )(*seqs, s_pairs, row(ln_g), row(ln_b),
      _block_ones(2 * LANES, RWKV_HD), ones_half)
    s_new = (s_new.reshape(batch, pairs, RWKV_HD, 2, RWKV_HD).transpose(0, 1, 3, 2, 4)
             .reshape(batch, RWKV_HEADS, RWKV_HD, RWKV_HD))
    return y, s_new


GLA_SUB = 16


def _gla_kernel(q_ref, k_ref, v_ref, g_ref, ad_ref, aup_ref, ab_ref, lng_ref, s0_ref,
                y_ref, s_ref, bc_scr, *, chunk, n_chunks, bb):
    L = chunk
    sub = min(GLA_SUB, L)

    @pl.when(pl.program_id(1) == 0)
    def _():
        s_ref[...] = s0_ref[...]

    lane = lax.broadcasted_iota(jnp.int32, (1, LANES), 1)
    head_mask = [(lane < GLA_DK).astype(F32), (lane >= GLA_DK).astype(F32)]
    tril = (lax.broadcasted_iota(jnp.int32, (L, L), 1)
            <= lax.broadcasted_iota(jnp.int32, (L, L), 0)).astype(F32)
    eye = (lax.broadcasted_iota(jnp.int32, (LANES, LANES), 0)
           == lax.broadcasted_iota(jnp.int32, (LANES, LANES), 1))

    pre = jnp.dot(ad_ref[...].astype(BF16), aup_ref[...], preferred_element_type=F32) + ab_ref[...]
    la = _log_sigmoid(pre) * (1.0 / GLA_TAU)
    for blk in range(bb * n_chunks):
        rows = slice(blk * L, (blk + 1) * L)
        bc_scr[rows, :] = jnp.dot(tril, la[rows], preferred_element_type=F32,
                                  precision=lax.Precision.HIGHEST)

    stack = lambda x: jnp.concatenate([x * head_mask[0], x * head_mask[1]], axis=0).astype(BF16)

    def compute(items):
        units = []
        for j, c in items:
            start = (j * n_chunks + c) * L
            rows = pl.ds(start if isinstance(start, int) else pl.multiple_of(start, L), L)
            for p in range(GLA_HEADS // 2):
                kcols = slice(p * LANES, (p + 1) * LANES)
                units.append(dict(
                    j=j, p=p, rows=rows, qp=q_ref[rows, kcols] * (GLA_DK ** -0.5), kp=k_ref[rows, kcols],
                    bc=bc_scr[rows, kcols], s_old=s_ref[j, p],
                    v_pair=v_ref[rows, 2 * p * GLA_DV:2 * (p + 1) * GLA_DV].astype(BF16)))
        for d in units:
            qp, kp, bc, v_pair = d["qp"], d["kp"], d["bc"], d["v_pair"]
            d["inter"] = jnp.dot(stack(qp * jnp.exp(bc)), d["s_old"].astype(BF16), preferred_element_type=F32)
            d["att"] = []
            for i in range(L // sub):
                lo, hi = i * sub, (i + 1) * sub
                c_i = bc[lo - 1:lo] if i > 0 else jnp.zeros((1, LANES), F32)
                qe = qp[lo:hi] * jnp.exp(bc[lo:hi] - c_i)
                ke = kp[:hi] * jnp.exp(c_i - bc[:hi])
                d["att"].append(_dot_nt(stack(qe), ke.astype(BF16)))
            bl = bc[L - 1:L]
            v_stack = jnp.concatenate([v_pair[:, :GLA_DV], v_pair[:, GLA_DV:]], axis=0)
            d["s_upd"] = _dot_tn(stack(kp * jnp.exp(bl - bc)), v_stack)
            d["dec_col"] = jnp.sum(jnp.where(eye, jnp.exp(bl), 0.0), axis=1, keepdims=True)
        for d in units:
            d["intra"] = []
            for i, att in enumerate(d["att"]):
                lo, hi = i * sub, (i + 1) * sub
                t_idx = lo + lax.broadcasted_iota(jnp.int32, (2 * sub, hi), 0) % sub
                s_idx = lax.broadcasted_iota(jnp.int32, (2 * sub, hi), 1)
                att = jnp.where(s_idx <= t_idx, att, 0.0)
                d["intra"].append(jnp.dot(att.astype(BF16), d["v_pair"][:hi], preferred_element_type=F32))
        stores = []
        for d in units:
            for e in range(2):
                h = 2 * d["p"] + e
                vcols = slice(h * GLA_DV, (h + 1) * GLA_DV)
                o = d["inter"][e * L:(e + 1) * L] + jnp.concatenate(
                    [blk[e * sub:(e + 1) * sub, e * GLA_DV:(e + 1) * GLA_DV] for blk in d["intra"]], axis=0)
                stores.append((y_ref, (d["rows"], vcols), o))
            stores.append((s_ref, (d["j"], d["p"]), d["dec_col"] * d["s_old"] + d["s_upd"]))
        return stores

    def commit(stores):
        for ref, idx, val in stores:
            ref[idx] = val

    def chunk_body(j, c, carry):
        commit(compute([(j, c)]))
        return carry

    if n_chunks == 1:
        commit(compute([(j, 0) for j in range(bb)]))
    else:
        for j in range(bb):
            lax.fori_loop(0, n_chunks, functools.partial(chunk_body, j), 0)

    for h in range(GLA_HEADS):
        vcols = slice(h * GLA_DV, (h + 1) * GLA_DV)
        o = y_ref[:, vcols]
        on = o * lax.rsqrt(jnp.mean(o * o, -1, keepdims=True) + EPS) * lng_ref[:, vcols]
        gate = g_ref[:, vcols]
        y_ref[:, vcols] = on * (gate * jax.nn.sigmoid(gate))


def gla_mixer(z, s0, a_up, a_b, ln_g, *, batch, seq_len, tb, bb, time_major):
    m_rows = batch * seq_len
    L = math.gcd(seq_len, CHUNK)
    n_tb = seq_len // tb
    pairs = GLA_HEADS // 2
    assert bb == 1 or (n_tb == 1 and not time_major)
    rows = bb * tb
    if time_major:
        zblk = lambda width, off: pl.BlockSpec((tb, width), lambda b, t: (t, (b * Z1_COLS + off) // width))
    else:
        zblk = lambda width, off: pl.BlockSpec((rows, width), lambda b, t: (b * n_tb + t, off // width))
    s_spec = pl.BlockSpec((bb, pairs, 2 * GLA_DK, GLA_DV), lambda b, t: (b, 0, 0, 0))
    const = lambda b, t: (0, 0)
    a_up_p = jnp.pad(a_up, ((0, LANES - GLA_GATE_RANK), (0, 0))).astype(BF16)
    y, s_new = pl.pallas_call(
        functools.partial(_gla_kernel, chunk=L, n_chunks=tb // L, bb=bb),
        grid=(batch // bb, n_tb),
        in_specs=[zblk(GLA_QK, Z1_Q), zblk(GLA_QK, Z1_K), zblk(HALF, Z1_V), zblk(HALF, Z1_G),
                  zblk(LANES, Z1_AD),
                  pl.BlockSpec((LANES, GLA_QK), const), pl.BlockSpec((1, GLA_QK), const),
                  pl.BlockSpec((1, HALF), const), s_spec],
        out_specs=[pl.BlockSpec((rows, HALF), lambda b, t: (b * n_tb + t, 0)), s_spec],
        out_shape=[jax.ShapeDtypeStruct((m_rows, HALF), F32),
                   jax.ShapeDtypeStruct((batch, pairs, 2 * GLA_DK, GLA_DV), F32)],
        scratch_shapes=[pltpu.VMEM((rows, GLA_QK), F32)],
        compiler_params=_cparams("parallel", "arbitrary"),
        name="gla",
    )(z, z, z, z, z, a_up_p, a_b.reshape(1, GLA_QK), ln_g.reshape(1, HALF),
      s0.reshape(batch, pairs, 2 * GLA_DK, GLA_DV))
    return y, s_new.reshape(batch, GLA_HEADS, GLA_DK, GLA_DV)


S5_KT = LANES // S5_GROUP
S5_SCAN_SPLIT = 2


def _s5_param_kernel(lre_ref, lim_ref, step_ref, bre_ref, bim_ref, bbre_ref, bbim_ref, pre_ref, pim_ref):
    lre, lim = lre_ref[...], lim_ref[...]
    step = jnp.exp(step_ref[...])
    mag = jnp.exp(lre * step)
    bar_re = mag * jnp.cos(lim * step)
    bar_im = mag * jnp.sin(lim * step)
    inv = 1.0 / (lre * lre + lim * lim)
    cre = ((bar_re - 1.0) * lre + bar_im * lim) * inv
    cim = (bar_im * lre - (bar_re - 1.0) * lim) * inv
    for g in range(S5_GROUPS):
        cr, ci = cre[g:g + 1], cim[g:g + 1]
        bbre_ref[g] = cr * bre_ref[g] - ci * bim_ref[g]
        bbim_ref[g] = cr * bim_ref[g] + ci * bre_ref[g]
    pre_ref[0] = bar_re
    pim_ref[0] = bar_im


def _s5_kernel(u_ref, wbre_ref, wbim_ref, lre_ref, lim_ref, x0re_ref, x0im_ref, wcre_ref, wcim_ref,
               d_ref, wglu_ref, y_ref, xre_ref, xim_ref, sre_scr, sim_scr, *, tt, bb):
    @pl.when(pl.program_id(1) == 0)
    def _():
        xre_ref[...] = x0re_ref[...]
        xim_ref[...] = x0im_ref[...]

    rows_all = tt * bb
    u = u_ref[...].reshape(rows_all, S5_CH)
    ub = u.astype(BF16)
    nblk = S5_CH // LANES
    wide = S5_STATE // nblk
    for kt in range(nblk):
        cols = slice(kt * wide, (kt + 1) * wide)
        ukt = ub[:, kt * LANES:(kt + 1) * LANES]
        sre_scr[:, cols] = jnp.dot(ukt, wbre_ref[kt], preferred_element_type=F32)
        sim_scr[:, cols] = jnp.dot(ukt, wbim_ref[kt], preferred_element_type=F32)

    for part in range(S5_SCAN_SPLIT):
        width = S5_STATE // S5_SCAN_SPLIT
        cols = slice(part * width, (part + 1) * width)
        lr, li = lre_ref[:, cols], lim_ref[:, cols]

        def step(t, carry, cols=cols, lr=lr, li=li):
            cr, ci = carry
            rows = pl.ds(pl.multiple_of(t * bb, bb), bb)
            nr = sre_scr[rows, cols] + (lr * cr - li * ci)
            ni = sim_scr[rows, cols] + (lr * ci + li * cr)
            sre_scr[rows, cols] = nr
            sim_scr[rows, cols] = ni
            return nr, ni

        cr, ci = lax.fori_loop(0, tt, step, (xre_ref[:, cols], xim_ref[:, cols]))
        xre_ref[:, cols] = cr
        xim_ref[:, cols] = ci

    parts = []
    for nt in range(nblk):
        cols = slice(nt * wide, (nt + 1) * wide)
        parts.append(jnp.dot(sre_scr[:, cols].astype(BF16), wcre_ref[nt], preferred_element_type=F32)
                     - jnp.dot(sim_scr[:, cols].astype(BF16), wcim_ref[nt], preferred_element_type=F32))
    y = jnp.concatenate(parts, axis=1) + d_ref[...] * u
    ys = y * (0.5 * (1.0 + jnp.tanh(math.sqrt(2.0 / math.pi) * (y + 0.044715 * (y * y * y)))))
    out = ys * jax.nn.sigmoid(jnp.dot(ys.astype(BF16), wglu_ref[...], preferred_element_type=F32))
    y_ref[...] = out.reshape(tt, bb, S5_CH)


def s5_mixer(u_tm, u_col_block, x0_re, x0_im, lam_re, lam_im, log_step, b_re, b_im, c_re, c_im, d_skip,
             w_glu, *, tt, bb):
    seq_len, batch, _ = u_tm.shape
    gpc = (S5_GROUPS, S5_GROUP, S5_P)
    bb_re, bb_im, p_re, p_im = pl.pallas_call(
        _s5_param_kernel,
        out_shape=[jax.ShapeDtypeStruct(gpc, F32), jax.ShapeDtypeStruct(gpc, F32),
                   jax.ShapeDtypeStruct((1, S5_GROUPS, S5_P), F32),
                   jax.ShapeDtypeStruct((1, S5_GROUPS, S5_P), F32)],
        name="s5_params",
    )(lam_re, lam_im, log_step.reshape(S5_GROUPS, 1), jnp.swapaxes(b_re, 1, 2), jnp.swapaxes(b_im, 1, 2))

    nblk = S5_CH // LANES
    eye = jnp.eye(S5_KT, dtype=F32)

    def in_blocks(w):
        w = w.reshape(nblk, S5_KT, S5_GROUP, 1, S5_P) * eye[None, :, None, :, None]
        return w.reshape(nblk, LANES, S5_KT * S5_P).astype(BF16)

    def out_blocks(c):
        w = jnp.swapaxes(c, 1, 2).reshape(nblk, S5_KT, S5_P, 1, S5_GROUP) * eye[None, :, None, :, None]
        return w.reshape(nblk, S5_KT * S5_P, LANES).astype(BF16)

    lam_rows = lambda p: jnp.broadcast_to(p.reshape(1, S5_STATE), (bb, S5_STATE))
    const2 = lambda b, t: (0, 0)
    const3 = lambda b, t: (0, 0, 0)
    st_spec = pl.BlockSpec((bb, S5_STATE), lambda b, t: (b, 0))
    wb_spec = pl.BlockSpec((nblk, LANES, S5_KT * S5_P), const3)
    wc_spec = pl.BlockSpec((nblk, S5_KT * S5_P, LANES), const3)
    lam_spec = pl.BlockSpec((bb, S5_STATE), const2)
    y, x_re, x_im = pl.pallas_call(
        functools.partial(_s5_kernel, tt=tt, bb=bb),
        grid=(batch // bb, seq_len // tt),
        in_specs=[pl.BlockSpec((tt, bb, S5_CH), lambda b, t: (t, b, u_col_block)),
                  wb_spec, wb_spec, lam_spec, lam_spec, st_spec, st_spec, wc_spec, wc_spec,
                  pl.BlockSpec((1, S5_CH), const2), pl.BlockSpec((S5_CH, S5_CH), const2)],
        out_specs=[pl.BlockSpec((tt, bb, S5_CH), lambda b, t: (t, b, 0)), st_spec, st_spec],
        out_shape=[jax.ShapeDtypeStruct((seq_len, batch, S5_CH), F32),
                   jax.ShapeDtypeStruct((batch, S5_STATE), F32),
                   jax.ShapeDtypeStruct((batch, S5_STATE), F32)],
        scratch_shapes=[pltpu.VMEM((tt * bb, S5_STATE), F32), pltpu.VMEM((tt * bb, S5_STATE), F32)],
        compiler_params=_cparams("parallel", "arbitrary"),
        name="s5",
    )(u_tm, in_blocks(bb_re), in_blocks(bb_im), lam_rows(p_re), lam_rows(p_im),
      x0_re.reshape(batch, S5_STATE), x0_im.reshape(batch, S5_STATE),
      out_blocks(c_re), out_blocks(c_im), d_skip.reshape(1, S5_CH), w_glu.astype(BF16))
    shape = (batch, S5_GROUPS, S5_P)
    return y, x_re.reshape(shape), x_im.reshape(shape)


def _pack_even_w_in(w_in):
    d = w_in.shape[0]
    m_cols = 4 * HALF + 2 * MLSTM_HEADS
    parts = [w_in[:, m_cols:],
             w_in[:, 4 * HALF:m_cols], jnp.zeros((d, Z0_Q - Z0_GATE - 2 * MLSTM_HEADS), w_in.dtype),
             w_in[:, :4 * HALF]]
    return jnp.concatenate(parts, axis=1).astype(BF16)


def _pack_odd_w_in(w_in):
    d = w_in.shape[0]
    gla_main = 2 * GLA_QK + 2 * HALF
    gla_cols = gla_main + GLA_GATE_RANK
    parts = [w_in[:, :gla_main], w_in[:, gla_cols:], w_in[:, gla_main:gla_cols]]
    w = jnp.concatenate(parts, axis=1)
    return jnp.pad(w, ((0, 0), (0, Z1_COLS - w.shape[1]))).astype(BF16)


def _trunk(x, st, prm, *, batch, seq_len):
    (st_C, st_n, st_m, st_S, st_shift, st_gla, st_re, st_im, st_conv) = st
    (norm_mix, norm_ffn, norm_final,
     e_w_in, e_b_if, e_mu, e_w0, e_w_up, e_a0, e_a_up, e_g_up, e_k_k, e_k_a, e_r_k, e_ln_m_g, e_ln_r_g,
     e_ln_r_b, e_w_out,
     o_w_in, o_a_up, o_a_b, o_ln_g, o_lam_re, o_lam_im, o_log_step, o_b_re, o_b_im, o_c_re, o_c_im, o_d,
     o_w_glu, o_w_out,
     f_w_up, f_conv_w, f_conv_b, f_w_down) = prm
    assert len(e_w_in) == 1 and len(o_w_in) == 1 and len(f_w_up) == 2, "wired for depth 2"
    long_seq = seq_len >= 512
    assert long_seq or seq_len == SUBLANES
    m_rows = batch * seq_len
    tiles = dict(
        tm_proj=min(m_rows, 1024),
        tm_tok=512 if long_seq else 256,
        tm_ffn=512 if long_seq else 128,
        tb=512 if long_seq else seq_len,
        bb_chunked=1 if long_seq else 4,
        tb_rwkv=64 if long_seq else seq_len,
        tt_s5=128 if long_seq else seq_len,
    )
    x = x.reshape(m_rows, D_MODEL)
    ffn = functools.partial(conv_ffn, batch=batch, seq_len=seq_len, tm=tiles["tm_ffn"])

    z0 = norm_matmul(x, norm_mix[0], _pack_even_w_in(e_w_in[0]), tm=tiles["tm_proj"], tn=512)
    y_m, c_new, n_new, m_new = mlstm_mixer(z0, e_b_if[0], st_C[0], st_n[0], st_m[0], e_ln_m_g[0],
                                           batch=batch, seq_len=seq_len, tb=tiles["tb"], bb=tiles["bb_chunked"])
    y_r, s_new = rwkv_mixer(z0, st_S[0], st_shift[0], e_mu[0], e_w0[0], e_w_up[0], e_a0[0], e_a_up[0],
                            e_g_up[0], e_k_k[0], e_k_a[0], e_r_k[0], e_ln_r_g[0], e_ln_r_b[0],
                            batch=batch, seq_len=seq_len, tm=tiles["tm_tok"], bb=SUBLANES,
                            tb=tiles["tb_rwkv"])
    if not long_seq:
        y_r = jnp.swapaxes(y_r, 0, 1).reshape(m_rows, HALF)
    shift_new = z0.reshape(batch, seq_len, Z0_COLS)[:, -1, Z0_RWKV:Z0_RWKV + RWKV_COLS]
    w_out = e_w_out[0].astype(BF16)
    x = out_proj(x, y_m, y_r, w_out[:HALF], w_out[HALF:], tm=tiles["tm_tok"])
    x, conv_new0 = ffn(x, norm_ffn[0], f_w_up[0].astype(BF16), f_conv_w[0], f_conv_b[0],
                       f_w_down[0].astype(BF16), st_conv[0], norm_final, final_norm=False)

    z1 = norm_matmul(x, norm_mix[1], _pack_odd_w_in(o_w_in[0]), tm=tiles["tm_proj"], tn=512,
                     time_major_seq_len=seq_len if long_seq else None)
    y_g, gla_new = gla_mixer(z1, st_gla[0], o_a_up[0], o_a_b[0], o_ln_g[0],
                             batch=batch, seq_len=seq_len, tb=tiles["tb"], bb=tiles["bb_chunked"],
                             time_major=long_seq)
    if long_seq:
        u_tm, u_col_block = z1.reshape(seq_len, batch, Z1_COLS), Z1_U // S5_CH
    else:
        u_tm = jnp.swapaxes(z1[:, Z1_U:Z1_U + S5_CH].reshape(batch, seq_len, S5_CH), 0, 1)
        u_col_block = 0
    y_s, re_new, im_new = s5_mixer(u_tm, u_col_block, st_re[0], st_im[0], o_lam_re[0], o_lam_im[0],
                                   o_log_step[0], o_b_re[0], o_b_im[0], o_c_re[0], o_c_im[0], o_d[0],
                                   o_w_glu[0], tt=tiles["tt_s5"], bb=SUBLANES)
    if not long_seq:
        y_s = jnp.swapaxes(y_s, 0, 1).reshape(m_rows, S5_CH)
    w_out = o_w_out[0].astype(BF16)
    x = out_proj(x, y_g, y_s, w_out[:HALF], w_out[HALF:], tm=tiles["tm_tok"])
    x, conv_new1 = ffn(x, norm_ffn[1], f_w_up[1].astype(BF16), f_conv_w[1], f_conv_b[1],
                       f_w_down[1].astype(BF16), st_conv[1], norm_final, final_norm=True)
    new_state = (c_new[None], n_new[None], m_new[None], s_new[None], shift_new[None],
                 gla_new[None], re_new[None], im_new[None], jnp.stack([conv_new0, conv_new1]))
    return x.reshape(batch, seq_len, D_MODEL), new_state


def kernel(x_prompt, x_sample, state_mlstm_C, state_mlstm_n, state_mlstm_m, state_rwkv_S, state_rwkv_shift,
           state_gla_S, state_s5_re, state_s5_im, state_ffn_conv,
           norm_mix, norm_ffn, norm_final,
           e_w_in, e_b_if, e_mu, e_w0, e_w_up, e_a0, e_a_up, e_g_up, e_k_k, e_k_a, e_r_k,
           e_ln_m_g, e_ln_r_g, e_ln_r_b, e_w_out,
           o_w_in, o_a_up, o_a_b, o_ln_g, o_lam_re, o_lam_im, o_log_step, o_b_re, o_b_im, o_c_re, o_c_im,
           o_d, o_w_glu, o_w_out,
           f_w_up, f_conv_w, f_conv_b, f_w_down):
    params = (norm_mix, norm_ffn, norm_final,
              e_w_in, e_b_if, e_mu, e_w0, e_w_up, e_a0, e_a_up, e_g_up, e_k_k, e_k_a, e_r_k,
              e_ln_m_g, e_ln_r_g, e_ln_r_b, e_w_out,
              o_w_in, o_a_up, o_a_b, o_ln_g, o_lam_re, o_lam_im, o_log_step, o_b_re, o_b_im, o_c_re, o_c_im,
              o_d, o_w_glu, o_w_out,
              f_w_up, f_conv_w, f_conv_b, f_w_down)
    bp, tp, _ = x_prompt.shape
    bs, ts, _ = x_sample.shape
    n_even, n_odd, depth = state_mlstm_C.shape[0], state_gla_S.shape[0], state_ffn_conv.shape[0]
    init = (jnp.zeros((n_even, bp, MLSTM_HEADS, MLSTM_HD, MLSTM_HD), F32),
            jnp.zeros((n_even, bp, MLSTM_HEADS, MLSTM_HD), F32),
            jnp.full((n_even, bp, MLSTM_HEADS), M_INIT, F32),
            jnp.zeros((n_even, bp, RWKV_HEADS, RWKV_HD, RWKV_HD), F32),
            jnp.zeros((n_even, bp, RWKV_COLS), F32),
            jnp.zeros((n_odd, bp, GLA_HEADS, GLA_DK, GLA_DV), F32),
            jnp.zeros((n_odd, bp, S5_GROUPS, S5_P), F32),
            jnp.zeros((n_odd, bp, S5_GROUPS, S5_P), F32),
            jnp.zeros((depth, bp, CONV_W - 1, 2 * D_FF), F32))
    past = (state_mlstm_C, state_mlstm_n, state_mlstm_m, state_rwkv_S, state_rwkv_shift,
            state_gla_S, state_s5_re, state_s5_im, state_ffn_conv)
    y_prompt, p_state = _trunk(x_prompt, init, params, batch=bp, seq_len=tp)
    y_sample, s_state = _trunk(x_sample, past, params, batch=bs, seq_len=ts)
    return (y_prompt, y_sample) + tuple(p_state) + tuple(s_state)
```

```python
import functools
import math

import jax
import jax.numpy as jnp
from jax import lax
from jax.experimental import pallas as pl
from jax.experimental.pallas import tpu as pltpu

F32 = jnp.float32
BF16 = jnp.bfloat16

D_MODEL = 1024
HALF = D_MODEL // 2
MLSTM_HEADS = 4
MLSTM_HD = HALF // MLSTM_HEADS
RWKV_HD = 64
RWKV_HEADS = HALF // RWKV_HD
RWKV_W_LORA = 64
RWKV_A_LORA = 64
RWKV_G_LORA = 128
RWKV_COLS = 3 * HALF + RWKV_W_LORA + RWKV_A_LORA + RWKV_G_LORA
GLA_HEADS = 4
GLA_DK = 64
GLA_DV = 128
GLA_QK = GLA_HEADS * GLA_DK
GLA_GATE_RANK = 16
GLA_TAU = 16.0
S5_CH = HALF
S5_GROUP = 16
S5_GROUPS = S5_CH // S5_GROUP
S5_P = 64
S5_STATE = S5_GROUPS * S5_P
D_FF = 2816
CONV_W = 3
CHUNK = 64
EPS = 1e-6
RWKV_LN_EPS = 64e-5
M_INIT = -1e30

LANES = 128
SUBLANES = 8
VMEM_LIMIT_BYTES = 56 * 1024 * 1024

Z0_RWKV = 0
Z0_GATE = RWKV_COLS
Z0_Q = Z0_GATE + 2 * LANES
Z0_K, Z0_V, Z0_O = Z0_Q + HALF, Z0_Q + 2 * HALF, Z0_Q + 3 * HALF
Z0_COLS = Z0_Q + 4 * HALF
RW_LORA = 3 * HALF
RW_G = RW_LORA + RWKV_W_LORA + RWKV_A_LORA
Z1_Q, Z1_K, Z1_V, Z1_G, Z1_U = 0, GLA_QK, 2 * GLA_QK, 2 * GLA_QK + HALF, 2 * GLA_QK + 2 * HALF
Z1_AD = Z1_U + S5_CH
Z1_COLS = 2560


def _cparams(*sem):
    return pltpu.CompilerParams(dimension_semantics=sem, vmem_limit_bytes=VMEM_LIMIT_BYTES)


def _rms(x, g):
    return x * lax.rsqrt(jnp.mean(x * x, -1, keepdims=True) + EPS) * g


def _norm_matmul_kernel(x_ref, g_ref, w_ref, o_ref, *, tn):
    h = _rms(x_ref[...], g_ref[...]).astype(BF16)
    for j in range(w_ref.shape[1] // tn):
        cols = slice(j * tn, (j + 1) * tn)
        o_ref[:, cols] = jnp.dot(h, w_ref[:, cols], preferred_element_type=F32)


def norm_matmul(x, g, w, *, tm, tn, time_major_seq_len=None):
    m, d = x.shape
    n = w.shape[1]
    if time_major_seq_len is None:
        out_shape = (m, n)
        out_spec = pl.BlockSpec((tm, n), lambda i: (i, 0))
    else:
        tiles_per_seq = time_major_seq_len // tm
        out_shape = (time_major_seq_len, (m // time_major_seq_len) * n)
        out_spec = pl.BlockSpec((tm, n), lambda i: (i % tiles_per_seq, i // tiles_per_seq))
    return pl.pallas_call(
        functools.partial(_norm_matmul_kernel, tn=tn),
        grid=(m // tm,),
        in_specs=[pl.BlockSpec((tm, d), lambda i: (i, 0)),
                  pl.BlockSpec((1, d), lambda i: (0, 0)),
                  pl.BlockSpec((d, n), lambda i: (0, 0), pipeline_mode=pl.Buffered(1))],
        out_specs=out_spec,
        out_shape=jax.ShapeDtypeStruct(out_shape, F32),
        compiler_params=_cparams("parallel"),
        name="norm_matmul",
    )(x, g.reshape(1, d), w)


def _out_proj_kernel(x_ref, ya_ref, yb_ref, wa_ref, wb_ref, o_ref):
    acc = jnp.dot(ya_ref[...].astype(BF16), wa_ref[...], preferred_element_type=F32)
    acc += jnp.dot(yb_ref[...].astype(BF16), wb_ref[...], preferred_element_type=F32)
    o_ref[...] = x_ref[...] + acc


def out_proj(x, ya, yb, wa, wb, *, tm):
    m, d = x.shape
    k = ya.shape[1]
    if yb.ndim == 3:
        tiles_per_seq = yb.shape[0] // tm
        yb = yb.reshape(yb.shape[0], -1)
        yb_spec = pl.BlockSpec((tm, k), lambda i: (i % tiles_per_seq, i // tiles_per_seq))
    else:
        yb_spec = pl.BlockSpec((tm, k), lambda i: (i, 0))
    return pl.pallas_call(
        _out_proj_kernel,
        grid=(m // tm,),
        in_specs=[pl.BlockSpec((tm, d), lambda i: (i, 0)),
                  pl.BlockSpec((tm, k), lambda i: (i, 0)),
                  yb_spec,
                  pl.BlockSpec((k, d), lambda i: (0, 0)),
                  pl.BlockSpec((k, d), lambda i: (0, 0))],
        out_specs=pl.BlockSpec((tm, d), lambda i: (i, 0)),
        out_shape=jax.ShapeDtypeStruct((m, d), F32),
        compiler_params=_cparams("parallel"),
        name="out_proj",
    )(x, ya, yb, wa, wb)


FFN_SUB = 256


def _ffn_kernel(x_ref, g_ref, wup_ref, cw_ref, cb_ref, wdn_ref, st_ref, gf_ref,
                o_ref, tail_ref, h_scr, act_scr, car_scr, *, tm, seq_len, final_norm):
    long_seq = seq_len >= tm
    i = pl.program_id(0)
    h_scr[...] = _rms(x_ref[...], g_ref[...]).astype(BF16)
    row = lax.broadcasted_iota(jnp.int32, (tm, FFN_SUB), 0)
    if long_seq:
        @pl.when((i * tm) % seq_len == 0)
        def _():
            car_scr[...] = st_ref[...]
    else:
        t_in_seq = row % seq_len

    row8 = lax.broadcasted_iota(jnp.int32, (SUBLANES, FFN_SUB), 0)

    def conv_part(col0):
        u = jnp.dot(h_scr[...], wup_ref[:, col0:col0 + FFN_SUB], preferred_element_type=F32)
        p1 = pltpu.roll(u, 1, 0)
        p2 = pltpu.roll(u, 2, 0)
        if long_seq:
            halo = car_scr[:, col0:col0 + FFN_SUB]
            h6 = halo[SUBLANES - 2:SUBLANES - 1]
            h7 = halo[SUBLANES - 1:SUBLANES]
            p1 = jnp.concatenate([jnp.where(row8 == 0, h7, p1[:SUBLANES]), p1[SUBLANES:]], axis=0)
            p2 = jnp.concatenate([jnp.where(row8 == 0, h6, jnp.where(row8 == 1, h7, p2[:SUBLANES])),
                                  p2[SUBLANES:]], axis=0)
            car_scr[:, col0:col0 + FFN_SUB] = u[tm - SUBLANES:]
            tail_ref[:, col0:col0 + FFN_SUB] = u[tm - SUBLANES:]
        else:
            e = st_ref[:, col0:col0 + FFN_SUB]
            p1 = jnp.where(t_in_seq == 0, pltpu.roll(e, tm - 1, 0), p1)
            p2 = jnp.where(t_in_seq < 2, e, p2)
            tail_ref[:, col0:col0 + FFN_SUB] = u
        cw = cw_ref[:, col0:col0 + FFN_SUB]
        return (cb_ref[:, col0:col0 + FFN_SUB]
                + (cw[0:1] * p2 + cw[1:2] * p1 + cw[2:3] * u))

    for c in range(D_FF // FFN_SUB):
        val = conv_part(c * FFN_SUB)
        gate = conv_part(D_FF + c * FFN_SUB)
        act_scr[:, c * FFN_SUB:(c + 1) * FFN_SUB] = (val * (gate * jax.nn.sigmoid(gate))).astype(BF16)

    y = x_ref[...] + jnp.dot(act_scr[...], wdn_ref[...], preferred_element_type=F32)
    if final_norm:
        y = _rms(y, gf_ref[...])
    o_ref[...] = y


def conv_ffn(x, g, w_up, conv_w, conv_b, w_down, conv0, gf, *, batch, seq_len, tm, final_norm):
    m, d = x.shape
    f2 = 2 * D_FF
    long_seq = seq_len >= tm
    if long_seq:
        assert seq_len % tm == 0
        tiles_per_seq = seq_len // tm
        st = jnp.pad(conv0, ((0, 0), (SUBLANES - (CONV_W - 1), 0), (0, 0)))
        st_spec = pl.BlockSpec((None, SUBLANES, f2), lambda i: (i // tiles_per_seq, 0, 0))
        tail_shape = jax.ShapeDtypeStruct((batch, SUBLANES, f2), F32)
        tail_spec = pl.BlockSpec((None, SUBLANES, f2), lambda i: (i // tiles_per_seq, 0, 0))
        sem = "arbitrary"
    else:
        assert seq_len == SUBLANES and tm % seq_len == 0
        st = jnp.pad(conv0, ((0, 0), (0, seq_len - (CONV_W - 1)), (0, 0))).reshape(m, f2)
        st_spec = pl.BlockSpec((tm, f2), lambda i: (i, 0))
        tail_shape = jax.ShapeDtypeStruct((m, f2), F32)
        tail_spec = pl.BlockSpec((tm, f2), lambda i: (i, 0))
        sem = "parallel"
    const = lambda i: (0, 0)
    out, tail = pl.pallas_call(
        functools.partial(_ffn_kernel, tm=tm, seq_len=seq_len, final_norm=final_norm),
        grid=(m // tm,),
        in_specs=[pl.BlockSpec((tm, d), lambda i: (i, 0)),
                  pl.BlockSpec((1, d), const),
                  pl.BlockSpec((d, f2), const, pipeline_mode=pl.Buffered(1)),
                  pl.BlockSpec((CONV_W, f2), const),
                  pl.BlockSpec((1, f2), const),
                  pl.BlockSpec((D_FF, d), const, pipeline_mode=pl.Buffered(1)),
                  st_spec,
                  pl.BlockSpec((1, d), const)],
        out_specs=[pl.BlockSpec((tm, d), lambda i: (i, 0)), tail_spec],
        out_shape=[jax.ShapeDtypeStruct((m, d), F32), tail_shape],
        scratch_shapes=[pltpu.VMEM((tm, d), BF16),
                        pltpu.VMEM((tm, D_FF), BF16),
                        pltpu.VMEM((SUBLANES, f2), F32)],
        compiler_params=_cparams(sem),
        name="conv_ffn",
    )(x, g.reshape(1, d), w_up, conv_w, conv_b.reshape(1, f2), w_down, st, gf.reshape(1, d))
    new_state = tail.reshape(batch, -1, f2)[:, -(CONV_W - 1):]
    return out, new_state


def _log_sigmoid(x):
    return jnp.minimum(x, 0.0) - jnp.log1p(jnp.exp(-jnp.abs(x)))


def _dot_nt(a, b):
    return lax.dot_general(a, b, (((1,), (1,)), ((), ())), preferred_element_type=F32)


def _dot_tn(a, b):
    return lax.dot_general(a, b, (((0,), (0,)), ((), ())), preferred_element_type=F32)


def _mlstm_kernel(bif_ref, q_ref, k_ref, v_ref, og_ref, gz_ref, gt_ref, c0_ref, n0_ref, m0_ref,
                  lng_ref, y_ref, c_ref, n_ref, m_ref, grow_scr, gcol_scr, bcol_scr, cmax_scr,
                  *, chunk, n_chunks, bb):
    L = chunk

    @pl.when(pl.program_id(1) == 0)
    def _():
        c_ref[...] = c0_ref[...]
        n_ref[...] = n0_ref[...]
        m_ref[...] = m0_ref[...]

    t_idx = lax.broadcasted_iota(jnp.int32, (L, L), 0)
    s_idx = lax.broadcasted_iota(jnp.int32, (L, L), 1)
    causal = s_idx <= t_idx
    scale = MLSTM_HD ** -0.5

    def compute(items):
        ph = []
        for j, c in items:
            start = (j * n_chunks + c) * L
            rows = pl.ds(start if isinstance(start, int) else pl.multiple_of(start, L), L)
            for h in range(MLSTM_HEADS):
                ph.append(gate_phase(j, c, h, rows))
        for d in ph:
            d["qk_raw"] = _dot_nt(d["qb"], d["kb"])
            d["qc"] = _dot_nt(d["qb"], d["c_old"].astype(BF16))
            d["c_upd"] = _dot_tn((d["v"] * d["wl_col"]).astype(BF16), d["kb"])
        for d in ph:
            d["qk"] = d["qk_raw"] * d["w"]
            d["pv"] = jnp.dot(d["qk"].astype(BF16), d["v"].astype(BF16), preferred_element_type=F32)
        stores = []
        for d in ph:
            stores += finish_phase(d)
        return stores

    def entry(j, c, h):
        return (j * n_chunks + c) * MLSTM_HEADS + h

    for j in range(bb):
        for c in range(n_chunks):
            rows = slice((j * n_chunks + c) * L, (j * n_chunks + c + 1) * L)
            for h in range(MLSTM_HEADS):
                b_i = bif_ref[h]
                b_f = bif_ref[MLSTM_HEADS + h]
                i_col = gz_ref[rows, h:h + 1] + b_i
                f_col = _log_sigmoid(gz_ref[rows, MLSTM_HEADS + h:MLSTM_HEADS + h + 1] + b_f)
                i_row = gt_ref[j, h, c:c + 1, :] + b_i
                f_row = _log_sigmoid(gt_ref[j, MLSTM_HEADS + h, c:c + 1, :] + b_f)
                b_col = jnp.sum(jnp.where(causal, f_row, 0.0), axis=1, keepdims=True)
                b_row = jnp.sum(jnp.where(t_idx <= s_idx, f_col, 0.0), axis=0, keepdims=True)
                g_row = i_row - b_row
                e = entry(j, c, h)
                grow_scr[e, 0:1, 0:L] = g_row
                gcol_scr[e, :, 0:1] = i_col - b_col
                bcol_scr[e, :, 0:1] = b_col
                cmax_scr[e, :, 0:1] = jnp.max(jnp.where(causal, g_row, -jnp.inf), axis=1, keepdims=True)

    def gate_phase(j, c, h, rows):
        cols = slice(h * MLSTM_HD, (h + 1) * MLSTM_HD)
        e = entry(j, c, h)
        g_row = grow_scr[e, 0:1, 0:L]
        b_col = bcol_scr[e, :, 0:1]
        m_old = m_ref[j, h]
        mu_col = jnp.maximum(m_old, cmax_scr[e, :, 0:1])
        w = jnp.where(causal, jnp.exp(g_row - mu_col), 0.0)
        s_inter = jnp.exp(m_old - mu_col)
        q = q_ref[rows, cols]
        ks = k_ref[rows, cols] * scale
        mu_last = mu_col[L - 1:L]
        return dict(j=j, h=h, rows=rows, cols=cols, q=q, ks=ks, v=v_ref[rows, cols],
                    qb=q.astype(BF16), kb=ks.astype(BF16), c_old=c_ref[j, h], n_old=n_ref[j, h],
                    w=w, s_inter=s_inter, m_t=b_col + mu_col,
                    wl_col=jnp.exp(gcol_scr[e, :, 0:1] - mu_last),
                    dec=jnp.exp(m_old - mu_last), m_new=b_col[L - 1:L] + mu_last)

    def finish_phase(d):
        j, h, rows, cols = d["j"], d["h"], d["rows"], d["cols"]
        num = d["s_inter"] * d["qc"] + d["pv"]
        den = (d["s_inter"] * jnp.sum(d["q"] * d["n_old"], axis=1, keepdims=True)
               + jnp.sum(d["qk"], axis=1, keepdims=True))
        hh = num / jnp.maximum(jnp.abs(den), jnp.exp(-d["m_t"]))
        return [(c_ref, (j, h), d["dec"] * d["c_old"] + d["c_upd"]),
                (n_ref, (j, h), d["dec"] * d["n_old"] + jnp.sum(d["wl_col"] * d["ks"], axis=0, keepdims=True)),
                (m_ref, (j, h), d["m_new"]),
                (y_ref, (rows, cols), hh)]

    def commit(stores):
        for ref, idx, val in stores:
            ref[idx] = val

    def chunk_body(j, c, carry):
        commit(compute([(j, c)]))
        return carry

    if n_chunks == 1:
        commit(compute([(j, 0) for j in range(bb)]))
    else:
        for j in range(bb):
            lax.fori_loop(0, n_chunks, functools.partial(chunk_body, j), 0)

    for h in range(MLSTM_HEADS):
        cols = slice(h * MLSTM_HD, (h + 1) * MLSTM_HD)
        hh = y_ref[:, cols]
        xc = hh - jnp.mean(hh, -1, keepdims=True)
        hn = xc * lax.rsqrt(jnp.mean(xc * xc, -1, keepdims=True) + EPS) * lng_ref[:, cols]
        y_ref[:, cols] = jax.nn.sigmoid(og_ref[:, cols]) * hn


def mlstm_mixer(z, b_if, c0, n0, m0, ln_g, *, batch, seq_len, tb, bb):
    m_rows = batch * seq_len
    L = math.gcd(seq_len, CHUNK)
    n_tb = seq_len // tb
    n_chunks = tb // L
    assert bb == 1 or n_tb == 1
    rows = bb * tb
    n_entries = bb * n_chunks * MLSTM_HEADS
    gates = z[:, Z0_GATE:Z0_GATE + 2 * MLSTM_HEADS].reshape(batch, seq_len, 2 * MLSTM_HEADS)
    gates_t = jnp.swapaxes(gates, 1, 2).reshape(batch, 2 * MLSTM_HEADS, seq_len // L, L)
    zcol = lambda off: pl.BlockSpec((rows, HALF), lambda b, t: (b * n_tb + t, off // HALF))
    state4 = lambda a, b_: pl.BlockSpec((bb, MLSTM_HEADS, a, b_), lambda b, t: (b, 0, 0, 0))
    y, c, n, m = pl.pallas_call(
        functools.partial(_mlstm_kernel, chunk=L, n_chunks=n_chunks, bb=bb),
        grid=(batch // bb, n_tb),
        in_specs=[pl.BlockSpec(memory_space=pltpu.SMEM),
                  zcol(Z0_Q), zcol(Z0_K), zcol(Z0_V), zcol(Z0_O),
                  pl.BlockSpec((rows, LANES), lambda b, t: (b * n_tb + t, Z0_GATE // LANES)),
                  pl.BlockSpec((bb, 2 * MLSTM_HEADS, n_chunks, L), lambda b, t: (b, 0, t, 0)),
                  state4(MLSTM_HD, MLSTM_HD), state4(1, MLSTM_HD), state4(1, 1),
                  pl.BlockSpec((1, HALF), lambda b, t: (0, 0))],
        out_specs=[pl.BlockSpec((rows, HALF), lambda b, t: (b * n_tb + t, 0)),
                   state4(MLSTM_HD, MLSTM_HD), state4(1, MLSTM_HD), state4(1, 1)],
        out_shape=[jax.ShapeDtypeStruct((m_rows, HALF), F32),
                   jax.ShapeDtypeStruct((batch, MLSTM_HEADS, MLSTM_HD, MLSTM_HD), F32),
                   jax.ShapeDtypeStruct((batch, MLSTM_HEADS, 1, MLSTM_HD), F32),
                   jax.ShapeDtypeStruct((batch, MLSTM_HEADS, 1, 1), F32)],
        scratch_shapes=[pltpu.VMEM((n_entries, SUBLANES, LANES), F32)]
        + [pltpu.VMEM((n_entries, L, LANES), F32)] * 3,
        compiler_params=_cparams("parallel", "arbitrary"),
        name="mlstm",
    )(b_if, z, z, z, z, z, gates_t,
      c0, n0.reshape(batch, MLSTM_HEADS, 1, MLSTM_HD), m0.reshape(batch, MLSTM_HEADS, 1, 1),
      ln_g.reshape(1, HALF))
    return y, c, n.reshape(batch, MLSTM_HEADS, MLSTM_HD), m.reshape(batch, MLSTM_HEADS)


def _split2(x):
    hi = x.astype(BF16)
    lo = (x - hi.astype(F32)).astype(BF16)
    return hi, lo


def _seg_sum(x, ones_blk):
    hi, lo = _split2(x)
    return (jnp.dot(hi, ones_blk, preferred_element_type=F32)
            + jnp.dot(lo, ones_blk, preferred_element_type=F32))


def _softplus(x):
    return jnp.maximum(x, 0.0) + jnp.log1p(jnp.exp(-jnp.abs(x)))


def _block_ones(n, seg):
    idx = jnp.arange(n) // seg
    return (idx[:, None] == idx[None, :]).astype(BF16)


def _rwkv_prep_kernel(z_ref, st_ref, mu_ref, w0_ref, wup_ref, a0_ref, aup_ref, gup_ref, kk_ref, ka_ref,
                      rk_ref, ones_ref, r_out, w_out, k_out, v_out, kk_out, kka_out, bv_out, g_out,
                      car_scr, *, tm, seq_len):
    zr = z_ref[...]
    row = lax.broadcasted_iota(jnp.int32, zr.shape, 0)
    prev = pltpu.roll(zr, 1, 0)
    if seq_len >= tm:
        @pl.when((pl.program_id(0) * tm) % seq_len == 0)
        def _():
            car_scr[...] = st_ref[...]
        prev = jnp.where(row == 0, car_scr[SUBLANES - 1:SUBLANES], prev)
        car_scr[...] = zr[tm - SUBLANES:]
    else:
        prev = jnp.where(row % seq_len == 0, st_ref[...], prev)
    zs = zr + (prev - zr) * mu_ref[...]
    r = zs[:, :HALF]
    kr = zs[:, HALF:2 * HALF]
    vr = zs[:, 2 * HALF:3 * HALF]
    lora = zs[:, RW_LORA:RW_G]
    ones = ones_ref[...]
    w_log = -_softplus(-(w0_ref[...] + jnp.dot(jnp.tanh(lora).astype(BF16), wup_ref[...],
                                               preferred_element_type=F32))) - 0.5
    a = jax.nn.sigmoid(a0_ref[...] + jnp.dot(lora.astype(BF16), aup_ref[...], preferred_element_type=F32))
    g = jnp.dot(jax.nn.sigmoid(zs[:, RW_G:]).astype(BF16), gup_ref[...], preferred_element_type=F32)
    kk = kr * kk_ref[...]
    kk = kk * lax.rsqrt(jnp.maximum(_seg_sum(kk * kk, ones), 1e-24))
    k2 = kr * (1.0 + (a - 1.0) * ka_ref[...])
    r_out[...] = r
    w_out[...] = jnp.exp(-jnp.exp(w_log))
    k_out[...] = k2
    v_out[...] = vr
    kk_out[...] = kk
    kka_out[...] = kk * a
    bv_out[...] = _seg_sum(r * k2 * rk_ref[...], ones) * vr
    g_out[...] = g


RWKV_GROUPS = 2


def _rwkv_rec_kernel(r_ref, w_ref, k_ref, v_ref, kk_ref, kka_ref, bv_ref, g_ref, s0_ref, lng_ref, lnb_ref,
                     onesy_ref, ones_ref, y_ref, s_ref, yraw_scr, lhs_scr, ylhs_scr, tmaj_scr=None,
                     *, bb, tb):
    @pl.when(pl.program_id(1) == 0)
    def _():
        s_ref[...] = s0_ref[...]

    n_lane_blk = HALF // LANES
    if tmaj_scr is not None:
        streams = (r_ref, w_ref, k_ref, v_ref, kk_ref, kka_ref, bv_ref, g_ref)
        for i, ref in enumerate(streams):
            for t in range(tb):
                for c in range(n_lane_blk):
                    tmaj_scr[i, t, :, c * LANES:(c + 1) * LANES] = ref[
                        pl.ds(t * n_lane_blk + c, bb, stride=tb * n_lane_blk), :]
        r_ref, w_ref, k_ref, v_ref, kk_ref, kka_ref, bv_ref, g_ref = (
            tmaj_scr.at[i] for i in range(len(streams)))

    pairs = RWKV_HEADS // 2
    n_tiles = bb * pairs
    rows_all = n_tiles * RWKV_HD
    shape = (RWKV_HD, LANES)
    eye2 = (lax.broadcasted_iota(jnp.int32, shape, 0)
            == lax.broadcasted_iota(jnp.int32, shape, 1) % RWKV_HD)
    eye2_swapped = ((lax.broadcasted_iota(jnp.int32, shape, 0) ^ 1)
                    == lax.broadcasted_iota(jnp.int32, shape, 1) % RWKV_HD)
    lane_even = lax.broadcasted_iota(jnp.int32, (1, LANES), 1) % 2 == 0
    y_rows = lambda idx: slice((idx // 2) * RWKV_HD, (idx // 2 + 1) * RWKV_HD)
    y_cols = lambda idx: slice((idx % 2) * LANES, (idx % 2 + 1) * LANES)

    per_group = n_tiles // RWKV_GROUPS

    def step(t, carry):
        def tile(idx):
            b, p = divmod(idx, pairs)
            cols = slice(p * LANES, (p + 1) * LANES)
            return b, p, (lambda ref: ref[t, b:b + 1, cols])

        groups = [range(g * per_group, (g + 1) * per_group) for g in range(RWKV_GROUPS)]
        g_rows = per_group * RWKV_HD * 3 // 2
        sums = []
        for g, grp in enumerate(groups):
            for idx in grp:
                b, p, row = tile(idx)
                rel = idx - grp[0]
                base = g * g_rows + rel * RWKV_HD
                hi, lo = _split2(s_ref[b, p] * row(kk_ref))
                lhs_scr[base:base + RWKV_HD, :LANES] = hi
                lhs_scr[base:base + RWKV_HD, LANES:] = lo
                v_row = row(v_ref)
                v_hi = v_row.astype(BF16).astype(F32)
                v_lo = v_row - v_hi
                v_lo_swapped = jnp.where(lane_even, pltpu.roll(v_lo, LANES - 1, 1), pltpu.roll(v_lo, 1, 1))
                v_base = g * g_rows + per_group * RWKV_HD + (rel // 2) * RWKV_HD
                lhs_scr[v_base:v_base + RWKV_HD, y_cols(rel)] = jnp.where(
                    eye2, v_hi, jnp.where(eye2_swapped, v_lo_swapped, 0.0)).astype(BF16)
            sums.append(jnp.dot(lhs_scr[g * g_rows:(g + 1) * g_rows, :], onesy_ref[...],
                                preferred_element_type=F32))
        ysums = []
        for grp, sm in zip(groups, sums):
            for idx in grp:
                b, p, row = tile(idx)
                rel = idx - grp[0]
                sa_rows = slice(rel * RWKV_HD, (rel + 1) * RWKV_HD)
                sa = sm[sa_rows, :LANES] + sm[sa_rows, LANES:]
                v_base = per_group * RWKV_HD + (rel // 2) * RWKV_HD
                v_tile = sm[v_base:v_base + RWKV_HD, y_cols(rel)]
                s = s_ref[b, p] * row(w_ref) - sa * row(kka_ref) + v_tile * row(k_ref)
                s_ref[b, p] = s
                ylhs_scr[y_rows(idx), y_cols(idx)] = (s * row(r_ref)).astype(BF16)
            g_rows = slice(y_rows(grp[0]).start, y_rows(grp[-1]).stop)
            ysums.append(jnp.dot(ylhs_scr[g_rows, :], onesy_ref[...], preferred_element_type=F32))
        for grp, ys in zip(groups, ysums):
            for idx in grp:
                b, p, row = tile(idx)
                rel = slice(y_rows(idx).start - y_rows(grp[0]).start, y_rows(idx).stop - y_rows(grp[0]).start)
                yraw_scr[t, b:b + 1, p * LANES:(p + 1) * LANES] = jnp.sum(
                    jnp.where(eye2, ys[rel, y_cols(idx)], 0.0), axis=0, keepdims=True)
        return carry

    lax.fori_loop(0, tb, step, 0)

    ones = ones_ref[...]
    y = yraw_scr[...].reshape(tb * bb, HALF)
    xc = y - _seg_sum(y, ones) * (1.0 / RWKV_HD)
    var = _seg_sum(xc * xc, ones) * (1.0 / RWKV_HD)
    yn = xc * lax.rsqrt(var + RWKV_LN_EPS) * lng_ref[...] + lnb_ref[...]
    out = (yn + bv_ref[...].reshape(tb * bb, HALF)) * g_ref[...].reshape(tb * bb, HALF)
    if tmaj_scr is None:
        y_ref[...] = out.reshape(tb, bb, HALF)
    else:
        for t in range(tb):
            for c in range(n_lane_blk):
                y_ref[pl.ds(t * n_lane_blk + c, bb, stride=tb * n_lane_blk), :] = out[
                    t * bb:(t + 1) * bb, c * LANES:(c + 1) * LANES]


def rwkv_mixer(z, s0, shift0, mu, w0, w_up, a0, a_up, g_up, k_k, k_a, r_k, ln_g, ln_b,
               *, batch, seq_len, tm, bb, tb):
    m_rows = batch * seq_len
    tm_shape = (seq_len, batch, HALF)
    if seq_len >= tm:
        tiles_per_seq = seq_len // tm
        st = jnp.pad(shift0[:, None], ((0, 0), (SUBLANES - 1, 0), (0, 0)))
        st_spec = pl.BlockSpec((None, SUBLANES, RWKV_COLS), lambda i: (i // tiles_per_seq, 0, 0))
        sem = "arbitrary"
        tok = pl.BlockSpec((tm, HALF), lambda i: (i % tiles_per_seq, i // tiles_per_seq))
        tok_shape = (seq_len, batch * HALF)
    else:
        st = jnp.pad(shift0[:, None], ((0, 0), (0, seq_len - 1), (0, 0))).reshape(m_rows, RWKV_COLS)
        st_spec = pl.BlockSpec((tm, RWKV_COLS), lambda i: (i, 0))
        sem = "parallel"
        tok = pl.BlockSpec((tm, HALF), lambda i: (i, 0))
        tok_shape = (m_rows, HALF)
    row = lambda a: a.reshape(1, -1)
    const = lambda i: (0, 0)
    vec = pl.BlockSpec((1, HALF), const)
    lora_rows = RWKV_W_LORA + RWKV_A_LORA
    w_up_p = jnp.pad(w_up, ((0, RWKV_A_LORA), (0, 0))).astype(BF16)
    a_up_p = jnp.pad(a_up, ((RWKV_W_LORA, 0), (0, 0))).astype(BF16)
    ones_half = _block_ones(HALF, RWKV_HD)
    prep = pl.pallas_call(
        functools.partial(_rwkv_prep_kernel, tm=tm, seq_len=seq_len),
        grid=(m_rows // tm,),
        in_specs=[pl.BlockSpec((tm, RWKV_COLS), lambda i: (i, Z0_RWKV // RWKV_COLS)),
                  st_spec,
                  pl.BlockSpec((1, RWKV_COLS), const),
                  vec, pl.BlockSpec((lora_rows, HALF), const),
                  vec, pl.BlockSpec((lora_rows, HALF), const),
                  pl.BlockSpec((RWKV_G_LORA, HALF), const),
                  vec, vec, vec,
                  pl.BlockSpec((HALF, HALF), const)],
        out_specs=[tok] * 8,
        out_shape=[jax.ShapeDtypeStruct(tok_shape, F32)] * 8,
        scratch_shapes=[pltpu.VMEM((SUBLANES, RWKV_COLS), F32)],
        compiler_params=_cparams(sem),
        name="rwkv_prep",
    )(z, st, row(mu), row(w0), w_up_p, row(a0), a_up_p, g_up.astype(BF16), row(k_k), row(k_a), row(r_k),
      ones_half)
    pairs = RWKV_HEADS // 2
    s_pairs = (s0.reshape(batch, pairs, 2, RWKV_HD, RWKV_HD).transpose(0, 1, 3, 2, 4)
               .reshape(batch, pairs, RWKV_HD, LANES))
    scratch = [pltpu.VMEM((tb, bb, HALF), F32),
               pltpu.VMEM((bb * pairs * RWKV_HD * 3 // 2, 2 * LANES), BF16),
               pltpu.VMEM((bb * pairs * RWKV_HD // 2, 2 * LANES), BF16)]
    if seq_len >= tm:
        seqs = [a.reshape(tm_shape) for a in prep]
        blk = pl.BlockSpec((tb, bb, HALF), lambda b, t: (t, b, 0))
        y_shape = tm_shape
    else:
        assert tb == seq_len
        n_lane_blk = HALF // LANES
        seqs = [a.reshape(m_rows * n_lane_blk, LANES) for a in prep]
        blk = pl.BlockSpec((bb * tb * n_lane_blk, LANES), lambda b, t: (b, 0))
        y_shape = (m_rows * n_lane_blk, LANES)
        scratch.append(pltpu.VMEM((len(prep), tb, bb, HALF), F32))
    s_spec = pl.BlockSpec((bb, pairs, RWKV_HD, LANES), lambda b, t: (b, 0, 0, 0))
    const2 = lambda b, t: (0, 0)
    y, s_new = pl.pallas_call(
        functools.partial(_rwkv_rec_kernel, bb=bb, tb=tb),
        grid=(batch // bb, seq_len // tb),
        in_specs=[blk] * 8 + [s_spec,
                              pl.BlockSpec((1, HALF), const2), pl.BlockSpec((1, HALF), const2),
                              pl.BlockSpec((2 * LANES, 2 * LANES), const2),
                              pl.BlockSpec((HALF, HALF), const2)],
        out_specs=[blk, s_spec],
        out_shape=[jax.ShapeDtypeStruct(y_shape, F32),
                   jax.ShapeDtypeStruct((batch, pairs, RWKV_HD, LANES), F32)],
        scratch_shapes=scratch,
        compiler_params=_cparams("parallel", "arbitrary"),
        name="rwkv_rec",
    )(*seqs, s_pairs, row(ln_g), row(ln_b),
      _block_ones(2 * LANES, RWKV_HD), ones_half)
    s_new = (s_new.reshape(batch, pairs, RWKV_HD, 2, RWKV_HD).transpose(0, 1, 3, 2, 4)
             .reshape(batch, RWKV_HEADS, RWKV_HD, RWKV_HD))
    if seq_len < tm:
        y = y.reshape(m_rows, HALF)
    return y, s_new


GLA_SUB = 16


def _gla_kernel(q_ref, k_ref, v_ref, g_ref, ad_ref, aup_ref, ab_ref, lng_ref, s0_ref,
                y_ref, s_ref, bc_scr, *, chunk, n_chunks, bb):
    L = chunk
    sub = min(GLA_SUB, L)

    @pl.when(pl.program_id(1) == 0)
    def _():
        s_ref[...] = s0_ref[...]

    lane = lax.broadcasted_iota(jnp.int32, (1, LANES), 1)
    head_mask = [(lane < GLA_DK).astype(F32), (lane >= GLA_DK).astype(F32)]
    tril = (lax.broadcasted_iota(jnp.int32, (L, L), 1)
            <= lax.broadcasted_iota(jnp.int32, (L, L), 0)).astype(F32)
    eye = (lax.broadcasted_iota(jnp.int32, (LANES, LANES), 0)
           == lax.broadcasted_iota(jnp.int32, (LANES, LANES), 1))

    pre = jnp.dot(ad_ref[...].astype(BF16), aup_ref[...], preferred_element_type=F32) + ab_ref[...]
    la = _log_sigmoid(pre) * (1.0 / GLA_TAU)
    for blk in range(bb * n_chunks):
        rows = slice(blk * L, (blk + 1) * L)
        bc_scr[rows, :] = jnp.dot(tril, la[rows], preferred_element_type=F32,
                                  precision=lax.Precision.HIGHEST)

    stack = lambda x: jnp.concatenate([x * head_mask[0], x * head_mask[1]], axis=0).astype(BF16)

    def compute(items):
        units = []
        for j, c in items:
            start = (j * n_chunks + c) * L
            rows = pl.ds(start if isinstance(start, int) else pl.multiple_of(start, L), L)
            for p in range(GLA_HEADS // 2):
                kcols = slice(p * LANES, (p + 1) * LANES)
                units.append(dict(
                    j=j, p=p, rows=rows, qp=q_ref[rows, kcols] * (GLA_DK ** -0.5), kp=k_ref[rows, kcols],
                    bc=bc_scr[rows, kcols], s_old=s_ref[j, p],
                    v_pair=v_ref[rows, 2 * p * GLA_DV:2 * (p + 1) * GLA_DV].astype(BF16)))
        for d in units:
            qp, kp, bc, v_pair = d["qp"], d["kp"], d["bc"], d["v_pair"]
            d["inter"] = jnp.dot(stack(qp * jnp.exp(bc)), d["s_old"].astype(BF16), preferred_element_type=F32)
            d["att"] = []
            for i in range(L // sub):
                lo, hi = i * sub, (i + 1) * sub
                c_i = bc[lo - 1:lo] if i > 0 else jnp.zeros((1, LANES), F32)
                qe = qp[lo:hi] * jnp.exp(bc[lo:hi] - c_i)
                ke = kp[:hi] * jnp.exp(c_i - bc[:hi])
                d["att"].append(_dot_nt(stack(qe), ke.astype(BF16)))
            bl = bc[L - 1:L]
            v_stack = jnp.concatenate([v_pair[:, :GLA_DV], v_pair[:, GLA_DV:]], axis=0)
            d["s_upd"] = _dot_tn(stack(kp * jnp.exp(bl - bc)), v_stack)
            d["dec_col"] = jnp.sum(jnp.where(eye, jnp.exp(bl), 0.0), axis=1, keepdims=True)
        for d in units:
            d["intra"] = []
            for i, att in enumerate(d["att"]):
                lo, hi = i * sub, (i + 1) * sub
                t_idx = lo + lax.broadcasted_iota(jnp.int32, (2 * sub, hi), 0) % sub
                s_idx = lax.broadcasted_iota(jnp.int32, (2 * sub, hi), 1)
                att = jnp.where(s_idx <= t_idx, att, 0.0)
                d["intra"].append(jnp.dot(att.astype(BF16), d["v_pair"][:hi], preferred_element_type=F32))
        stores = []
        for d in units:
            for e in range(2):
                h = 2 * d["p"] + e
                vcols = slice(h * GLA_DV, (h + 1) * GLA_DV)
                o = d["inter"][e * L:(e + 1) * L] + jnp.concatenate(
                    [blk[e * sub:(e + 1) * sub, e * GLA_DV:(e + 1) * GLA_DV] for blk in d["intra"]], axis=0)
                stores.append((y_ref, (d["rows"], vcols), o))
            stores.append((s_ref, (d["j"], d["p"]), d["dec_col"] * d["s_old"] + d["s_upd"]))
        return stores

    def commit(stores):
        for ref, idx, val in stores:
            ref[idx] = val

    def chunk_body(j, c, carry):
        commit(compute([(j, c)]))
        return carry

    if n_chunks == 1:
        commit(compute([(j, 0) for j in range(bb)]))
    else:
        for j in range(bb):
            lax.fori_loop(0, n_chunks, functools.partial(chunk_body, j), 0)

    for h in range(GLA_HEADS):
        vcols = slice(h * GLA_DV, (h + 1) * GLA_DV)
        o = y_ref[:, vcols]
        on = o * lax.rsqrt(jnp.mean(o * o, -1, keepdims=True) + EPS) * lng_ref[:, vcols]
        gate = g_ref[:, vcols]
        y_ref[:, vcols] = on * (gate * jax.nn.sigmoid(gate))


def gla_mixer(z, s0, a_up, a_b, ln_g, *, batch, seq_len, tb, bb, time_major):
    m_rows = batch * seq_len
    L = math.gcd(seq_len, CHUNK)
    n_tb = seq_len // tb
    pairs = GLA_HEADS // 2
    assert bb == 1 or (n_tb == 1 and not time_major)
    rows = bb * tb
    if time_major:
        zblk = lambda width, off: pl.BlockSpec((tb, width), lambda b, t: (t, (b * Z1_COLS + off) // width))
    else:
        zblk = lambda width, off: pl.BlockSpec((rows, width), lambda b, t: (b * n_tb + t, off // width))
    s_spec = pl.BlockSpec((bb, pairs, 2 * GLA_DK, GLA_DV), lambda b, t: (b, 0, 0, 0))
    const = lambda b, t: (0, 0)
    a_up_p = jnp.pad(a_up, ((0, LANES - GLA_GATE_RANK), (0, 0))).astype(BF16)
    y, s_new = pl.pallas_call(
        functools.partial(_gla_kernel, chunk=L, n_chunks=tb // L, bb=bb),
        grid=(batch // bb, n_tb),
        in_specs=[zblk(GLA_QK, Z1_Q), zblk(GLA_QK, Z1_K), zblk(HALF, Z1_V), zblk(HALF, Z1_G),
                  zblk(LANES, Z1_AD),
                  pl.BlockSpec((LANES, GLA_QK), const), pl.BlockSpec((1, GLA_QK), const),
                  pl.BlockSpec((1, HALF), const), s_spec],
        out_specs=[pl.BlockSpec((rows, HALF), lambda b, t: (b * n_tb + t, 0)), s_spec],
        out_shape=[jax.ShapeDtypeStruct((m_rows, HALF), F32),
                   jax.ShapeDtypeStruct((batch, pairs, 2 * GLA_DK, GLA_DV), F32)],
        scratch_shapes=[pltpu.VMEM((rows, GLA_QK), F32)],
        compiler_params=_cparams("parallel", "arbitrary"),
        name="gla",
    )(z, z, z, z, z, a_up_p, a_b.reshape(1, GLA_QK), ln_g.reshape(1, HALF),
      s0.reshape(batch, pairs, 2 * GLA_DK, GLA_DV))
    return y, s_new.reshape(batch, GLA_HEADS, GLA_DK, GLA_DV)


S5_KT = LANES // S5_GROUP
S5_SCAN_SPLIT = 2


def _s5_param_kernel(lre_ref, lim_ref, step_ref, bre_ref, bim_ref, bbre_ref, bbim_ref, pre_ref, pim_ref):
    lre, lim = lre_ref[...], lim_ref[...]
    step = jnp.exp(step_ref[...])
    mag = jnp.exp(lre * step)
    bar_re = mag * jnp.cos(lim * step)
    bar_im = mag * jnp.sin(lim * step)
    inv = 1.0 / (lre * lre + lim * lim)
    cre = ((bar_re - 1.0) * lre + bar_im * lim) * inv
    cim = (bar_im * lre - (bar_re - 1.0) * lim) * inv
    for g in range(S5_GROUPS):
        cr, ci = cre[g:g + 1], cim[g:g + 1]
        bbre_ref[g] = cr * bre_ref[g] - ci * bim_ref[g]
        bbim_ref[g] = cr * bim_ref[g] + ci * bre_ref[g]
    pre_ref[0] = bar_re
    pim_ref[0] = bar_im


def _s5_kernel(u_ref, wbre_ref, wbim_ref, lre_ref, lim_ref, x0re_ref, x0im_ref, wcre_ref, wcim_ref,
               d_ref, wglu_ref, y_ref, xre_ref, xim_ref, sre_scr, sim_scr, *, tt, bb, row_major):
    @pl.when(pl.program_id(1) == 0)
    def _():
        xre_ref[...] = x0re_ref[...]
        xim_ref[...] = x0im_ref[...]

    rows_all = tt * bb
    n_lane_blk = S5_CH // LANES
    if row_major:
        row_blks, u_blk0 = Z1_COLS // LANES, Z1_U // LANES
        u = jnp.concatenate(
            [jnp.concatenate([u_ref[pl.ds(t * row_blks + u_blk0 + c, bb, stride=tt * row_blks), :]
                              for c in range(n_lane_blk)], axis=1) for t in range(tt)], axis=0)
    else:
        u = u_ref[...].reshape(rows_all, S5_CH)
    ub = u.astype(BF16)
    nblk = S5_CH // LANES
    wide = S5_STATE // nblk
    for kt in range(nblk):
        cols = slice(kt * wide, (kt + 1) * wide)
        ukt = ub[:, kt * LANES:(kt + 1) * LANES]
        sre_scr[:, cols] = jnp.dot(ukt, wbre_ref[kt], preferred_element_type=F32)
        sim_scr[:, cols] = jnp.dot(ukt, wbim_ref[kt], preferred_element_type=F32)

    for part in range(S5_SCAN_SPLIT):
        width = S5_STATE // S5_SCAN_SPLIT
        cols = slice(part * width, (part + 1) * width)
        lr, li = lre_ref[:, cols], lim_ref[:, cols]

        def step(t, carry, cols=cols, lr=lr, li=li):
            cr, ci = carry
            rows = pl.ds(pl.multiple_of(t * bb, bb), bb)
            nr = sre_scr[rows, cols] + (lr * cr - li * ci)
            ni = sim_scr[rows, cols] + (lr * ci + li * cr)
            sre_scr[rows, cols] = nr
            sim_scr[rows, cols] = ni
            return nr, ni

        cr, ci = lax.fori_loop(0, tt, step, (xre_ref[:, cols], xim_ref[:, cols]))
        xre_ref[:, cols] = cr
        xim_ref[:, cols] = ci

    parts = []
    for nt in range(nblk):
        cols = slice(nt * wide, (nt + 1) * wide)
        parts.append(jnp.dot(sre_scr[:, cols].astype(BF16), wcre_ref[nt], preferred_element_type=F32)
                     - jnp.dot(sim_scr[:, cols].astype(BF16), wcim_ref[nt], preferred_element_type=F32))
    y = jnp.concatenate(parts, axis=1) + d_ref[...] * u
    ys = y * (0.5 * (1.0 + jnp.tanh(math.sqrt(2.0 / math.pi) * (y + 0.044715 * (y * y * y)))))
    out = ys * jax.nn.sigmoid(jnp.dot(ys.astype(BF16), wglu_ref[...], preferred_element_type=F32))
    if row_major:
        for t in range(tt):
            for c in range(n_lane_blk):
                y_ref[pl.ds(t * n_lane_blk + c, bb, stride=tt * n_lane_blk), :] = out[
                    t * bb:(t + 1) * bb, c * LANES:(c + 1) * LANES]
    else:
        y_ref[...] = out.reshape(tt, bb, S5_CH)


def s5_mixer(z, x0_re, x0_im, lam_re, lam_im, log_step, b_re, b_im, c_re, c_im, d_skip,
             w_glu, *, batch, seq_len, tt, bb, time_major):
    u_col_block = Z1_U // S5_CH
    if time_major:
        z = z.reshape(seq_len, batch, Z1_COLS)
        u_spec = pl.BlockSpec((tt, bb, S5_CH), lambda b, t: (t, b, u_col_block))
        y_spec = pl.BlockSpec((tt, bb, S5_CH), lambda b, t: (t, b, 0))
        y_shape = (seq_len, batch, S5_CH)
    else:
        assert tt == seq_len
        row_blks, y_blks = Z1_COLS // LANES, S5_CH // LANES
        z = z.reshape(batch * seq_len * row_blks, LANES)
        u_spec = pl.BlockSpec((bb * tt * row_blks, LANES), lambda b, t: (b, 0))
        y_spec = pl.BlockSpec((bb * tt * y_blks, LANES), lambda b, t: (b, 0))
        y_shape = (batch * seq_len * y_blks, LANES)
    gpc = (S5_GROUPS, S5_GROUP, S5_P)
    bb_re, bb_im, p_re, p_im = pl.pallas_call(
        _s5_param_kernel,
        out_shape=[jax.ShapeDtypeStruct(gpc, F32), jax.ShapeDtypeStruct(gpc, F32),
                   jax.ShapeDtypeStruct((1, S5_GROUPS, S5_P), F32),
                   jax.ShapeDtypeStruct((1, S5_GROUPS, S5_P), F32)],
        name="s5_params",
    )(lam_re, lam_im, log_step.reshape(S5_GROUPS, 1), jnp.swapaxes(b_re, 1, 2), jnp.swapaxes(b_im, 1, 2))

    nblk = S5_CH // LANES
    eye = jnp.eye(S5_KT, dtype=F32)

    def in_blocks(w):
        w = w.reshape(nblk, S5_KT, S5_GROUP, 1, S5_P) * eye[None, :, None, :, None]
        return w.reshape(nblk, LANES, S5_KT * S5_P).astype(BF16)

    def out_blocks(c):
        w = jnp.swapaxes(c, 1, 2).reshape(nblk, S5_KT, S5_P, 1, S5_GROUP) * eye[None, :, None, :, None]
        return w.reshape(nblk, S5_KT * S5_P, LANES).astype(BF16)

    lam_rows = lambda p: jnp.broadcast_to(p.reshape(1, S5_STATE), (bb, S5_STATE))
    const2 = lambda b, t: (0, 0)
    const3 = lambda b, t: (0, 0, 0)
    st_spec = pl.BlockSpec((bb, S5_STATE), lambda b, t: (b, 0))
    wb_spec = pl.BlockSpec((nblk, LANES, S5_KT * S5_P), const3)
    wc_spec = pl.BlockSpec((nblk, S5_KT * S5_P, LANES), const3)
    lam_spec = pl.BlockSpec((bb, S5_STATE), const2)
    y, x_re, x_im = pl.pallas_call(
        functools.partial(_s5_kernel, tt=tt, bb=bb, row_major=not time_major),
        grid=(batch // bb, seq_len // tt),
        in_specs=[u_spec,
                  wb_spec, wb_spec, lam_spec, lam_spec, st_spec, st_spec, wc_spec, wc_spec,
                  pl.BlockSpec((1, S5_CH), const2), pl.BlockSpec((S5_CH, S5_CH), const2)],
        out_specs=[y_spec, st_spec, st_spec],
        out_shape=[jax.ShapeDtypeStruct(y_shape, F32),
                   jax.ShapeDtypeStruct((batch, S5_STATE), F32),
                   jax.ShapeDtypeStruct((batch, S5_STATE), F32)],
        scratch_shapes=[pltpu.VMEM((tt * bb, S5_STATE), F32), pltpu.VMEM((tt * bb, S5_STATE), F32)],
        compiler_params=_cparams("parallel", "arbitrary"),
        name="s5",
    )(z, in_blocks(bb_re), in_blocks(bb_im), lam_rows(p_re), lam_rows(p_im),
      x0_re.reshape(batch, S5_STATE), x0_im.reshape(batch, S5_STATE),
      out_blocks(c_re), out_blocks(c_im), d_skip.reshape(1, S5_CH), w_glu.astype(BF16))
    shape = (batch, S5_GROUPS, S5_P)
    if not time_major:
        y = y.reshape(batch * seq_len, S5_CH)
    return y, x_re.reshape(shape), x_im.reshape(shape)


def _pack_even_w_in(w_in):
    d = w_in.shape[0]
    m_cols = 4 * HALF + 2 * MLSTM_HEADS
    parts = [w_in[:, m_cols:],
             w_in[:, 4 * HALF:m_cols], jnp.zeros((d, Z0_Q - Z0_GATE - 2 * MLSTM_HEADS), w_in.dtype),
             w_in[:, :4 * HALF]]
    return jnp.concatenate(parts, axis=1).astype(BF16)


def _pack_odd_w_in(w_in):
    d = w_in.shape[0]
    gla_main = 2 * GLA_QK + 2 * HALF
    gla_cols = gla_main + GLA_GATE_RANK
    parts = [w_in[:, :gla_main], w_in[:, gla_cols:], w_in[:, gla_main:gla_cols]]
    w = jnp.concatenate(parts, axis=1)
    return jnp.pad(w, ((0, 0), (0, Z1_COLS - w.shape[1]))).astype(BF16)


def _trunk(x, st, prm, *, batch, seq_len):
    (st_C, st_n, st_m, st_S, st_shift, st_gla, st_re, st_im, st_conv) = st
    (norm_mix, norm_ffn, norm_final,
     e_w_in, e_b_if, e_mu, e_w0, e_w_up, e_a0, e_a_up, e_g_up, e_k_k, e_k_a, e_r_k, e_ln_m_g, e_ln_r_g,
     e_ln_r_b, e_w_out,
     o_w_in, o_a_up, o_a_b, o_ln_g, o_lam_re, o_lam_im, o_log_step, o_b_re, o_b_im, o_c_re, o_c_im, o_d,
     o_w_glu, o_w_out,
     f_w_up, f_conv_w, f_conv_b, f_w_down) = prm
    assert len(e_w_in) == 1 and len(o_w_in) == 1 and len(f_w_up) == 2, "wired for depth 2"
    long_seq = seq_len >= 512
    assert long_seq or seq_len == SUBLANES
    m_rows = batch * seq_len
    tiles = dict(
        tm_proj=512,
        tm_tok=512 if long_seq else 256,
        tm_ffn=512 if long_seq else 128,
        tb=512 if long_seq else seq_len,
        bb_chunked=1 if long_seq else 4,
        tb_rwkv=64 if long_seq else seq_len,
        tt_s5=128 if long_seq else seq_len,
    )
    x = x.reshape(m_rows, D_MODEL)
    ffn = functools.partial(conv_ffn, batch=batch, seq_len=seq_len, tm=tiles["tm_ffn"])

    z0 = norm_matmul(x, norm_mix[0], _pack_even_w_in(e_w_in[0]), tm=tiles["tm_proj"], tn=512)
    y_m, c_new, n_new, m_new = mlstm_mixer(z0, e_b_if[0], st_C[0], st_n[0], st_m[0], e_ln_m_g[0],
                                           batch=batch, seq_len=seq_len, tb=tiles["tb"], bb=tiles["bb_chunked"])
    y_r, s_new = rwkv_mixer(z0, st_S[0], st_shift[0], e_mu[0], e_w0[0], e_w_up[0], e_a0[0], e_a_up[0],
                            e_g_up[0], e_k_k[0], e_k_a[0], e_r_k[0], e_ln_r_g[0], e_ln_r_b[0],
                            batch=batch, seq_len=seq_len, tm=tiles["tm_tok"], bb=SUBLANES,
                            tb=tiles["tb_rwkv"])
    shift_new = z0.reshape(batch, seq_len, Z0_COLS)[:, -1, Z0_RWKV:Z0_RWKV + RWKV_COLS]
    w_out = e_w_out[0].astype(BF16)
    x = out_proj(x, y_m, y_r, w_out[:HALF], w_out[HALF:], tm=tiles["tm_tok"])
    x, conv_new0 = ffn(x, norm_ffn[0], f_w_up[0].astype(BF16), f_conv_w[0], f_conv_b[0],
                       f_w_down[0].astype(BF16), st_conv[0], norm_final, final_norm=False)

    z1 = norm_matmul(x, norm_mix[1], _pack_odd_w_in(o_w_in[0]), tm=tiles["tm_proj"], tn=512,
                     time_major_seq_len=seq_len if long_seq else None)
    y_g, gla_new = gla_mixer(z1, st_gla[0], o_a_up[0], o_a_b[0], o_ln_g[0],
                             batch=batch, seq_len=seq_len, tb=tiles["tb"], bb=tiles["bb_chunked"],
                             time_major=long_seq)
    y_s, re_new, im_new = s5_mixer(z1, st_re[0], st_im[0], o_lam_re[0], o_lam_im[0],
                                   o_log_step[0], o_b_re[0], o_b_im[0], o_c_re[0], o_c_im[0], o_d[0],
                                   o_w_glu[0], batch=batch, seq_len=seq_len, tt=tiles["tt_s5"],
                                   bb=SUBLANES, time_major=long_seq)
    w_out = o_w_out[0].astype(BF16)
    x = out_proj(x, y_g, y_s, w_out[:HALF], w_out[HALF:], tm=tiles["tm_tok"])
    x, conv_new1 = ffn(x, norm_ffn[1], f_w_up[1].astype(BF16), f_conv_w[1], f_conv_b[1],
                       f_w_down[1].astype(BF16), st_conv[1], norm_final, final_norm=True)
    new_state = (c_new[None], n_new[None], m_new[None], s_new[None], shift_new[None],
                 gla_new[None], re_new[None], im_new[None], jnp.stack([conv_new0, conv_new1]))
    return x.reshape(batch, seq_len, D_MODEL), new_state


def kernel(x_prompt, x_sample, state_mlstm_C, state_mlstm_n, state_mlstm_m, state_rwkv_S, state_rwkv_shift,
           state_gla_S, state_s5_re, state_s5_im, state_ffn_conv,
           norm_mix, norm_ffn, norm_final,
           e_w_in, e_b_if, e_mu, e_w0, e_w_up, e_a0, e_a_up, e_g_up, e_k_k, e_k_a, e_r_k,
           e_ln_m_g, e_ln_r_g, e_ln_r_b, e_w_out,
           o_w_in, o_a_up, o_a_b, o_ln_g, o_lam_re, o_lam_im, o_log_step, o_b_re, o_b_im, o_c_re, o_c_im,
           o_d, o_w_glu, o_w_out,
           f_w_up, f_conv_w, f_conv_b, f_w_down):
    params = (norm_mix, norm_ffn, norm_final,
              e_w_in, e_b_if, e_mu, e_w0, e_w_up, e_a0, e_a_up, e_g_up, e_k_k, e_k_a, e_r_k,
              e_ln_m_g, e_ln_r_g, e_ln_r_b, e_w_out,
              o_w_in, o_a_up, o_a_b, o_ln_g, o_lam_re, o_lam_im, o_log_step, o_b_re, o_b_im, o_c_re, o_c_im,
              o_d, o_w_glu, o_w_out,
              f_w_up, f_conv_w, f_conv_b, f_w_down)
    bp, tp, _ = x_prompt.shape
    bs, ts, _ = x_sample.shape
    n_even, n_odd, depth = state_mlstm_C.shape[0], state_gla_S.shape[0], state_ffn_conv.shape[0]
    init = (jnp.zeros((n_even, bp, MLSTM_HEADS, MLSTM_HD, MLSTM_HD), F32),
            jnp.zeros((n_even, bp, MLSTM_HEADS, MLSTM_HD), F32),
            jnp.full((n_even, bp, MLSTM_HEADS), M_INIT, F32),
            jnp.zeros((n_even, bp, RWKV_HEADS, RWKV_HD, RWKV_HD), F32),
            jnp.zeros((n_even, bp, RWKV_COLS), F32),
            jnp.zeros((n_odd, bp, GLA_HEADS, GLA_DK, GLA_DV), F32),
            jnp.zeros((n_odd, bp, S5_GROUPS, S5_P), F32),
            jnp.zeros((n_odd, bp, S5_GROUPS, S5_P), F32),
            jnp.zeros((depth, bp, CONV_W - 1, 2 * D_FF), F32))
    past = (state_mlstm_C, state_mlstm_n, state_mlstm_m, state_rwkv_S, state_rwkv_shift,
            state_gla_S, state_s5_re, state_s5_im, state_ffn_conv)
    y_prompt, p_state = _trunk(x_prompt, init, params, batch=bp, seq_len=tp)
    y_sample, s_state = _trunk(x_sample, past, params, batch=bs, seq_len=ts)
    return (y_prompt, y_sample) + tuple(p_state) + tuple(s_state)
```

```python
import functools
import math

import jax
import jax.numpy as jnp
from jax import lax
from jax.experimental import pallas as pl
from jax.experimental.pallas import tpu as pltpu

F32 = jnp.float32
BF16 = jnp.bfloat16

D_MODEL = 1024
HALF = D_MODEL // 2
MLSTM_HEADS = 4
MLSTM_HD = HALF // MLSTM_HEADS
RWKV_HD = 64
RWKV_HEADS = HALF // RWKV_HD
RWKV_W_LORA = 64
RWKV_A_LORA = 64
RWKV_G_LORA = 128
RWKV_COLS = 3 * HALF + RWKV_W_LORA + RWKV_A_LORA + RWKV_G_LORA
GLA_HEADS = 4
GLA_DK = 64
GLA_DV = 128
GLA_QK = GLA_HEADS * GLA_DK
GLA_GATE_RANK = 16
GLA_TAU = 16.0
S5_CH = HALF
S5_GROUP = 16
S5_GROUPS = S5_CH // S5_GROUP
S5_P = 64
S5_STATE = S5_GROUPS * S5_P
D_FF = 2816
CONV_W = 3
CHUNK = 64
EPS = 1e-6
RWKV_LN_EPS = 64e-5
M_INIT = -1e30

LANES = 128
SUBLANES = 8
VMEM_LIMIT_BYTES = 56 * 1024 * 1024

Z0_RWKV = 0
Z0_GATE = RWKV_COLS
Z0_Q = Z0_GATE + 2 * LANES
Z0_K, Z0_V, Z0_O = Z0_Q + HALF, Z0_Q + 2 * HALF, Z0_Q + 3 * HALF
Z0_COLS = Z0_Q + 4 * HALF
RW_LORA = 3 * HALF
RW_G = RW_LORA + RWKV_W_LORA + RWKV_A_LORA
Z1_Q, Z1_K, Z1_V, Z1_G, Z1_U = 0, GLA_QK, 2 * GLA_QK, 2 * GLA_QK + HALF, 2 * GLA_QK + 2 * HALF
Z1_AD = Z1_U + S5_CH
Z1_COLS = 2560


def _cparams(*sem):
    return pltpu.CompilerParams(dimension_semantics=sem, vmem_limit_bytes=VMEM_LIMIT_BYTES)


def _rms(x, g):
    return x * lax.rsqrt(jnp.mean(x * x, -1, keepdims=True) + EPS) * g


def _norm_matmul_kernel(x_ref, g_ref, w_ref, o_ref, *, tn):
    h = _rms(x_ref[...], g_ref[...]).astype(BF16)
    for j in range(w_ref.shape[1] // tn):
        cols = slice(j * tn, (j + 1) * tn)
        o_ref[:, cols] = jnp.dot(h, w_ref[:, cols], preferred_element_type=F32)


def norm_matmul(x, g, w, *, tm, tn, time_major_seq_len=None):
    m, d = x.shape
    n = w.shape[1]
    if time_major_seq_len is None:
        out_shape = (m, n)
        out_spec = pl.BlockSpec((tm, n), lambda i: (i, 0))
    else:
        tiles_per_seq = time_major_seq_len // tm
        out_shape = (time_major_seq_len, (m // time_major_seq_len) * n)
        out_spec = pl.BlockSpec((tm, n), lambda i: (i % tiles_per_seq, i // tiles_per_seq))
    return pl.pallas_call(
        functools.partial(_norm_matmul_kernel, tn=tn),
        grid=(m // tm,),
        in_specs=[pl.BlockSpec((tm, d), lambda i: (i, 0)),
                  pl.BlockSpec((1, d), lambda i: (0, 0)),
                  pl.BlockSpec((d, n), lambda i: (0, 0), pipeline_mode=pl.Buffered(1))],
        out_specs=out_spec,
        out_shape=jax.ShapeDtypeStruct(out_shape, F32),
        compiler_params=_cparams("parallel"),
        name="norm_matmul",
    )(x, g.reshape(1, d), w)


def _out_proj_kernel(x_ref, ya_ref, yb_ref, wa_ref, wb_ref, o_ref):
    acc = jnp.dot(ya_ref[...].astype(BF16), wa_ref[...], preferred_element_type=F32)
    acc += jnp.dot(yb_ref[...].astype(BF16), wb_ref[...], preferred_element_type=F32)
    o_ref[...] = x_ref[...] + acc


def out_proj(x, ya, yb, wa, wb, *, tm):
    m, d = x.shape
    k = ya.shape[1]
    if yb.ndim == 3:
        tiles_per_seq = yb.shape[0] // tm
        yb = yb.reshape(yb.shape[0], -1)
        yb_spec = pl.BlockSpec((tm, k), lambda i: (i % tiles_per_seq, i // tiles_per_seq))
    else:
        yb_spec = pl.BlockSpec((tm, k), lambda i: (i, 0))
    return pl.pallas_call(
        _out_proj_kernel,
        grid=(m // tm,),
        in_specs=[pl.BlockSpec((tm, d), lambda i: (i, 0)),
                  pl.BlockSpec((tm, k), lambda i: (i, 0)),
                  yb_spec,
                  pl.BlockSpec((k, d), lambda i: (0, 0)),
                  pl.BlockSpec((k, d), lambda i: (0, 0))],
        out_specs=pl.BlockSpec((tm, d), lambda i: (i, 0)),
        out_shape=jax.ShapeDtypeStruct((m, d), F32),
        compiler_params=_cparams("parallel"),
        name="out_proj",
    )(x, ya, yb, wa, wb)


FFN_SUB = 256


def _ffn_kernel(x_ref, g_ref, wup_ref, cw_ref, cb_ref, wdn_ref, st_ref, gf_ref,
                o_ref, tail_ref, h_scr, act_scr, car_scr, *, tm, seq_len, final_norm):
    long_seq = seq_len >= tm
    i = pl.program_id(0)
    h_scr[...] = _rms(x_ref[...], g_ref[...]).astype(BF16)
    row = lax.broadcasted_iota(jnp.int32, (tm, FFN_SUB), 0)
    if long_seq:
        @pl.when((i * tm) % seq_len == 0)
        def _():
            car_scr[...] = st_ref[...]
    else:
        t_in_seq = row % seq_len

    row8 = lax.broadcasted_iota(jnp.int32, (SUBLANES, FFN_SUB), 0)

    def conv_part(col0):
        u = jnp.dot(h_scr[...], wup_ref[:, col0:col0 + FFN_SUB], preferred_element_type=F32)
        p1 = pltpu.roll(u, 1, 0)
        p2 = pltpu.roll(u, 2, 0)
        if long_seq:
            halo = car_scr[:, col0:col0 + FFN_SUB]
            h6 = halo[SUBLANES - 2:SUBLANES - 1]
            h7 = halo[SUBLANES - 1:SUBLANES]
            p1 = jnp.concatenate([jnp.where(row8 == 0, h7, p1[:SUBLANES]), p1[SUBLANES:]], axis=0)
            p2 = jnp.concatenate([jnp.where(row8 == 0, h6, jnp.where(row8 == 1, h7, p2[:SUBLANES])),
                                  p2[SUBLANES:]], axis=0)
            car_scr[:, col0:col0 + FFN_SUB] = u[tm - SUBLANES:]
            tail_ref[:, col0:col0 + FFN_SUB] = u[tm - SUBLANES:]
        else:
            e = st_ref[:, col0:col0 + FFN_SUB]
            p1 = jnp.where(t_in_seq == 0, pltpu.roll(e, tm - 1, 0), p1)
            p2 = jnp.where(t_in_seq < 2, e, p2)
            tail_ref[:, col0:col0 + FFN_SUB] = u
        cw = cw_ref[:, col0:col0 + FFN_SUB]
        return (cb_ref[:, col0:col0 + FFN_SUB]
                + (cw[0:1] * p2 + cw[1:2] * p1 + cw[2:3] * u))

    for c in range(D_FF // FFN_SUB):
        val = conv_part(c * FFN_SUB)
        gate = conv_part(D_FF + c * FFN_SUB)
        act_scr[:, c * FFN_SUB:(c + 1) * FFN_SUB] = (val * (gate * jax.nn.sigmoid(gate))).astype(BF16)

    y = x_ref[...] + jnp.dot(act_scr[...], wdn_ref[...], preferred_element_type=F32)
    if final_norm:
        y = _rms(y, gf_ref[...])
    o_ref[...] = y


def conv_ffn(x, g, w_up, conv_w, conv_b, w_down, conv0, gf, *, batch, seq_len, tm, final_norm):
    m, d = x.shape
    f2 = 2 * D_FF
    long_seq = seq_len >= tm
    if long_seq:
        assert seq_len % tm == 0
        tiles_per_seq = seq_len // tm
        st = jnp.pad(conv0, ((0, 0), (SUBLANES - (CONV_W - 1), 0), (0, 0)))
        st_spec = pl.BlockSpec((None, SUBLANES, f2), lambda i: (i // tiles_per_seq, 0, 0))
        tail_shape = jax.ShapeDtypeStruct((batch, SUBLANES, f2), F32)
        tail_spec = pl.BlockSpec((None, SUBLANES, f2), lambda i: (i // tiles_per_seq, 0, 0))
        sem = "arbitrary"
    else:
        assert seq_len == SUBLANES and tm % seq_len == 0
        st = jnp.pad(conv0, ((0, 0), (0, seq_len - (CONV_W - 1)), (0, 0))).reshape(m, f2)
        st_spec = pl.BlockSpec((tm, f2), lambda i: (i, 0))
        tail_shape = jax.ShapeDtypeStruct((m, f2), F32)
        tail_spec = pl.BlockSpec((tm, f2), lambda i: (i, 0))
        sem = "parallel"
    const = lambda i: (0, 0)
    out, tail = pl.pallas_call(
        functools.partial(_ffn_kernel, tm=tm, seq_len=seq_len, final_norm=final_norm),
        grid=(m // tm,),
        in_specs=[pl.BlockSpec((tm, d), lambda i: (i, 0)),
                  pl.BlockSpec((1, d), const),
                  pl.BlockSpec((d, f2), const, pipeline_mode=pl.Buffered(1)),
                  pl.BlockSpec((CONV_W, f2), const),
                  pl.BlockSpec((1, f2), const),
                  pl.BlockSpec((D_FF, d), const, pipeline_mode=pl.Buffered(1)),
                  st_spec,
                  pl.BlockSpec((1, d), const)],
        out_specs=[pl.BlockSpec((tm, d), lambda i: (i, 0)), tail_spec],
        out_shape=[jax.ShapeDtypeStruct((m, d), F32), tail_shape],
        scratch_shapes=[pltpu.VMEM((tm, d), BF16),
                        pltpu.VMEM((tm, D_FF), BF16),
                        pltpu.VMEM((SUBLANES, f2), F32)],
        compiler_params=_cparams(sem),
        name="conv_ffn",
    )(x, g.reshape(1, d), w_up, conv_w, conv_b.reshape(1, f2), w_down, st, gf.reshape(1, d))
    new_state = tail.reshape(batch, -1, f2)[:, -(CONV_W - 1):]
    return out, new_state


def _log_sigmoid(x):
    return jnp.minimum(x, 0.0) - jnp.log1p(jnp.exp(-jnp.abs(x)))


def _dot_nt(a, b):
    return lax.dot_general(a, b, (((1,), (1,)), ((), ())), preferred_element_type=F32)


def _dot_tn(a, b):
    return lax.dot_general(a, b, (((0,), (0,)), ((), ())), preferred_element_type=F32)


def _mlstm_kernel(bif_ref, q_ref, k_ref, v_ref, og_ref, gz_ref, gt_ref, c0_ref, n0_ref, m0_ref,
                  lng_ref, y_ref, c_ref, n_ref, m_ref, grow_scr, gcol_scr, bcol_scr, cmax_scr,
                  *, chunk, n_chunks, bb):
    L = chunk

    @pl.when(pl.program_id(1) == 0)
    def _():
        c_ref[...] = c0_ref[...]
        n_ref[...] = n0_ref[...]
        m_ref[...] = m0_ref[...]

    t_idx = lax.broadcasted_iota(jnp.int32, (L, L), 0)
    s_idx = lax.broadcasted_iota(jnp.int32, (L, L), 1)
    causal = s_idx <= t_idx
    scale = MLSTM_HD ** -0.5

    def compute(items):
        ph = []
        for j, c in items:
            start = (j * n_chunks + c) * L
            rows = pl.ds(start if isinstance(start, int) else pl.multiple_of(start, L), L)
            for h in range(MLSTM_HEADS):
                ph.append(gate_phase(j, c, h, rows))
        for d in ph:
            d["qk_raw"] = _dot_nt(d["qb"], d["kb"])
            d["qc"] = _dot_nt(d["qb"], d["c_old"].astype(BF16))
            d["c_upd"] = _dot_tn((d["v"] * d["wl_col"]).astype(BF16), d["kb"])
        for d in ph:
            d["qk"] = d["qk_raw"] * d["w"]
            d["pv"] = jnp.dot(d["qk"].astype(BF16), d["v"].astype(BF16), preferred_element_type=F32)
        stores = []
        for d in ph:
            stores += finish_phase(d)
        return stores

    def entry(j, c, h):
        return (j * n_chunks + c) * MLSTM_HEADS + h

    for j in range(bb):
        for c in range(n_chunks):
            rows = slice((j * n_chunks + c) * L, (j * n_chunks + c + 1) * L)
            for h in range(MLSTM_HEADS):
                b_i = bif_ref[h]
                b_f = bif_ref[MLSTM_HEADS + h]
                i_col = gz_ref[rows, h:h + 1] + b_i
                f_col = _log_sigmoid(gz_ref[rows, MLSTM_HEADS + h:MLSTM_HEADS + h + 1] + b_f)
                i_row = gt_ref[j, h, c:c + 1, :] + b_i
                f_row = _log_sigmoid(gt_ref[j, MLSTM_HEADS + h, c:c + 1, :] + b_f)
                b_col = jnp.sum(jnp.where(causal, f_row, 0.0), axis=1, keepdims=True)
                b_row = jnp.sum(jnp.where(t_idx <= s_idx, f_col, 0.0), axis=0, keepdims=True)
                g_row = i_row - b_row
                e = entry(j, c, h)
                grow_scr[e, 0:1, 0:L] = g_row
                gcol_scr[e, :, 0:1] = i_col - b_col
                bcol_scr[e, :, 0:1] = b_col
                cmax_scr[e, :, 0:1] = jnp.max(jnp.where(causal, g_row, -jnp.inf), axis=1, keepdims=True)

    def gate_phase(j, c, h, rows):
        cols = slice(h * MLSTM_HD, (h + 1) * MLSTM_HD)
        e = entry(j, c, h)
        g_row = grow_scr[e, 0:1, 0:L]
        b_col = bcol_scr[e, :, 0:1]
        m_old = m_ref[j, h]
        mu_col = jnp.maximum(m_old, cmax_scr[e, :, 0:1])
        w = jnp.where(causal, jnp.exp(g_row - mu_col), 0.0)
        s_inter = jnp.exp(m_old - mu_col)
        q = q_ref[rows, cols]
        ks = k_ref[rows, cols] * scale
        mu_last = mu_col[L - 1:L]
        return dict(j=j, h=h, rows=rows, cols=cols, q=q, ks=ks, v=v_ref[rows, cols],
                    qb=q.astype(BF16), kb=ks.astype(BF16), c_old=c_ref[j, h], n_old=n_ref[j, h],
                    w=w, s_inter=s_inter, m_t=b_col + mu_col,
                    wl_col=jnp.exp(gcol_scr[e, :, 0:1] - mu_last),
                    dec=jnp.exp(m_old - mu_last), m_new=b_col[L - 1:L] + mu_last)

    def finish_phase(d):
        j, h, rows, cols = d["j"], d["h"], d["rows"], d["cols"]
        num = d["s_inter"] * d["qc"] + d["pv"]
        den = (d["s_inter"] * jnp.sum(d["q"] * d["n_old"], axis=1, keepdims=True)
               + jnp.sum(d["qk"], axis=1, keepdims=True))
        hh = num / jnp.maximum(jnp.abs(den), jnp.exp(-d["m_t"]))
        return [(c_ref, (j, h), d["dec"] * d["c_old"] + d["c_upd"]),
                (n_ref, (j, h), d["dec"] * d["n_old"] + jnp.sum(d["wl_col"] * d["ks"], axis=0, keepdims=True)),
                (m_ref, (j, h), d["m_new"]),
                (y_ref, (rows, cols), hh)]

    def commit(stores):
        for ref, idx, val in stores:
            ref[idx] = val

    def chunk_body(j, c, carry):
        commit(compute([(j, c)]))
        return carry

    if n_chunks == 1:
        commit(compute([(j, 0) for j in range(bb)]))
    else:
        for j in range(bb):
            lax.fori_loop(0, n_chunks, functools.partial(chunk_body, j), 0)

    for h in range(MLSTM_HEADS):
        cols = slice(h * MLSTM_HD, (h + 1) * MLSTM_HD)
        hh = y_ref[:, cols]
        xc = hh - jnp.mean(hh, -1, keepdims=True)
        hn = xc * lax.rsqrt(jnp.mean(xc * xc, -1, keepdims=True) + EPS) * lng_ref[:, cols]
        y_ref[:, cols] = jax.nn.sigmoid(og_ref[:, cols]) * hn


def mlstm_mixer(z, b_if, c0, n0, m0, ln_g, *, batch, seq_len, tb, bb):
    m_rows = batch * seq_len
    L = math.gcd(seq_len, CHUNK)
    n_tb = seq_len // tb
    n_chunks = tb // L
    assert bb == 1 or n_tb == 1
    rows = bb * tb
    n_entries = bb * n_chunks * MLSTM_HEADS
    gates = z[:, Z0_GATE:Z0_GATE + 2 * MLSTM_HEADS].reshape(batch, seq_len, 2 * MLSTM_HEADS)
    gates_t = jnp.swapaxes(gates, 1, 2).reshape(batch, 2 * MLSTM_HEADS, seq_len // L, L)
    zcol = lambda off: pl.BlockSpec((rows, HALF), lambda b, t: (b * n_tb + t, off // HALF))
    state4 = lambda a, b_: pl.BlockSpec((bb, MLSTM_HEADS, a, b_), lambda b, t: (b, 0, 0, 0))
    y, c, n, m = pl.pallas_call(
        functools.partial(_mlstm_kernel, chunk=L, n_chunks=n_chunks, bb=bb),
        grid=(batch // bb, n_tb),
        in_specs=[pl.BlockSpec(memory_space=pltpu.SMEM),
                  zcol(Z0_Q), zcol(Z0_K), zcol(Z0_V), zcol(Z0_O),
                  pl.BlockSpec((rows, LANES), lambda b, t: (b * n_tb + t, Z0_GATE // LANES)),
                  pl.BlockSpec((bb, 2 * MLSTM_HEADS, n_chunks, L), lambda b, t: (b, 0, t, 0)),
                  state4(MLSTM_HD, MLSTM_HD), state4(1, MLSTM_HD), state4(1, 1),
                  pl.BlockSpec((1, HALF), lambda b, t: (0, 0))],
        out_specs=[pl.BlockSpec((rows, HALF), lambda b, t: (b * n_tb + t, 0)),
                   state4(MLSTM_HD, MLSTM_HD), state4(1, MLSTM_HD), state4(1, 1)],
        out_shape=[jax.ShapeDtypeStruct((m_rows, HALF), F32),
                   jax.ShapeDtypeStruct((batch, MLSTM_HEADS, MLSTM_HD, MLSTM_HD), F32),
                   jax.ShapeDtypeStruct((batch, MLSTM_HEADS, 1, MLSTM_HD), F32),
                   jax.ShapeDtypeStruct((batch, MLSTM_HEADS, 1, 1), F32)],
        scratch_shapes=[pltpu.VMEM((n_entries, SUBLANES, LANES), F32)]
        + [pltpu.VMEM((n_entries, L, LANES), F32)] * 3,
        compiler_params=_cparams("parallel", "arbitrary"),
        name="mlstm",
    )(b_if, z, z, z, z, z, gates_t,
      c0, n0.reshape(batch, MLSTM_HEADS, 1, MLSTM_HD), m0.reshape(batch, MLSTM_HEADS, 1, 1),
      ln_g.reshape(1, HALF))
    return y, c, n.reshape(batch, MLSTM_HEADS, MLSTM_HD), m.reshape(batch, MLSTM_HEADS)


def _split2(x):
    hi = x.astype(BF16)
    lo = (x - hi.astype(F32)).astype(BF16)
    return hi, lo


def _seg_sum(x, ones_blk):
    hi, lo = _split2(x)
    return (jnp.dot(hi, ones_blk, preferred_element_type=F32)
            + jnp.dot(lo, ones_blk, preferred_element_type=F32))


def _softplus(x):
    return jnp.maximum(x, 0.0) + jnp.log1p(jnp.exp(-jnp.abs(x)))


def _block_ones(n, seg):
    idx = jnp.arange(n) // seg
    return (idx[:, None] == idx[None, :]).astype(BF16)


def _rwkv_prep_kernel(z_ref, st_ref, mu_ref, w0_ref, wup_ref, a0_ref, aup_ref, gup_ref, kk_ref, ka_ref,
                      rk_ref, ones_ref, r_out, w_out, k_out, v_out, kk_out, kka_out, bv_out, g_out,
                      car_scr, *, tm, seq_len):
    zr = z_ref[...]
    row = lax.broadcasted_iota(jnp.int32, zr.shape, 0)
    prev = pltpu.roll(zr, 1, 0)
    if seq_len >= tm:
        @pl.when((pl.program_id(0) * tm) % seq_len == 0)
        def _():
            car_scr[...] = st_ref[...]
        prev = jnp.where(row == 0, car_scr[SUBLANES - 1:SUBLANES], prev)
        car_scr[...] = zr[tm - SUBLANES:]
    else:
        prev = jnp.where(row % seq_len == 0, st_ref[...], prev)
    zs = zr + (prev - zr) * mu_ref[...]
    r = zs[:, :HALF]
    kr = zs[:, HALF:2 * HALF]
    vr = zs[:, 2 * HALF:3 * HALF]
    lora = zs[:, RW_LORA:RW_G]
    ones = ones_ref[...]
    w_log = -_softplus(-(w0_ref[...] + jnp.dot(jnp.tanh(lora).astype(BF16), wup_ref[...],
                                               preferred_element_type=F32))) - 0.5
    a = jax.nn.sigmoid(a0_ref[...] + jnp.dot(lora.astype(BF16), aup_ref[...], preferred_element_type=F32))
    g = jnp.dot(jax.nn.sigmoid(zs[:, RW_G:]).astype(BF16), gup_ref[...], preferred_element_type=F32)
    kk = kr * kk_ref[...]
    kk = kk * lax.rsqrt(jnp.maximum(_seg_sum(kk * kk, ones), 1e-24))
    k2 = kr * (1.0 + (a - 1.0) * ka_ref[...])
    r_out[...] = r
    w_out[...] = jnp.exp(-jnp.exp(w_log))
    k_out[...] = k2
    v_out[...] = vr
    kk_out[...] = kk
    kka_out[...] = kk * a
    bv_out[...] = _seg_sum(r * k2 * rk_ref[...], ones) * vr
    g_out[...] = g


RWKV_GROUPS = 2


def _rwkv_rec_kernel(r_ref, w_ref, k_ref, v_ref, kk_ref, kka_ref, bv_ref, g_ref, s0_ref, lng_ref, lnb_ref,
                     onesy_ref, ones_ref, y_ref, s_ref, yraw_scr, lhs_scr, ylhs_scr, tmaj_scr=None,
                     *, bb, tb):
    @pl.when(pl.program_id(1) == 0)
    def _():
        s_ref[...] = s0_ref[...]

    if tmaj_scr is not None:
        streams = (r_ref, w_ref, k_ref, v_ref, kk_ref, kka_ref, bv_ref, g_ref)
        for i, ref in enumerate(streams):
            for t in range(tb):
                tmaj_scr[i, t] = ref[:, t, :]
        r_ref, w_ref, k_ref, v_ref, kk_ref, kka_ref, bv_ref, g_ref = (
            tmaj_scr.at[i] for i in range(len(streams)))

    pairs = RWKV_HEADS // 2
    n_tiles = bb * pairs
    rows_all = n_tiles * RWKV_HD
    shape = (RWKV_HD, LANES)
    eye2 = (lax.broadcasted_iota(jnp.int32, shape, 0)
            == lax.broadcasted_iota(jnp.int32, shape, 1) % RWKV_HD)
    eye2_swapped = ((lax.broadcasted_iota(jnp.int32, shape, 0) ^ 1)
                    == lax.broadcasted_iota(jnp.int32, shape, 1) % RWKV_HD)
    lane_even = lax.broadcasted_iota(jnp.int32, (1, LANES), 1) % 2 == 0
    y_rows = lambda idx: slice((idx // 2) * RWKV_HD, (idx // 2 + 1) * RWKV_HD)
    y_cols = lambda idx: slice((idx % 2) * LANES, (idx % 2 + 1) * LANES)

    per_group = n_tiles // RWKV_GROUPS

    def step(t, carry):
        def tile(idx):
            b, p = divmod(idx, pairs)
            cols = slice(p * LANES, (p + 1) * LANES)
            return b, p, (lambda ref: ref[t, b:b + 1, cols])

        groups = [range(g * per_group, (g + 1) * per_group) for g in range(RWKV_GROUPS)]
        g_rows = per_group * RWKV_HD * 3 // 2
        sums = []
        for g, grp in enumerate(groups):
            for idx in grp:
                b, p, row = tile(idx)
                rel = idx - grp[0]
                base = g * g_rows + rel * RWKV_HD
                hi, lo = _split2(s_ref[b, p] * row(kk_ref))
                lhs_scr[base:base + RWKV_HD, :LANES] = hi
                lhs_scr[base:base + RWKV_HD, LANES:] = lo
                v_row = row(v_ref)
                v_hi = v_row.astype(BF16).astype(F32)
                v_lo = v_row - v_hi
                v_lo_swapped = jnp.where(lane_even, pltpu.roll(v_lo, LANES - 1, 1), pltpu.roll(v_lo, 1, 1))
                v_base = g * g_rows + per_group * RWKV_HD + (rel // 2) * RWKV_HD
                lhs_scr[v_base:v_base + RWKV_HD, y_cols(rel)] = jnp.where(
                    eye2, v_hi, jnp.where(eye2_swapped, v_lo_swapped, 0.0)).astype(BF16)
            sums.append(jnp.dot(lhs_scr[g * g_rows:(g + 1) * g_rows, :], onesy_ref[...],
                                preferred_element_type=F32))
        ysums = []
        for grp, sm in zip(groups, sums):
            for idx in grp:
                b, p, row = tile(idx)
                rel = idx - grp[0]
                sa_rows = slice(rel * RWKV_HD, (rel + 1) * RWKV_HD)
                sa = sm[sa_rows, :LANES] + sm[sa_rows, LANES:]
                v_base = per_group * RWKV_HD + (rel // 2) * RWKV_HD
                v_tile = sm[v_base:v_base + RWKV_HD, y_cols(rel)]
                s = s_ref[b, p] * row(w_ref) - sa * row(kka_ref) + v_tile * row(k_ref)
                s_ref[b, p] = s
                ylhs_scr[y_rows(idx), y_cols(idx)] = (s * row(r_ref)).astype(BF16)
            g_rows = slice(y_rows(grp[0]).start, y_rows(grp[-1]).stop)
            ysums.append(jnp.dot(ylhs_scr[g_rows, :], onesy_ref[...], preferred_element_type=F32))
        for grp, ys in zip(groups, ysums):
            for idx in grp:
                b, p, row = tile(idx)
                rel = slice(y_rows(idx).start - y_rows(grp[0]).start, y_rows(idx).stop - y_rows(grp[0]).start)
                yraw_scr[t, b:b + 1, p * LANES:(p + 1) * LANES] = jnp.sum(
                    jnp.where(eye2, ys[rel, y_cols(idx)], 0.0), axis=0, keepdims=True)
        return carry

    lax.fori_loop(0, tb, step, 0)

    ones = ones_ref[...]
    y = yraw_scr[...].reshape(tb * bb, HALF)
    xc = y - _seg_sum(y, ones) * (1.0 / RWKV_HD)
    var = _seg_sum(xc * xc, ones) * (1.0 / RWKV_HD)
    yn = xc * lax.rsqrt(var + RWKV_LN_EPS) * lng_ref[...] + lnb_ref[...]
    out = (yn + bv_ref[...].reshape(tb * bb, HALF)) * g_ref[...].reshape(tb * bb, HALF)
    if tmaj_scr is None:
        y_ref[...] = out.reshape(tb, bb, HALF)
    else:
        for t in range(tb):
            y_ref[:, t, :] = out[t * bb:(t + 1) * bb]


def rwkv_mixer(z, s0, shift0, mu, w0, w_up, a0, a_up, g_up, k_k, k_a, r_k, ln_g, ln_b,
               *, batch, seq_len, tm, bb, tb):
    m_rows = batch * seq_len
    tm_shape = (seq_len, batch, HALF)
    if seq_len >= tm:
        tiles_per_seq = seq_len // tm
        st = jnp.pad(shift0[:, None], ((0, 0), (SUBLANES - 1, 0), (0, 0)))
        st_spec = pl.BlockSpec((None, SUBLANES, RWKV_COLS), lambda i: (i // tiles_per_seq, 0, 0))
        sem = "arbitrary"
        tok = pl.BlockSpec((tm, HALF), lambda i: (i % tiles_per_seq, i // tiles_per_seq))
        tok_shape = (seq_len, batch * HALF)
    else:
        st = jnp.pad(shift0[:, None], ((0, 0), (0, seq_len - 1), (0, 0))).reshape(m_rows, RWKV_COLS)
        st_spec = pl.BlockSpec((tm, RWKV_COLS), lambda i: (i, 0))
        sem = "parallel"
        tok = pl.BlockSpec((tm, HALF), lambda i: (i, 0))
        tok_shape = (m_rows, HALF)
    row = lambda a: a.reshape(1, -1)
    const = lambda i: (0, 0)
    vec = pl.BlockSpec((1, HALF), const)
    lora_rows = RWKV_W_LORA + RWKV_A_LORA
    w_up_p = jnp.pad(w_up, ((0, RWKV_A_LORA), (0, 0))).astype(BF16)
    a_up_p = jnp.pad(a_up, ((RWKV_W_LORA, 0), (0, 0))).astype(BF16)
    ones_half = _block_ones(HALF, RWKV_HD)
    prep = pl.pallas_call(
        functools.partial(_rwkv_prep_kernel, tm=tm, seq_len=seq_len),
        grid=(m_rows // tm,),
        in_specs=[pl.BlockSpec((tm, RWKV_COLS), lambda i: (i, Z0_RWKV // RWKV_COLS)),
                  st_spec,
                  pl.BlockSpec((1, RWKV_COLS), const),
                  vec, pl.BlockSpec((lora_rows, HALF), const),
                  vec, pl.BlockSpec((lora_rows, HALF), const),
                  pl.BlockSpec((RWKV_G_LORA, HALF), const),
                  vec, vec, vec,
                  pl.BlockSpec((HALF, HALF), const)],
        out_specs=[tok] * 8,
        out_shape=[jax.ShapeDtypeStruct(tok_shape, F32)] * 8,
        scratch_shapes=[pltpu.VMEM((SUBLANES, RWKV_COLS), F32)],
        compiler_params=_cparams(sem),
        name="rwkv_prep",
    )(z, st, row(mu), row(w0), w_up_p, row(a0), a_up_p, g_up.astype(BF16), row(k_k), row(k_a), row(r_k),
      ones_half)
    pairs = RWKV_HEADS // 2
    s_pairs = (s0.reshape(batch, pairs, 2, RWKV_HD, RWKV_HD).transpose(0, 1, 3, 2, 4)
               .reshape(batch, pairs, RWKV_HD, LANES))
    scratch = [pltpu.VMEM((tb, bb, HALF), F32),
               pltpu.VMEM((bb * pairs * RWKV_HD * 3 // 2, 2 * LANES), BF16),
               pltpu.VMEM((bb * pairs * RWKV_HD // 2, 2 * LANES), BF16)]
    if seq_len >= tm:
        seqs = [a.reshape(tm_shape) for a in prep]
        blk = pl.BlockSpec((tb, bb, HALF), lambda b, t: (t, b, 0))
        y_shape = tm_shape
    else:
        assert tb == seq_len
        seqs = [a.reshape(batch, seq_len, HALF) for a in prep]
        blk = pl.BlockSpec((bb, tb, HALF), lambda b, t: (b, 0, 0))
        y_shape = (batch, seq_len, HALF)
        scratch.append(pltpu.VMEM((len(prep), tb, bb, HALF), F32))
    s_spec = pl.BlockSpec((bb, pairs, RWKV_HD, LANES), lambda b, t: (b, 0, 0, 0))
    const2 = lambda b, t: (0, 0)
    y, s_new = pl.pallas_call(
        functools.partial(_rwkv_rec_kernel, bb=bb, tb=tb),
        grid=(batch // bb, seq_len // tb),
        in_specs=[blk] * 8 + [s_spec,
                              pl.BlockSpec((1, HALF), const2), pl.BlockSpec((1, HALF), const2),
                              pl.BlockSpec((2 * LANES, 2 * LANES), const2),
                              pl.BlockSpec((HALF, HALF), const2)],
        out_specs=[blk, s_spec],
        out_shape=[jax.ShapeDtypeStruct(y_shape, F32),
                   jax.ShapeDtypeStruct((batch, pairs, RWKV_HD, LANES), F32)],
        scratch_shapes=scratch,
        compiler_params=_cparams("parallel", "arbitrary"),
        name="rwkv_rec",
    )(*seqs, s_pairs, row(ln_g), row(ln_b),
      _block_ones(2 * LANES, RWKV_HD), ones_half)
    s_new = (s_new.reshape(batch, pairs, RWKV_HD, 2, RWKV_HD).transpose(0, 1, 3, 2, 4)
             .reshape(batch, RWKV_HEADS, RWKV_HD, RWKV_HD))
    if seq_len < tm:
        y = y.reshape(m_rows, HALF)
    return y, s_new


GLA_SUB = 16


def _gla_kernel(q_ref, k_ref, v_ref, g_ref, ad_ref, aup_ref, ab_ref, lng_ref, s0_ref,
                y_ref, s_ref, bc_scr, *, chunk, n_chunks, bb):
    L = chunk
    sub = min(GLA_SUB, L)

    @pl.when(pl.program_id(1) == 0)
    def _():
        s_ref[...] = s0_ref[...]

    lane = lax.broadcasted_iota(jnp.int32, (1, LANES), 1)
    head_mask = [(lane < GLA_DK).astype(F32), (lane >= GLA_DK).astype(F32)]
    tril = (lax.broadcasted_iota(jnp.int32, (L, L), 1)
            <= lax.broadcasted_iota(jnp.int32, (L, L), 0)).astype(F32)
    eye = (lax.broadcasted_iota(jnp.int32, (LANES, LANES), 0)
           == lax.broadcasted_iota(jnp.int32, (LANES, LANES), 1))

    pre = jnp.dot(ad_ref[...].astype(BF16), aup_ref[...], preferred_element_type=F32) + ab_ref[...]
    la = _log_sigmoid(pre) * (1.0 / GLA_TAU)
    for blk in range(bb * n_chunks):
        rows = slice(blk * L, (blk + 1) * L)
        bc_scr[rows, :] = jnp.dot(tril, la[rows], preferred_element_type=F32,
                                  precision=lax.Precision.HIGHEST)

    stack = lambda x: jnp.concatenate([x * head_mask[0], x * head_mask[1]], axis=0).astype(BF16)

    def compute(items):
        units = []
        for j, c in items:
            start = (j * n_chunks + c) * L
            rows = pl.ds(start if isinstance(start, int) else pl.multiple_of(start, L), L)
            for p in range(GLA_HEADS // 2):
                kcols = slice(p * LANES, (p + 1) * LANES)
                units.append(dict(
                    j=j, p=p, rows=rows, qp=q_ref[rows, kcols] * (GLA_DK ** -0.5), kp=k_ref[rows, kcols],
                    bc=bc_scr[rows, kcols], s_old=s_ref[j, p],
                    v_pair=v_ref[rows, 2 * p * GLA_DV:2 * (p + 1) * GLA_DV].astype(BF16)))
        for d in units:
            qp, kp, bc, v_pair = d["qp"], d["kp"], d["bc"], d["v_pair"]
            d["inter"] = jnp.dot(stack(qp * jnp.exp(bc)), d["s_old"].astype(BF16), preferred_element_type=F32)
            d["att"] = []
            for i in range(L // sub):
                lo, hi = i * sub, (i + 1) * sub
                c_i = bc[lo - 1:lo] if i > 0 else jnp.zeros((1, LANES), F32)
                qe = qp[lo:hi] * jnp.exp(bc[lo:hi] - c_i)
                ke = kp[:hi] * jnp.exp(c_i - bc[:hi])
                d["att"].append(_dot_nt(stack(qe), ke.astype(BF16)))
            bl = bc[L - 1:L]
            v_stack = jnp.concatenate([v_pair[:, :GLA_DV], v_pair[:, GLA_DV:]], axis=0)
            d["s_upd"] = _dot_tn(stack(kp * jnp.exp(bl - bc)), v_stack)
            d["dec_col"] = jnp.sum(jnp.where(eye, jnp.exp(bl), 0.0), axis=1, keepdims=True)
        for d in units:
            d["intra"] = []
            for i, att in enumerate(d["att"]):
                lo, hi = i * sub, (i + 1) * sub
                t_idx = lo + lax.broadcasted_iota(jnp.int32, (2 * sub, hi), 0) % sub
                s_idx = lax.broadcasted_iota(jnp.int32, (2 * sub, hi), 1)
                att = jnp.where(s_idx <= t_idx, att, 0.0)
                d["intra"].append(jnp.dot(att.astype(BF16), d["v_pair"][:hi], preferred_element_type=F32))
        stores = []
        for d in units:
            for e in range(2):
                h = 2 * d["p"] + e
                vcols = slice(h * GLA_DV, (h + 1) * GLA_DV)
                o = d["inter"][e * L:(e + 1) * L] + jnp.concatenate(
                    [blk[e * sub:(e + 1) * sub, e * GLA_DV:(e + 1) * GLA_DV] for blk in d["intra"]], axis=0)
                stores.append((y_ref, (d["rows"], vcols), o))
            stores.append((s_ref, (d["j"], d["p"]), d["dec_col"] * d["s_old"] + d["s_upd"]))
        return stores

    def commit(stores):
        for ref, idx, val in stores:
            ref[idx] = val

    def chunk_body(j, c, carry):
        commit(compute([(j, c)]))
        return carry

    if n_chunks == 1:
        commit(compute([(j, 0) for j in range(bb)]))
    else:
        for j in range(bb):
            lax.fori_loop(0, n_chunks, functools.partial(chunk_body, j), 0)

    for h in range(GLA_HEADS):
        vcols = slice(h * GLA_DV, (h + 1) * GLA_DV)
        o = y_ref[:, vcols]
        on = o * lax.rsqrt(jnp.mean(o * o, -1, keepdims=True) + EPS) * lng_ref[:, vcols]
        gate = g_ref[:, vcols]
        y_ref[:, vcols] = on * (gate * jax.nn.sigmoid(gate))


def gla_mixer(z, s0, a_up, a_b, ln_g, *, batch, seq_len, tb, bb, time_major):
    m_rows = batch * seq_len
    L = math.gcd(seq_len, CHUNK)
    n_tb = seq_len // tb
    pairs = GLA_HEADS // 2
    assert bb == 1 or (n_tb == 1 and not time_major)
    rows = bb * tb
    if time_major:
        zblk = lambda width, off: pl.BlockSpec((tb, width), lambda b, t: (t, (b * Z1_COLS + off) // width))
    else:
        zblk = lambda width, off: pl.BlockSpec((rows, width), lambda b, t: (b * n_tb + t, off // width))
    s_spec = pl.BlockSpec((bb, pairs, 2 * GLA_DK, GLA_DV), lambda b, t: (b, 0, 0, 0))
    const = lambda b, t: (0, 0)
    a_up_p = jnp.pad(a_up, ((0, LANES - GLA_GATE_RANK), (0, 0))).astype(BF16)
    y, s_new = pl.pallas_call(
        functools.partial(_gla_kernel, chunk=L, n_chunks=tb // L, bb=bb),
        grid=(batch // bb, n_tb),
        in_specs=[zblk(GLA_QK, Z1_Q), zblk(GLA_QK, Z1_K), zblk(HALF, Z1_V), zblk(HALF, Z1_G),
                  zblk(LANES, Z1_AD),
                  pl.BlockSpec((LANES, GLA_QK), const), pl.BlockSpec((1, GLA_QK), const),
                  pl.BlockSpec((1, HALF), const), s_spec],
        out_specs=[pl.BlockSpec((rows, HALF), lambda b, t: (b * n_tb + t, 0)), s_spec],
        out_shape=[jax.ShapeDtypeStruct((m_rows, HALF), F32),
                   jax.ShapeDtypeStruct((batch, pairs, 2 * GLA_DK, GLA_DV), F32)],
        scratch_shapes=[pltpu.VMEM((rows, GLA_QK), F32)],
        compiler_params=_cparams("parallel", "arbitrary"),
        name="gla",
    )(z, z, z, z, z, a_up_p, a_b.reshape(1, GLA_QK), ln_g.reshape(1, HALF),
      s0.reshape(batch, pairs, 2 * GLA_DK, GLA_DV))
    return y, s_new.reshape(batch, GLA_HEADS, GLA_DK, GLA_DV)


S5_KT = LANES // S5_GROUP
S5_SCAN_SPLIT = 2


def _s5_param_kernel(lre_ref, lim_ref, step_ref, bre_ref, bim_ref, bbre_ref, bbim_ref, pre_ref, pim_ref):
    lre, lim = lre_ref[...], lim_ref[...]
    step = jnp.exp(step_ref[...])
    mag = jnp.exp(lre * step)
    bar_re = mag * jnp.cos(lim * step)
    bar_im = mag * jnp.sin(lim * step)
    inv = 1.0 / (lre * lre + lim * lim)
    cre = ((bar_re - 1.0) * lre + bar_im * lim) * inv
    cim = (bar_im * lre - (bar_re - 1.0) * lim) * inv
    for g in range(S5_GROUPS):
        cr, ci = cre[g:g + 1], cim[g:g + 1]
        bbre_ref[g] = cr * bre_ref[g] - ci * bim_ref[g]
        bbim_ref[g] = cr * bim_ref[g] + ci * bre_ref[g]
    pre_ref[0] = bar_re
    pim_ref[0] = bar_im


def _s5_kernel(u_ref, wbre_ref, wbim_ref, lre_ref, lim_ref, x0re_ref, x0im_ref, wcre_ref, wcim_ref,
               d_ref, wglu_ref, y_ref, xre_ref, xim_ref, sre_scr, sim_scr, *, tt, bb, row_major):
    @pl.when(pl.program_id(1) == 0)
    def _():
        xre_ref[...] = x0re_ref[...]
        xim_ref[...] = x0im_ref[...]

    rows_all = tt * bb
    if row_major:
        u = jnp.concatenate([u_ref[:, t, :] for t in range(tt)], axis=0)
    else:
        u = u_ref[...].reshape(rows_all, S5_CH)
    ub = u.astype(BF16)
    nblk = S5_CH // LANES
    wide = S5_STATE // nblk
    for kt in range(nblk):
        cols = slice(kt * wide, (kt + 1) * wide)
        ukt = ub[:, kt * LANES:(kt + 1) * LANES]
        sre_scr[:, cols] = jnp.dot(ukt, wbre_ref[kt], preferred_element_type=F32)
        sim_scr[:, cols] = jnp.dot(ukt, wbim_ref[kt], preferred_element_type=F32)

    for part in range(S5_SCAN_SPLIT):
        width = S5_STATE // S5_SCAN_SPLIT
        cols = slice(part * width, (part + 1) * width)
        lr, li = lre_ref[:, cols], lim_ref[:, cols]

        def step(t, carry, cols=cols, lr=lr, li=li):
            cr, ci = carry
            rows = pl.ds(pl.multiple_of(t * bb, bb), bb)
            nr = sre_scr[rows, cols] + (lr * cr - li * ci)
            ni = sim_scr[rows, cols] + (lr * ci + li * cr)
            sre_scr[rows, cols] = nr
            sim_scr[rows, cols] = ni
            return nr, ni

        cr, ci = lax.fori_loop(0, tt, step, (xre_ref[:, cols], xim_ref[:, cols]))
        xre_ref[:, cols] = cr
        xim_ref[:, cols] = ci

    parts = []
    for nt in range(nblk):
        cols = slice(nt * wide, (nt + 1) * wide)
        parts.append(jnp.dot(sre_scr[:, cols].astype(BF16), wcre_ref[nt], preferred_element_type=F32)
                     - jnp.dot(sim_scr[:, cols].astype(BF16), wcim_ref[nt], preferred_element_type=F32))
    y = jnp.concatenate(parts, axis=1) + d_ref[...] * u
    ys = y * (0.5 * (1.0 + jnp.tanh(math.sqrt(2.0 / math.pi) * (y + 0.044715 * (y * y * y)))))
    out = ys * jax.nn.sigmoid(jnp.dot(ys.astype(BF16), wglu_ref[...], preferred_element_type=F32))
    if row_major:
        for t in range(tt):
            y_ref[:, t, :] = out[t * bb:(t + 1) * bb]
    else:
        y_ref[...] = out.reshape(tt, bb, S5_CH)


def s5_mixer(z, x0_re, x0_im, lam_re, lam_im, log_step, b_re, b_im, c_re, c_im, d_skip,
             w_glu, *, batch, seq_len, tt, bb, time_major):
    u_col_block = Z1_U // S5_CH
    if time_major:
        z = z.reshape(seq_len, batch, Z1_COLS)
        u_spec = pl.BlockSpec((tt, bb, S5_CH), lambda b, t: (t, b, u_col_block))
        y_spec = pl.BlockSpec((tt, bb, S5_CH), lambda b, t: (t, b, 0))
        y_shape = (seq_len, batch, S5_CH)
    else:
        assert tt == seq_len
        z = z.reshape(batch, seq_len, Z1_COLS)
        u_spec = pl.BlockSpec((bb, tt, S5_CH), lambda b, t: (b, 0, u_col_block))
        y_spec = pl.BlockSpec((bb, tt, S5_CH), lambda b, t: (b, 0, 0))
        y_shape = (batch, seq_len, S5_CH)
    gpc = (S5_GROUPS, S5_GROUP, S5_P)
    bb_re, bb_im, p_re, p_im = pl.pallas_call(
        _s5_param_kernel,
        out_shape=[jax.ShapeDtypeStruct(gpc, F32), jax.ShapeDtypeStruct(gpc, F32),
                   jax.ShapeDtypeStruct((1, S5_GROUPS, S5_P), F32),
                   jax.ShapeDtypeStruct((1, S5_GROUPS, S5_P), F32)],
        name="s5_params",
    )(lam_re, lam_im, log_step.reshape(S5_GROUPS, 1), jnp.swapaxes(b_re, 1, 2), jnp.swapaxes(b_im, 1, 2))

    nblk = S5_CH // LANES
    eye = jnp.eye(S5_KT, dtype=F32)

    def in_blocks(w):
        w = w.reshape(nblk, S5_KT, S5_GROUP, 1, S5_P) * eye[None, :, None, :, None]
        return w.reshape(nblk, LANES, S5_KT * S5_P).astype(BF16)

    def out_blocks(c):
        w = jnp.swapaxes(c, 1, 2).reshape(nblk, S5_KT, S5_P, 1, S5_GROUP) * eye[None, :, None, :, None]
        return w.reshape(nblk, S5_KT * S5_P, LANES).astype(BF16)

    lam_rows = lambda p: jnp.broadcast_to(p.reshape(1, S5_STATE), (bb, S5_STATE))
    const2 = lambda b, t: (0, 0)
    const3 = lambda b, t: (0, 0, 0)
    st_spec = pl.BlockSpec((bb, S5_STATE), lambda b, t: (b, 0))
    wb_spec = pl.BlockSpec((nblk, LANES, S5_KT * S5_P), const3)
    wc_spec = pl.BlockSpec((nblk, S5_KT * S5_P, LANES), const3)
    lam_spec = pl.BlockSpec((bb, S5_STATE), const2)
    y, x_re, x_im = pl.pallas_call(
        functools.partial(_s5_kernel, tt=tt, bb=bb, row_major=not time_major),
        grid=(batch // bb, seq_len // tt),
        in_specs=[u_spec,
                  wb_spec, wb_spec, lam_spec, lam_spec, st_spec, st_spec, wc_spec, wc_spec,
                  pl.BlockSpec((1, S5_CH), const2), pl.BlockSpec((S5_CH, S5_CH), const2)],
        out_specs=[y_spec, st_spec, st_spec],
        out_shape=[jax.ShapeDtypeStruct(y_shape, F32),
                   jax.ShapeDtypeStruct((batch, S5_STATE), F32),
                   jax.ShapeDtypeStruct((batch, S5_STATE), F32)],
        scratch_shapes=[pltpu.VMEM((tt * bb, S5_STATE), F32), pltpu.VMEM((tt * bb, S5_STATE), F32)],
        compiler_params=_cparams("parallel", "arbitrary"),
        name="s5",
    )(z, in_blocks(bb_re), in_blocks(bb_im), lam_rows(p_re), lam_rows(p_im),
      x0_re.reshape(batch, S5_STATE), x0_im.reshape(batch, S5_STATE),
      out_blocks(c_re), out_blocks(c_im), d_skip.reshape(1, S5_CH), w_glu.astype(BF16))
    shape = (batch, S5_GROUPS, S5_P)
    if not time_major:
        y = y.reshape(batch * seq_len, S5_CH)
    return y, x_re.reshape(shape), x_im.reshape(shape)


def _pack_even_w_in(w_in):
    d = w_in.shape[0]
    m_cols = 4 * HALF + 2 * MLSTM_HEADS
    parts = [w_in[:, m_cols:],
             w_in[:, 4 * HALF:m_cols], jnp.zeros((d, Z0_Q - Z0_GATE - 2 * MLSTM_HEADS), w_in.dtype),
             w_in[:, :4 * HALF]]
    return jnp.concatenate(parts, axis=1).astype(BF16)


def _pack_odd_w_in(w_in):
    d = w_in.shape[0]
    gla_main = 2 * GLA_QK + 2 * HALF
    gla_cols = gla_main + GLA_GATE_RANK
    parts = [w_in[:, :gla_main], w_in[:, gla_cols:], w_in[:, gla_main:gla_cols]]
    w = jnp.concatenate(parts, axis=1)
    return jnp.pad(w, ((0, 0), (0, Z1_COLS - w.shape[1]))).astype(BF16)


def _trunk(x, st, prm, *, batch, seq_len):
    (st_C, st_n, st_m, st_S, st_shift, st_gla, st_re, st_im, st_conv) = st
    (norm_mix, norm_ffn, norm_final,
     e_w_in, e_b_if, e_mu, e_w0, e_w_up, e_a0, e_a_up, e_g_up, e_k_k, e_k_a, e_r_k, e_ln_m_g, e_ln_r_g,
     e_ln_r_b, e_w_out,
     o_w_in, o_a_up, o_a_b, o_ln_g, o_lam_re, o_lam_im, o_log_step, o_b_re, o_b_im, o_c_re, o_c_im, o_d,
     o_w_glu, o_w_out,
     f_w_up, f_conv_w, f_conv_b, f_w_down) = prm
    assert len(e_w_in) == 1 and len(o_w_in) == 1 and len(f_w_up) == 2, "wired for depth 2"
    long_seq = seq_len >= 512
    assert long_seq or seq_len == SUBLANES
    m_rows = batch * seq_len
    tiles = dict(
        tm_proj=512,
        tm_tok=512 if long_seq else 256,
        tm_ffn=512 if long_seq else 128,
        tb=512 if long_seq else seq_len,
        bb_chunked=1 if long_seq else 4,
        tb_rwkv=64 if long_seq else seq_len,
        tt_s5=128 if long_seq else seq_len,
    )
    x = x.reshape(m_rows, D_MODEL)
    ffn = functools.partial(conv_ffn, batch=batch, seq_len=seq_len, tm=tiles["tm_ffn"])

    z0 = norm_matmul(x, norm_mix[0], _pack_even_w_in(e_w_in[0]), tm=tiles["tm_proj"], tn=512)
    y_m, c_new, n_new, m_new = mlstm_mixer(z0, e_b_if[0], st_C[0], st_n[0], st_m[0], e_ln_m_g[0],
                                           batch=batch, seq_len=seq_len, tb=tiles["tb"], bb=tiles["bb_chunked"])
    y_r, s_new = rwkv_mixer(z0, st_S[0], st_shift[0], e_mu[0], e_w0[0], e_w_up[0], e_a0[0], e_a_up[0],
                            e_g_up[0], e_k_k[0], e_k_a[0], e_r_k[0], e_ln_r_g[0], e_ln_r_b[0],
                            batch=batch, seq_len=seq_len, tm=tiles["tm_tok"], bb=SUBLANES,
                            tb=tiles["tb_rwkv"])
    shift_new = z0.reshape(batch, seq_len, Z0_COLS)[:, -1, Z0_RWKV:Z0_RWKV + RWKV_COLS]
    w_out = e_w_out[0].astype(BF16)
    x = out_proj(x, y_m, y_r, w_out[:HALF], w_out[HALF:], tm=tiles["tm_tok"])
    x, conv_new0 = ffn(x, norm_ffn[0], f_w_up[0].astype(BF16), f_conv_w[0], f_conv_b[0],
                       f_w_down[0].astype(BF16), st_conv[0], norm_final, final_norm=False)

    z1 = norm_matmul(x, norm_mix[1], _pack_odd_w_in(o_w_in[0]), tm=tiles["tm_proj"], tn=512,
                     time_major_seq_len=seq_len if long_seq else None)
    y_g, gla_new = gla_mixer(z1, st_gla[0], o_a_up[0], o_a_b[0], o_ln_g[0],
                             batch=batch, seq_len=seq_len, tb=tiles["tb"], bb=tiles["bb_chunked"],
                             time_major=long_seq)
    y_s, re_new, im_new = s5_mixer(z1, st_re[0], st_im[0], o_lam_re[0], o_lam_im[0],
                                   o_log_step[0], o_b_re[0], o_b_im[0], o_c_re[0], o_c_im[0], o_d[0],
                                   o_w_glu[0], batch=batch, seq_len=seq_len, tt=tiles["tt_s5"],
                                   bb=SUBLANES, time_major=long_seq)
    w_out = o_w_out[0].astype(BF16)
    x = out_proj(x, y_g, y_s, w_out[:HALF], w_out[HALF:], tm=tiles["tm_tok"])
    x, conv_new1 = ffn(x, norm_ffn[1], f_w_up[1].astype(BF16), f_conv_w[1], f_conv_b[1],
                       f_w_down[1].astype(BF16), st_conv[1], norm_final, final_norm=True)
    new_state = (c_new[None], n_new[None], m_new[None], s_new[None], shift_new[None],
                 gla_new[None], re_new[None], im_new[None], jnp.stack([conv_new0, conv_new1]))
    return x.reshape(batch, seq_len, D_MODEL), new_state


def kernel(x_prompt, x_sample, state_mlstm_C, state_mlstm_n, state_mlstm_m, state_rwkv_S, state_rwkv_shift,
           state_gla_S, state_s5_re, state_s5_im, state_ffn_conv,
           norm_mix, norm_ffn, norm_final,
           e_w_in, e_b_if, e_mu, e_w0, e_w_up, e_a0, e_a_up, e_g_up, e_k_k, e_k_a, e_r_k,
           e_ln_m_g, e_ln_r_g, e_ln_r_b, e_w_out,
           o_w_in, o_a_up, o_a_b, o_ln_g, o_lam_re, o_lam_im, o_log_step, o_b_re, o_b_im, o_c_re, o_c_im,
           o_d, o_w_glu, o_w_out,
           f_w_up, f_conv_w, f_conv_b, f_w_down):
    params = (norm_mix, norm_ffn, norm_final,
              e_w_in, e_b_if, e_mu, e_w0, e_w_up, e_a0, e_a_up, e_g_up, e_k_k, e_k_a, e_r_k,
              e_ln_m_g, e_ln_r_g, e_ln_r_b, e_w_out,
              o_w_in, o_a_up, o_a_b, o_ln_g, o_lam_re, o_lam_im, o_log_step, o_b_re, o_b_im, o_c_re, o_c_im,
              o_d, o_w_glu, o_w_out,
              f_w_up, f_conv_w, f_conv_b, f_w_down)
    bp, tp, _ = x_prompt.shape
    bs, ts, _ = x_sample.shape
    n_even, n_odd, depth = state_mlstm_C.shape[0], state_gla_S.shape[0], state_ffn_conv.shape[0]
    init = (jnp.zeros((n_even, bp, MLSTM_HEADS, MLSTM_HD, MLSTM_HD), F32),
            jnp.zeros((n_even, bp, MLSTM_HEADS, MLSTM_HD), F32),
            jnp.full((n_even, bp, MLSTM_HEADS), M_INIT, F32),
            jnp.zeros((n_even, bp, RWKV_HEADS, RWKV_HD, RWKV_HD), F32),
            jnp.zeros((n_even, bp, RWKV_COLS), F32),
            jnp.zeros((n_odd, bp, GLA_HEADS, GLA_DK, GLA_DV), F32),
            jnp.zeros((n_odd, bp, S5_GROUPS, S5_P), F32),
            jnp.zeros((n_odd, bp, S5_GROUPS, S5_P), F32),
            jnp.zeros((depth, bp, CONV_W - 1, 2 * D_FF), F32))
    past = (state_mlstm_C, state_mlstm_n, state_mlstm_m, state_rwkv_S, state_rwkv_shift,
            state_gla_S, state_s5_re, state_s5_im, state_ffn_conv)
    y_prompt, p_state = _trunk(x_prompt, init, params, batch=bp, seq_len=tp)
    y_sample, s_state = _trunk(x_sample, past, params, batch=bs, seq_len=ts)
    return (y_prompt, y_sample) + tuple(p_state) + tuple(s_state)
```

```python
import functools
import math

import jax
import jax.numpy as jnp
from jax import lax
from jax.experimental import pallas as pl
from jax.experimental.pallas import tpu as pltpu

F32 = jnp.float32
BF16 = jnp.bfloat16

D_MODEL = 1024
HALF = D_MODEL // 2
MLSTM_HEADS = 4
MLSTM_HD = HALF // MLSTM_HEADS
RWKV_HD = 64
RWKV_HEADS = HALF // RWKV_HD
RWKV_W_LORA = 64
RWKV_A_LORA = 64
RWKV_G_LORA = 128
RWKV_COLS = 3 * HALF + RWKV_W_LORA + RWKV_A_LORA + RWKV_G_LORA
GLA_HEADS = 4
GLA_DK = 64
GLA_DV = 128
GLA_QK = GLA_HEADS * GLA_DK
GLA_GATE_RANK = 16
GLA_TAU = 16.0
S5_CH = HALF
S5_GROUP = 16
S5_GROUPS = S5_CH // S5_GROUP
S5_P = 64
S5_STATE = S5_GROUPS * S5_P
D_FF = 2816
CONV_W = 3
CHUNK = 64
EPS = 1e-6
RWKV_LN_EPS = 64e-5
M_INIT = -1e30

LANES = 128
SUBLANES = 8
VMEM_LIMIT_BYTES = 56 * 1024 * 1024

Z0_RWKV = 0
Z0_GATE = RWKV_COLS
Z0_Q = Z0_GATE + 2 * LANES
Z0_K, Z0_V, Z0_O = Z0_Q + HALF, Z0_Q + 2 * HALF, Z0_Q + 3 * HALF
Z0_COLS = Z0_Q + 4 * HALF
RW_LORA = 3 * HALF
RW_G = RW_LORA + RWKV_W_LORA + RWKV_A_LORA
Z1_Q, Z1_K, Z1_V, Z1_G, Z1_U = 0, GLA_QK, 2 * GLA_QK, 2 * GLA_QK + HALF, 2 * GLA_QK + 2 * HALF
Z1_AD = Z1_U + S5_CH
Z1_COLS = 2560


def _cparams(*sem):
    return pltpu.CompilerParams(dimension_semantics=sem, vmem_limit_bytes=VMEM_LIMIT_BYTES)


def _rms(x, g):
    return x * lax.rsqrt(jnp.mean(x * x, -1, keepdims=True) + EPS) * g


def _norm_matmul_kernel(x_ref, g_ref, w_ref, o_ref, *, tn):
    h = _rms(x_ref[...], g_ref[...]).astype(BF16)
    for j in range(w_ref.shape[1] // tn):
        cols = slice(j * tn, (j + 1) * tn)
        o_ref[:, cols] = jnp.dot(h, w_ref[:, cols], preferred_element_type=F32)


def norm_matmul(x, g, w, *, tm, tn, time_major_seq_len=None):
    m, d = x.shape
    n = w.shape[1]
    if time_major_seq_len is None:
        out_shape = (m, n)
        out_spec = pl.BlockSpec((tm, n), lambda i: (i, 0))
    else:
        tiles_per_seq = time_major_seq_len // tm
        out_shape = (time_major_seq_len, (m // time_major_seq_len) * n)
        out_spec = pl.BlockSpec((tm, n), lambda i: (i % tiles_per_seq, i // tiles_per_seq))
    return pl.pallas_call(
        functools.partial(_norm_matmul_kernel, tn=tn),
        grid=(m // tm,),
        in_specs=[pl.BlockSpec((tm, d), lambda i: (i, 0)),
                  pl.BlockSpec((1, d), lambda i: (0, 0)),
                  pl.BlockSpec((d, n), lambda i: (0, 0), pipeline_mode=pl.Buffered(1))],
        out_specs=out_spec,
        out_shape=jax.ShapeDtypeStruct(out_shape, F32),
        compiler_params=_cparams("parallel"),
        name="norm_matmul",
    )(x, g.reshape(1, d), w)


def _out_proj_kernel(x_ref, ya_ref, yb_ref, wa_ref, wb_ref, o_ref):
    acc = jnp.dot(ya_ref[...].astype(BF16), wa_ref[...], preferred_element_type=F32)
    acc += jnp.dot(yb_ref[...].astype(BF16), wb_ref[...], preferred_element_type=F32)
    o_ref[...] = x_ref[...] + acc


def out_proj(x, ya, yb, wa, wb, *, tm):
    m, d = x.shape
    k = ya.shape[1]
    if yb.ndim == 3:
        tiles_per_seq = yb.shape[0] // tm
        yb = yb.reshape(yb.shape[0], -1)
        yb_spec = pl.BlockSpec((tm, k), lambda i: (i % tiles_per_seq, i // tiles_per_seq))
    else:
        yb_spec = pl.BlockSpec((tm, k), lambda i: (i, 0))
    return pl.pallas_call(
        _out_proj_kernel,
        grid=(m // tm,),
        in_specs=[pl.BlockSpec((tm, d), lambda i: (i, 0)),
                  pl.BlockSpec((tm, k), lambda i: (i, 0)),
                  yb_spec,
                  pl.BlockSpec((k, d), lambda i: (0, 0)),
                  pl.BlockSpec((k, d), lambda i: (0, 0))],
        out_specs=pl.BlockSpec((tm, d), lambda i: (i, 0)),
        out_shape=jax.ShapeDtypeStruct((m, d), F32),
        compiler_params=_cparams("parallel"),
        name="out_proj",
    )(x, ya, yb, wa, wb)


FFN_SUB = 256


def _ffn_kernel(x_ref, g_ref, wup_ref, cw_ref, cb_ref, wdn_ref, st_ref, gf_ref,
                o_ref, tail_ref, h_scr, act_scr, car_scr, *, tm, seq_len, final_norm):
    long_seq = seq_len >= tm
    i = pl.program_id(0)
    h_scr[...] = _rms(x_ref[...], g_ref[...]).astype(BF16)
    row = lax.broadcasted_iota(jnp.int32, (tm, FFN_SUB), 0)
    if long_seq:
        @pl.when((i * tm) % seq_len == 0)
        def _():
            car_scr[...] = st_ref[...]
    else:
        t_in_seq = row % seq_len

    row8 = lax.broadcasted_iota(jnp.int32, (SUBLANES, FFN_SUB), 0)

    def conv_part(col0):
        u = jnp.dot(h_scr[...], wup_ref[:, col0:col0 + FFN_SUB], preferred_element_type=F32)
        p1 = pltpu.roll(u, 1, 0)
        p2 = pltpu.roll(u, 2, 0)
        if long_seq:
            halo = car_scr[:, col0:col0 + FFN_SUB]
            h6 = halo[SUBLANES - 2:SUBLANES - 1]
            h7 = halo[SUBLANES - 1:SUBLANES]
            p1 = jnp.concatenate([jnp.where(row8 == 0, h7, p1[:SUBLANES]), p1[SUBLANES:]], axis=0)
            p2 = jnp.concatenate([jnp.where(row8 == 0, h6, jnp.where(row8 == 1, h7, p2[:SUBLANES])),
                                  p2[SUBLANES:]], axis=0)
            car_scr[:, col0:col0 + FFN_SUB] = u[tm - SUBLANES:]
            tail_ref[:, col0:col0 + FFN_SUB] = u[tm - SUBLANES:]
        else:
            e = st_ref[:, col0:col0 + FFN_SUB]
            p1 = jnp.where(t_in_seq == 0, pltpu.roll(e, tm - 1, 0), p1)
            p2 = jnp.where(t_in_seq < 2, e, p2)
            tail_ref[:, col0:col0 + FFN_SUB] = u
        cw = cw_ref[:, col0:col0 + FFN_SUB]
        return (cb_ref[:, col0:col0 + FFN_SUB]
                + (cw[0:1] * p2 + cw[1:2] * p1 + cw[2:3] * u))

    for c in range(D_FF // FFN_SUB):
        val = conv_part(c * FFN_SUB)
        gate = conv_part(D_FF + c * FFN_SUB)
        act_scr[:, c * FFN_SUB:(c + 1) * FFN_SUB] = (val * (gate * jax.nn.sigmoid(gate))).astype(BF16)

    y = x_ref[...] + jnp.dot(act_scr[...], wdn_ref[...], preferred_element_type=F32)
    if final_norm:
        y = _rms(y, gf_ref[...])
    o_ref[...] = y


def conv_ffn(x, g, w_up, conv_w, conv_b, w_down, conv0, gf, *, batch, seq_len, tm, final_norm):
    m, d = x.shape
    f2 = 2 * D_FF
    long_seq = seq_len >= tm
    if long_seq:
        assert seq_len % tm == 0
        tiles_per_seq = seq_len // tm
        st = jnp.pad(conv0, ((0, 0), (SUBLANES - (CONV_W - 1), 0), (0, 0)))
        st_spec = pl.BlockSpec((None, SUBLANES, f2), lambda i: (i // tiles_per_seq, 0, 0))
        tail_shape = jax.ShapeDtypeStruct((batch, SUBLANES, f2), F32)
        tail_spec = pl.BlockSpec((None, SUBLANES, f2), lambda i: (i // tiles_per_seq, 0, 0))
        sem = "arbitrary"
    else:
        assert seq_len == SUBLANES and tm % seq_len == 0
        st = jnp.pad(conv0, ((0, 0), (0, seq_len - (CONV_W - 1)), (0, 0))).reshape(m, f2)
        st_spec = pl.BlockSpec((tm, f2), lambda i: (i, 0))
        tail_shape = jax.ShapeDtypeStruct((m, f2), F32)
        tail_spec = pl.BlockSpec((tm, f2), lambda i: (i, 0))
        sem = "parallel"
    const = lambda i: (0, 0)
    out, tail = pl.pallas_call(
        functools.partial(_ffn_kernel, tm=tm, seq_len=seq_len, final_norm=final_norm),
        grid=(m // tm,),
        in_specs=[pl.BlockSpec((tm, d), lambda i: (i, 0)),
                  pl.BlockSpec((1, d), const),
                  pl.BlockSpec((d, f2), const, pipeline_mode=pl.Buffered(1)),
                  pl.BlockSpec((CONV_W, f2), const),
                  pl.BlockSpec((1, f2), const),
                  pl.BlockSpec((D_FF, d), const, pipeline_mode=pl.Buffered(1)),
                  st_spec,
                  pl.BlockSpec((1, d), const)],
        out_specs=[pl.BlockSpec((tm, d), lambda i: (i, 0)), tail_spec],
        out_shape=[jax.ShapeDtypeStruct((m, d), F32), tail_shape],
        scratch_shapes=[pltpu.VMEM((tm, d), BF16),
                        pltpu.VMEM((tm, D_FF), BF16),
                        pltpu.VMEM((SUBLANES, f2), F32)],
        compiler_params=_cparams(sem),
        name="conv_ffn",
    )(x, g.reshape(1, d), w_up, conv_w, conv_b.reshape(1, f2), w_down, st, gf.reshape(1, d))
    new_state = tail.reshape(batch, -1, f2)[:, -(CONV_W - 1):]
    return out, new_state


def _log_sigmoid(x):
    return jnp.minimum(x, 0.0) - jnp.log1p(jnp.exp(-jnp.abs(x)))


def _dot_nt(a, b):
    return lax.dot_general(a, b, (((1,), (1,)), ((), ())), preferred_element_type=F32)


def _dot_tn(a, b):
    return lax.dot_general(a, b, (((0,), (0,)), ((), ())), preferred_element_type=F32)


def _mlstm_kernel(bif_ref, q_ref, k_ref, v_ref, og_ref, gz_ref, gt_ref, c0_ref, n0_ref, m0_ref,
                  lng_ref, y_ref, c_ref, n_ref, m_ref, grow_scr, gcol_scr, bcol_scr, cmax_scr,
                  *, chunk, n_chunks, bb):
    L = chunk

    @pl.when(pl.program_id(1) == 0)
    def _():
        c_ref[...] = c0_ref[...]
        n_ref[...] = n0_ref[...]
        m_ref[...] = m0_ref[...]

    t_idx = lax.broadcasted_iota(jnp.int32, (L, L), 0)
    s_idx = lax.broadcasted_iota(jnp.int32, (L, L), 1)
    causal = s_idx <= t_idx
    scale = MLSTM_HD ** -0.5

    def compute(items):
        ph = []
        for j, c in items:
            start = (j * n_chunks + c) * L
            rows = pl.ds(start if isinstance(start, int) else pl.multiple_of(start, L), L)
            for h in range(MLSTM_HEADS):
                ph.append(gate_phase(j, c, h, rows))
        for d in ph:
            d["qk_raw"] = _dot_nt(d["qb"], d["kb"])
            d["qc"] = _dot_nt(d["qb"], d["c_old"].astype(BF16))
            d["c_upd"] = _dot_tn((d["v"] * d["wl_col"]).astype(BF16), d["kb"])
        for d in ph:
            d["qk"] = d["qk_raw"] * d["w"]
            d["pv"] = jnp.dot(d["qk"].astype(BF16), d["v"].astype(BF16), preferred_element_type=F32)
        stores = []
        for d in ph:
            stores += finish_phase(d)
        return stores

    def entry(j, c, h):
        return (j * n_chunks + c) * MLSTM_HEADS + h

    lane = lax.broadcasted_iota(jnp.int32, (1, LANES), 1)
    bias_row = jnp.zeros((1, LANES), F32)
    for i in range(2 * MLSTM_HEADS):
        bias_row = jnp.where(lane == i, bif_ref[i], bias_row)
    gates_col = gz_ref[...] + bias_row
    gates_col = jnp.where(lane >= MLSTM_HEADS, _log_sigmoid(gates_col), gates_col)
    sub = lax.broadcasted_iota(jnp.int32, (2 * MLSTM_HEADS, 1), 0)
    bias_col = jnp.zeros((2 * MLSTM_HEADS, 1), F32)
    for i in range(2 * MLSTM_HEADS):
        bias_col = jnp.where(sub == i, bif_ref[i], bias_col)
    tril = causal.astype(F32)
    triu = (t_idx <= s_idx).astype(F32)
    exact = dict(preferred_element_type=F32, precision=lax.Precision.HIGHEST)
    for j in range(bb):
        for c in range(n_chunks):
            rows = slice((j * n_chunks + c) * L, (j * n_chunks + c + 1) * L)
            g_c = gates_col[rows]
            b_c = jnp.dot(tril, g_c, **exact)
            g_r = gt_ref[j, :, c, :] + bias_col
            g_r = jnp.where(sub >= MLSTM_HEADS, _log_sigmoid(g_r), g_r)
            b_r = jnp.dot(g_r, triu, **exact)
            for h in range(MLSTM_HEADS):
                g_row = g_r[h:h + 1] - b_r[MLSTM_HEADS + h:MLSTM_HEADS + h + 1]
                b_col = b_c[:, MLSTM_HEADS + h:MLSTM_HEADS + h + 1]
                e = entry(j, c, h)
                grow_scr[e, 0:1, 0:L] = g_row
                gcol_scr[e, :, 0:1] = g_c[:, h:h + 1] - b_col
                bcol_scr[e, :, 0:1] = b_col
                cmax_scr[e, :, 0:1] = jnp.max(jnp.where(causal, g_row, -jnp.inf), axis=1, keepdims=True)

    def gate_phase(j, c, h, rows):
        cols = slice(h * MLSTM_HD, (h + 1) * MLSTM_HD)
        e = entry(j, c, h)
        g_row = grow_scr[e, 0:1, 0:L]
        b_col = bcol_scr[e, :, 0:1]
        m_old = m_ref[j, h]
        mu_col = jnp.maximum(m_old, cmax_scr[e, :, 0:1])
        w = jnp.where(causal, jnp.exp(g_row - mu_col), 0.0)
        s_inter = jnp.exp(m_old - mu_col)
        q = q_ref[rows, cols]
        ks = k_ref[rows, cols] * scale
        mu_last = mu_col[L - 1:L]
        return dict(j=j, h=h, rows=rows, cols=cols, q=q, ks=ks, v=v_ref[rows, cols],
                    qb=q.astype(BF16), kb=ks.astype(BF16), c_old=c_ref[j, h], n_old=n_ref[j, h],
                    w=w, s_inter=s_inter, m_t=b_col + mu_col,
                    wl_col=jnp.exp(gcol_scr[e, :, 0:1] - mu_last),
                    dec=jnp.exp(m_old - mu_last), m_new=b_col[L - 1:L] + mu_last)

    def finish_phase(d):
        j, h, rows, cols = d["j"], d["h"], d["rows"], d["cols"]
        num = d["s_inter"] * d["qc"] + d["pv"]
        den = (d["s_inter"] * jnp.sum(d["q"] * d["n_old"], axis=1, keepdims=True)
               + jnp.sum(d["qk"], axis=1, keepdims=True))
        hh = num / jnp.maximum(jnp.abs(den), jnp.exp(-d["m_t"]))
        return [(c_ref, (j, h), d["dec"] * d["c_old"] + d["c_upd"]),
                (n_ref, (j, h), d["dec"] * d["n_old"] + jnp.sum(d["wl_col"] * d["ks"], axis=0, keepdims=True)),
                (m_ref, (j, h), d["m_new"]),
                (y_ref, (rows, cols), hh)]

    def commit(stores):
        for ref, idx, val in stores:
            ref[idx] = val

    def chunk_body(j, c, carry):
        commit(compute([(j, c)]))
        return carry

    if n_chunks == 1:
        commit(compute([(j, 0) for j in range(bb)]))
    else:
        for j in range(bb):
            lax.fori_loop(0, n_chunks, functools.partial(chunk_body, j), 0)

    for h in range(MLSTM_HEADS):
        cols = slice(h * MLSTM_HD, (h + 1) * MLSTM_HD)
        hh = y_ref[:, cols]
        xc = hh - jnp.mean(hh, -1, keepdims=True)
        hn = xc * lax.rsqrt(jnp.mean(xc * xc, -1, keepdims=True) + EPS) * lng_ref[:, cols]
        y_ref[:, cols] = jax.nn.sigmoid(og_ref[:, cols]) * hn


def mlstm_mixer(z, b_if, c0, n0, m0, ln_g, *, batch, seq_len, tb, bb):
    m_rows = batch * seq_len
    L = math.gcd(seq_len, CHUNK)
    n_tb = seq_len // tb
    n_chunks = tb // L
    assert bb == 1 or n_tb == 1
    rows = bb * tb
    n_entries = bb * n_chunks * MLSTM_HEADS
    gates = z[:, Z0_GATE:Z0_GATE + 2 * MLSTM_HEADS].reshape(batch, seq_len, 2 * MLSTM_HEADS)
    gates_t = jnp.swapaxes(gates, 1, 2).reshape(batch, 2 * MLSTM_HEADS, seq_len // L, L)
    zcol = lambda off: pl.BlockSpec((rows, HALF), lambda b, t: (b * n_tb + t, off // HALF))
    state4 = lambda a, b_: pl.BlockSpec((bb, MLSTM_HEADS, a, b_), lambda b, t: (b, 0, 0, 0))
    y, c, n, m = pl.pallas_call(
        functools.partial(_mlstm_kernel, chunk=L, n_chunks=n_chunks, bb=bb),
        grid=(batch // bb, n_tb),
        in_specs=[pl.BlockSpec(memory_space=pltpu.SMEM),
                  zcol(Z0_Q), zcol(Z0_K), zcol(Z0_V), zcol(Z0_O),
                  pl.BlockSpec((rows, LANES), lambda b, t: (b * n_tb + t, Z0_GATE // LANES)),
                  pl.BlockSpec((bb, 2 * MLSTM_HEADS, n_chunks, L), lambda b, t: (b, 0, t, 0)),
                  state4(MLSTM_HD, MLSTM_HD), state4(1, MLSTM_HD), state4(1, 1),
                  pl.BlockSpec((1, HALF), lambda b, t: (0, 0))],
        out_specs=[pl.BlockSpec((rows, HALF), lambda b, t: (b * n_tb + t, 0)),
                   state4(MLSTM_HD, MLSTM_HD), state4(1, MLSTM_HD), state4(1, 1)],
        out_shape=[jax.ShapeDtypeStruct((m_rows, HALF), F32),
                   jax.ShapeDtypeStruct((batch, MLSTM_HEADS, MLSTM_HD, MLSTM_HD), F32),
                   jax.ShapeDtypeStruct((batch, MLSTM_HEADS, 1, MLSTM_HD), F32),
                   jax.ShapeDtypeStruct((batch, MLSTM_HEADS, 1, 1), F32)],
        scratch_shapes=[pltpu.VMEM((n_entries, SUBLANES, LANES), F32)]
        + [pltpu.VMEM((n_entries, L, LANES), F32)] * 3,
        compiler_params=_cparams("parallel", "arbitrary"),
        name="mlstm",
    )(b_if, z, z, z, z, z, gates_t,
      c0, n0.reshape(batch, MLSTM_HEADS, 1, MLSTM_HD), m0.reshape(batch, MLSTM_HEADS, 1, 1),
      ln_g.reshape(1, HALF))
    return y, c, n.reshape(batch, MLSTM_HEADS, MLSTM_HD), m.reshape(batch, MLSTM_HEADS)


def _split2(x):
    hi = x.astype(BF16)
    lo = (x - hi.astype(F32)).astype(BF16)
    return hi, lo


def _seg_sum(x, ones_blk):
    hi, lo = _split2(x)
    return (jnp.dot(hi, ones_blk, preferred_element_type=F32)
            + jnp.dot(lo, ones_blk, preferred_element_type=F32))


def _softplus(x):
    return jnp.maximum(x, 0.0) + jnp.log1p(jnp.exp(-jnp.abs(x)))


def _block_ones(n, seg):
    idx = jnp.arange(n) // seg
    return (idx[:, None] == idx[None, :]).astype(BF16)


def _rwkv_prep_kernel(z_ref, st_ref, mu_ref, w0_ref, wup_ref, a0_ref, aup_ref, gup_ref, kk_ref, ka_ref,
                      rk_ref, ones_ref, r_out, w_out, k_out, v_out, kk_out, kka_out, bv_out, g_out,
                      car_scr, *, tm, seq_len):
    zr = z_ref[...]
    row = lax.broadcasted_iota(jnp.int32, zr.shape, 0)
    prev = pltpu.roll(zr, 1, 0)
    if seq_len >= tm:
        @pl.when((pl.program_id(0) * tm) % seq_len == 0)
        def _():
            car_scr[...] = st_ref[...]
        prev = jnp.where(row == 0, car_scr[SUBLANES - 1:SUBLANES], prev)
        car_scr[...] = zr[tm - SUBLANES:]
    else:
        prev = jnp.where(row % seq_len == 0, st_ref[...], prev)
    zs = zr + (prev - zr) * mu_ref[...]
    r = zs[:, :HALF]
    kr = zs[:, HALF:2 * HALF]
    vr = zs[:, 2 * HALF:3 * HALF]
    lora = zs[:, RW_LORA:RW_G]
    ones = ones_ref[...]
    w_log = -_softplus(-(w0_ref[...] + jnp.dot(jnp.tanh(lora).astype(BF16), wup_ref[...],
                                               preferred_element_type=F32))) - 0.5
    a = jax.nn.sigmoid(a0_ref[...] + jnp.dot(lora.astype(BF16), aup_ref[...], preferred_element_type=F32))
    g = jnp.dot(jax.nn.sigmoid(zs[:, RW_G:]).astype(BF16), gup_ref[...], preferred_element_type=F32)
    kk = kr * kk_ref[...]
    kk = kk * lax.rsqrt(jnp.maximum(_seg_sum(kk * kk, ones), 1e-24))
    k2 = kr * (1.0 + (a - 1.0) * ka_ref[...])
    r_out[...] = r
    w_out[...] = jnp.exp(-jnp.exp(w_log))
    k_out[...] = k2
    v_out[...] = vr
    kk_out[...] = kk
    kka_out[...] = kk * a
    bv_out[...] = _seg_sum(r * k2 * rk_ref[...], ones) * vr
    g_out[...] = g


RWKV_GROUPS = 2


def _rwkv_rec_kernel(r_ref, w_ref, k_ref, v_ref, kk_ref, kka_ref, bv_ref, g_ref, s0_ref, lng_ref, lnb_ref,
                     onesy_ref, ones_ref, y_ref, s_ref, yraw_scr, lhs_scr, ylhs_scr, tmaj_scr=None,
                     *, bb, tb):
    @pl.when(pl.program_id(1) == 0)
    def _():
        s_ref[...] = s0_ref[...]

    if tmaj_scr is not None:
        streams = (r_ref, w_ref, k_ref, v_ref, kk_ref, kka_ref, bv_ref, g_ref)
        for i, ref in enumerate(streams):
            for t in range(tb):
                tmaj_scr[i, t] = ref[:, t, :]
        r_ref, w_ref, k_ref, v_ref, kk_ref, kka_ref, bv_ref, g_ref = (
            tmaj_scr.at[i] for i in range(len(streams)))

    pairs = RWKV_HEADS // 2
    n_tiles = bb * pairs
    shape = (RWKV_HD, LANES)
    eye2 = (lax.broadcasted_iota(jnp.int32, shape, 0)
            == lax.broadcasted_iota(jnp.int32, shape, 1) % RWKV_HD)
    eye2_swapped = ((lax.broadcasted_iota(jnp.int32, shape, 0) ^ 1)
                    == lax.broadcasted_iota(jnp.int32, shape, 1) % RWKV_HD)
    lane_even = lax.broadcasted_iota(jnp.int32, (1, LANES), 1) % 2 == 0
    y_rows = lambda idx: slice((idx // 2) * RWKV_HD, (idx // 2 + 1) * RWKV_HD)
    y_cols = lambda idx: slice((idx % 2) * LANES, (idx % 2 + 1) * LANES)

    per_group = n_tiles // RWKV_GROUPS

    def step(t, carry):
        def tile(idx):
            b, p = divmod(idx, pairs)
            cols = slice(p * LANES, (p + 1) * LANES)
            return b, p, (lambda ref: ref[t, b:b + 1, cols])

        groups = [range(g * per_group, (g + 1) * per_group) for g in range(RWKV_GROUPS)]
        g_rows = per_group * RWKV_HD * 3 // 2
        sums = []
        for g, grp in enumerate(groups):
            for idx in grp:
                b, p, row = tile(idx)
                rel = idx - grp[0]
                base = g * g_rows + rel * RWKV_HD
                hi, lo = _split2(s_ref[b, p] * row(kk_ref))
                lhs_scr[base:base + RWKV_HD, :LANES] = hi
                lhs_scr[base:base + RWKV_HD, LANES:] = lo
                v_row = row(v_ref)
                v_hi = v_row.astype(BF16).astype(F32)
                v_lo = v_row - v_hi
                v_lo_swapped = jnp.where(lane_even, pltpu.roll(v_lo, LANES - 1, 1), pltpu.roll(v_lo, 1, 1))
                v_base = g * g_rows + per_group * RWKV_HD + (rel // 2) * RWKV_HD
                lhs_scr[v_base:v_base + RWKV_HD, y_cols(rel)] = jnp.where(
                    eye2, v_hi, jnp.where(eye2_swapped, v_lo_swapped, 0.0)).astype(BF16)
            sums.append(jnp.dot(lhs_scr[g * g_rows:(g + 1) * g_rows, :], onesy_ref[...],
                                preferred_element_type=F32))
        ysums = []
        for grp, sm in zip(groups, sums):
            for idx in grp:
                b, p, row = tile(idx)
                rel = idx - grp[0]
                sa_rows = slice(rel * RWKV_HD, (rel + 1) * RWKV_HD)
                sa = sm[sa_rows, :LANES] + sm[sa_rows, LANES:]
                v_base = per_group * RWKV_HD + (rel // 2) * RWKV_HD
                v_tile = sm[v_base:v_base + RWKV_HD, y_cols(rel)]
                s = s_ref[b, p] * row(w_ref) - sa * row(kka_ref) + v_tile * row(k_ref)
                s_ref[b, p] = s
                ylhs_scr[y_rows(idx), y_cols(idx)] = (s * row(r_ref)).astype(BF16)
            g_rows = slice(y_rows(grp[0]).start, y_rows(grp[-1]).stop)
            ysums.append(jnp.dot(ylhs_scr[g_rows, :], onesy_ref[...], preferred_element_type=F32))
        for grp, ys in zip(groups, ysums):
            for idx in grp:
                b, p, row = tile(idx)
                rel = slice(y_rows(idx).start - y_rows(grp[0]).start, y_rows(idx).stop - y_rows(grp[0]).start)
                yraw_scr[t, b:b + 1, p * LANES:(p + 1) * LANES] = jnp.sum(
                    jnp.where(eye2, ys[rel, y_cols(idx)], 0.0), axis=0, keepdims=True)
        return carry

    lax.fori_loop(0, tb, step, 0)

    ones = ones_ref[...]
    y = yraw_scr[...].reshape(tb * bb, HALF)
    xc = y - _seg_sum(y, ones) * (1.0 / RWKV_HD)
    var = _seg_sum(xc * xc, ones) * (1.0 / RWKV_HD)
    yn = xc * lax.rsqrt(var + RWKV_LN_EPS) * lng_ref[...] + lnb_ref[...]
    out = (yn + bv_ref[...].reshape(tb * bb, HALF)) * g_ref[...].reshape(tb * bb, HALF)
    if tmaj_scr is None:
        y_ref[...] = out.reshape(tb, bb, HALF)
    else:
        for t in range(tb):
            y_ref[:, t, :] = out[t * bb:(t + 1) * bb]


def rwkv_mixer(z, s0, shift0, mu, w0, w_up, a0, a_up, g_up, k_k, k_a, r_k, ln_g, ln_b,
               *, batch, seq_len, tm, bb, tb):
    m_rows = batch * seq_len
    tm_shape = (seq_len, batch, HALF)
    if seq_len >= tm:
        tiles_per_seq = seq_len // tm
        st = jnp.pad(shift0[:, None], ((0, 0), (SUBLANES - 1, 0), (0, 0)))
        st_spec = pl.BlockSpec((None, SUBLANES, RWKV_COLS), lambda i: (i // tiles_per_seq, 0, 0))
        sem = "arbitrary"
        tok = pl.BlockSpec((tm, HALF), lambda i: (i % tiles_per_seq, i // tiles_per_seq))
        tok_shape = (seq_len, batch * HALF)
    else:
        st = jnp.pad(shift0[:, None], ((0, 0), (0, seq_len - 1), (0, 0))).reshape(m_rows, RWKV_COLS)
        st_spec = pl.BlockSpec((tm, RWKV_COLS), lambda i: (i, 0))
        sem = "parallel"
        tok = pl.BlockSpec((tm, HALF), lambda i: (i, 0))
        tok_shape = (m_rows, HALF)
    row = lambda a: a.reshape(1, -1)
    const = lambda i: (0, 0)
    vec = pl.BlockSpec((1, HALF), const)
    lora_rows = RWKV_W_LORA + RWKV_A_LORA
    w_up_p = jnp.pad(w_up, ((0, RWKV_A_LORA), (0, 0))).astype(BF16)
    a_up_p = jnp.pad(a_up, ((RWKV_W_LORA, 0), (0, 0))).astype(BF16)
    ones_half = _block_ones(HALF, RWKV_HD)
    prep = pl.pallas_call(
        functools.partial(_rwkv_prep_kernel, tm=tm, seq_len=seq_len),
        grid=(m_rows // tm,),
        in_specs=[pl.BlockSpec((tm, RWKV_COLS), lambda i: (i, Z0_RWKV // RWKV_COLS)),
                  st_spec,
                  pl.BlockSpec((1, RWKV_COLS), const),
                  vec, pl.BlockSpec((lora_rows, HALF), const),
                  vec, pl.BlockSpec((lora_rows, HALF), const),
                  pl.BlockSpec((RWKV_G_LORA, HALF), const),
                  vec, vec, vec,
                  pl.BlockSpec((HALF, HALF), const)],
        out_specs=[tok] * 8,
        out_shape=[jax.ShapeDtypeStruct(tok_shape, F32)] * 8,
        scratch_shapes=[pltpu.VMEM((SUBLANES, RWKV_COLS), F32)],
        compiler_params=_cparams(sem),
        name="rwkv_prep",
    )(z, st, row(mu), row(w0), w_up_p, row(a0), a_up_p, g_up.astype(BF16), row(k_k), row(k_a), row(r_k),
      ones_half)
    pairs = RWKV_HEADS // 2
    s_pairs = (s0.reshape(batch, pairs, 2, RWKV_HD, RWKV_HD).transpose(0, 1, 3, 2, 4)
               .reshape(batch, pairs, RWKV_HD, LANES))
    scratch = [pltpu.VMEM((tb, bb, HALF), F32),
               pltpu.VMEM((bb * pairs * RWKV_HD * 3 // 2, 2 * LANES), BF16),
               pltpu.VMEM((bb * pairs * RWKV_HD // 2, 2 * LANES), BF16)]
    if seq_len >= tm:
        seqs = [a.reshape(tm_shape) for a in prep]
        blk = pl.BlockSpec((tb, bb, HALF), lambda b, t: (t, b, 0))
        y_shape = tm_shape
    else:
        assert tb == seq_len
        seqs = [a.reshape(batch, seq_len, HALF) for a in prep]
        blk = pl.BlockSpec((bb, tb, HALF), lambda b, t: (b, 0, 0))
        y_shape = (batch, seq_len, HALF)
        scratch.append(pltpu.VMEM((len(prep), tb, bb, HALF), F32))
    s_spec = pl.BlockSpec((bb, pairs, RWKV_HD, LANES), lambda b, t: (b, 0, 0, 0))
    const2 = lambda b, t: (0, 0)
    y, s_new = pl.pallas_call(
        functools.partial(_rwkv_rec_kernel, bb=bb, tb=tb),
        grid=(batch // bb, seq_len // tb),
        in_specs=[blk] * 8 + [s_spec,
                              pl.BlockSpec((1, HALF), const2), pl.BlockSpec((1, HALF), const2),
                              pl.BlockSpec((2 * LANES, 2 * LANES), const2),
                              pl.BlockSpec((HALF, HALF), const2)],
        out_specs=[blk, s_spec],
        out_shape=[jax.ShapeDtypeStruct(y_shape, F32),
                   jax.ShapeDtypeStruct((batch, pairs, RWKV_HD, LANES), F32)],
        scratch_shapes=scratch,
        compiler_params=_cparams("parallel", "arbitrary"),
        name="rwkv_rec",
    )(*seqs, s_pairs, row(ln_g), row(ln_b),
      _block_ones(2 * LANES, RWKV_HD), ones_half)
    s_new = (s_new.reshape(batch, pairs, RWKV_HD, 2, RWKV_HD).transpose(0, 1, 3, 2, 4)
             .reshape(batch, RWKV_HEADS, RWKV_HD, RWKV_HD))
    if seq_len < tm:
        y = y.reshape(m_rows, HALF)
    return y, s_new


GLA_SUB = 16


def _gla_kernel(q_ref, k_ref, v_ref, g_ref, ad_ref, aup_ref, ab_ref, lng_ref, s0_ref,
                y_ref, s_ref, bc_scr, *, chunk, n_chunks, bb):
    L = chunk
    sub = min(GLA_SUB, L)

    @pl.when(pl.program_id(1) == 0)
    def _():
        s_ref[...] = s0_ref[...]

    lane = lax.broadcasted_iota(jnp.int32, (1, LANES), 1)
    head_mask = [(lane < GLA_DK).astype(F32), (lane >= GLA_DK).astype(F32)]
    tril = (lax.broadcasted_iota(jnp.int32, (L, L), 1)
            <= lax.broadcasted_iota(jnp.int32, (L, L), 0)).astype(F32)
    eye = (lax.broadcasted_iota(jnp.int32, (LANES, LANES), 0)
           == lax.broadcasted_iota(jnp.int32, (LANES, LANES), 1))

    pre = jnp.dot(ad_ref[...].astype(BF16), aup_ref[...], preferred_element_type=F32) + ab_ref[...]
    la = _log_sigmoid(pre) * (1.0 / GLA_TAU)
    for blk in range(bb * n_chunks):
        rows = slice(blk * L, (blk + 1) * L)
        bc_scr[rows, :] = jnp.dot(tril, la[rows], preferred_element_type=F32,
                                  precision=lax.Precision.HIGHEST)

    stack = lambda x: jnp.concatenate([x * head_mask[0], x * head_mask[1]], axis=0).astype(BF16)

    def compute(items):
        units = []
        for j, c in items:
            start = (j * n_chunks + c) * L
            rows = pl.ds(start if isinstance(start, int) else pl.multiple_of(start, L), L)
            for p in range(GLA_HEADS // 2):
                kcols = slice(p * LANES, (p + 1) * LANES)
                units.append(dict(
                    j=j, p=p, rows=rows, qp=q_ref[rows, kcols] * (GLA_DK ** -0.5), kp=k_ref[rows, kcols],
                    bc=bc_scr[rows, kcols], s_old=s_ref[j, p],
                    v_pair=v_ref[rows, 2 * p * GLA_DV:2 * (p + 1) * GLA_DV].astype(BF16)))
        for d in units:
            qp, kp, bc, v_pair = d["qp"], d["kp"], d["bc"], d["v_pair"]
            d["inter"] = jnp.dot(stack(qp * jnp.exp(bc)), d["s_old"].astype(BF16), preferred_element_type=F32)
            d["att"] = []
            for i in range(L // sub):
                lo, hi = i * sub, (i + 1) * sub
                c_i = bc[lo - 1:lo] if i > 0 else jnp.zeros((1, LANES), F32)
                qe = qp[lo:hi] * jnp.exp(bc[lo:hi] - c_i)
                ke = kp[:hi] * jnp.exp(c_i - bc[:hi])
                d["att"].append(_dot_nt(stack(qe), ke.astype(BF16)))
            bl = bc[L - 1:L]
            v_stack = jnp.concatenate([v_pair[:, :GLA_DV], v_pair[:, GLA_DV:]], axis=0)
            d["s_upd"] = _dot_tn(stack(kp * jnp.exp(bl - bc)), v_stack)
            d["dec_col"] = jnp.sum(jnp.where(eye, jnp.exp(bl), 0.0), axis=1, keepdims=True)
        for d in units:
            d["intra"] = []
            for i, att in enumerate(d["att"]):
                lo, hi = i * sub, (i + 1) * sub
                t_idx = lo + lax.broadcasted_iota(jnp.int32, (2 * sub, hi), 0) % sub
                s_idx = lax.broadcasted_iota(jnp.int32, (2 * sub, hi), 1)
                att = jnp.where(s_idx <= t_idx, att, 0.0)
                d["intra"].append(jnp.dot(att.astype(BF16), d["v_pair"][:hi], preferred_element_type=F32))
        stores = []
        for d in units:
            for e in range(2):
                h = 2 * d["p"] + e
                vcols = slice(h * GLA_DV, (h + 1) * GLA_DV)
                o = d["inter"][e * L:(e + 1) * L] + jnp.concatenate(
                    [blk[e * sub:(e + 1) * sub, e * GLA_DV:(e + 1) * GLA_DV] for blk in d["intra"]], axis=0)
                stores.append((y_ref, (d["rows"], vcols), o))
            stores.append((s_ref, (d["j"], d["p"]), d["dec_col"] * d["s_old"] + d["s_upd"]))
        return stores

    def commit(stores):
        for ref, idx, val in stores:
            ref[idx] = val

    def chunk_body(j, c, carry):
        commit(compute([(j, c)]))
        return carry

    if n_chunks == 1:
        commit(compute([(j, 0) for j in range(bb)]))
    else:
        for j in range(bb):
            lax.fori_loop(0, n_chunks, functools.partial(chunk_body, j), 0)

    for h in range(GLA_HEADS):
        vcols = slice(h * GLA_DV, (h + 1) * GLA_DV)
        o = y_ref[:, vcols]
        on = o * lax.rsqrt(jnp.mean(o * o, -1, keepdims=True) + EPS) * lng_ref[:, vcols]
        gate = g_ref[:, vcols]
        y_ref[:, vcols] = on * (gate * jax.nn.sigmoid(gate))


def gla_mixer(z, s0, a_up, a_b, ln_g, *, batch, seq_len, tb, bb, time_major):
    m_rows = batch * seq_len
    L = math.gcd(seq_len, CHUNK)
    n_tb = seq_len // tb
    pairs = GLA_HEADS // 2
    assert bb == 1 or (n_tb == 1 and not time_major)
    rows = bb * tb
    if time_major:
        zblk = lambda width, off: pl.BlockSpec((tb, width), lambda b, t: (t, (b * Z1_COLS + off) // width))
    else:
        zblk = lambda width, off: pl.BlockSpec((rows, width), lambda b, t: (b * n_tb + t, off // width))
    s_spec = pl.BlockSpec((bb, pairs, 2 * GLA_DK, GLA_DV), lambda b, t: (b, 0, 0, 0))
    const = lambda b, t: (0, 0)
    a_up_p = jnp.pad(a_up, ((0, LANES - GLA_GATE_RANK), (0, 0))).astype(BF16)
    y, s_new = pl.pallas_call(
        functools.partial(_gla_kernel, chunk=L, n_chunks=tb // L, bb=bb),
        grid=(batch // bb, n_tb),
        in_specs=[zblk(GLA_QK, Z1_Q), zblk(GLA_QK, Z1_K), zblk(HALF, Z1_V), zblk(HALF, Z1_G),
                  zblk(LANES, Z1_AD),
                  pl.BlockSpec((LANES, GLA_QK), const), pl.BlockSpec((1, GLA_QK), const),
                  pl.BlockSpec((1, HALF), const), s_spec],
        out_specs=[pl.BlockSpec((rows, HALF), lambda b, t: (b * n_tb + t, 0)), s_spec],
        out_shape=[jax.ShapeDtypeStruct((m_rows, HALF), F32),
                   jax.ShapeDtypeStruct((batch, pairs, 2 * GLA_DK, GLA_DV), F32)],
        scratch_shapes=[pltpu.VMEM((rows, GLA_QK), F32)],
        compiler_params=_cparams("parallel", "arbitrary"),
        name="gla",
    )(z, z, z, z, z, a_up_p, a_b.reshape(1, GLA_QK), ln_g.reshape(1, HALF),
      s0.reshape(batch, pairs, 2 * GLA_DK, GLA_DV))
    return y, s_new.reshape(batch, GLA_HEADS, GLA_DK, GLA_DV)


S5_KT = LANES // S5_GROUP
S5_SCAN_SPLIT = 2


def _s5_param_kernel(lre_ref, lim_ref, step_ref, bre_ref, bim_ref, bbre_ref, bbim_ref, pre_ref, pim_ref):
    lre, lim = lre_ref[...], lim_ref[...]
    step = jnp.exp(step_ref[...])
    mag = jnp.exp(lre * step)
    bar_re = mag * jnp.cos(lim * step)
    bar_im = mag * jnp.sin(lim * step)
    inv = 1.0 / (lre * lre + lim * lim)
    cre = ((bar_re - 1.0) * lre + bar_im * lim) * inv
    cim = (bar_im * lre - (bar_re - 1.0) * lim) * inv
    for g in range(S5_GROUPS):
        cr, ci = cre[g:g + 1], cim[g:g + 1]
        bbre_ref[g] = cr * bre_ref[g] - ci * bim_ref[g]
        bbim_ref[g] = cr * bim_ref[g] + ci * bre_ref[g]
    pre_ref[0] = bar_re
    pim_ref[0] = bar_im


def _s5_kernel(u_ref, wbre_ref, wbim_ref, lre_ref, lim_ref, x0re_ref, x0im_ref, wcre_ref, wcim_ref,
               d_ref, wglu_ref, y_ref, xre_ref, xim_ref, sre_scr, sim_scr, *, tt, bb, row_major):
    @pl.when(pl.program_id(1) == 0)
    def _():
        xre_ref[...] = x0re_ref[...]
        xim_ref[...] = x0im_ref[...]

    rows_all = tt * bb
    if row_major:
        u = jnp.concatenate([u_ref[:, t, :] for t in range(tt)], axis=0)
    else:
        u = u_ref[...].reshape(rows_all, S5_CH)
    ub = u.astype(BF16)
    nblk = S5_CH // LANES
    wide = S5_STATE // nblk
    for kt in range(nblk):
        cols = slice(kt * wide, (kt + 1) * wide)
        ukt = ub[:, kt * LANES:(kt + 1) * LANES]
        sre_scr[:, cols] = jnp.dot(ukt, wbre_ref[kt], preferred_element_type=F32)
        sim_scr[:, cols] = jnp.dot(ukt, wbim_ref[kt], preferred_element_type=F32)

    for part in range(S5_SCAN_SPLIT):
        width = S5_STATE // S5_SCAN_SPLIT
        cols = slice(part * width, (part + 1) * width)
        lr, li = lre_ref[:, cols], lim_ref[:, cols]

        def step(t, carry, cols=cols, lr=lr, li=li):
            cr, ci = carry
            rows = pl.ds(pl.multiple_of(t * bb, bb), bb)
            nr = sre_scr[rows, cols] + (lr * cr - li * ci)
            ni = sim_scr[rows, cols] + (lr * ci + li * cr)
            sre_scr[rows, cols] = nr
            sim_scr[rows, cols] = ni
            return nr, ni

        cr, ci = lax.fori_loop(0, tt, step, (xre_ref[:, cols], xim_ref[:, cols]))
        xre_ref[:, cols] = cr
        xim_ref[:, cols] = ci

    parts = []
    for nt in range(nblk):
        cols = slice(nt * wide, (nt + 1) * wide)
        parts.append(jnp.dot(sre_scr[:, cols].astype(BF16), wcre_ref[nt], preferred_element_type=F32)
                     - jnp.dot(sim_scr[:, cols].astype(BF16), wcim_ref[nt], preferred_element_type=F32))
    y = jnp.concatenate(parts, axis=1) + d_ref[...] * u
    ys = y * (0.5 * (1.0 + jnp.tanh(math.sqrt(2.0 / math.pi) * (y + 0.044715 * (y * y * y)))))
    out = ys * jax.nn.sigmoid(jnp.dot(ys.astype(BF16), wglu_ref[...], preferred_element_type=F32))
    if row_major:
        for t in range(tt):
            y_ref[:, t, :] = out[t * bb:(t + 1) * bb]
    else:
        y_ref[...] = out.reshape(tt, bb, S5_CH)


def s5_mixer(z, x0_re, x0_im, lam_re, lam_im, log_step, b_re, b_im, c_re, c_im, d_skip,
             w_glu, *, batch, seq_len, tt, bb, time_major):
    u_col_block = Z1_U // S5_CH
    if time_major:
        z = z.reshape(seq_len, batch, Z1_COLS)
        u_spec = pl.BlockSpec((tt, bb, S5_CH), lambda b, t: (t, b, u_col_block))
        y_spec = pl.BlockSpec((tt, bb, S5_CH), lambda b, t: (t, b, 0))
        y_shape = (seq_len, batch, S5_CH)
    else:
        assert tt == seq_len
        z = z.reshape(batch, seq_len, Z1_COLS)
        u_spec = pl.BlockSpec((bb, tt, S5_CH), lambda b, t: (b, 0, u_col_block))
        y_spec = pl.BlockSpec((bb, tt, S5_CH), lambda b, t: (b, 0, 0))
        y_shape = (batch, seq_len, S5_CH)
    gpc = (S5_GROUPS, S5_GROUP, S5_P)
    bb_re, bb_im, p_re, p_im = pl.pallas_call(
        _s5_param_kernel,
        out_shape=[jax.ShapeDtypeStruct(gpc, F32), jax.ShapeDtypeStruct(gpc, F32),
                   jax.ShapeDtypeStruct((1, S5_GROUPS, S5_P), F32),
                   jax.ShapeDtypeStruct((1, S5_GROUPS, S5_P), F32)],
        name="s5_params",
    )(lam_re, lam_im, log_step.reshape(S5_GROUPS, 1), jnp.swapaxes(b_re, 1, 2), jnp.swapaxes(b_im, 1, 2))

    nblk = S5_CH // LANES
    eye = jnp.eye(S5_KT, dtype=F32)

    def in_blocks(w):
        w = w.reshape(nblk, S5_KT, S5_GROUP, 1, S5_P) * eye[None, :, None, :, None]
        return w.reshape(nblk, LANES, S5_KT * S5_P).astype(BF16)

    def out_blocks(c):
        w = jnp.swapaxes(c, 1, 2).reshape(nblk, S5_KT, S5_P, 1, S5_GROUP) * eye[None, :, None, :, None]
        return w.reshape(nblk, S5_KT * S5_P, LANES).astype(BF16)

    lam_rows = lambda p: jnp.broadcast_to(p.reshape(1, S5_STATE), (bb, S5_STATE))
    const2 = lambda b, t: (0, 0)
    const3 = lambda b, t: (0, 0, 0)
    st_spec = pl.BlockSpec((bb, S5_STATE), lambda b, t: (b, 0))
    wb_spec = pl.BlockSpec((nblk, LANES, S5_KT * S5_P), const3)
    wc_spec = pl.BlockSpec((nblk, S5_KT * S5_P, LANES), const3)
    lam_spec = pl.BlockSpec((bb, S5_STATE), const2)
    y, x_re, x_im = pl.pallas_call(
        functools.partial(_s5_kernel, tt=tt, bb=bb, row_major=not time_major),
        grid=(batch // bb, seq_len // tt),
        in_specs=[u_spec,
                  wb_spec, wb_spec, lam_spec, lam_spec, st_spec, st_spec, wc_spec, wc_spec,
                  pl.BlockSpec((1, S5_CH), const2), pl.BlockSpec((S5_CH, S5_CH), const2)],
        out_specs=[y_spec, st_spec, st_spec],
        out_shape=[jax.ShapeDtypeStruct(y_shape, F32),
                   jax.ShapeDtypeStruct((batch, S5_STATE), F32),
                   jax.ShapeDtypeStruct((batch, S5_STATE), F32)],
        scratch_shapes=[pltpu.VMEM((tt * bb, S5_STATE), F32), pltpu.VMEM((tt * bb, S5_STATE), F32)],
        compiler_params=_cparams("parallel", "arbitrary"),
        name="s5",
    )(z, in_blocks(bb_re), in_blocks(bb_im), lam_rows(p_re), lam_rows(p_im),
      x0_re.reshape(batch, S5_STATE), x0_im.reshape(batch, S5_STATE),
      out_blocks(c_re), out_blocks(c_im), d_skip.reshape(1, S5_CH), w_glu.astype(BF16))
    shape = (batch, S5_GROUPS, S5_P)
    if not time_major:
        y = y.reshape(batch * seq_len, S5_CH)
    return y, x_re.reshape(shape), x_im.reshape(shape)


def _pack_even_w_in(w_in):
    d = w_in.shape[0]
    m_cols = 4 * HALF + 2 * MLSTM_HEADS
    parts = [w_in[:, m_cols:],
             w_in[:, 4 * HALF:m_cols], jnp.zeros((d, Z0_Q - Z0_GATE - 2 * MLSTM_HEADS), w_in.dtype),
             w_in[:, :4 * HALF]]
    return jnp.concatenate(parts, axis=1).astype(BF16)


def _pack_odd_w_in(w_in):
    gla_main = 2 * GLA_QK + 2 * HALF
    gla_cols = gla_main + GLA_GATE_RANK
    parts = [w_in[:, :gla_main], w_in[:, gla_cols:], w_in[:, gla_main:gla_cols]]
    w = jnp.concatenate(parts, axis=1)
    return jnp.pad(w, ((0, 0), (0, Z1_COLS - w.shape[1]))).astype(BF16)


def _trunk(x, st, prm, *, batch, seq_len):
    (st_C, st_n, st_m, st_S, st_shift, st_gla, st_re, st_im, st_conv) = st
    (norm_mix, norm_ffn, norm_final,
     e_w_in, e_b_if, e_mu, e_w0, e_w_up, e_a0, e_a_up, e_g_up, e_k_k, e_k_a, e_r_k, e_ln_m_g, e_ln_r_g,
     e_ln_r_b, e_w_out,
     o_w_in, o_a_up, o_a_b, o_ln_g, o_lam_re, o_lam_im, o_log_step, o_b_re, o_b_im, o_c_re, o_c_im, o_d,
     o_w_glu, o_w_out,
     f_w_up, f_conv_w, f_conv_b, f_w_down) = prm
    assert len(e_w_in) == 1 and len(o_w_in) == 1 and len(f_w_up) == 2, "wired for depth 2"
    long_seq = seq_len >= 512
    assert long_seq or seq_len == SUBLANES
    m_rows = batch * seq_len
    tiles = dict(
        tm_proj=512,
        tm_tok=512 if long_seq else 256,
        tm_ffn=512 if long_seq else 128,
        tb=512 if long_seq else seq_len,
        bb_chunked=1 if long_seq else 4,
        tb_rwkv=64 if long_seq else seq_len,
        tt_s5=128 if long_seq else seq_len,
    )
    x = x.reshape(m_rows, D_MODEL)
    ffn = functools.partial(conv_ffn, batch=batch, seq_len=seq_len, tm=tiles["tm_ffn"])

    z0 = norm_matmul(x, norm_mix[0], _pack_even_w_in(e_w_in[0]), tm=tiles["tm_proj"], tn=512)
    y_m, c_new, n_new, m_new = mlstm_mixer(z0, e_b_if[0], st_C[0], st_n[0], st_m[0], e_ln_m_g[0],
                                           batch=batch, seq_len=seq_len, tb=tiles["tb"], bb=tiles["bb_chunked"])
    y_r, s_new = rwkv_mixer(z0, st_S[0], st_shift[0], e_mu[0], e_w0[0], e_w_up[0], e_a0[0], e_a_up[0],
                            e_g_up[0], e_k_k[0], e_k_a[0], e_r_k[0], e_ln_r_g[0], e_ln_r_b[0],
                            batch=batch, seq_len=seq_len, tm=tiles["tm_tok"], bb=SUBLANES,
                            tb=tiles["tb_rwkv"])
    shift_new = z0.reshape(batch, seq_len, Z0_COLS)[:, -1, Z0_RWKV:Z0_RWKV + RWKV_COLS]
    w_out = e_w_out[0].astype(BF16)
    x = out_proj(x, y_m, y_r, w_out[:HALF], w_out[HALF:], tm=tiles["tm_tok"])
    x, conv_new0 = ffn(x, norm_ffn[0], f_w_up[0].astype(BF16), f_conv_w[0], f_conv_b[0],
                       f_w_down[0].astype(BF16), st_conv[0], norm_final, final_norm=False)

    z1 = norm_matmul(x, norm_mix[1], _pack_odd_w_in(o_w_in[0]), tm=tiles["tm_proj"], tn=512,
                     time_major_seq_len=seq_len if long_seq else None)
    y_g, gla_new = gla_mixer(z1, st_gla[0], o_a_up[0], o_a_b[0], o_ln_g[0],
                             batch=batch, seq_len=seq_len, tb=tiles["tb"], bb=tiles["bb_chunked"],
                             time_major=long_seq)
    y_s, re_new, im_new = s5_mixer(z1, st_re[0], st_im[0], o_lam_re[0], o_lam_im[0],
                                   o_log_step[0], o_b_re[0], o_b_im[0], o_c_re[0], o_c_im[0], o_d[0],
                                   o_w_glu[0], batch=batch, seq_len=seq_len, tt=tiles["tt_s5"],
                                   bb=SUBLANES, time_major=long_seq)
    w_out = o_w_out[0].astype(BF16)
    x = out_proj(x, y_g, y_s, w_out[:HALF], w_out[HALF:], tm=tiles["tm_tok"])
    x, conv_new1 = ffn(x, norm_ffn[1], f_w_up[1].astype(BF16), f_conv_w[1], f_conv_b[1],
                       f_w_down[1].astype(BF16), st_conv[1], norm_final, final_norm=True)
    new_state = (c_new[None], n_new[None], m_new[None], s_new[None], shift_new[None],
                 gla_new[None], re_new[None], im_new[None], jnp.stack([conv_new0, conv_new1]))
    return x.reshape(batch, seq_len, D_MODEL), new_state


def kernel(x_prompt, x_sample, state_mlstm_C, state_mlstm_n, state_mlstm_m, state_rwkv_S, state_rwkv_shift,
           state_gla_S, state_s5_re, state_s5_im, state_ffn_conv,
           norm_mix, norm_ffn, norm_final,
           e_w_in, e_b_if, e_mu, e_w0, e_w_up, e_a0, e_a_up, e_g_up, e_k_k, e_k_a, e_r_k,
           e_ln_m_g, e_ln_r_g, e_ln_r_b, e_w_out,
           o_w_in, o_a_up, o_a_b, o_ln_g, o_lam_re, o_lam_im, o_log_step, o_b_re, o_b_im, o_c_re, o_c_im,
           o_d, o_w_glu, o_w_out,
           f_w_up, f_conv_w, f_conv_b, f_w_down):
    params = (norm_mix, norm_ffn, norm_final,
              e_w_in, e_b_if, e_mu, e_w0, e_w_up, e_a0, e_a_up, e_g_up, e_k_k, e_k_a, e_r_k,
              e_ln_m_g, e_ln_r_g, e_ln_r_b, e_w_out,
              o_w_in, o_a_up, o_a_b, o_ln_g, o_lam_re, o_lam_im, o_log_step, o_b_re, o_b_im, o_c_re, o_c_im,
              o_d, o_w_glu, o_w_out,
              f_w_up, f_conv_w, f_conv_b, f_w_down)
    bp, tp, _ = x_prompt.shape
    bs, ts, _ = x_sample.shape
    n_even, n_odd, depth = state_mlstm_C.shape[0], state_gla_S.shape[0], state_ffn_conv.shape[0]
    init = (jnp.zeros((n_even, bp, MLSTM_HEADS, MLSTM_HD, MLSTM_HD), F32),
            jnp.zeros((n_even, bp, MLSTM_HEADS, MLSTM_HD), F32),
            jnp.full((n_even, bp, MLSTM_HEADS), M_INIT, F32),
            jnp.zeros((n_even, bp, RWKV_HEADS, RWKV_HD, RWKV_HD), F32),
            jnp.zeros((n_even, bp, RWKV_COLS), F32),
            jnp.zeros((n_odd, bp, GLA_HEADS, GLA_DK, GLA_DV), F32),
            jnp.zeros((n_odd, bp, S5_GROUPS, S5_P), F32),
            jnp.zeros((n_odd, bp, S5_GROUPS, S5_P), F32),
            jnp.zeros((depth, bp, CONV_W - 1, 2 * D_FF), F32))
    past = (state_mlstm_C, state_mlstm_n, state_mlstm_m, state_rwkv_S, state_rwkv_shift,
            state_gla_S, state_s5_re, state_s5_im, state_ffn_conv)
    y_prompt, p_state = _trunk(x_prompt, init, params, batch=bp, seq_len=tp)
    y_sample, s_state = _trunk(x_sample, past, params, batch=bs, seq_len=ts)
    return (y_prompt, y_sample) + tuple(p_state) + tuple(s_state)
```

```python
import functools
import math

import jax
import jax.numpy as jnp
from jax import lax
from jax.experimental import pallas as pl
from jax.experimental.pallas import tpu as pltpu

F32 = jnp.float32
BF16 = jnp.bfloat16

D_MODEL = 1024
HALF = D_MODEL // 2
MLSTM_HEADS = 4
MLSTM_HD = HALF // MLSTM_HEADS
RWKV_HD = 64
RWKV_HEADS = HALF // RWKV_HD
RWKV_W_LORA = 64
RWKV_A_LORA = 64
RWKV_G_LORA = 128
RWKV_COLS = 3 * HALF + RWKV_W_LORA + RWKV_A_LORA + RWKV_G_LORA
GLA_HEADS = 4
GLA_DK = 64
GLA_DV = 128
GLA_QK = GLA_HEADS * GLA_DK
GLA_GATE_RANK = 16
GLA_TAU = 16.0
S5_CH = HALF
S5_GROUP = 16
S5_GROUPS = S5_CH // S5_GROUP
S5_P = 64
S5_STATE = S5_GROUPS * S5_P
D_FF = 2816
CONV_W = 3
CHUNK = 64
EPS = 1e-6
RWKV_LN_EPS = 64e-5
M_INIT = -1e30

LANES = 128
SUBLANES = 8
VMEM_LIMIT_BYTES = 56 * 1024 * 1024

Z0_RWKV = 0
Z0_GATE = RWKV_COLS
Z0_Q = Z0_GATE + 2 * LANES
Z0_K, Z0_V, Z0_O = Z0_Q + HALF, Z0_Q + 2 * HALF, Z0_Q + 3 * HALF
Z0_COLS = Z0_Q + 4 * HALF
RW_LORA = 3 * HALF
RW_G = RW_LORA + RWKV_W_LORA + RWKV_A_LORA
Z1_Q, Z1_K, Z1_V, Z1_G, Z1_U = 0, GLA_QK, 2 * GLA_QK, 2 * GLA_QK + HALF, 2 * GLA_QK + 2 * HALF
Z1_AD = Z1_U + S5_CH
Z1_COLS = 2560


def _cparams(*sem):
    return pltpu.CompilerParams(dimension_semantics=sem, vmem_limit_bytes=VMEM_LIMIT_BYTES)


def _rms(x, g):
    return x * lax.rsqrt(jnp.mean(x * x, -1, keepdims=True) + EPS) * g


def _norm_matmul_kernel(x_ref, g_ref, w_ref, o_ref, *, tn):
    h = _rms(x_ref[...], g_ref[...]).astype(BF16)
    for j in range(w_ref.shape[1] // tn):
        cols = slice(j * tn, (j + 1) * tn)
        o_ref[:, cols] = jnp.dot(h, w_ref[:, cols], preferred_element_type=F32)


def norm_matmul(x, g, w, *, tm, tn, time_major_seq_len=None):
    m, d = x.shape
    n = w.shape[1]
    if time_major_seq_len is None:
        out_shape = (m, n)
        out_spec = pl.BlockSpec((tm, n), lambda i: (i, 0))
    else:
        tiles_per_seq = time_major_seq_len // tm
        out_shape = (time_major_seq_len, (m // time_major_seq_len) * n)
        out_spec = pl.BlockSpec((tm, n), lambda i: (i % tiles_per_seq, i // tiles_per_seq))
    return pl.pallas_call(
        functools.partial(_norm_matmul_kernel, tn=tn),
        grid=(m // tm,),
        in_specs=[pl.BlockSpec((tm, d), lambda i: (i, 0)),
                  pl.BlockSpec((1, d), lambda i: (0, 0)),
                  pl.BlockSpec((d, n), lambda i: (0, 0), pipeline_mode=pl.Buffered(1))],
        out_specs=out_spec,
        out_shape=jax.ShapeDtypeStruct(out_shape, F32),
        compiler_params=_cparams("parallel"),
        name="norm_matmul",
    )(x, g.reshape(1, d), w)


def _out_proj_kernel(x_ref, ya_ref, yb_ref, wa_ref, wb_ref, o_ref):
    acc = jnp.dot(ya_ref[...].astype(BF16), wa_ref[...], preferred_element_type=F32)
    acc += jnp.dot(yb_ref[...].astype(BF16), wb_ref[...], preferred_element_type=F32)
    o_ref[...] = x_ref[...] + acc


def out_proj(x, ya, yb, wa, wb, *, tm):
    m, d = x.shape
    k = ya.shape[1]
    if yb.ndim == 3:
        tiles_per_seq = yb.shape[0] // tm
        yb = yb.reshape(yb.shape[0], -1)
        yb_spec = pl.BlockSpec((tm, k), lambda i: (i % tiles_per_seq, i // tiles_per_seq))
    else:
        yb_spec = pl.BlockSpec((tm, k), lambda i: (i, 0))
    return pl.pallas_call(
        _out_proj_kernel,
        grid=(m // tm,),
        in_specs=[pl.BlockSpec((tm, d), lambda i: (i, 0)),
                  pl.BlockSpec((tm, k), lambda i: (i, 0)),
                  yb_spec,
                  pl.BlockSpec((k, d), lambda i: (0, 0)),
                  pl.BlockSpec((k, d), lambda i: (0, 0))],
        out_specs=pl.BlockSpec((tm, d), lambda i: (i, 0)),
        out_shape=jax.ShapeDtypeStruct((m, d), F32),
        compiler_params=_cparams("parallel"),
        name="out_proj",
    )(x, ya, yb, wa, wb)


FFN_SUB = 256


def _ffn_kernel(x_ref, g_ref, wup_ref, cw_ref, cb_ref, wdn_ref, st_ref, gf_ref,
                o_ref, tail_ref, h_scr, act_scr, car_scr, *, tm, seq_len, final_norm):
    long_seq = seq_len >= tm
    i = pl.program_id(0)
    h_scr[...] = _rms(x_ref[...], g_ref[...]).astype(BF16)
    row = lax.broadcasted_iota(jnp.int32, (tm, FFN_SUB), 0)
    if long_seq:
        @pl.when((i * tm) % seq_len == 0)
        def _():
            car_scr[...] = st_ref[...]
    else:
        t_in_seq = row % seq_len

    row8 = lax.broadcasted_iota(jnp.int32, (SUBLANES, FFN_SUB), 0)

    def conv_part(col0):
        u = jnp.dot(h_scr[...], wup_ref[:, col0:col0 + FFN_SUB], preferred_element_type=F32)
        p1 = pltpu.roll(u, 1, 0)
        p2 = pltpu.roll(u, 2, 0)
        if long_seq:
            halo = car_scr[:, col0:col0 + FFN_SUB]
            h6 = halo[SUBLANES - 2:SUBLANES - 1]
            h7 = halo[SUBLANES - 1:SUBLANES]
            p1 = jnp.concatenate([jnp.where(row8 == 0, h7, p1[:SUBLANES]), p1[SUBLANES:]], axis=0)
            p2 = jnp.concatenate([jnp.where(row8 == 0, h6, jnp.where(row8 == 1, h7, p2[:SUBLANES])),
                                  p2[SUBLANES:]], axis=0)
            car_scr[:, col0:col0 + FFN_SUB] = u[tm - SUBLANES:]
            tail_ref[:, col0:col0 + FFN_SUB] = u[tm - SUBLANES:]
        else:
            e = st_ref[:, col0:col0 + FFN_SUB]
            p1 = jnp.where(t_in_seq == 0, pltpu.roll(e, tm - 1, 0), p1)
            p2 = jnp.where(t_in_seq < 2, e, p2)
            tail_ref[:, col0:col0 + FFN_SUB] = u
        cw = cw_ref[:, col0:col0 + FFN_SUB]
        return (cb_ref[:, col0:col0 + FFN_SUB]
                + (cw[0:1] * p2 + cw[1:2] * p1 + cw[2:3] * u))

    for c in range(D_FF // FFN_SUB):
        val = conv_part(c * FFN_SUB)
        gate = conv_part(D_FF + c * FFN_SUB)
        act_scr[:, c * FFN_SUB:(c + 1) * FFN_SUB] = (val * (gate * jax.nn.sigmoid(gate))).astype(BF16)

    y = x_ref[...] + jnp.dot(act_scr[...], wdn_ref[...], preferred_element_type=F32)
    if final_norm:
        y = _rms(y, gf_ref[...])
    o_ref[...] = y


def conv_ffn(x, g, w_up, conv_w, conv_b, w_down, conv0, gf, *, batch, seq_len, tm, final_norm):
    m, d = x.shape
    f2 = 2 * D_FF
    long_seq = seq_len >= tm
    if long_seq:
        assert seq_len % tm == 0
        tiles_per_seq = seq_len // tm
        st = jnp.pad(conv0, ((0, 0), (SUBLANES - (CONV_W - 1), 0), (0, 0)))
        st_spec = pl.BlockSpec((None, SUBLANES, f2), lambda i: (i // tiles_per_seq, 0, 0))
        tail_shape = jax.ShapeDtypeStruct((batch, SUBLANES, f2), F32)
        tail_spec = pl.BlockSpec((None, SUBLANES, f2), lambda i: (i // tiles_per_seq, 0, 0))
        sem = "arbitrary"
    else:
        assert seq_len == SUBLANES and tm % seq_len == 0
        st = jnp.pad(conv0, ((0, 0), (0, seq_len - (CONV_W - 1)), (0, 0))).reshape(m, f2)
        st_spec = pl.BlockSpec((tm, f2), lambda i: (i, 0))
        tail_shape = jax.ShapeDtypeStruct((m, f2), F32)
        tail_spec = pl.BlockSpec((tm, f2), lambda i: (i, 0))
        sem = "parallel"
    const = lambda i: (0, 0)
    out, tail = pl.pallas_call(
        functools.partial(_ffn_kernel, tm=tm, seq_len=seq_len, final_norm=final_norm),
        grid=(m // tm,),
        in_specs=[pl.BlockSpec((tm, d), lambda i: (i, 0)),
                  pl.BlockSpec((1, d), const),
                  pl.BlockSpec((d, f2), const, pipeline_mode=pl.Buffered(1)),
                  pl.BlockSpec((CONV_W, f2), const),
                  pl.BlockSpec((1, f2), const),
                  pl.BlockSpec((D_FF, d), const, pipeline_mode=pl.Buffered(1)),
                  st_spec,
                  pl.BlockSpec((1, d), const)],
        out_specs=[pl.BlockSpec((tm, d), lambda i: (i, 0)), tail_spec],
        out_shape=[jax.ShapeDtypeStruct((m, d), F32), tail_shape],
        scratch_shapes=[pltpu.VMEM((tm, d), BF16),
                        pltpu.VMEM((tm, D_FF), BF16),
                        pltpu.VMEM((SUBLANES, f2), F32)],
        compiler_params=_cparams(sem),
        name="conv_ffn",
    )(x, g.reshape(1, d), w_up, conv_w, conv_b.reshape(1, f2), w_down, st, gf.reshape(1, d))
    new_state = tail.reshape(batch, -1, f2)[:, -(CONV_W - 1):]
    return out, new_state


def _log_sigmoid(x):
    return jnp.minimum(x, 0.0) - jnp.log1p(jnp.exp(-jnp.abs(x)))


def _dot_nt(a, b):
    return lax.dot_general(a, b, (((1,), (1,)), ((), ())), preferred_element_type=F32)


def _dot_tn(a, b):
    return lax.dot_general(a, b, (((0,), (0,)), ((), ())), preferred_element_type=F32)


def _mlstm_kernel(bif_ref, q_ref, k_ref, v_ref, og_ref, gz_ref, gt_ref, c0_ref, n0_ref, m0_ref,
                  lng_ref, y_ref, c_ref, n_ref, m_ref, grow_scr, gcol_scr, bcol_scr, cmax_scr,
                  *, chunk, n_chunks, bb):
    L = chunk

    @pl.when(pl.program_id(1) == 0)
    def _():
        c_ref[...] = c0_ref[...]
        n_ref[...] = n0_ref[...]
        m_ref[...] = m0_ref[...]

    t_idx = lax.broadcasted_iota(jnp.int32, (L, L), 0)
    s_idx = lax.broadcasted_iota(jnp.int32, (L, L), 1)
    causal = s_idx <= t_idx
    scale = MLSTM_HD ** -0.5

    def compute(items):
        ph = []
        for j, c in items:
            start = (j * n_chunks + c) * L
            rows = pl.ds(start if isinstance(start, int) else pl.multiple_of(start, L), L)
            for h in range(MLSTM_HEADS):
                ph.append(gate_phase(j, c, h, rows))
        for d in ph:
            d["qk_raw"] = _dot_nt(d["qb"], d["kb"])
            d["qc"] = _dot_nt(d["qb"], d["c_old"].astype(BF16))
            d["c_upd"] = _dot_tn((d["v"] * d["wl_col"]).astype(BF16), d["kb"])
        for d in ph:
            d["qk"] = d["qk_raw"] * d["w"]
            d["pv"] = jnp.dot(d["qk"].astype(BF16), d["v"].astype(BF16), preferred_element_type=F32)
        stores = []
        for d in ph:
            stores += finish_phase(d)
        return stores

    def entry(j, c, h):
        return (j * n_chunks + c) * MLSTM_HEADS + h

    lane = lax.broadcasted_iota(jnp.int32, (1, LANES), 1)
    bias_row = jnp.zeros((1, LANES), F32)
    for i in range(2 * MLSTM_HEADS):
        bias_row = jnp.where(lane == i, bif_ref[i], bias_row)
    gates_col = gz_ref[...] + bias_row
    gates_col = jnp.where(lane >= MLSTM_HEADS, _log_sigmoid(gates_col), gates_col)
    sub = lax.broadcasted_iota(jnp.int32, (2 * MLSTM_HEADS, 1), 0)
    bias_col = jnp.zeros((2 * MLSTM_HEADS, 1), F32)
    for i in range(2 * MLSTM_HEADS):
        bias_col = jnp.where(sub == i, bif_ref[i], bias_col)
    tril = causal.astype(F32)
    triu = (t_idx <= s_idx).astype(F32)
    exact = dict(preferred_element_type=F32, precision=lax.Precision.HIGHEST)
    for j in range(bb):
        for c in range(n_chunks):
            rows = slice((j * n_chunks + c) * L, (j * n_chunks + c + 1) * L)
            g_c = gates_col[rows]
            b_c = jnp.dot(tril, g_c, **exact)
            g_r = gt_ref[j, :, c, :] + bias_col
            g_r = jnp.where(sub >= MLSTM_HEADS, _log_sigmoid(g_r), g_r)
            b_r = jnp.dot(g_r, triu, **exact)
            for h in range(MLSTM_HEADS):
                g_row = g_r[h:h + 1] - b_r[MLSTM_HEADS + h:MLSTM_HEADS + h + 1]
                b_col = b_c[:, MLSTM_HEADS + h:MLSTM_HEADS + h + 1]
                e = entry(j, c, h)
                grow_scr[e, 0:1, 0:L] = g_row
                gcol_scr[e, :, 0:1] = g_c[:, h:h + 1] - b_col
                bcol_scr[e, :, 0:1] = b_col
                cmax_scr[e, :, 0:1] = jnp.max(jnp.where(causal, g_row, -jnp.inf), axis=1, keepdims=True)

    def gate_phase(j, c, h, rows):
        cols = slice(h * MLSTM_HD, (h + 1) * MLSTM_HD)
        e = entry(j, c, h)
        g_row = grow_scr[e, 0:1, 0:L]
        b_col = bcol_scr[e, :, 0:1]
        m_old = m_ref[j, h]
        mu_col = jnp.maximum(m_old, cmax_scr[e, :, 0:1])
        w = jnp.where(causal, jnp.exp(g_row - mu_col), 0.0)
        s_inter = jnp.exp(m_old - mu_col)
        q = q_ref[rows, cols]
        ks = k_ref[rows, cols] * scale
        mu_last = mu_col[L - 1:L]
        return dict(j=j, h=h, rows=rows, cols=cols, q=q, ks=ks, v=v_ref[rows, cols],
                    qb=q.astype(BF16), kb=ks.astype(BF16), c_old=c_ref[j, h], n_old=n_ref[j, h],
                    w=w, s_inter=s_inter, m_t=b_col + mu_col,
                    wl_col=jnp.exp(gcol_scr[e, :, 0:1] - mu_last),
                    dec=jnp.exp(m_old - mu_last), m_new=b_col[L - 1:L] + mu_last)

    def finish_phase(d):
        j, h, rows, cols = d["j"], d["h"], d["rows"], d["cols"]
        num = d["s_inter"] * d["qc"] + d["pv"]
        den = (d["s_inter"] * jnp.sum(d["q"] * d["n_old"], axis=1, keepdims=True)
               + jnp.sum(d["qk"], axis=1, keepdims=True))
        hh = num / jnp.maximum(jnp.abs(den), jnp.exp(-d["m_t"]))
        return [(c_ref, (j, h), d["dec"] * d["c_old"] + d["c_upd"]),
                (n_ref, (j, h), d["dec"] * d["n_old"] + jnp.sum(d["wl_col"] * d["ks"], axis=0, keepdims=True)),
                (m_ref, (j, h), d["m_new"]),
                (y_ref, (rows, cols), hh)]

    def commit(stores):
        for ref, idx, val in stores:
            ref[idx] = val

    def chunk_body(j, c, carry):
        commit(compute([(j, c)]))
        return carry

    if n_chunks == 1:
        commit(compute([(j, 0) for j in range(bb)]))
    else:
        for j in range(bb):
            lax.fori_loop(0, n_chunks, functools.partial(chunk_body, j), 0)

    for h in range(MLSTM_HEADS):
        cols = slice(h * MLSTM_HD, (h + 1) * MLSTM_HD)
        hh = y_ref[:, cols]
        xc = hh - jnp.mean(hh, -1, keepdims=True)
        hn = xc * lax.rsqrt(jnp.mean(xc * xc, -1, keepdims=True) + EPS) * lng_ref[:, cols]
        y_ref[:, cols] = jax.nn.sigmoid(og_ref[:, cols]) * hn


def mlstm_mixer(z, b_if, c0, n0, m0, ln_g, *, batch, seq_len, tb, bb):
    m_rows = batch * seq_len
    L = math.gcd(seq_len, CHUNK)
    n_tb = seq_len // tb
    n_chunks = tb // L
    assert bb == 1 or n_tb == 1
    rows = bb * tb
    n_entries = bb * n_chunks * MLSTM_HEADS
    gates = z[:, Z0_GATE:Z0_GATE + 2 * MLSTM_HEADS].reshape(batch, seq_len, 2 * MLSTM_HEADS)
    gates_t = jnp.swapaxes(gates, 1, 2).reshape(batch, 2 * MLSTM_HEADS, seq_len // L, L)
    zcol = lambda off: pl.BlockSpec((rows, HALF), lambda b, t: (b * n_tb + t, off // HALF))
    state4 = lambda a, b_: pl.BlockSpec((bb, MLSTM_HEADS, a, b_), lambda b, t: (b, 0, 0, 0))
    y, c, n, m = pl.pallas_call(
        functools.partial(_mlstm_kernel, chunk=L, n_chunks=n_chunks, bb=bb),
        grid=(batch // bb, n_tb),
        in_specs=[pl.BlockSpec(memory_space=pltpu.SMEM),
                  zcol(Z0_Q), zcol(Z0_K), zcol(Z0_V), zcol(Z0_O),
                  pl.BlockSpec((rows, LANES), lambda b, t: (b * n_tb + t, Z0_GATE // LANES)),
                  pl.BlockSpec((bb, 2 * MLSTM_HEADS, n_chunks, L), lambda b, t: (b, 0, t, 0)),
                  state4(MLSTM_HD, MLSTM_HD), state4(1, MLSTM_HD), state4(1, 1),
                  pl.BlockSpec((1, HALF), lambda b, t: (0, 0))],
        out_specs=[pl.BlockSpec((rows, HALF), lambda b, t: (b * n_tb + t, 0)),
                   state4(MLSTM_HD, MLSTM_HD), state4(1, MLSTM_HD), state4(1, 1)],
        out_shape=[jax.ShapeDtypeStruct((m_rows, HALF), F32),
                   jax.ShapeDtypeStruct((batch, MLSTM_HEADS, MLSTM_HD, MLSTM_HD), F32),
                   jax.ShapeDtypeStruct((batch, MLSTM_HEADS, 1, MLSTM_HD), F32),
                   jax.ShapeDtypeStruct((batch, MLSTM_HEADS, 1, 1), F32)],
        scratch_shapes=[pltpu.VMEM((n_entries, SUBLANES, LANES), F32)]
        + [pltpu.VMEM((n_entries, L, LANES), F32)] * 3,
        compiler_params=_cparams("parallel", "arbitrary"),
        name="mlstm",
    )(b_if, z, z, z, z, z, gates_t,
      c0, n0.reshape(batch, MLSTM_HEADS, 1, MLSTM_HD), m0.reshape(batch, MLSTM_HEADS, 1, 1),
      ln_g.reshape(1, HALF))
    return y, c, n.reshape(batch, MLSTM_HEADS, MLSTM_HD), m.reshape(batch, MLSTM_HEADS)


def _split2(x):
    hi = x.astype(BF16)
    lo = (x - hi.astype(F32)).astype(BF16)
    return hi, lo


def _seg_sum(x, ones_blk):
    hi, lo = _split2(x)
    return (jnp.dot(hi, ones_blk, preferred_element_type=F32)
            + jnp.dot(lo, ones_blk, preferred_element_type=F32))


def _softplus(x):
    return jnp.maximum(x, 0.0) + jnp.log1p(jnp.exp(-jnp.abs(x)))


def _block_ones(n, seg):
    idx = jnp.arange(n) // seg
    return (idx[:, None] == idx[None, :]).astype(BF16)


def _rwkv_prep_kernel(z_ref, st_ref, mu_ref, w0_ref, wup_ref, a0_ref, aup_ref, gup_ref, kk_ref, ka_ref,
                      rk_ref, ones_ref, r_out, w_out, k_out, v_out, kk_out, kka_out, bv_out, g_out,
                      car_scr, *, tm, seq_len):
    zr = z_ref[...]
    row = lax.broadcasted_iota(jnp.int32, zr.shape, 0)
    prev = pltpu.roll(zr, 1, 0)
    if seq_len >= tm:
        @pl.when((pl.program_id(0) * tm) % seq_len == 0)
        def _():
            car_scr[...] = st_ref[...]
        prev = jnp.where(row == 0, car_scr[SUBLANES - 1:SUBLANES], prev)
        car_scr[...] = zr[tm - SUBLANES:]
    else:
        prev = jnp.where(row % seq_len == 0, st_ref[...], prev)
    zs = zr + (prev - zr) * mu_ref[...]
    r = zs[:, :HALF]
    kr = zs[:, HALF:2 * HALF]
    vr = zs[:, 2 * HALF:3 * HALF]
    lora = zs[:, RW_LORA:RW_G]
    ones = ones_ref[...]
    w_log = -_softplus(-(w0_ref[...] + jnp.dot(jnp.tanh(lora).astype(BF16), wup_ref[...],
                                               preferred_element_type=F32))) - 0.5
    a = jax.nn.sigmoid(a0_ref[...] + jnp.dot(lora.astype(BF16), aup_ref[...], preferred_element_type=F32))
    g = jnp.dot(jax.nn.sigmoid(zs[:, RW_G:]).astype(BF16), gup_ref[...], preferred_element_type=F32)
    kk = kr * kk_ref[...]
    kk = kk * lax.rsqrt(jnp.maximum(_seg_sum(kk * kk, ones), 1e-24))
    k2 = kr * (1.0 + (a - 1.0) * ka_ref[...])
    r_out[...] = r
    w_out[...] = jnp.exp(-jnp.exp(w_log))
    k_out[...] = k2
    v_out[...] = vr
    kk_out[...] = kk
    kka_out[...] = kk * a
    bv_out[...] = _seg_sum(r * k2 * rk_ref[...], ones) * vr
    g_out[...] = g


RWKV_GROUPS = 2


def _rwkv_rec_kernel(r_ref, w_ref, k_ref, v_ref, kk_ref, kka_ref, bv_ref, g_ref, s0_ref, lng_ref, lnb_ref,
                     onesy_ref, ones_ref, y_ref, s_ref, yraw_scr, lhs_scr, ylhs_scr, tmaj_scr=None,
                     *, bb, tb):
    @pl.when(pl.program_id(1) == 0)
    def _():
        s_ref[...] = s0_ref[...]

    if tmaj_scr is not None:
        streams = (r_ref, w_ref, k_ref, v_ref, kk_ref, kka_ref, bv_ref, g_ref)
        for i, ref in enumerate(streams):
            for t in range(tb):
                tmaj_scr[i, t] = ref[:, t, :]
        r_ref, w_ref, k_ref, v_ref, kk_ref, kka_ref, bv_ref, g_ref = (
            tmaj_scr.at[i] for i in range(len(streams)))

    pairs = RWKV_HEADS // 2
    n_tiles = bb * pairs
    shape = (RWKV_HD, LANES)
    eye2 = (lax.broadcasted_iota(jnp.int32, shape, 0)
            == lax.broadcasted_iota(jnp.int32, shape, 1) % RWKV_HD)
    pair_diag = (lax.broadcasted_iota(jnp.int32, shape, 0) // 2
                 == (lax.broadcasted_iota(jnp.int32, shape, 1) % RWKV_HD) // 2)
    lane_even = lax.broadcasted_iota(jnp.int32, (1, LANES), 1) % 2 == 0
    sublane_even = lax.broadcasted_iota(jnp.int32, (SUBLANES, 1), 0) % 2 == 0
    y_rows = lambda idx: slice((idx // 2) * RWKV_HD, (idx // 2 + 1) * RWKV_HD)
    y_cols = lambda idx: slice((idx % 2) * LANES, (idx % 2 + 1) * LANES)

    per_group = n_tiles // RWKV_GROUPS

    def step(t, carry):
        def tile(idx):
            b, p = divmod(idx, pairs)
            cols = slice(p * LANES, (p + 1) * LANES)
            return b, p, (lambda ref: ref[t, b:b + 1, cols])

        groups = [range(g * per_group, (g + 1) * per_group) for g in range(RWKV_GROUPS)]
        g_rows = per_group * RWKV_HD * 3 // 2
        sums = []
        for g, grp in enumerate(groups):
            for idx in grp:
                b, p, row = tile(idx)
                rel = idx - grp[0]
                base = g * g_rows + rel * RWKV_HD
                hi, lo = _split2(s_ref[b, p] * row(kk_ref))
                lhs_scr[base:base + RWKV_HD, :LANES] = hi
                lhs_scr[base:base + RWKV_HD, LANES:] = lo
                v_row = row(v_ref)
                v_hi = v_row.astype(BF16).astype(F32)
                v_lo = v_row - v_hi
                v_lo_swapped = jnp.where(lane_even, pltpu.roll(v_lo, LANES - 1, 1), pltpu.roll(v_lo, 1, 1))
                pattern = jnp.where(sublane_even, jnp.where(lane_even, v_hi, v_lo_swapped),
                                    jnp.where(lane_even, v_lo_swapped, v_hi))
                v_base = g * g_rows + per_group * RWKV_HD + (rel // 2) * RWKV_HD
                lhs_scr[v_base:v_base + RWKV_HD, y_cols(rel)] = jnp.where(
                    pair_diag, jnp.concatenate([pattern] * (RWKV_HD // SUBLANES), axis=0), 0.0).astype(BF16)
            sums.append(jnp.dot(lhs_scr[g * g_rows:(g + 1) * g_rows, :], onesy_ref[...],
                                preferred_element_type=F32))
        ysums = []
        for grp, sm in zip(groups, sums):
            for idx in grp:
                b, p, row = tile(idx)
                rel = idx - grp[0]
                sa_rows = slice(rel * RWKV_HD, (rel + 1) * RWKV_HD)
                sa = sm[sa_rows, :LANES] + sm[sa_rows, LANES:]
                v_base = per_group * RWKV_HD + (rel // 2) * RWKV_HD
                v_tile = sm[v_base:v_base + RWKV_HD, y_cols(rel)]
                s = s_ref[b, p] * row(w_ref) - sa * row(kka_ref) + v_tile * row(k_ref)
                s_ref[b, p] = s
                ylhs_scr[y_rows(idx), y_cols(idx)] = (s * row(r_ref)).astype(BF16)
            g_rows = slice(y_rows(grp[0]).start, y_rows(grp[-1]).stop)
            ysums.append(jnp.dot(ylhs_scr[g_rows, :], onesy_ref[...], preferred_element_type=F32))
        for grp, ys in zip(groups, ysums):
            for idx in grp:
                b, p, row = tile(idx)
                rel = slice(y_rows(idx).start - y_rows(grp[0]).start, y_rows(idx).stop - y_rows(grp[0]).start)
                yraw_scr[t, b:b + 1, p * LANES:(p + 1) * LANES] = jnp.sum(
                    jnp.where(eye2, ys[rel, y_cols(idx)], 0.0), axis=0, keepdims=True)
        return carry

    lax.fori_loop(0, tb, step, 0)

    ones = ones_ref[...]
    y = yraw_scr[...].reshape(tb * bb, HALF)
    xc = y - _seg_sum(y, ones) * (1.0 / RWKV_HD)
    var = _seg_sum(xc * xc, ones) * (1.0 / RWKV_HD)
    yn = xc * lax.rsqrt(var + RWKV_LN_EPS) * lng_ref[...] + lnb_ref[...]
    out = (yn + bv_ref[...].reshape(tb * bb, HALF)) * g_ref[...].reshape(tb * bb, HALF)
    if tmaj_scr is None:
        y_ref[...] = out.reshape(tb, bb, HALF)
    else:
        for t in range(tb):
            y_ref[:, t, :] = out[t * bb:(t + 1) * bb]


def rwkv_mixer(z, s0, shift0, mu, w0, w_up, a0, a_up, g_up, k_k, k_a, r_k, ln_g, ln_b,
               *, batch, seq_len, tm, bb, tb):
    m_rows = batch * seq_len
    tm_shape = (seq_len, batch, HALF)
    if seq_len >= tm:
        tiles_per_seq = seq_len // tm
        st = jnp.pad(shift0[:, None], ((0, 0), (SUBLANES - 1, 0), (0, 0)))
        st_spec = pl.BlockSpec((None, SUBLANES, RWKV_COLS), lambda i: (i // tiles_per_seq, 0, 0))
        sem = "arbitrary"
        tok = pl.BlockSpec((tm, HALF), lambda i: (i % tiles_per_seq, i // tiles_per_seq))
        tok_shape = (seq_len, batch * HALF)
    else:
        st = jnp.pad(shift0[:, None], ((0, 0), (0, seq_len - 1), (0, 0))).reshape(m_rows, RWKV_COLS)
        st_spec = pl.BlockSpec((tm, RWKV_COLS), lambda i: (i, 0))
        sem = "parallel"
        tok = pl.BlockSpec((tm, HALF), lambda i: (i, 0))
        tok_shape = (m_rows, HALF)
    row = lambda a: a.reshape(1, -1)
    const = lambda i: (0, 0)
    vec = pl.BlockSpec((1, HALF), const)
    lora_rows = RWKV_W_LORA + RWKV_A_LORA
    w_up_p = jnp.pad(w_up, ((0, RWKV_A_LORA), (0, 0))).astype(BF16)
    a_up_p = jnp.pad(a_up, ((RWKV_W_LORA, 0), (0, 0))).astype(BF16)
    ones_half = _block_ones(HALF, RWKV_HD)
    prep = pl.pallas_call(
        functools.partial(_rwkv_prep_kernel, tm=tm, seq_len=seq_len),
        grid=(m_rows // tm,),
        in_specs=[pl.BlockSpec((tm, RWKV_COLS), lambda i: (i, Z0_RWKV // RWKV_COLS)),
                  st_spec,
                  pl.BlockSpec((1, RWKV_COLS), const),
                  vec, pl.BlockSpec((lora_rows, HALF), const),
                  vec, pl.BlockSpec((lora_rows, HALF), const),
                  pl.BlockSpec((RWKV_G_LORA, HALF), const),
                  vec, vec, vec,
                  pl.BlockSpec((HALF, HALF), const)],
        out_specs=[tok] * 8,
        out_shape=[jax.ShapeDtypeStruct(tok_shape, F32)] * 8,
        scratch_shapes=[pltpu.VMEM((SUBLANES, RWKV_COLS), F32)],
        compiler_params=_cparams(sem),
        name="rwkv_prep",
    )(z, st, row(mu), row(w0), w_up_p, row(a0), a_up_p, g_up.astype(BF16), row(k_k), row(k_a), row(r_k),
      ones_half)
    pairs = RWKV_HEADS // 2
    s_pairs = (s0.reshape(batch, pairs, 2, RWKV_HD, RWKV_HD).transpose(0, 1, 3, 2, 4)
               .reshape(batch, pairs, RWKV_HD, LANES))
    scratch = [pltpu.VMEM((tb, bb, HALF), F32),
               pltpu.VMEM((bb * pairs * RWKV_HD * 3 // 2, 2 * LANES), BF16),
               pltpu.VMEM((bb * pairs * RWKV_HD // 2, 2 * LANES), BF16)]
    if seq_len >= tm:
        seqs = [a.reshape(tm_shape) for a in prep]
        blk = pl.BlockSpec((tb, bb, HALF), lambda b, t: (t, b, 0))
        y_shape = tm_shape
    else:
        assert tb == seq_len
        seqs = [a.reshape(batch, seq_len, HALF) for a in prep]
        blk = pl.BlockSpec((bb, tb, HALF), lambda b, t: (b, 0, 0))
        y_shape = (batch, seq_len, HALF)
        scratch.append(pltpu.VMEM((len(prep), tb, bb, HALF), F32))
    s_spec = pl.BlockSpec((bb, pairs, RWKV_HD, LANES), lambda b, t: (b, 0, 0, 0))
    const2 = lambda b, t: (0, 0)
    y, s_new = pl.pallas_call(
        functools.partial(_rwkv_rec_kernel, bb=bb, tb=tb),
        grid=(batch // bb, seq_len // tb),
        in_specs=[blk] * 8 + [s_spec,
                              pl.BlockSpec((1, HALF), const2), pl.BlockSpec((1, HALF), const2),
                              pl.BlockSpec((2 * LANES, 2 * LANES), const2),
                              pl.BlockSpec((HALF, HALF), const2)],
        out_specs=[blk, s_spec],
        out_shape=[jax.ShapeDtypeStruct(y_shape, F32),
                   jax.ShapeDtypeStruct((batch, pairs, RWKV_HD, LANES), F32)],
        scratch_shapes=scratch,
        compiler_params=_cparams("parallel", "arbitrary"),
        name="rwkv_rec",
    )(*seqs, s_pairs, row(ln_g), row(ln_b),
      _block_ones(2 * LANES, RWKV_HD), ones_half)
    s_new = (s_new.reshape(batch, pairs, RWKV_HD, 2, RWKV_HD).transpose(0, 1, 3, 2, 4)
             .reshape(batch, RWKV_HEADS, RWKV_HD, RWKV_HD))
    if seq_len < tm:
        y = y.reshape(m_rows, HALF)
    return y, s_new


GLA_SUB = 16


def _gla_kernel(q_ref, k_ref, v_ref, g_ref, ad_ref, aup_ref, ab_ref, lng_ref, s0_ref,
                y_ref, s_ref, bc_scr, *, chunk, n_chunks, bb):
    L = chunk
    sub = min(GLA_SUB, L)

    @pl.when(pl.program_id(1) == 0)
    def _():
        s_ref[...] = s0_ref[...]

    lane = lax.broadcasted_iota(jnp.int32, (1, LANES), 1)
    head_mask = [(lane < GLA_DK).astype(F32), (lane >= GLA_DK).astype(F32)]
    tril = (lax.broadcasted_iota(jnp.int32, (L, L), 1)
            <= lax.broadcasted_iota(jnp.int32, (L, L), 0)).astype(F32)
    eye = (lax.broadcasted_iota(jnp.int32, (LANES, LANES), 0)
           == lax.broadcasted_iota(jnp.int32, (LANES, LANES), 1))

    pre = jnp.dot(ad_ref[...].astype(BF16), aup_ref[...], preferred_element_type=F32) + ab_ref[...]
    la = _log_sigmoid(pre) * (1.0 / GLA_TAU)
    for blk in range(bb * n_chunks):
        rows = slice(blk * L, (blk + 1) * L)
        bc_scr[rows, :] = jnp.dot(tril, la[rows], preferred_element_type=F32,
                                  precision=lax.Precision.HIGHEST)

    stack = lambda x: jnp.concatenate([x * head_mask[0], x * head_mask[1]], axis=0).astype(BF16)

    def compute(items):
        units = []
        for j, c in items:
            start = (j * n_chunks + c) * L
            rows = pl.ds(start if isinstance(start, int) else pl.multiple_of(start, L), L)
            for p in range(GLA_HEADS // 2):
                kcols = slice(p * LANES, (p + 1) * LANES)
                units.append(dict(
                    j=j, p=p, rows=rows, qp=q_ref[rows, kcols] * (GLA_DK ** -0.5), kp=k_ref[rows, kcols],
                    bc=bc_scr[rows, kcols], s_old=s_ref[j, p],
                    v_pair=v_ref[rows, 2 * p * GLA_DV:2 * (p + 1) * GLA_DV].astype(BF16)))
        for d in units:
            qp, kp, bc, v_pair = d["qp"], d["kp"], d["bc"], d["v_pair"]
            d["inter"] = jnp.dot(stack(qp * jnp.exp(bc)), d["s_old"].astype(BF16), preferred_element_type=F32)
            d["att"] = []
            for i in range(L // sub):
                lo, hi = i * sub, (i + 1) * sub
                c_i = bc[lo - 1:lo] if i > 0 else jnp.zeros((1, LANES), F32)
                qe = qp[lo:hi] * jnp.exp(bc[lo:hi] - c_i)
                ke = kp[:hi] * jnp.exp(c_i - bc[:hi])
                d["att"].append(_dot_nt(stack(qe), ke.astype(BF16)))
            bl = bc[L - 1:L]
            v_stack = jnp.concatenate([v_pair[:, :GLA_DV], v_pair[:, GLA_DV:]], axis=0)
            d["s_upd"] = _dot_tn(stack(kp * jnp.exp(bl - bc)), v_stack)
            d["dec_col"] = jnp.sum(jnp.where(eye, jnp.exp(bl), 0.0), axis=1, keepdims=True)
        for d in units:
            d["intra"] = []
            for i, att in enumerate(d["att"]):
                lo, hi = i * sub, (i + 1) * sub
                t_idx = lo + lax.broadcasted_iota(jnp.int32, (2 * sub, hi), 0) % sub
                s_idx = lax.broadcasted_iota(jnp.int32, (2 * sub, hi), 1)
                att = jnp.where(s_idx <= t_idx, att, 0.0)
                d["intra"].append(jnp.dot(att.astype(BF16), d["v_pair"][:hi], preferred_element_type=F32))
        stores = []
        for d in units:
            for e in range(2):
                h = 2 * d["p"] + e
                vcols = slice(h * GLA_DV, (h + 1) * GLA_DV)
                o = d["inter"][e * L:(e + 1) * L] + jnp.concatenate(
                    [blk[e * sub:(e + 1) * sub, e * GLA_DV:(e + 1) * GLA_DV] for blk in d["intra"]], axis=0)
                stores.append((y_ref, (d["rows"], vcols), o))
            stores.append((s_ref, (d["j"], d["p"]), d["dec_col"] * d["s_old"] + d["s_upd"]))
        return stores

    def commit(stores):
        for ref, idx, val in stores:
            ref[idx] = val

    def chunk_body(j, c, carry):
        commit(compute([(j, c)]))
        return carry

    if n_chunks == 1:
        commit(compute([(j, 0) for j in range(bb)]))
    else:
        for j in range(bb):
            lax.fori_loop(0, n_chunks, functools.partial(chunk_body, j), 0)

    for h in range(GLA_HEADS):
        vcols = slice(h * GLA_DV, (h + 1) * GLA_DV)
        o = y_ref[:, vcols]
        on = o * lax.rsqrt(jnp.mean(o * o, -1, keepdims=True) + EPS) * lng_ref[:, vcols]
        gate = g_ref[:, vcols]
        y_ref[:, vcols] = on * (gate * jax.nn.sigmoid(gate))


def gla_mixer(z, s0, a_up, a_b, ln_g, *, batch, seq_len, tb, bb, time_major):
    m_rows = batch * seq_len
    L = math.gcd(seq_len, CHUNK)
    n_tb = seq_len // tb
    pairs = GLA_HEADS // 2
    assert bb == 1 or (n_tb == 1 and not time_major)
    rows = bb * tb
    if time_major:
        zblk = lambda width, off: pl.BlockSpec((tb, width), lambda b, t: (t, (b * Z1_COLS + off) // width))
    else:
        zblk = lambda width, off: pl.BlockSpec((rows, width), lambda b, t: (b * n_tb + t, off // width))
    s_spec = pl.BlockSpec((bb, pairs, 2 * GLA_DK, GLA_DV), lambda b, t: (b, 0, 0, 0))
    const = lambda b, t: (0, 0)
    a_up_p = jnp.pad(a_up, ((0, LANES - GLA_GATE_RANK), (0, 0))).astype(BF16)
    y, s_new = pl.pallas_call(
        functools.partial(_gla_kernel, chunk=L, n_chunks=tb // L, bb=bb),
        grid=(batch // bb, n_tb),
        in_specs=[zblk(GLA_QK, Z1_Q), zblk(GLA_QK, Z1_K), zblk(HALF, Z1_V), zblk(HALF, Z1_G),
                  zblk(LANES, Z1_AD),
                  pl.BlockSpec((LANES, GLA_QK), const), pl.BlockSpec((1, GLA_QK), const),
                  pl.BlockSpec((1, HALF), const), s_spec],
        out_specs=[pl.BlockSpec((rows, HALF), lambda b, t: (b * n_tb + t, 0)), s_spec],
        out_shape=[jax.ShapeDtypeStruct((m_rows, HALF), F32),
                   jax.ShapeDtypeStruct((batch, pairs, 2 * GLA_DK, GLA_DV), F32)],
        scratch_shapes=[pltpu.VMEM((rows, GLA_QK), F32)],
        compiler_params=_cparams("parallel", "arbitrary"),
        name="gla",
    )(z, z, z, z, z, a_up_p, a_b.reshape(1, GLA_QK), ln_g.reshape(1, HALF),
      s0.reshape(batch, pairs, 2 * GLA_DK, GLA_DV))
    return y, s_new.reshape(batch, GLA_HEADS, GLA_DK, GLA_DV)


S5_KT = LANES // S5_GROUP
S5_SCAN_SPLIT = 2


def _s5_param_kernel(lre_ref, lim_ref, step_ref, bre_ref, bim_ref, bbre_ref, bbim_ref, pre_ref, pim_ref):
    lre, lim = lre_ref[...], lim_ref[...]
    step = jnp.exp(step_ref[...])
    mag = jnp.exp(lre * step)
    bar_re = mag * jnp.cos(lim * step)
    bar_im = mag * jnp.sin(lim * step)
    inv = 1.0 / (lre * lre + lim * lim)
    cre = ((bar_re - 1.0) * lre + bar_im * lim) * inv
    cim = (bar_im * lre - (bar_re - 1.0) * lim) * inv
    for g in range(S5_GROUPS):
        cr, ci = cre[g:g + 1], cim[g:g + 1]
        bbre_ref[g] = cr * bre_ref[g] - ci * bim_ref[g]
        bbim_ref[g] = cr * bim_ref[g] + ci * bre_ref[g]
    pre_ref[0] = bar_re
    pim_ref[0] = bar_im


def _s5_kernel(u_ref, wbre_ref, wbim_ref, lre_ref, lim_ref, x0re_ref, x0im_ref, wcre_ref, wcim_ref,
               d_ref, wglu_ref, y_ref, xre_ref, xim_ref, sre_scr, sim_scr, *, tt, bb, row_major):
    @pl.when(pl.program_id(1) == 0)
    def _():
        xre_ref[...] = x0re_ref[...]
        xim_ref[...] = x0im_ref[...]

    rows_all = tt * bb
    if row_major:
        u = jnp.concatenate([u_ref[:, t, :] for t in range(tt)], axis=0)
    else:
        u = u_ref[...].reshape(rows_all, S5_CH)
    ub = u.astype(BF16)
    nblk = S5_CH // LANES
    wide = S5_STATE // nblk
    for kt in range(nblk):
        cols = slice(kt * wide, (kt + 1) * wide)
        ukt = ub[:, kt * LANES:(kt + 1) * LANES]
        sre_scr[:, cols] = jnp.dot(ukt, wbre_ref[kt], preferred_element_type=F32)
        sim_scr[:, cols] = jnp.dot(ukt, wbim_ref[kt], preferred_element_type=F32)

    for part in range(S5_SCAN_SPLIT):
        width = S5_STATE // S5_SCAN_SPLIT
        cols = slice(part * width, (part + 1) * width)
        lr, li = lre_ref[:, cols], lim_ref[:, cols]

        def step(t, carry, cols=cols, lr=lr, li=li):
            cr, ci = carry
            rows = pl.ds(pl.multiple_of(t * bb, bb), bb)
            nr = sre_scr[rows, cols] + (lr * cr - li * ci)
            ni = sim_scr[rows, cols] + (lr * ci + li * cr)
            sre_scr[rows, cols] = nr
            sim_scr[rows, cols] = ni
            return nr, ni

        cr, ci = lax.fori_loop(0, tt, step, (xre_ref[:, cols], xim_ref[:, cols]))
        xre_ref[:, cols] = cr
        xim_ref[:, cols] = ci

    parts = []
    for nt in range(nblk):
        cols = slice(nt * wide, (nt + 1) * wide)
        parts.append(jnp.dot(sre_scr[:, cols].astype(BF16), wcre_ref[nt], preferred_element_type=F32)
                     - jnp.dot(sim_scr[:, cols].astype(BF16), wcim_ref[nt], preferred_element_type=F32))
    y = jnp.concatenate(parts, axis=1) + d_ref[...] * u
    ys = y * (0.5 * (1.0 + jnp.tanh(math.sqrt(2.0 / math.pi) * (y + 0.044715 * (y * y * y)))))
    out = ys * jax.nn.sigmoid(jnp.dot(ys.astype(BF16), wglu_ref[...], preferred_element_type=F32))
    if row_major:
        for t in range(tt):
            y_ref[:, t, :] = out[t * bb:(t + 1) * bb]
    else:
        y_ref[...] = out.reshape(tt, bb, S5_CH)


def s5_mixer(z, x0_re, x0_im, lam_re, lam_im, log_step, b_re, b_im, c_re, c_im, d_skip,
             w_glu, *, batch, seq_len, tt, bb, time_major):
    u_col_block = Z1_U // S5_CH
    if time_major:
        z = z.reshape(seq_len, batch, Z1_COLS)
        u_spec = pl.BlockSpec((tt, bb, S5_CH), lambda b, t: (t, b, u_col_block))
        y_spec = pl.BlockSpec((tt, bb, S5_CH), lambda b, t: (t, b, 0))
        y_shape = (seq_len, batch, S5_CH)
    else:
        assert tt == seq_len
        z = z.reshape(batch, seq_len, Z1_COLS)
        u_spec = pl.BlockSpec((bb, tt, S5_CH), lambda b, t: (b, 0, u_col_block))
        y_spec = pl.BlockSpec((bb, tt, S5_CH), lambda b, t: (b, 0, 0))
        y_shape = (batch, seq_len, S5_CH)
    gpc = (S5_GROUPS, S5_GROUP, S5_P)
    bb_re, bb_im, p_re, p_im = pl.pallas_call(
        _s5_param_kernel,
        out_shape=[jax.ShapeDtypeStruct(gpc, F32), jax.ShapeDtypeStruct(gpc, F32),
                   jax.ShapeDtypeStruct((1, S5_GROUPS, S5_P), F32),
                   jax.ShapeDtypeStruct((1, S5_GROUPS, S5_P), F32)],
        name="s5_params",
    )(lam_re, lam_im, log_step.reshape(S5_GROUPS, 1), jnp.swapaxes(b_re, 1, 2), jnp.swapaxes(b_im, 1, 2))

    nblk = S5_CH // LANES
    eye = jnp.eye(S5_KT, dtype=F32)

    def in_blocks(w):
        w = w.reshape(nblk, S5_KT, S5_GROUP, 1, S5_P) * eye[None, :, None, :, None]
        return w.reshape(nblk, LANES, S5_KT * S5_P).astype(BF16)

    def out_blocks(c):
        w = jnp.swapaxes(c, 1, 2).reshape(nblk, S5_KT, S5_P, 1, S5_GROUP) * eye[None, :, None, :, None]
        return w.reshape(nblk, S5_KT * S5_P, LANES).astype(BF16)

    lam_rows = lambda p: jnp.broadcast_to(p.reshape(1, S5_STATE), (bb, S5_STATE))
    const2 = lambda b, t: (0, 0)
    const3 = lambda b, t: (0, 0, 0)
    st_spec = pl.BlockSpec((bb, S5_STATE), lambda b, t: (b, 0))
    wb_spec = pl.BlockSpec((nblk, LANES, S5_KT * S5_P), const3)
    wc_spec = pl.BlockSpec((nblk, S5_KT * S5_P, LANES), const3)
    lam_spec = pl.BlockSpec((bb, S5_STATE), const2)
    y, x_re, x_im = pl.pallas_call(
        functools.partial(_s5_kernel, tt=tt, bb=bb, row_major=not time_major),
        grid=(batch // bb, seq_len // tt),
        in_specs=[u_spec,
                  wb_spec, wb_spec, lam_spec, lam_spec, st_spec, st_spec, wc_spec, wc_spec,
                  pl.BlockSpec((1, S5_CH), const2), pl.BlockSpec((S5_CH, S5_CH), const2)],
        out_specs=[y_spec, st_spec, st_spec],
        out_shape=[jax.ShapeDtypeStruct(y_shape, F32),
                   jax.ShapeDtypeStruct((batch, S5_STATE), F32),
                   jax.ShapeDtypeStruct((batch, S5_STATE), F32)],
        scratch_shapes=[pltpu.VMEM((tt * bb, S5_STATE), F32), pltpu.VMEM((tt * bb, S5_STATE), F32)],
        compiler_params=_cparams("parallel", "arbitrary"),
        name="s5",
    )(z, in_blocks(bb_re), in_blocks(bb_im), lam_rows(p_re), lam_rows(p_im),
      x0_re.reshape(batch, S5_STATE), x0_im.reshape(batch, S5_STATE),
      out_blocks(c_re), out_blocks(c_im), d_skip.reshape(1, S5_CH), w_glu.astype(BF16))
    shape = (batch, S5_GROUPS, S5_P)
    if not time_major:
        y = y.reshape(batch * seq_len, S5_CH)
    return y, x_re.reshape(shape), x_im.reshape(shape)


def _pack_even_w_in(w_in):
    d = w_in.shape[0]
    m_cols = 4 * HALF + 2 * MLSTM_HEADS
    parts = [w_in[:, m_cols:],
             w_in[:, 4 * HALF:m_cols], jnp.zeros((d, Z0_Q - Z0_GATE - 2 * MLSTM_HEADS), w_in.dtype),
             w_in[:, :4 * HALF]]
    return jnp.concatenate(parts, axis=1).astype(BF16)


def _pack_odd_w_in(w_in):
    gla_main = 2 * GLA_QK + 2 * HALF
    gla_cols = gla_main + GLA_GATE_RANK
    parts = [w_in[:, :gla_main], w_in[:, gla_cols:], w_in[:, gla_main:gla_cols]]
    w = jnp.concatenate(parts, axis=1)
    return jnp.pad(w, ((0, 0), (0, Z1_COLS - w.shape[1]))).astype(BF16)


def _trunk(x, st, prm, *, batch, seq_len):
    (st_C, st_n, st_m, st_S, st_shift, st_gla, st_re, st_im, st_conv) = st
    (norm_mix, norm_ffn, norm_final,
     e_w_in, e_b_if, e_mu, e_w0, e_w_up, e_a0, e_a_up, e_g_up, e_k_k, e_k_a, e_r_k, e_ln_m_g, e_ln_r_g,
     e_ln_r_b, e_w_out,
     o_w_in, o_a_up, o_a_b, o_ln_g, o_lam_re, o_lam_im, o_log_step, o_b_re, o_b_im, o_c_re, o_c_im, o_d,
     o_w_glu, o_w_out,
     f_w_up, f_conv_w, f_conv_b, f_w_down) = prm
    assert len(e_w_in) == 1 and len(o_w_in) == 1 and len(f_w_up) == 2, "wired for depth 2"
    long_seq = seq_len >= 512
    assert long_seq or seq_len == SUBLANES
    m_rows = batch * seq_len
    tiles = dict(
        tm_proj=512,
        tm_tok=512 if long_seq else 256,
        tm_ffn=512 if long_seq else 128,
        tb=512 if long_seq else seq_len,
        bb_chunked=1 if long_seq else 4,
        tb_rwkv=128 if long_seq else seq_len,
        tt_s5=128 if long_seq else seq_len,
    )
    x = x.reshape(m_rows, D_MODEL)
    ffn = functools.partial(conv_ffn, batch=batch, seq_len=seq_len, tm=tiles["tm_ffn"])

    z0 = norm_matmul(x, norm_mix[0], _pack_even_w_in(e_w_in[0]), tm=tiles["tm_proj"], tn=512)
    y_m, c_new, n_new, m_new = mlstm_mixer(z0, e_b_if[0], st_C[0], st_n[0], st_m[0], e_ln_m_g[0],
                                           batch=batch, seq_len=seq_len, tb=tiles["tb"], bb=tiles["bb_chunked"])
    y_r, s_new = rwkv_mixer(z0, st_S[0], st_shift[0], e_mu[0], e_w0[0], e_w_up[0], e_a0[0], e_a_up[0],
                            e_g_up[0], e_k_k[0], e_k_a[0], e_r_k[0], e_ln_r_g[0], e_ln_r_b[0],
                            batch=batch, seq_len=seq_len, tm=tiles["tm_tok"], bb=SUBLANES,
                            tb=tiles["tb_rwkv"])
    shift_new = z0.reshape(batch, seq_len, Z0_COLS)[:, -1, Z0_RWKV:Z0_RWKV + RWKV_COLS]
    w_out = e_w_out[0].astype(BF16)
    x = out_proj(x, y_m, y_r, w_out[:HALF], w_out[HALF:], tm=tiles["tm_tok"])
    x, conv_new0 = ffn(x, norm_ffn[0], f_w_up[0].astype(BF16), f_conv_w[0], f_conv_b[0],
                       f_w_down[0].astype(BF16), st_conv[0], norm_final, final_norm=False)

    z1 = norm_matmul(x, norm_mix[1], _pack_odd_w_in(o_w_in[0]), tm=tiles["tm_proj"], tn=512,
                     time_major_seq_len=seq_len if long_seq else None)
    y_g, gla_new = gla_mixer(z1, st_gla[0], o_a_up[0], o_a_b[0], o_ln_g[0],
                             batch=batch, seq_len=seq_len, tb=tiles["tb"], bb=tiles["bb_chunked"],
                             time_major=long_seq)
    y_s, re_new, im_new = s5_mixer(z1, st_re[0], st_im[0], o_lam_re[0], o_lam_im[0],
                                   o_log_step[0], o_b_re[0], o_b_im[0], o_c_re[0], o_c_im[0], o_d[0],
                                   o_w_glu[0], batch=batch, seq_len=seq_len, tt=tiles["tt_s5"],
                                   bb=SUBLANES, time_major=long_seq)
    w_out = o_w_out[0].astype(BF16)
    x = out_proj(x, y_g, y_s, w_out[:HALF], w_out[HALF:], tm=tiles["tm_tok"])
    x, conv_new1 = ffn(x, norm_ffn[1], f_w_up[1].astype(BF16), f_conv_w[1], f_conv_b[1],
                       f_w_down[1].astype(BF16), st_conv[1], norm_final, final_norm=True)
    new_state = (c_new[None], n_new[None], m_new[None], s_new[None], shift_new[None],
                 gla_new[None], re_new[None], im_new[None], jnp.stack([conv_new0, conv_new1]))
    return x.reshape(batch, seq_len, D_MODEL), new_state


def kernel(x_prompt, x_sample, state_mlstm_C, state_mlstm_n, state_mlstm_m, state_rwkv_S, state_rwkv_shift,
           state_gla_S, state_s5_re, state_s5_im, state_ffn_conv,
           norm_mix, norm_ffn, norm_final,
           e_w_in, e_b_if, e_mu, e_w0, e_w_up, e_a0, e_a_up, e_g_up, e_k_k, e_k_a, e_r_k,
           e_ln_m_g, e_ln_r_g, e_ln_r_b, e_w_out,
           o_w_in, o_a_up, o_a_b, o_ln_g, o_lam_re, o_lam_im, o_log_step, o_b_re, o_b_im, o_c_re, o_c_im,
           o_d, o_w_glu, o_w_out,
           f_w_up, f_conv_w, f_conv_b, f_w_down):
    params = (norm_mix, norm_ffn, norm_final,
              e_w_in, e_b_if, e_mu, e_w0, e_w_up, e_a0, e_a_up, e_g_up, e_k_k, e_k_a, e_r_k,
              e_ln_m_g, e_ln_r_g, e_ln_r_b, e_w_out,
              o_w_in, o_a_up, o_a_b, o_ln_g, o_lam_re, o_lam_im, o_log_step, o_b_re, o_b_im, o_c_re, o_c_im,
              o_d, o_w_glu, o_w_out,
              f_w_up, f_conv_w, f_conv_b, f_w_down)
    bp, tp, _ = x_prompt.shape
    bs, ts, _ = x_sample.shape
    n_even, n_odd, depth = state_mlstm_C.shape[0], state_gla_S.shape[0], state_ffn_conv.shape[0]
    init = (jnp.zeros((n_even, bp, MLSTM_HEADS, MLSTM_HD, MLSTM_HD), F32),
            jnp.zeros((n_even, bp, MLSTM_HEADS, MLSTM_HD), F32),
            jnp.full((n_even, bp, MLSTM_HEADS), M_INIT, F32),
            jnp.zeros((n_even, bp, RWKV_HEADS, RWKV_HD, RWKV_HD), F32),
            jnp.zeros((n_even, bp, RWKV_COLS), F32),
            jnp.zeros((n_odd, bp, GLA_HEADS, GLA_DK, GLA_DV), F32),
            jnp.zeros((n_odd, bp, S5_GROUPS, S5_P), F32),
            jnp.zeros((n_odd, bp, S5_GROUPS, S5_P), F32),
            jnp.zeros((depth, bp, CONV_W - 1, 2 * D_FF), F32))
    past = (state_mlstm_C, state_mlstm_n, state_mlstm_m, state_rwkv_S, state_rwkv_shift,
            state_gla_S, state_s5_re, state_s5_im, state_ffn_conv)
    y_prompt, p_state = _trunk(x_prompt, init, params, batch=bp, seq_len=tp)
    y_sample, s_state = _trunk(x_sample, past, params, batch=bs, seq_len=ts)
    return (y_prompt, y_sample) + tuple(p_state) + tuple(s_state)
```

```python
import functools
import math

import jax
import jax.numpy as jnp
from jax import lax
from jax.experimental import pallas as pl
from jax.experimental.pallas import tpu as pltpu

F32 = jnp.float32
BF16 = jnp.bfloat16

D_MODEL = 1024
HALF = D_MODEL // 2
MLSTM_HEADS = 4
MLSTM_HD = HALF // MLSTM_HEADS
RWKV_HD = 64
RWKV_HEADS = HALF // RWKV_HD
RWKV_W_LORA = 64
RWKV_A_LORA = 64
RWKV_G_LORA = 128
RWKV_COLS = 3 * HALF + RWKV_W_LORA + RWKV_A_LORA + RWKV_G_LORA
GLA_HEADS = 4
GLA_DK = 64
GLA_DV = 128
GLA_QK = GLA_HEADS * GLA_DK
GLA_GATE_RANK = 16
GLA_TAU = 16.0
S5_CH = HALF
S5_GROUP = 16
S5_GROUPS = S5_CH // S5_GROUP
S5_P = 64
S5_STATE = S5_GROUPS * S5_P
D_FF = 2816
CONV_W = 3
CHUNK = 64
EPS = 1e-6
RWKV_LN_EPS = 64e-5
M_INIT = -1e30

LANES = 128
SUBLANES = 8
VMEM_LIMIT_BYTES = 56 * 1024 * 1024

Z0_RWKV = 0
Z0_GATE = RWKV_COLS
Z0_Q = Z0_GATE + 2 * LANES
Z0_K, Z0_V, Z0_O = Z0_Q + HALF, Z0_Q + 2 * HALF, Z0_Q + 3 * HALF
Z0_COLS = Z0_Q + 4 * HALF
RW_LORA = 3 * HALF
RW_G = RW_LORA + RWKV_W_LORA + RWKV_A_LORA
Z1_Q, Z1_K, Z1_V, Z1_G, Z1_U = 0, GLA_QK, 2 * GLA_QK, 2 * GLA_QK + HALF, 2 * GLA_QK + 2 * HALF
Z1_AD = Z1_U + S5_CH
Z1_COLS = 2560


def _cparams(*sem):
    return pltpu.CompilerParams(dimension_semantics=sem, vmem_limit_bytes=VMEM_LIMIT_BYTES)


def _rms(x, g):
    return x * lax.rsqrt(jnp.mean(x * x, -1, keepdims=True) + EPS) * g


def _norm_matmul_kernel(x_ref, g_ref, w_ref, o_ref, *, tn):
    h = _rms(x_ref[...], g_ref[...]).astype(BF16)
    for j in range(w_ref.shape[1] // tn):
        cols = slice(j * tn, (j + 1) * tn)
        o_ref[:, cols] = jnp.dot(h, w_ref[:, cols], preferred_element_type=F32)


def norm_matmul(x, g, w, *, tm, tn, time_major_seq_len=None):
    m, d = x.shape
    n = w.shape[1]
    if time_major_seq_len is None:
        out_shape = (m, n)
        out_spec = pl.BlockSpec((tm, n), lambda i: (i, 0))
    else:
        tiles_per_seq = time_major_seq_len // tm
        out_shape = (time_major_seq_len, (m // time_major_seq_len) * n)
        out_spec = pl.BlockSpec((tm, n), lambda i: (i % tiles_per_seq, i // tiles_per_seq))
    return pl.pallas_call(
        functools.partial(_norm_matmul_kernel, tn=tn),
        grid=(m // tm,),
        in_specs=[pl.BlockSpec((tm, d), lambda i: (i, 0)),
                  pl.BlockSpec((1, d), lambda i: (0, 0)),
                  pl.BlockSpec((d, n), lambda i: (0, 0), pipeline_mode=pl.Buffered(1))],
        out_specs=out_spec,
        out_shape=jax.ShapeDtypeStruct(out_shape, F32),
        compiler_params=_cparams("parallel"),
        name="norm_matmul",
    )(x, g.reshape(1, d), w)


def _out_proj_kernel(x_ref, ya_ref, yb_ref, wa_ref, wb_ref, o_ref):
    acc = jnp.dot(ya_ref[...].astype(BF16), wa_ref[...], preferred_element_type=F32)
    acc += jnp.dot(yb_ref[...].astype(BF16), wb_ref[...], preferred_element_type=F32)
    o_ref[...] = x_ref[...] + acc


def out_proj(x, ya, yb, wa, wb, *, tm):
    m, d = x.shape
    k = ya.shape[1]
    if yb.ndim == 3:
        tiles_per_seq = yb.shape[0] // tm
        yb = yb.reshape(yb.shape[0], -1)
        yb_spec = pl.BlockSpec((tm, k), lambda i: (i % tiles_per_seq, i // tiles_per_seq))
    else:
        yb_spec = pl.BlockSpec((tm, k), lambda i: (i, 0))
    return pl.pallas_call(
        _out_proj_kernel,
        grid=(m // tm,),
        in_specs=[pl.BlockSpec((tm, d), lambda i: (i, 0)),
                  pl.BlockSpec((tm, k), lambda i: (i, 0)),
                  yb_spec,
                  pl.BlockSpec((k, d), lambda i: (0, 0)),
                  pl.BlockSpec((k, d), lambda i: (0, 0))],
        out_specs=pl.BlockSpec((tm, d), lambda i: (i, 0)),
        out_shape=jax.ShapeDtypeStruct((m, d), F32),
        compiler_params=_cparams("parallel"),
        name="out_proj",
    )(x, ya, yb, wa, wb)


FFN_SUB = 256


def _ffn_kernel(x_ref, g_ref, wup_ref, cw_ref, cb_ref, wdn_ref, st_ref, gf_ref,
                o_ref, tail_ref, h_scr, act_scr, car_scr, *, tm, seq_len, final_norm):
    long_seq = seq_len >= tm
    i = pl.program_id(0)
    h_scr[...] = _rms(x_ref[...], g_ref[...]).astype(BF16)
    row = lax.broadcasted_iota(jnp.int32, (tm, FFN_SUB), 0)
    if long_seq:
        @pl.when((i * tm) % seq_len == 0)
        def _():
            car_scr[...] = st_ref[...]
    else:
        t_in_seq = row % seq_len

    row8 = lax.broadcasted_iota(jnp.int32, (SUBLANES, FFN_SUB), 0)

    def conv_part(col0):
        u = jnp.dot(h_scr[...], wup_ref[:, col0:col0 + FFN_SUB], preferred_element_type=F32)
        p1 = pltpu.roll(u, 1, 0)
        p2 = pltpu.roll(u, 2, 0)
        if long_seq:
            halo = car_scr[:, col0:col0 + FFN_SUB]
            h6 = halo[SUBLANES - 2:SUBLANES - 1]
            h7 = halo[SUBLANES - 1:SUBLANES]
            p1 = jnp.concatenate([jnp.where(row8 == 0, h7, p1[:SUBLANES]), p1[SUBLANES:]], axis=0)
            p2 = jnp.concatenate([jnp.where(row8 == 0, h6, jnp.where(row8 == 1, h7, p2[:SUBLANES])),
                                  p2[SUBLANES:]], axis=0)
            car_scr[:, col0:col0 + FFN_SUB] = u[tm - SUBLANES:]
            tail_ref[:, col0:col0 + FFN_SUB] = u[tm - SUBLANES:]
        else:
            e = st_ref[:, col0:col0 + FFN_SUB]
            p1 = jnp.where(t_in_seq == 0, pltpu.roll(e, tm - 1, 0), p1)
            p2 = jnp.where(t_in_seq < 2, e, p2)
            tail_ref[:, col0:col0 + FFN_SUB] = u
        cw = cw_ref[:, col0:col0 + FFN_SUB]
        return (cb_ref[:, col0:col0 + FFN_SUB]
                + (cw[0:1] * p2 + cw[1:2] * p1 + cw[2:3] * u))

    for c in range(D_FF // FFN_SUB):
        val = conv_part(c * FFN_SUB)
        gate = conv_part(D_FF + c * FFN_SUB)
        act_scr[:, c * FFN_SUB:(c + 1) * FFN_SUB] = (val * (gate * jax.nn.sigmoid(gate))).astype(BF16)

    y = x_ref[...] + jnp.dot(act_scr[...], wdn_ref[...], preferred_element_type=F32)
    if final_norm:
        y = _rms(y, gf_ref[...])
    o_ref[...] = y


def conv_ffn(x, g, w_up, conv_w, conv_b, w_down, conv0, gf, *, batch, seq_len, tm, final_norm):
    m, d = x.shape
    f2 = 2 * D_FF
    long_seq = seq_len >= tm
    if long_seq:
        assert seq_len % tm == 0
        tiles_per_seq = seq_len // tm
        st = jnp.pad(conv0, ((0, 0), (SUBLANES - (CONV_W - 1), 0), (0, 0)))
        st_spec = pl.BlockSpec((None, SUBLANES, f2), lambda i: (i // tiles_per_seq, 0, 0))
        tail_shape = jax.ShapeDtypeStruct((batch, SUBLANES, f2), F32)
        tail_spec = pl.BlockSpec((None, SUBLANES, f2), lambda i: (i // tiles_per_seq, 0, 0))
        sem = "arbitrary"
    else:
        assert seq_len == SUBLANES and tm % seq_len == 0
        st = jnp.pad(conv0, ((0, 0), (0, seq_len - (CONV_W - 1)), (0, 0))).reshape(m, f2)
        st_spec = pl.BlockSpec((tm, f2), lambda i: (i, 0))
        tail_shape = jax.ShapeDtypeStruct((m, f2), F32)
        tail_spec = pl.BlockSpec((tm, f2), lambda i: (i, 0))
        sem = "parallel"
    const = lambda i: (0, 0)
    out, tail = pl.pallas_call(
        functools.partial(_ffn_kernel, tm=tm, seq_len=seq_len, final_norm=final_norm),
        grid=(m // tm,),
        in_specs=[pl.BlockSpec((tm, d), lambda i: (i, 0)),
                  pl.BlockSpec((1, d), const),
                  pl.BlockSpec((d, f2), const, pipeline_mode=pl.Buffered(1)),
                  pl.BlockSpec((CONV_W, f2), const),
                  pl.BlockSpec((1, f2), const),
                  pl.BlockSpec((D_FF, d), const, pipeline_mode=pl.Buffered(1)),
                  st_spec,
                  pl.BlockSpec((1, d), const)],
        out_specs=[pl.BlockSpec((tm, d), lambda i: (i, 0)), tail_spec],
        out_shape=[jax.ShapeDtypeStruct((m, d), F32), tail_shape],
        scratch_shapes=[pltpu.VMEM((tm, d), BF16),
                        pltpu.VMEM((tm, D_FF), BF16),
                        pltpu.VMEM((SUBLANES, f2), F32)],
        compiler_params=_cparams(sem),
        name="conv_ffn",
    )(x, g.reshape(1, d), w_up, conv_w, conv_b.reshape(1, f2), w_down, st, gf.reshape(1, d))
    new_state = tail.reshape(batch, -1, f2)[:, -(CONV_W - 1):]
    return out, new_state


def _log_sigmoid(x):
    return jnp.minimum(x, 0.0) - jnp.log1p(jnp.exp(-jnp.abs(x)))


def _dot_nt(a, b):
    return lax.dot_general(a, b, (((1,), (1,)), ((), ())), preferred_element_type=F32)


def _dot_tn(a, b):
    return lax.dot_general(a, b, (((0,), (0,)), ((), ())), preferred_element_type=F32)


def _mlstm_kernel(bif_ref, q_ref, k_ref, v_ref, og_ref, gz_ref, gt_ref, c0_ref, n0_ref, m0_ref,
                  lng_ref, y_ref, c_ref, n_ref, m_ref, grow_scr, gcol_scr, bcol_scr, cmax_scr,
                  *, chunk, n_chunks, bb):
    L = chunk

    @pl.when(pl.program_id(1) == 0)
    def _():
        c_ref[...] = c0_ref[...]
        n_ref[...] = n0_ref[...]
        m_ref[...] = m0_ref[...]

    t_idx = lax.broadcasted_iota(jnp.int32, (L, L), 0)
    s_idx = lax.broadcasted_iota(jnp.int32, (L, L), 1)
    causal = s_idx <= t_idx
    scale = MLSTM_HD ** -0.5

    def compute(items):
        ph = []
        for j, c in items:
            start = (j * n_chunks + c) * L
            rows = pl.ds(start if isinstance(start, int) else pl.multiple_of(start, L), L)
            for h in range(MLSTM_HEADS):
                ph.append(gate_phase(j, c, h, rows))
        for d in ph:
            d["qk_raw"] = _dot_nt(d["qb"], d["kb"])
            d["qc"] = _dot_nt(d["qb"], d["c_old"].astype(BF16))
            d["c_upd"] = _dot_tn((d["v"] * d["wl_col"]).astype(BF16), d["kb"])
        for d in ph:
            d["qk"] = d["qk_raw"] * d["w"]
            d["pv"] = jnp.dot(d["qk"].astype(BF16), d["v"].astype(BF16), preferred_element_type=F32)
        stores = []
        for d in ph:
            stores += finish_phase(d)
        return stores

    def entry(j, c, h):
        return (j * n_chunks + c) * MLSTM_HEADS + h

    lane = lax.broadcasted_iota(jnp.int32, (1, LANES), 1)
    bias_row = jnp.zeros((1, LANES), F32)
    for i in range(2 * MLSTM_HEADS):
        bias_row = jnp.where(lane == i, bif_ref[i], bias_row)
    gates_col = gz_ref[...] + bias_row
    gates_col = jnp.where(lane >= MLSTM_HEADS, _log_sigmoid(gates_col), gates_col)
    sub = lax.broadcasted_iota(jnp.int32, (2 * MLSTM_HEADS, 1), 0)
    bias_col = jnp.zeros((2 * MLSTM_HEADS, 1), F32)
    for i in range(2 * MLSTM_HEADS):
        bias_col = jnp.where(sub == i, bif_ref[i], bias_col)
    tril = causal.astype(F32)
    triu = (t_idx <= s_idx).astype(F32)
    exact = dict(preferred_element_type=F32, precision=lax.Precision.HIGHEST)
    for j in range(bb):
        for c in range(n_chunks):
            rows = slice((j * n_chunks + c) * L, (j * n_chunks + c + 1) * L)
            g_c = gates_col[rows]
            b_c = jnp.dot(tril, g_c, **exact)
            g_r = gt_ref[j, :, c, :] + bias_col
            g_r = jnp.where(sub >= MLSTM_HEADS, _log_sigmoid(g_r), g_r)
            b_r = jnp.dot(g_r, triu, **exact)
            for h in range(MLSTM_HEADS):
                g_row = g_r[h:h + 1] - b_r[MLSTM_HEADS + h:MLSTM_HEADS + h + 1]
                b_col = b_c[:, MLSTM_HEADS + h:MLSTM_HEADS + h + 1]
                e = entry(j, c, h)
                grow_scr[e, 0:1, 0:L] = g_row
                gcol_scr[e, :, 0:1] = g_c[:, h:h + 1] - b_col
                bcol_scr[e, :, 0:1] = b_col
                cmax_scr[e, :, 0:1] = jnp.max(jnp.where(causal, g_row, -jnp.inf), axis=1, keepdims=True)

    def gate_phase(j, c, h, rows):
        cols = slice(h * MLSTM_HD, (h + 1) * MLSTM_HD)
        e = entry(j, c, h)
        g_row = grow_scr[e, 0:1, 0:L]
        b_col = bcol_scr[e, :, 0:1]
        m_old = m_ref[j, h]
        mu_col = jnp.maximum(m_old, cmax_scr[e, :, 0:1])
        w = jnp.where(causal, jnp.exp(g_row - mu_col), 0.0)
        s_inter = jnp.exp(m_old - mu_col)
        q = q_ref[rows, cols]
        ks = k_ref[rows, cols] * scale
        mu_last = mu_col[L - 1:L]
        return dict(j=j, h=h, rows=rows, cols=cols, q=q, ks=ks, v=v_ref[rows, cols],
                    qb=q.astype(BF16), kb=ks.astype(BF16), c_old=c_ref[j, h], n_old=n_ref[j, h],
                    w=w, s_inter=s_inter, m_t=b_col + mu_col,
                    wl_col=jnp.exp(gcol_scr[e, :, 0:1] - mu_last),
                    dec=jnp.exp(m_old - mu_last), m_new=b_col[L - 1:L] + mu_last)

    def finish_phase(d):
        j, h, rows, cols = d["j"], d["h"], d["rows"], d["cols"]
        num = d["s_inter"] * d["qc"] + d["pv"]
        den = (d["s_inter"] * jnp.sum(d["q"] * d["n_old"], axis=1, keepdims=True)
               + jnp.sum(d["qk"], axis=1, keepdims=True))
        hh = num / jnp.maximum(jnp.abs(den), jnp.exp(-d["m_t"]))
        return [(c_ref, (j, h), d["dec"] * d["c_old"] + d["c_upd"]),
                (n_ref, (j, h), d["dec"] * d["n_old"] + jnp.sum(d["wl_col"] * d["ks"], axis=0, keepdims=True)),
                (m_ref, (j, h), d["m_new"]),
                (y_ref, (rows, cols), hh)]

    def commit(stores):
        for ref, idx, val in stores:
            ref[idx] = val

    def chunk_body(j, c, carry):
        commit(compute([(j, c)]))
        return carry

    if n_chunks == 1:
        commit(compute([(j, 0) for j in range(bb)]))
    else:
        for j in range(bb):
            lax.fori_loop(0, n_chunks, functools.partial(chunk_body, j), 0)

    for h in range(MLSTM_HEADS):
        cols = slice(h * MLSTM_HD, (h + 1) * MLSTM_HD)
        hh = y_ref[:, cols]
        xc = hh - jnp.mean(hh, -1, keepdims=True)
        hn = xc * lax.rsqrt(jnp.mean(xc * xc, -1, keepdims=True) + EPS) * lng_ref[:, cols]
        y_ref[:, cols] = jax.nn.sigmoid(og_ref[:, cols]) * hn


def mlstm_mixer(z, b_if, c0, n0, m0, ln_g, *, batch, seq_len, tb, bb):
    m_rows = batch * seq_len
    L = math.gcd(seq_len, CHUNK)
    n_tb = seq_len // tb
    n_chunks = tb // L
    assert bb == 1 or n_tb == 1
    rows = bb * tb
    n_entries = bb * n_chunks * MLSTM_HEADS
    gates = z[:, Z0_GATE:Z0_GATE + 2 * MLSTM_HEADS].reshape(batch, seq_len, 2 * MLSTM_HEADS)
    gates_t = jnp.swapaxes(gates, 1, 2).reshape(batch, 2 * MLSTM_HEADS, seq_len // L, L)
    zcol = lambda off: pl.BlockSpec((rows, HALF), lambda b, t: (b * n_tb + t, off // HALF))
    state4 = lambda a, b_: pl.BlockSpec((bb, MLSTM_HEADS, a, b_), lambda b, t: (b, 0, 0, 0))
    y, c, n, m = pl.pallas_call(
        functools.partial(_mlstm_kernel, chunk=L, n_chunks=n_chunks, bb=bb),
        grid=(batch // bb, n_tb),
        in_specs=[pl.BlockSpec(memory_space=pltpu.SMEM),
                  zcol(Z0_Q), zcol(Z0_K), zcol(Z0_V), zcol(Z0_O),
                  pl.BlockSpec((rows, LANES), lambda b, t: (b * n_tb + t, Z0_GATE // LANES)),
                  pl.BlockSpec((bb, 2 * MLSTM_HEADS, n_chunks, L), lambda b, t: (b, 0, t, 0)),
                  state4(MLSTM_HD, MLSTM_HD), state4(1, MLSTM_HD), state4(1, 1),
                  pl.BlockSpec((1, HALF), lambda b, t: (0, 0))],
        out_specs=[pl.BlockSpec((rows, HALF), lambda b, t: (b * n_tb + t, 0)),
                   state4(MLSTM_HD, MLSTM_HD), state4(1, MLSTM_HD), state4(1, 1)],
        out_shape=[jax.ShapeDtypeStruct((m_rows, HALF), F32),
                   jax.ShapeDtypeStruct((batch, MLSTM_HEADS, MLSTM_HD, MLSTM_HD), F32),
                   jax.ShapeDtypeStruct((batch, MLSTM_HEADS, 1, MLSTM_HD), F32),
                   jax.ShapeDtypeStruct((batch, MLSTM_HEADS, 1, 1), F32)],
        scratch_shapes=[pltpu.VMEM((n_entries, SUBLANES, LANES), F32)]
        + [pltpu.VMEM((n_entries, L, LANES), F32)] * 3,
        compiler_params=_cparams("parallel", "arbitrary"),
        name="mlstm",
    )(b_if, z, z, z, z, z, gates_t,
      c0, n0.reshape(batch, MLSTM_HEADS, 1, MLSTM_HD), m0.reshape(batch, MLSTM_HEADS, 1, 1),
      ln_g.reshape(1, HALF))
    return y, c, n.reshape(batch, MLSTM_HEADS, MLSTM_HD), m.reshape(batch, MLSTM_HEADS)


def _split2(x):
    hi = x.astype(BF16)
    lo = (x - hi.astype(F32)).astype(BF16)
    return hi, lo


def _seg_sum(x, ones_blk):
    hi, lo = _split2(x)
    return (jnp.dot(hi, ones_blk, preferred_element_type=F32)
            + jnp.dot(lo, ones_blk, preferred_element_type=F32))


def _softplus(x):
    return jnp.maximum(x, 0.0) + jnp.log1p(jnp.exp(-jnp.abs(x)))


def _block_ones(n, seg):
    idx = jnp.arange(n) // seg
    return (idx[:, None] == idx[None, :]).astype(BF16)


def _rwkv_prep_kernel(z_ref, st_ref, mu_ref, w0_ref, wup_ref, a0_ref, aup_ref, gup_ref, kk_ref, ka_ref,
                      rk_ref, ones_ref, r_out, w_out, k_out, v_out, kk_out, kka_out, bv_out, g_out,
                      car_scr, *, tm, seq_len):
    zr = z_ref[...]
    row = lax.broadcasted_iota(jnp.int32, zr.shape, 0)
    prev = pltpu.roll(zr, 1, 0)
    if seq_len >= tm:
        @pl.when((pl.program_id(0) * tm) % seq_len == 0)
        def _():
            car_scr[...] = st_ref[...]
        prev = jnp.where(row == 0, car_scr[SUBLANES - 1:SUBLANES], prev)
        car_scr[...] = zr[tm - SUBLANES:]
    else:
        prev = jnp.where(row % seq_len == 0, st_ref[...], prev)
    zs = zr + (prev - zr) * mu_ref[...]
    r = zs[:, :HALF]
    kr = zs[:, HALF:2 * HALF]
    vr = zs[:, 2 * HALF:3 * HALF]
    lora = zs[:, RW_LORA:RW_G]
    ones = ones_ref[...]
    w_log = -_softplus(-(w0_ref[...] + jnp.dot(jnp.tanh(lora).astype(BF16), wup_ref[...],
                                               preferred_element_type=F32))) - 0.5
    a = jax.nn.sigmoid(a0_ref[...] + jnp.dot(lora.astype(BF16), aup_ref[...], preferred_element_type=F32))
    g = jnp.dot(jax.nn.sigmoid(zs[:, RW_G:]).astype(BF16), gup_ref[...], preferred_element_type=F32)
    kk = kr * kk_ref[...]
    kk = kk * lax.rsqrt(jnp.maximum(_seg_sum(kk * kk, ones), 1e-24))
    k2 = kr * (1.0 + (a - 1.0) * ka_ref[...])
    r_out[...] = r
    w_out[...] = jnp.exp(-jnp.exp(w_log))
    k_out[...] = k2
    v_out[...] = vr
    kk_out[...] = kk
    kka_out[...] = kk * a
    bv_out[...] = _seg_sum(r * k2 * rk_ref[...], ones) * vr
    g_out[...] = g


RWKV_GROUPS = 2
RWKV_UNROLL = 4


def _rwkv_rec_kernel(r_ref, w_ref, k_ref, v_ref, kk_ref, kka_ref, bv_ref, g_ref, s0_ref, lng_ref, lnb_ref,
                     onesy_ref, ones_ref, y_ref, s_ref, yraw_scr, lhs_scr, ylhs_scr, tmaj_scr=None,
                     *, bb, tb):
    @pl.when(pl.program_id(1) == 0)
    def _():
        s_ref[...] = s0_ref[...]

    if tmaj_scr is not None:
        streams = (r_ref, w_ref, k_ref, v_ref, kk_ref, kka_ref, bv_ref, g_ref)
        for i, ref in enumerate(streams):
            for t in range(tb):
                tmaj_scr[i, t] = ref[:, t, :]
        r_ref, w_ref, k_ref, v_ref, kk_ref, kka_ref, bv_ref, g_ref = (
            tmaj_scr.at[i] for i in range(len(streams)))

    pairs = RWKV_HEADS // 2
    n_tiles = bb * pairs
    shape = (RWKV_HD, LANES)
    eye2 = (lax.broadcasted_iota(jnp.int32, shape, 0)
            == lax.broadcasted_iota(jnp.int32, shape, 1) % RWKV_HD)
    pair_diag = (lax.broadcasted_iota(jnp.int32, shape, 0) // 2
                 == (lax.broadcasted_iota(jnp.int32, shape, 1) % RWKV_HD) // 2)
    lane_even = lax.broadcasted_iota(jnp.int32, (1, LANES), 1) % 2 == 0
    sublane_even = lax.broadcasted_iota(jnp.int32, (SUBLANES, 1), 0) % 2 == 0
    y_rows = lambda idx: slice((idx // 2) * RWKV_HD, (idx // 2 + 1) * RWKV_HD)
    y_cols = lambda idx: slice((idx % 2) * LANES, (idx % 2 + 1) * LANES)

    per_group = n_tiles // RWKV_GROUPS

    def step(t, carry):
        def tile(idx):
            b, p = divmod(idx, pairs)
            cols = slice(p * LANES, (p + 1) * LANES)
            return b, p, (lambda ref: ref[t, b:b + 1, cols])

        groups = [range(g * per_group, (g + 1) * per_group) for g in range(RWKV_GROUPS)]
        g_rows = per_group * RWKV_HD * 3 // 2
        sums = []
        for g, grp in enumerate(groups):
            for idx in grp:
                b, p, row = tile(idx)
                rel = idx - grp[0]
                base = g * g_rows + rel * RWKV_HD
                hi, lo = _split2(s_ref[b, p] * row(kk_ref))
                lhs_scr[base:base + RWKV_HD, :LANES] = hi
                lhs_scr[base:base + RWKV_HD, LANES:] = lo
                v_row = row(v_ref)
                v_hi = v_row.astype(BF16).astype(F32)
                v_lo = v_row - v_hi
                v_lo_swapped = jnp.where(lane_even, pltpu.roll(v_lo, LANES - 1, 1), pltpu.roll(v_lo, 1, 1))
                pattern = jnp.where(sublane_even, jnp.where(lane_even, v_hi, v_lo_swapped),
                                    jnp.where(lane_even, v_lo_swapped, v_hi))
                v_base = g * g_rows + per_group * RWKV_HD + (rel // 2) * RWKV_HD
                lhs_scr[v_base:v_base + RWKV_HD, y_cols(rel)] = jnp.where(
                    pair_diag, jnp.concatenate([pattern] * (RWKV_HD // SUBLANES), axis=0), 0.0).astype(BF16)
            sums.append(jnp.dot(lhs_scr[g * g_rows:(g + 1) * g_rows, :], onesy_ref[...],
                                preferred_element_type=F32))
        ysums = []
        for grp, sm in zip(groups, sums):
            for idx in grp:
                b, p, row = tile(idx)
                rel = idx - grp[0]
                sa_rows = slice(rel * RWKV_HD, (rel + 1) * RWKV_HD)
                sa = sm[sa_rows, :LANES] + sm[sa_rows, LANES:]
                v_base = per_group * RWKV_HD + (rel // 2) * RWKV_HD
                v_tile = sm[v_base:v_base + RWKV_HD, y_cols(rel)]
                s = s_ref[b, p] * row(w_ref) - sa * row(kka_ref) + v_tile * row(k_ref)
                s_ref[b, p] = s
                ylhs_scr[y_rows(idx), y_cols(idx)] = (s * row(r_ref)).astype(BF16)
            g_rows = slice(y_rows(grp[0]).start, y_rows(grp[-1]).stop)
            ysums.append(jnp.dot(ylhs_scr[g_rows, :], onesy_ref[...], preferred_element_type=F32))
        for grp, ys in zip(groups, ysums):
            for idx in grp:
                b, p, row = tile(idx)
                rel = slice(y_rows(idx).start - y_rows(grp[0]).start, y_rows(idx).stop - y_rows(grp[0]).start)
                yraw_scr[t, b:b + 1, p * LANES:(p + 1) * LANES] = jnp.sum(
                    jnp.where(eye2, ys[rel, y_cols(idx)], 0.0), axis=0, keepdims=True)
        return carry

    lax.fori_loop(0, tb, step, 0, unroll=RWKV_UNROLL)

    ones = ones_ref[...]
    y = yraw_scr[...].reshape(tb * bb, HALF)
    xc = y - _seg_sum(y, ones) * (1.0 / RWKV_HD)
    var = _seg_sum(xc * xc, ones) * (1.0 / RWKV_HD)
    yn = xc * lax.rsqrt(var + RWKV_LN_EPS) * lng_ref[...] + lnb_ref[...]
    out = (yn + bv_ref[...].reshape(tb * bb, HALF)) * g_ref[...].reshape(tb * bb, HALF)
    if tmaj_scr is None:
        y_ref[...] = out.reshape(tb, bb, HALF)
    else:
        for t in range(tb):
            y_ref[:, t, :] = out[t * bb:(t + 1) * bb]


def rwkv_mixer(z, s0, shift0, mu, w0, w_up, a0, a_up, g_up, k_k, k_a, r_k, ln_g, ln_b,
               *, batch, seq_len, tm, bb, tb):
    m_rows = batch * seq_len
    tm_shape = (seq_len, batch, HALF)
    if seq_len >= tm:
        tiles_per_seq = seq_len // tm
        st = jnp.pad(shift0[:, None], ((0, 0), (SUBLANES - 1, 0), (0, 0)))
        st_spec = pl.BlockSpec((None, SUBLANES, RWKV_COLS), lambda i: (i // tiles_per_seq, 0, 0))
        sem = "arbitrary"
        tok = pl.BlockSpec((tm, HALF), lambda i: (i % tiles_per_seq, i // tiles_per_seq))
        tok_shape = (seq_len, batch * HALF)
    else:
        st = jnp.pad(shift0[:, None], ((0, 0), (0, seq_len - 1), (0, 0))).reshape(m_rows, RWKV_COLS)
        st_spec = pl.BlockSpec((tm, RWKV_COLS), lambda i: (i, 0))
        sem = "parallel"
        tok = pl.BlockSpec((tm, HALF), lambda i: (i, 0))
        tok_shape = (m_rows, HALF)
    row = lambda a: a.reshape(1, -1)
    const = lambda i: (0, 0)
    vec = pl.BlockSpec((1, HALF), const)
    lora_rows = RWKV_W_LORA + RWKV_A_LORA
    w_up_p = jnp.pad(w_up, ((0, RWKV_A_LORA), (0, 0))).astype(BF16)
    a_up_p = jnp.pad(a_up, ((RWKV_W_LORA, 0), (0, 0))).astype(BF16)
    ones_half = _block_ones(HALF, RWKV_HD)
    prep = pl.pallas_call(
        functools.partial(_rwkv_prep_kernel, tm=tm, seq_len=seq_len),
        grid=(m_rows // tm,),
        in_specs=[pl.BlockSpec((tm, RWKV_COLS), lambda i: (i, Z0_RWKV // RWKV_COLS)),
                  st_spec,
                  pl.BlockSpec((1, RWKV_COLS), const),
                  vec, pl.BlockSpec((lora_rows, HALF), const),
                  vec, pl.BlockSpec((lora_rows, HALF), const),
                  pl.BlockSpec((RWKV_G_LORA, HALF), const),
                  vec, vec, vec,
                  pl.BlockSpec((HALF, HALF), const)],
        out_specs=[tok] * 8,
        out_shape=[jax.ShapeDtypeStruct(tok_shape, F32)] * 8,
        scratch_shapes=[pltpu.VMEM((SUBLANES, RWKV_COLS), F32)],
        compiler_params=_cparams(sem),
        name="rwkv_prep",
    )(z, st, row(mu), row(w0), w_up_p, row(a0), a_up_p, g_up.astype(BF16), row(k_k), row(k_a), row(r_k),
      ones_half)
    pairs = RWKV_HEADS // 2
    s_pairs = (s0.reshape(batch, pairs, 2, RWKV_HD, RWKV_HD).transpose(0, 1, 3, 2, 4)
               .reshape(batch, pairs, RWKV_HD, LANES))
    scratch = [pltpu.VMEM((tb, bb, HALF), F32),
               pltpu.VMEM((bb * pairs * RWKV_HD * 3 // 2, 2 * LANES), BF16),
               pltpu.VMEM((bb * pairs * RWKV_HD // 2, 2 * LANES), BF16)]
    if seq_len >= tm:
        seqs = [a.reshape(tm_shape) for a in prep]
        blk = pl.BlockSpec((tb, bb, HALF), lambda b, t: (t, b, 0))
        y_shape = tm_shape
    else:
        assert tb == seq_len
        seqs = [a.reshape(batch, seq_len, HALF) for a in prep]
        blk = pl.BlockSpec((bb, tb, HALF), lambda b, t: (b, 0, 0))
        y_shape = (batch, seq_len, HALF)
        scratch.append(pltpu.VMEM((len(prep), tb, bb, HALF), F32))
    s_spec = pl.BlockSpec((bb, pairs, RWKV_HD, LANES), lambda b, t: (b, 0, 0, 0))
    const2 = lambda b, t: (0, 0)
    y, s_new = pl.pallas_call(
        functools.partial(_rwkv_rec_kernel, bb=bb, tb=tb),
        grid=(batch // bb, seq_len // tb),
        in_specs=[blk] * 8 + [s_spec,
                              pl.BlockSpec((1, HALF), const2), pl.BlockSpec((1, HALF), const2),
                              pl.BlockSpec((2 * LANES, 2 * LANES), const2),
                              pl.BlockSpec((HALF, HALF), const2)],
        out_specs=[blk, s_spec],
        out_shape=[jax.ShapeDtypeStruct(y_shape, F32),
                   jax.ShapeDtypeStruct((batch, pairs, RWKV_HD, LANES), F32)],
        scratch_shapes=scratch,
        compiler_params=_cparams("parallel", "arbitrary"),
        name="rwkv_rec",
    )(*seqs, s_pairs, row(ln_g), row(ln_b),
      _block_ones(2 * LANES, RWKV_HD), ones_half)
    s_new = (s_new.reshape(batch, pairs, RWKV_HD, 2, RWKV_HD).transpose(0, 1, 3, 2, 4)
             .reshape(batch, RWKV_HEADS, RWKV_HD, RWKV_HD))
    if seq_len < tm:
        y = y.reshape(m_rows, HALF)
    return y, s_new


GLA_SUB = 16


def _gla_kernel(q_ref, k_ref, v_ref, g_ref, ad_ref, aup_ref, ab_ref, lng_ref, s0_ref,
                y_ref, s_ref, bc_scr, *, chunk, n_chunks, bb):
    L = chunk
    sub = min(GLA_SUB, L)

    @pl.when(pl.program_id(1) == 0)
    def _():
        s_ref[...] = s0_ref[...]

    lane = lax.broadcasted_iota(jnp.int32, (1, LANES), 1)
    head_mask = [(lane < GLA_DK).astype(F32), (lane >= GLA_DK).astype(F32)]
    tril = (lax.broadcasted_iota(jnp.int32, (L, L), 1)
            <= lax.broadcasted_iota(jnp.int32, (L, L), 0)).astype(F32)
    eye = (lax.broadcasted_iota(jnp.int32, (LANES, LANES), 0)
           == lax.broadcasted_iota(jnp.int32, (LANES, LANES), 1))

    pre = jnp.dot(ad_ref[...].astype(BF16), aup_ref[...], preferred_element_type=F32) + ab_ref[...]
    la = _log_sigmoid(pre) * (1.0 / GLA_TAU)
    for blk in range(bb * n_chunks):
        rows = slice(blk * L, (blk + 1) * L)
        bc_scr[rows, :] = jnp.dot(tril, la[rows], preferred_element_type=F32,
                                  precision=lax.Precision.HIGHEST)

    stack = lambda x: jnp.concatenate([x * head_mask[0], x * head_mask[1]], axis=0).astype(BF16)

    def compute(items):
        units = []
        for j, c in items:
            start = (j * n_chunks + c) * L
            rows = pl.ds(start if isinstance(start, int) else pl.multiple_of(start, L), L)
            for p in range(GLA_HEADS // 2):
                kcols = slice(p * LANES, (p + 1) * LANES)
                units.append(dict(
                    j=j, p=p, rows=rows, qp=q_ref[rows, kcols] * (GLA_DK ** -0.5), kp=k_ref[rows, kcols],
                    bc=bc_scr[rows, kcols], s_old=s_ref[j, p],
                    v_pair=v_ref[rows, 2 * p * GLA_DV:2 * (p + 1) * GLA_DV].astype(BF16)))
        for d in units:
            qp, kp, bc, v_pair = d["qp"], d["kp"], d["bc"], d["v_pair"]
            d["inter"] = jnp.dot(stack(qp * jnp.exp(bc)), d["s_old"].astype(BF16), preferred_element_type=F32)
            d["att"] = []
            for i in range(L // sub):
                lo, hi = i * sub, (i + 1) * sub
                c_i = bc[lo - 1:lo] if i > 0 else jnp.zeros((1, LANES), F32)
                qe = qp[lo:hi] * jnp.exp(bc[lo:hi] - c_i)
                ke = kp[:hi] * jnp.exp(c_i - bc[:hi])
                d["att"].append(_dot_nt(stack(qe), ke.astype(BF16)))
            bl = bc[L - 1:L]
            v_stack = jnp.concatenate([v_pair[:, :GLA_DV], v_pair[:, GLA_DV:]], axis=0)
            d["s_upd"] = _dot_tn(stack(kp * jnp.exp(bl - bc)), v_stack)
            d["dec_col"] = jnp.sum(jnp.where(eye, jnp.exp(bl), 0.0), axis=1, keepdims=True)
        for d in units:
            d["intra"] = []
            for i, att in enumerate(d["att"]):
                lo, hi = i * sub, (i + 1) * sub
                t_idx = lo + lax.broadcasted_iota(jnp.int32, (2 * sub, hi), 0) % sub
                s_idx = lax.broadcasted_iota(jnp.int32, (2 * sub, hi), 1)
                att = jnp.where(s_idx <= t_idx, att, 0.0)
                d["intra"].append(jnp.dot(att.astype(BF16), d["v_pair"][:hi], preferred_element_type=F32))
        stores = []
        for d in units:
            for e in range(2):
                h = 2 * d["p"] + e
                vcols = slice(h * GLA_DV, (h + 1) * GLA_DV)
                o = d["inter"][e * L:(e + 1) * L] + jnp.concatenate(
                    [blk[e * sub:(e + 1) * sub, e * GLA_DV:(e + 1) * GLA_DV] for blk in d["intra"]], axis=0)
                stores.append((y_ref, (d["rows"], vcols), o))
            stores.append((s_ref, (d["j"], d["p"]), d["dec_col"] * d["s_old"] + d["s_upd"]))
        return stores

    def commit(stores):
        for ref, idx, val in stores:
            ref[idx] = val

    def chunk_body(j, c, carry):
        commit(compute([(j, c)]))
        return carry

    if n_chunks == 1:
        commit(compute([(j, 0) for j in range(bb)]))
    else:
        for j in range(bb):
            lax.fori_loop(0, n_chunks, functools.partial(chunk_body, j), 0)

    for h in range(GLA_HEADS):
        vcols = slice(h * GLA_DV, (h + 1) * GLA_DV)
        o = y_ref[:, vcols]
        on = o * lax.rsqrt(jnp.mean(o * o, -1, keepdims=True) + EPS) * lng_ref[:, vcols]
        gate = g_ref[:, vcols]
        y_ref[:, vcols] = on * (gate * jax.nn.sigmoid(gate))


def gla_mixer(z, s0, a_up, a_b, ln_g, *, batch, seq_len, tb, bb, time_major):
    m_rows = batch * seq_len
    L = math.gcd(seq_len, CHUNK)
    n_tb = seq_len // tb
    pairs = GLA_HEADS // 2
    assert bb == 1 or (n_tb == 1 and not time_major)
    rows = bb * tb
    if time_major:
        zblk = lambda width, off: pl.BlockSpec((tb, width), lambda b, t: (t, (b * Z1_COLS + off) // width))
    else:
        zblk = lambda width, off: pl.BlockSpec((rows, width), lambda b, t: (b * n_tb + t, off // width))
    s_spec = pl.BlockSpec((bb, pairs, 2 * GLA_DK, GLA_DV), lambda b, t: (b, 0, 0, 0))
    const = lambda b, t: (0, 0)
    a_up_p = jnp.pad(a_up, ((0, LANES - GLA_GATE_RANK), (0, 0))).astype(BF16)
    y, s_new = pl.pallas_call(
        functools.partial(_gla_kernel, chunk=L, n_chunks=tb // L, bb=bb),
        grid=(batch // bb, n_tb),
        in_specs=[zblk(GLA_QK, Z1_Q), zblk(GLA_QK, Z1_K), zblk(HALF, Z1_V), zblk(HALF, Z1_G),
                  zblk(LANES, Z1_AD),
                  pl.BlockSpec((LANES, GLA_QK), const), pl.BlockSpec((1, GLA_QK), const),
                  pl.BlockSpec((1, HALF), const), s_spec],
        out_specs=[pl.BlockSpec((rows, HALF), lambda b, t: (b * n_tb + t, 0)), s_spec],
        out_shape=[jax.ShapeDtypeStruct((m_rows, HALF), F32),
                   jax.ShapeDtypeStruct((batch, pairs, 2 * GLA_DK, GLA_DV), F32)],
        scratch_shapes=[pltpu.VMEM((rows, GLA_QK), F32)],
        compiler_params=_cparams("parallel", "arbitrary"),
        name="gla",
    )(z, z, z, z, z, a_up_p, a_b.reshape(1, GLA_QK), ln_g.reshape(1, HALF),
      s0.reshape(batch, pairs, 2 * GLA_DK, GLA_DV))
    return y, s_new.reshape(batch, GLA_HEADS, GLA_DK, GLA_DV)


S5_KT = LANES // S5_GROUP
S5_SCAN_SPLIT = 2


def _s5_param_kernel(lre_ref, lim_ref, step_ref, bre_ref, bim_ref, bbre_ref, bbim_ref, pre_ref, pim_ref):
    lre, lim = lre_ref[...], lim_ref[...]
    step = jnp.exp(step_ref[...])
    mag = jnp.exp(lre * step)
    bar_re = mag * jnp.cos(lim * step)
    bar_im = mag * jnp.sin(lim * step)
    inv = 1.0 / (lre * lre + lim * lim)
    cre = ((bar_re - 1.0) * lre + bar_im * lim) * inv
    cim = (bar_im * lre - (bar_re - 1.0) * lim) * inv
    for g in range(S5_GROUPS):
        cr, ci = cre[g:g + 1], cim[g:g + 1]
        bbre_ref[g] = cr * bre_ref[g] - ci * bim_ref[g]
        bbim_ref[g] = cr * bim_ref[g] + ci * bre_ref[g]
    pre_ref[0] = bar_re
    pim_ref[0] = bar_im


def _s5_kernel(u_ref, wbre_ref, wbim_ref, lre_ref, lim_ref, x0re_ref, x0im_ref, wcre_ref, wcim_ref,
               d_ref, wglu_ref, y_ref, xre_ref, xim_ref, sre_scr, sim_scr, *, tt, bb, row_major):
    @pl.when(pl.program_id(1) == 0)
    def _():
        xre_ref[...] = x0re_ref[...]
        xim_ref[...] = x0im_ref[...]

    rows_all = tt * bb
    if row_major:
        u = jnp.concatenate([u_ref[:, t, :] for t in range(tt)], axis=0)
    else:
        u = u_ref[...].reshape(rows_all, S5_CH)
    ub = u.astype(BF16)
    nblk = S5_CH // LANES
    wide = S5_STATE // nblk
    for kt in range(nblk):
        cols = slice(kt * wide, (kt + 1) * wide)
        ukt = ub[:, kt * LANES:(kt + 1) * LANES]
        sre_scr[:, cols] = jnp.dot(ukt, wbre_ref[kt], preferred_element_type=F32)
        sim_scr[:, cols] = jnp.dot(ukt, wbim_ref[kt], preferred_element_type=F32)

    for part in range(S5_SCAN_SPLIT):
        width = S5_STATE // S5_SCAN_SPLIT
        cols = slice(part * width, (part + 1) * width)
        lr, li = lre_ref[:, cols], lim_ref[:, cols]

        def step(t, carry, cols=cols, lr=lr, li=li):
            cr, ci = carry
            rows = pl.ds(pl.multiple_of(t * bb, bb), bb)
            nr = sre_scr[rows, cols] + (lr * cr - li * ci)
            ni = sim_scr[rows, cols] + (lr * ci + li * cr)
            sre_scr[rows, cols] = nr
            sim_scr[rows, cols] = ni
            return nr, ni

        cr, ci = lax.fori_loop(0, tt, step, (xre_ref[:, cols], xim_ref[:, cols]))
        xre_ref[:, cols] = cr
        xim_ref[:, cols] = ci

    parts = []
    for nt in range(nblk):
        cols = slice(nt * wide, (nt + 1) * wide)
        parts.append(jnp.dot(sre_scr[:, cols].astype(BF16), wcre_ref[nt], preferred_element_type=F32)
                     - jnp.dot(sim_scr[:, cols].astype(BF16), wcim_ref[nt], preferred_element_type=F32))
    y = jnp.concatenate(parts, axis=1) + d_ref[...] * u
    ys = y * (0.5 * (1.0 + jnp.tanh(math.sqrt(2.0 / math.pi) * (y + 0.044715 * (y * y * y)))))
    out = ys * jax.nn.sigmoid(jnp.dot(ys.astype(BF16), wglu_ref[...], preferred_element_type=F32))
    if row_major:
        for t in range(tt):
            y_ref[:, t, :] = out[t * bb:(t + 1) * bb]
    else:
        y_ref[...] = out.reshape(tt, bb, S5_CH)


def s5_mixer(z, x0_re, x0_im, lam_re, lam_im, log_step, b_re, b_im, c_re, c_im, d_skip,
             w_glu, *, batch, seq_len, tt, bb, time_major):
    u_col_block = Z1_U // S5_CH
    if time_major:
        z = z.reshape(seq_len, batch, Z1_COLS)
        u_spec = pl.BlockSpec((tt, bb, S5_CH), lambda b, t: (t, b, u_col_block))
        y_spec = pl.BlockSpec((tt, bb, S5_CH), lambda b, t: (t, b, 0))
        y_shape = (seq_len, batch, S5_CH)
    else:
        assert tt == seq_len
        z = z.reshape(batch, seq_len, Z1_COLS)
        u_spec = pl.BlockSpec((bb, tt, S5_CH), lambda b, t: (b, 0, u_col_block))
        y_spec = pl.BlockSpec((bb, tt, S5_CH), lambda b, t: (b, 0, 0))
        y_shape = (batch, seq_len, S5_CH)
    gpc = (S5_GROUPS, S5_GROUP, S5_P)
    bb_re, bb_im, p_re, p_im = pl.pallas_call(
        _s5_param_kernel,
        out_shape=[jax.ShapeDtypeStruct(gpc, F32), jax.ShapeDtypeStruct(gpc, F32),
                   jax.ShapeDtypeStruct((1, S5_GROUPS, S5_P), F32),
                   jax.ShapeDtypeStruct((1, S5_GROUPS, S5_P), F32)],
        name="s5_params",
    )(lam_re, lam_im, log_step.reshape(S5_GROUPS, 1), jnp.swapaxes(b_re, 1, 2), jnp.swapaxes(b_im, 1, 2))

    nblk = S5_CH // LANES
    eye = jnp.eye(S5_KT, dtype=F32)

    def in_blocks(w):
        w = w.reshape(nblk, S5_KT, S5_GROUP, 1, S5_P) * eye[None, :, None, :, None]
        return w.reshape(nblk, LANES, S5_KT * S5_P).astype(BF16)

    def out_blocks(c):
        w = jnp.swapaxes(c, 1, 2).reshape(nblk, S5_KT, S5_P, 1, S5_GROUP) * eye[None, :, None, :, None]
        return w.reshape(nblk, S5_KT * S5_P, LANES).astype(BF16)

    lam_rows = lambda p: jnp.broadcast_to(p.reshape(1, S5_STATE), (bb, S5_STATE))
    const2 = lambda b, t: (0, 0)
    const3 = lambda b, t: (0, 0, 0)
    st_spec = pl.BlockSpec((bb, S5_STATE), lambda b, t: (b, 0))
    wb_spec = pl.BlockSpec((nblk, LANES, S5_KT * S5_P), const3)
    wc_spec = pl.BlockSpec((nblk, S5_KT * S5_P, LANES), const3)
    lam_spec = pl.BlockSpec((bb, S5_STATE), const2)
    y, x_re, x_im = pl.pallas_call(
        functools.partial(_s5_kernel, tt=tt, bb=bb, row_major=not time_major),
        grid=(batch // bb, seq_len // tt),
        in_specs=[u_spec,
                  wb_spec, wb_spec, lam_spec, lam_spec, st_spec, st_spec, wc_spec, wc_spec,
                  pl.BlockSpec((1, S5_CH), const2), pl.BlockSpec((S5_CH, S5_CH), const2)],
        out_specs=[y_spec, st_spec, st_spec],
        out_shape=[jax.ShapeDtypeStruct(y_shape, F32),
                   jax.ShapeDtypeStruct((batch, S5_STATE), F32),
                   jax.ShapeDtypeStruct((batch, S5_STATE), F32)],
        scratch_shapes=[pltpu.VMEM((tt * bb, S5_STATE), F32), pltpu.VMEM((tt * bb, S5_STATE), F32)],
        compiler_params=_cparams("parallel", "arbitrary"),
        name="s5",
    )(z, in_blocks(bb_re), in_blocks(bb_im), lam_rows(p_re), lam_rows(p_im),
      x0_re.reshape(batch, S5_STATE), x0_im.reshape(batch, S5_STATE),
      out_blocks(c_re), out_blocks(c_im), d_skip.reshape(1, S5_CH), w_glu.astype(BF16))
    shape = (batch, S5_GROUPS, S5_P)
    if not time_major:
        y = y.reshape(batch * seq_len, S5_CH)
    return y, x_re.reshape(shape), x_im.reshape(shape)


def _pack_even_w_in(w_in):
    d = w_in.shape[0]
    m_cols = 4 * HALF + 2 * MLSTM_HEADS
    parts = [w_in[:, m_cols:],
             w_in[:, 4 * HALF:m_cols], jnp.zeros((d, Z0_Q - Z0_GATE - 2 * MLSTM_HEADS), w_in.dtype),
             w_in[:, :4 * HALF]]
    return jnp.concatenate(parts, axis=1).astype(BF16)


def _pack_odd_w_in(w_in):
    gla_main = 2 * GLA_QK + 2 * HALF
    gla_cols = gla_main + GLA_GATE_RANK
    parts = [w_in[:, :gla_main], w_in[:, gla_cols:], w_in[:, gla_main:gla_cols]]
    w = jnp.concatenate(parts, axis=1)
    return jnp.pad(w, ((0, 0), (0, Z1_COLS - w.shape[1]))).astype(BF16)


def _trunk(x, st, prm, *, batch, seq_len):
    (st_C, st_n, st_m, st_S, st_shift, st_gla, st_re, st_im, st_conv) = st
    (norm_mix, norm_ffn, norm_final,
     e_w_in, e_b_if, e_mu, e_w0, e_w_up, e_a0, e_a_up, e_g_up, e_k_k, e_k_a, e_r_k, e_ln_m_g, e_ln_r_g,
     e_ln_r_b, e_w_out,
     o_w_in, o_a_up, o_a_b, o_ln_g, o_lam_re, o_lam_im, o_log_step, o_b_re, o_b_im, o_c_re, o_c_im, o_d,
     o_w_glu, o_w_out,
     f_w_up, f_conv_w, f_conv_b, f_w_down) = prm
    assert len(e_w_in) == 1 and len(o_w_in) == 1 and len(f_w_up) == 2, "wired for depth 2"
    long_seq = seq_len >= 512
    assert long_seq or seq_len == SUBLANES
    m_rows = batch * seq_len
    tiles = dict(
        tm_proj=512,
        tm_tok=512 if long_seq else 256,
        tm_ffn=512 if long_seq else 128,
        tb=512 if long_seq else seq_len,
        bb_chunked=1 if long_seq else 4,
        tb_rwkv=128 if long_seq else seq_len,
        tt_s5=128 if long_seq else seq_len,
    )
    x = x.reshape(m_rows, D_MODEL)
    ffn = functools.partial(conv_ffn, batch=batch, seq_len=seq_len, tm=tiles["tm_ffn"])

    z0 = norm_matmul(x, norm_mix[0], _pack_even_w_in(e_w_in[0]), tm=tiles["tm_proj"], tn=512)
    y_m, c_new, n_new, m_new = mlstm_mixer(z0, e_b_if[0], st_C[0], st_n[0], st_m[0], e_ln_m_g[0],
                                           batch=batch, seq_len=seq_len, tb=tiles["tb"], bb=tiles["bb_chunked"])
    y_r, s_new = rwkv_mixer(z0, st_S[0], st_shift[0], e_mu[0], e_w0[0], e_w_up[0], e_a0[0], e_a_up[0],
                            e_g_up[0], e_k_k[0], e_k_a[0], e_r_k[0], e_ln_r_g[0], e_ln_r_b[0],
                            batch=batch, seq_len=seq_len, tm=tiles["tm_tok"], bb=SUBLANES,
                            tb=tiles["tb_rwkv"])
    shift_new = z0.reshape(batch, seq_len, Z0_COLS)[:, -1, Z0_RWKV:Z0_RWKV + RWKV_COLS]
    w_out = e_w_out[0].astype(BF16)
    x = out_proj(x, y_m, y_r, w_out[:HALF], w_out[HALF:], tm=tiles["tm_tok"])
    x, conv_new0 = ffn(x, norm_ffn[0], f_w_up[0].astype(BF16), f_conv_w[0], f_conv_b[0],
                       f_w_down[0].astype(BF16), st_conv[0], norm_final, final_norm=False)

    z1 = norm_matmul(x, norm_mix[1], _pack_odd_w_in(o_w_in[0]), tm=tiles["tm_proj"], tn=512,
                     time_major_seq_len=seq_len if long_seq else None)
    y_g, gla_new = gla_mixer(z1, st_gla[0], o_a_up[0], o_a_b[0], o_ln_g[0],
                             batch=batch, seq_len=seq_len, tb=tiles["tb"], bb=tiles["bb_chunked"],
                             time_major=long_seq)
    y_s, re_new, im_new = s5_mixer(z1, st_re[0], st_im[0], o_lam_re[0], o_lam_im[0],
                                   o_log_step[0], o_b_re[0], o_b_im[0], o_c_re[0], o_c_im[0], o_d[0],
                                   o_w_glu[0], batch=batch, seq_len=seq_len, tt=tiles["tt_s5"],
                                   bb=SUBLANES, time_major=long_seq)
    w_out = o_w_out[0].astype(BF16)
    x = out_proj(x, y_g, y_s, w_out[:HALF], w_out[HALF:], tm=tiles["tm_tok"])
    x, conv_new1 = ffn(x, norm_ffn[1], f_w_up[1].astype(BF16), f_conv_w[1], f_conv_b[1],
                       f_w_down[1].astype(BF16), st_conv[1], norm_final, final_norm=True)
    new_state = (c_new[None], n_new[None], m_new[None], s_new[None], shift_new[None],
                 gla_new[None], re_new[None], im_new[None], jnp.stack([conv_new0, conv_new1]))
    return x.reshape(batch, seq_len, D_MODEL), new_state


def kernel(x_prompt, x_sample, state_mlstm_C, state_mlstm_n, state_mlstm_m, state_rwkv_S, state_rwkv_shift,
           state_gla_S, state_s5_re, state_s5_im, state_ffn_conv,
           norm_mix, norm_ffn, norm_final,
           e_w_in, e_b_if, e_mu, e_w0, e_w_up, e_a0, e_a_up, e_g_up, e_k_k, e_k_a, e_r_k,
           e_ln_m_g, e_ln_r_g, e_ln_r_b, e_w_out,
           o_w_in, o_a_up, o_a_b, o_ln_g, o_lam_re, o_lam_im, o_log_step, o_b_re, o_b_im, o_c_re, o_c_im,
           o_d, o_w_glu, o_w_out,
           f_w_up, f_conv_w, f_conv_b, f_w_down):
    params = (norm_mix, norm_ffn, norm_final,
              e_w_in, e_b_if, e_mu, e_w0, e_w_up, e_a0, e_a_up, e_g_up, e_k_k, e_k_a, e_r_k,
              e_ln_m_g, e_ln_r_g, e_ln_r_b, e_w_out,
              o_w_in, o_a_up, o_a_b, o_ln_g, o_lam_re, o_lam_im, o_log_step, o_b_re, o_b_im, o_c_re, o_c_im,
              o_d, o_w_glu, o_w_out,
              f_w_up, f_conv_w, f_conv_b, f_w_down)
    bp, tp, _ = x_prompt.shape
    bs, ts, _ = x_sample.shape
    n_even, n_odd, depth = state_mlstm_C.shape[0], state_gla_S.shape[0], state_ffn_conv.shape[0]
    init = (jnp.zeros((n_even, bp, MLSTM_HEADS, MLSTM_HD, MLSTM_HD), F32),
            jnp.zeros((n_even, bp, MLSTM_HEADS, MLSTM_HD), F32),
            jnp.full((n_even, bp, MLSTM_HEADS), M_INIT, F32),
            jnp.zeros((n_even, bp, RWKV_HEADS, RWKV_HD, RWKV_HD), F32),
            jnp.zeros((n_even, bp, RWKV_COLS), F32),
            jnp.zeros((n_odd, bp, GLA_HEADS, GLA_DK, GLA_DV), F32),
            jnp.zeros((n_odd, bp, S5_GROUPS, S5_P), F32),
            jnp.zeros((n_odd, bp, S5_GROUPS, S5_P), F32),
            jnp.zeros((depth, bp, CONV_W - 1, 2 * D_FF), F32))
    past = (state_mlstm_C, state_mlstm_n, state_mlstm_m, state_rwkv_S, state_rwkv_shift,
            state_gla_S, state_s5_re, state_s5_im, state_ffn_conv)
    y_prompt, p_state = _trunk(x_prompt, init, params, batch=bp, seq_len=tp)
    y_sample, s_state = _trunk(x_sample, past, params, batch=bs, seq_len=ts)
    return (y_prompt, y_sample) + tuple(p_state) + tuple(s_state)
```

```python
import functools
import math

import jax
import jax.numpy as jnp
from jax import lax
from jax.experimental import pallas as pl
from jax.experimental.pallas import tpu as pltpu

F32 = jnp.float32
BF16 = jnp.bfloat16

D_MODEL = 1024
HALF = D_MODEL // 2
MLSTM_HEADS = 4
MLSTM_HD = HALF // MLSTM_HEADS
RWKV_HD = 64
RWKV_HEADS = HALF // RWKV_HD
RWKV_W_LORA = 64
RWKV_A_LORA = 64
RWKV_G_LORA = 128
RWKV_COLS = 3 * HALF + RWKV_W_LORA + RWKV_A_LORA + RWKV_G_LORA
GLA_HEADS = 4
GLA_DK = 64
GLA_DV = 128
GLA_QK = GLA_HEADS * GLA_DK
GLA_GATE_RANK = 16
GLA_TAU = 16.0
S5_CH = HALF
S5_GROUP = 16
S5_GROUPS = S5_CH // S5_GROUP
S5_P = 64
S5_STATE = S5_GROUPS * S5_P
D_FF = 2816
CONV_W = 3
CHUNK = 64
EPS = 1e-6
RWKV_LN_EPS = 64e-5
M_INIT = -1e30

LANES = 128
SUBLANES = 8
VMEM_LIMIT_BYTES = 56 * 1024 * 1024

Z0_RWKV = 0
Z0_GATE = RWKV_COLS
Z0_Q = Z0_GATE + 2 * LANES
Z0_K, Z0_V, Z0_O = Z0_Q + HALF, Z0_Q + 2 * HALF, Z0_Q + 3 * HALF
Z0_COLS = Z0_Q + 4 * HALF
RW_LORA = 3 * HALF
RW_G = RW_LORA + RWKV_W_LORA + RWKV_A_LORA
Z1_Q, Z1_K, Z1_V, Z1_G, Z1_U = 0, GLA_QK, 2 * GLA_QK, 2 * GLA_QK + HALF, 2 * GLA_QK + 2 * HALF
Z1_AD = Z1_U + S5_CH
Z1_COLS = 2560


def _cparams(*sem):
    return pltpu.CompilerParams(dimension_semantics=sem, vmem_limit_bytes=VMEM_LIMIT_BYTES)


def _rms(x, g):
    return x * lax.rsqrt(jnp.mean(x * x, -1, keepdims=True) + EPS) * g


def _norm_matmul_kernel(x_ref, g_ref, w_ref, o_ref, *, tn):
    h = _rms(x_ref[...], g_ref[...]).astype(BF16)
    for j in range(w_ref.shape[1] // tn):
        cols = slice(j * tn, (j + 1) * tn)
        o_ref[:, cols] = jnp.dot(h, w_ref[:, cols], preferred_element_type=F32)


def norm_matmul(x, g, w, *, tm, tn, time_major_seq_len=None):
    m, d = x.shape
    n = w.shape[1]
    if time_major_seq_len is None:
        out_shape = (m, n)
        out_spec = pl.BlockSpec((tm, n), lambda i: (i, 0))
    else:
        tiles_per_seq = time_major_seq_len // tm
        out_shape = (time_major_seq_len, (m // time_major_seq_len) * n)
        out_spec = pl.BlockSpec((tm, n), lambda i: (i % tiles_per_seq, i // tiles_per_seq))
    return pl.pallas_call(
        functools.partial(_norm_matmul_kernel, tn=tn),
        grid=(m // tm,),
        in_specs=[pl.BlockSpec((tm, d), lambda i: (i, 0)),
                  pl.BlockSpec((1, d), lambda i: (0, 0)),
                  pl.BlockSpec((d, n), lambda i: (0, 0), pipeline_mode=pl.Buffered(1))],
        out_specs=out_spec,
        out_shape=jax.ShapeDtypeStruct(out_shape, F32),
        compiler_params=_cparams("parallel"),
        name="norm_matmul",
    )(x, g.reshape(1, d), w)


def _out_proj_kernel(x_ref, ya_ref, yb_ref, wa_ref, wb_ref, o_ref):
    acc = jnp.dot(ya_ref[...].astype(BF16), wa_ref[...], preferred_element_type=F32)
    acc += jnp.dot(yb_ref[...].astype(BF16), wb_ref[...], preferred_element_type=F32)
    o_ref[...] = x_ref[...] + acc


def out_proj(x, ya, yb, wa, wb, *, tm):
    m, d = x.shape
    k = ya.shape[1]
    if yb.ndim == 3:
        tiles_per_seq = yb.shape[0] // tm
        yb = yb.reshape(yb.shape[0], -1)
        yb_spec = pl.BlockSpec((tm, k), lambda i: (i % tiles_per_seq, i // tiles_per_seq))
    else:
        yb_spec = pl.BlockSpec((tm, k), lambda i: (i, 0))
    return pl.pallas_call(
        _out_proj_kernel,
        grid=(m // tm,),
        in_specs=[pl.BlockSpec((tm, d), lambda i: (i, 0)),
                  pl.BlockSpec((tm, k), lambda i: (i, 0)),
                  yb_spec,
                  pl.BlockSpec((k, d), lambda i: (0, 0)),
                  pl.BlockSpec((k, d), lambda i: (0, 0))],
        out_specs=pl.BlockSpec((tm, d), lambda i: (i, 0)),
        out_shape=jax.ShapeDtypeStruct((m, d), F32),
        compiler_params=_cparams("parallel"),
        name="out_proj",
    )(x, ya, yb, wa, wb)


FFN_SUB = 256


def _ffn_kernel(x_ref, g_ref, wup_ref, cw_ref, cb_ref, wdn_ref, st_ref, gf_ref,
                o_ref, tail_ref, h_scr, act_scr, car_scr, *, tm, seq_len, final_norm):
    long_seq = seq_len >= tm
    i = pl.program_id(0)
    h_scr[...] = _rms(x_ref[...], g_ref[...]).astype(BF16)
    row = lax.broadcasted_iota(jnp.int32, (tm, FFN_SUB), 0)
    if long_seq:
        @pl.when((i * tm) % seq_len == 0)
        def _():
            car_scr[...] = st_ref[...]
    else:
        t_in_seq = row % seq_len

    row8 = lax.broadcasted_iota(jnp.int32, (SUBLANES, FFN_SUB), 0)

    def conv_part(col0):
        u = jnp.dot(h_scr[...], wup_ref[:, col0:col0 + FFN_SUB], preferred_element_type=F32)
        p1 = pltpu.roll(u, 1, 0)
        p2 = pltpu.roll(u, 2, 0)
        if long_seq:
            halo = car_scr[:, col0:col0 + FFN_SUB]
            h6 = halo[SUBLANES - 2:SUBLANES - 1]
            h7 = halo[SUBLANES - 1:SUBLANES]
            p1 = jnp.concatenate([jnp.where(row8 == 0, h7, p1[:SUBLANES]), p1[SUBLANES:]], axis=0)
            p2 = jnp.concatenate([jnp.where(row8 == 0, h6, jnp.where(row8 == 1, h7, p2[:SUBLANES])),
                                  p2[SUBLANES:]], axis=0)
            car_scr[:, col0:col0 + FFN_SUB] = u[tm - SUBLANES:]
            tail_ref[:, col0:col0 + FFN_SUB] = u[tm - SUBLANES:]
        else:
            e = st_ref[:, col0:col0 + FFN_SUB]
            p1 = jnp.where(t_in_seq == 0, pltpu.roll(e, tm - 1, 0), p1)
            p2 = jnp.where(t_in_seq < 2, e, p2)
            tail_ref[:, col0:col0 + FFN_SUB] = u
        cw = cw_ref[:, col0:col0 + FFN_SUB]
        return (cb_ref[:, col0:col0 + FFN_SUB]
                + (cw[0:1] * p2 + cw[1:2] * p1 + cw[2:3] * u))

    for c in range(D_FF // FFN_SUB):
        val = conv_part(c * FFN_SUB)
        gate = conv_part(D_FF + c * FFN_SUB)
        act_scr[:, c * FFN_SUB:(c + 1) * FFN_SUB] = (val * (gate * jax.nn.sigmoid(gate))).astype(BF16)

    y = x_ref[...] + jnp.dot(act_scr[...], wdn_ref[...], preferred_element_type=F32)
    if final_norm:
        y = _rms(y, gf_ref[...])
    o_ref[...] = y


def conv_ffn(x, g, w_up, conv_w, conv_b, w_down, conv0, gf, *, batch, seq_len, tm, final_norm):
    m, d = x.shape
    f2 = 2 * D_FF
    long_seq = seq_len >= tm
    if long_seq:
        assert seq_len % tm == 0
        tiles_per_seq = seq_len // tm
        st = jnp.pad(conv0, ((0, 0), (SUBLANES - (CONV_W - 1), 0), (0, 0)))
        st_spec = pl.BlockSpec((None, SUBLANES, f2), lambda i: (i // tiles_per_seq, 0, 0))
        tail_shape = jax.ShapeDtypeStruct((batch, SUBLANES, f2), F32)
        tail_spec = pl.BlockSpec((None, SUBLANES, f2), lambda i: (i // tiles_per_seq, 0, 0))
        sem = "arbitrary"
    else:
        assert seq_len == SUBLANES and tm % seq_len == 0
        st = jnp.pad(conv0, ((0, 0), (0, seq_len - (CONV_W - 1)), (0, 0))).reshape(m, f2)
        st_spec = pl.BlockSpec((tm, f2), lambda i: (i, 0))
        tail_shape = jax.ShapeDtypeStruct((m, f2), F32)
        tail_spec = pl.BlockSpec((tm, f2), lambda i: (i, 0))
        sem = "parallel"
    const = lambda i: (0, 0)
    out, tail = pl.pallas_call(
        functools.partial(_ffn_kernel, tm=tm, seq_len=seq_len, final_norm=final_norm),
        grid=(m // tm,),
        in_specs=[pl.BlockSpec((tm, d), lambda i: (i, 0)),
                  pl.BlockSpec((1, d), const),
                  pl.BlockSpec((d, f2), const, pipeline_mode=pl.Buffered(1)),
                  pl.BlockSpec((CONV_W, f2), const),
                  pl.BlockSpec((1, f2), const),
                  pl.BlockSpec((D_FF, d), const, pipeline_mode=pl.Buffered(1)),
                  st_spec,
                  pl.BlockSpec((1, d), const)],
        out_specs=[pl.BlockSpec((tm, d), lambda i: (i, 0)), tail_spec],
        out_shape=[jax.ShapeDtypeStruct((m, d), F32), tail_shape],
        scratch_shapes=[pltpu.VMEM((tm, d), BF16),
                        pltpu.VMEM((tm, D_FF), BF16),
                        pltpu.VMEM((SUBLANES, f2), F32)],
        compiler_params=_cparams(sem),
        name="conv_ffn",
    )(x, g.reshape(1, d), w_up, conv_w, conv_b.reshape(1, f2), w_down, st, gf.reshape(1, d))
    new_state = tail.reshape(batch, -1, f2)[:, -(CONV_W - 1):]
    return out, new_state


def _log_sigmoid(x):
    return jnp.minimum(x, 0.0) - jnp.log1p(jnp.exp(-jnp.abs(x)))


def _dot_nt(a, b):
    return lax.dot_general(a, b, (((1,), (1,)), ((), ())), preferred_element_type=F32)


def _dot_tn(a, b):
    return lax.dot_general(a, b, (((0,), (0,)), ((), ())), preferred_element_type=F32)


def _mlstm_kernel(bif_ref, q_ref, k_ref, v_ref, og_ref, gz_ref, gt_ref, c0_ref, n0_ref, m0_ref,
                  lng_ref, y_ref, c_ref, n_ref, m_ref, grow_scr, gcol_scr, bcol_scr, cmax_scr,
                  *, chunk, n_chunks, bb):
    L = chunk

    @pl.when(pl.program_id(1) == 0)
    def _():
        c_ref[...] = c0_ref[...]
        n_ref[...] = n0_ref[...]
        m_ref[...] = m0_ref[...]

    t_idx = lax.broadcasted_iota(jnp.int32, (L, L), 0)
    s_idx = lax.broadcasted_iota(jnp.int32, (L, L), 1)
    causal = s_idx <= t_idx
    scale = MLSTM_HD ** -0.5

    def compute(items):
        ph = []
        for j, c in items:
            start = (j * n_chunks + c) * L
            rows = pl.ds(start if isinstance(start, int) else pl.multiple_of(start, L), L)
            for h in range(MLSTM_HEADS):
                ph.append(gate_phase(j, c, h, rows))
        for d in ph:
            d["qk_raw"] = _dot_nt(d["qb"], d["kb"])
            d["qc"] = _dot_nt(d["qb"], d["c_old"].astype(BF16))
            d["c_upd"] = _dot_tn((d["v"] * d["wl_col"]).astype(BF16), d["kb"])
        for d in ph:
            d["qk"] = d["qk_raw"] * d["w"]
            d["pv"] = jnp.dot(d["qk"].astype(BF16), d["v"].astype(BF16), preferred_element_type=F32)
        stores = []
        for d in ph:
            stores += finish_phase(d)
        return stores

    def entry(j, c, h):
        return (j * n_chunks + c) * MLSTM_HEADS + h

    lane = lax.broadcasted_iota(jnp.int32, (1, LANES), 1)
    bias_row = jnp.zeros((1, LANES), F32)
    for i in range(2 * MLSTM_HEADS):
        bias_row = jnp.where(lane == i, bif_ref[i], bias_row)
    gates_col = gz_ref[...] + bias_row
    gates_col = jnp.where(lane >= MLSTM_HEADS, _log_sigmoid(gates_col), gates_col)
    sub = lax.broadcasted_iota(jnp.int32, (2 * MLSTM_HEADS, 1), 0)
    bias_col = jnp.zeros((2 * MLSTM_HEADS, 1), F32)
    for i in range(2 * MLSTM_HEADS):
        bias_col = jnp.where(sub == i, bif_ref[i], bias_col)
    tril = causal.astype(F32)
    triu = (t_idx <= s_idx).astype(F32)
    exact = dict(preferred_element_type=F32, precision=lax.Precision.HIGHEST)
    for j in range(bb):
        for c in range(n_chunks):
            rows = slice((j * n_chunks + c) * L, (j * n_chunks + c + 1) * L)
            g_c = gates_col[rows]
            b_c = jnp.dot(tril, g_c, **exact)
            g_r = gt_ref[j, :, c, :] + bias_col
            g_r = jnp.where(sub >= MLSTM_HEADS, _log_sigmoid(g_r), g_r)
            b_r = jnp.dot(g_r, triu, **exact)
            for h in range(MLSTM_HEADS):
                g_row = g_r[h:h + 1] - b_r[MLSTM_HEADS + h:MLSTM_HEADS + h + 1]
                b_col = b_c[:, MLSTM_HEADS + h:MLSTM_HEADS + h + 1]
                e = entry(j, c, h)
                grow_scr[e, 0:1, 0:L] = g_row
                gcol_scr[e, :, 0:1] = g_c[:, h:h + 1] - b_col
                bcol_scr[e, :, 0:1] = b_col
                cmax_scr[e, :, 0:1] = jnp.max(jnp.where(causal, g_row, -jnp.inf), axis=1, keepdims=True)

    def gate_phase(j, c, h, rows):
        cols = slice(h * MLSTM_HD, (h + 1) * MLSTM_HD)
        e = entry(j, c, h)
        g_row = grow_scr[e, 0:1, 0:L]
        b_col = bcol_scr[e, :, 0:1]
        m_old = m_ref[j, h]
        mu_col = jnp.maximum(m_old, cmax_scr[e, :, 0:1])
        w = jnp.where(causal, jnp.exp(g_row - mu_col), 0.0)
        s_inter = jnp.exp(m_old - mu_col)
        q = q_ref[rows, cols]
        ks = k_ref[rows, cols] * scale
        mu_last = mu_col[L - 1:L]
        return dict(j=j, h=h, rows=rows, cols=cols, q=q, ks=ks, v=v_ref[rows, cols],
                    qb=q.astype(BF16), kb=ks.astype(BF16), c_old=c_ref[j, h], n_old=n_ref[j, h],
                    w=w, s_inter=s_inter, m_t=b_col + mu_col,
                    wl_col=jnp.exp(gcol_scr[e, :, 0:1] - mu_last),
                    dec=jnp.exp(m_old - mu_last), m_new=b_col[L - 1:L] + mu_last)

    def finish_phase(d):
        j, h, rows, cols = d["j"], d["h"], d["rows"], d["cols"]
        num = d["s_inter"] * d["qc"] + d["pv"]
        den = (d["s_inter"] * jnp.sum(d["q"] * d["n_old"], axis=1, keepdims=True)
               + jnp.sum(d["qk"], axis=1, keepdims=True))
        hh = num / jnp.maximum(jnp.abs(den), jnp.exp(-d["m_t"]))
        return [(c_ref, (j, h), d["dec"] * d["c_old"] + d["c_upd"]),
                (n_ref, (j, h), d["dec"] * d["n_old"] + jnp.sum(d["wl_col"] * d["ks"], axis=0, keepdims=True)),
                (m_ref, (j, h), d["m_new"]),
                (y_ref, (rows, cols), hh)]

    def commit(stores):
        for ref, idx, val in stores:
            ref[idx] = val

    def chunk_body(j, c, carry):
        commit(compute([(j, c)]))
        return carry

    if n_chunks == 1:
        commit(compute([(j, 0) for j in range(bb)]))
    else:
        for j in range(bb):
            lax.fori_loop(0, n_chunks, functools.partial(chunk_body, j), 0)

    for h in range(MLSTM_HEADS):
        cols = slice(h * MLSTM_HD, (h + 1) * MLSTM_HD)
        hh = y_ref[:, cols]
        xc = hh - jnp.mean(hh, -1, keepdims=True)
        hn = xc * lax.rsqrt(jnp.mean(xc * xc, -1, keepdims=True) + EPS) * lng_ref[:, cols]
        y_ref[:, cols] = jax.nn.sigmoid(og_ref[:, cols]) * hn


def mlstm_mixer(z, b_if, c0, n0, m0, ln_g, *, batch, seq_len, tb, bb):
    m_rows = batch * seq_len
    L = math.gcd(seq_len, CHUNK)
    n_tb = seq_len // tb
    n_chunks = tb // L
    assert bb == 1 or n_tb == 1
    rows = bb * tb
    n_entries = bb * n_chunks * MLSTM_HEADS
    gates = z[:, Z0_GATE:Z0_GATE + 2 * MLSTM_HEADS].reshape(batch, seq_len, 2 * MLSTM_HEADS)
    gates_t = jnp.swapaxes(gates, 1, 2).reshape(batch, 2 * MLSTM_HEADS, seq_len // L, L)
    zcol = lambda off: pl.BlockSpec((rows, HALF), lambda b, t: (b * n_tb + t, off // HALF))
    state4 = lambda a, b_: pl.BlockSpec((bb, MLSTM_HEADS, a, b_), lambda b, t: (b, 0, 0, 0))
    y, c, n, m = pl.pallas_call(
        functools.partial(_mlstm_kernel, chunk=L, n_chunks=n_chunks, bb=bb),
        grid=(batch // bb, n_tb),
        in_specs=[pl.BlockSpec(memory_space=pltpu.SMEM),
                  zcol(Z0_Q), zcol(Z0_K), zcol(Z0_V), zcol(Z0_O),
                  pl.BlockSpec((rows, LANES), lambda b, t: (b * n_tb + t, Z0_GATE // LANES)),
                  pl.BlockSpec((bb, 2 * MLSTM_HEADS, n_chunks, L), lambda b, t: (b, 0, t, 0)),
                  state4(MLSTM_HD, MLSTM_HD), state4(1, MLSTM_HD), state4(1, 1),
                  pl.BlockSpec((1, HALF), lambda b, t: (0, 0))],
        out_specs=[pl.BlockSpec((rows, HALF), lambda b, t: (b * n_tb + t, 0)),
                   state4(MLSTM_HD, MLSTM_HD), state4(1, MLSTM_HD), state4(1, 1)],
        out_shape=[jax.ShapeDtypeStruct((m_rows, HALF), F32),
                   jax.ShapeDtypeStruct((batch, MLSTM_HEADS, MLSTM_HD, MLSTM_HD), F32),
                   jax.ShapeDtypeStruct((batch, MLSTM_HEADS, 1, MLSTM_HD), F32),
                   jax.ShapeDtypeStruct((batch, MLSTM_HEADS, 1, 1), F32)],
        scratch_shapes=[pltpu.VMEM((n_entries, SUBLANES, LANES), F32)]
        + [pltpu.VMEM((n_entries, L, LANES), F32)] * 3,
        compiler_params=_cparams("parallel", "arbitrary"),
        name="mlstm",
    )(b_if, z, z, z, z, z, gates_t,
      c0, n0.reshape(batch, MLSTM_HEADS, 1, MLSTM_HD), m0.reshape(batch, MLSTM_HEADS, 1, 1),
      ln_g.reshape(1, HALF))
    return y, c, n.reshape(batch, MLSTM_HEADS, MLSTM_HD), m.reshape(batch, MLSTM_HEADS)


def _split2(x):
    hi = x.astype(BF16)
    lo = (x - hi.astype(F32)).astype(BF16)
    return hi, lo


def _seg_sum(x, ones_blk):
    hi, lo = _split2(x)
    return (jnp.dot(hi, ones_blk, preferred_element_type=F32)
            + jnp.dot(lo, ones_blk, preferred_element_type=F32))


def _softplus(x):
    return jnp.maximum(x, 0.0) + jnp.log1p(jnp.exp(-jnp.abs(x)))


def _block_ones(n, seg):
    idx = jnp.arange(n) // seg
    return (idx[:, None] == idx[None, :]).astype(BF16)


def _rwkv_prep_kernel(z_ref, st_ref, mu_ref, w0_ref, wup_ref, a0_ref, aup_ref, gup_ref, kk_ref, ka_ref,
                      rk_ref, ones_ref, r_out, w_out, k_out, v_out, kk_out, kka_out, bv_out, g_out,
                      car_scr, *, tm, seq_len):
    zr = z_ref[...]
    row = lax.broadcasted_iota(jnp.int32, zr.shape, 0)
    prev = pltpu.roll(zr, 1, 0)
    if seq_len >= tm:
        @pl.when((pl.program_id(0) * tm) % seq_len == 0)
        def _():
            car_scr[...] = st_ref[...]
        prev = jnp.where(row == 0, car_scr[SUBLANES - 1:SUBLANES], prev)
        car_scr[...] = zr[tm - SUBLANES:]
    else:
        prev = jnp.where(row % seq_len == 0, st_ref[...], prev)
    zs = zr + (prev - zr) * mu_ref[...]
    r = zs[:, :HALF]
    kr = zs[:, HALF:2 * HALF]
    vr = zs[:, 2 * HALF:3 * HALF]
    lora = zs[:, RW_LORA:RW_G]
    ones = ones_ref[...]
    w_log = -_softplus(-(w0_ref[...] + jnp.dot(jnp.tanh(lora).astype(BF16), wup_ref[...],
                                               preferred_element_type=F32))) - 0.5
    a = jax.nn.sigmoid(a0_ref[...] + jnp.dot(lora.astype(BF16), aup_ref[...], preferred_element_type=F32))
    g = jnp.dot(jax.nn.sigmoid(zs[:, RW_G:]).astype(BF16), gup_ref[...], preferred_element_type=F32)
    kk = kr * kk_ref[...]
    kk = kk * lax.rsqrt(jnp.maximum(_seg_sum(kk * kk, ones), 1e-24))
    k2 = kr * (1.0 + (a - 1.0) * ka_ref[...])
    r_out[...] = r
    w_out[...] = jnp.exp(-jnp.exp(w_log))
    k_out[...] = k2
    v_out[...] = vr
    kk_out[...] = kk
    kka_out[...] = kk * a
    bv_out[...] = _seg_sum(r * k2 * rk_ref[...], ones) * vr
    g_out[...] = g


RWKV_GROUPS = 2


def _rwkv_rec_kernel(r_ref, w_ref, k_ref, v_ref, kk_ref, kka_ref, bv_ref, g_ref, s0_ref, lng_ref, lnb_ref,
                     onesy_ref, ones_ref, y_ref, s_ref, yraw_scr, lhs_scr, ylhs_scr, tmaj_scr=None,
                     *, bb, tb):
    @pl.when(pl.program_id(1) == 0)
    def _():
        s_ref[...] = s0_ref[...]

    if tmaj_scr is not None:
        streams = (r_ref, w_ref, k_ref, v_ref, kk_ref, kka_ref, bv_ref, g_ref)
        for i, ref in enumerate(streams):
            for t in range(tb):
                tmaj_scr[i, t] = ref[:, t, :]
        r_ref, w_ref, k_ref, v_ref, kk_ref, kka_ref, bv_ref, g_ref = (
            tmaj_scr.at[i] for i in range(len(streams)))

    pairs = RWKV_HEADS // 2
    n_tiles = bb * pairs
    shape = (RWKV_HD, LANES)
    eye2 = (lax.broadcasted_iota(jnp.int32, shape, 0)
            == lax.broadcasted_iota(jnp.int32, shape, 1) % RWKV_HD)
    pair_diag = (lax.broadcasted_iota(jnp.int32, shape, 0) // 2
                 == (lax.broadcasted_iota(jnp.int32, shape, 1) % RWKV_HD) // 2)
    lane_even = lax.broadcasted_iota(jnp.int32, (1, LANES), 1) % 2 == 0
    sublane_even = lax.broadcasted_iota(jnp.int32, (SUBLANES, 1), 0) % 2 == 0
    y_rows = lambda idx: slice((idx // 2) * RWKV_HD, (idx // 2 + 1) * RWKV_HD)
    y_cols = lambda idx: slice((idx % 2) * LANES, (idx % 2 + 1) * LANES)

    per_group = n_tiles // RWKV_GROUPS

    def step(t, carry):
        def tile(idx):
            b, p = divmod(idx, pairs)
            cols = slice(p * LANES, (p + 1) * LANES)
            return b, p, (lambda ref: ref[t, b:b + 1, cols])

        groups = [range(g * per_group, (g + 1) * per_group) for g in range(RWKV_GROUPS)]
        g_rows = per_group * RWKV_HD * 3 // 2
        sums = []
        for g, grp in enumerate(groups):
            for idx in grp:
                b, p, row = tile(idx)
                rel = idx - grp[0]
                base = g * g_rows + rel * RWKV_HD
                hi, lo = _split2(s_ref[b, p] * row(kk_ref))
                lhs_scr[base:base + RWKV_HD, :LANES] = hi
                lhs_scr[base:base + RWKV_HD, LANES:] = lo
                v_row = row(v_ref)
                v_hi = v_row.astype(BF16).astype(F32)
                v_lo = v_row - v_hi
                v_lo_swapped = jnp.where(lane_even, pltpu.roll(v_lo, LANES - 1, 1), pltpu.roll(v_lo, 1, 1))
                pattern = jnp.where(sublane_even, jnp.where(lane_even, v_hi, v_lo_swapped),
                                    jnp.where(lane_even, v_lo_swapped, v_hi))
                v_base = g * g_rows + per_group * RWKV_HD + (rel // 2) * RWKV_HD
                lhs_scr[v_base:v_base + RWKV_HD, y_cols(rel)] = jnp.where(
                    pair_diag, jnp.concatenate([pattern] * (RWKV_HD // SUBLANES), axis=0), 0.0).astype(BF16)
            sums.append(jnp.dot(lhs_scr[g * g_rows:(g + 1) * g_rows, :], onesy_ref[...],
                                preferred_element_type=F32))
        ysums = []
        for grp, sm in zip(groups, sums):
            for idx in grp:
                b, p, row = tile(idx)
                rel = idx - grp[0]
                sa_rows = slice(rel * RWKV_HD, (rel + 1) * RWKV_HD)
                sa = sm[sa_rows, :LANES] + sm[sa_rows, LANES:]
                v_base = per_group * RWKV_HD + (rel // 2) * RWKV_HD
                v_tile = sm[v_base:v_base + RWKV_HD, y_cols(rel)]
                s = s_ref[b, p] * row(w_ref) - sa * row(kka_ref) + v_tile * row(k_ref)
                s_ref[b, p] = s
                ylhs_scr[y_rows(idx), y_cols(idx)] = (s * row(r_ref)).astype(BF16)
            g_rows = slice(y_rows(grp[0]).start, y_rows(grp[-1]).stop)
            ysums.append(jnp.dot(ylhs_scr[g_rows, :], onesy_ref[...], preferred_element_type=F32))
        for grp, ys in zip(groups, ysums):
            for idx in grp:
                b, p, row = tile(idx)
                rel = slice(y_rows(idx).start - y_rows(grp[0]).start, y_rows(idx).stop - y_rows(grp[0]).start)
                yraw_scr[t, b:b + 1, p * LANES:(p + 1) * LANES] = jnp.sum(
                    jnp.where(eye2, ys[rel, y_cols(idx)], 0.0), axis=0, keepdims=True)
        return carry

    lax.fori_loop(0, tb, step, 0, unroll=2)

    ones = ones_ref[...]
    y = yraw_scr[...].reshape(tb * bb, HALF)
    xc = y - _seg_sum(y, ones) * (1.0 / RWKV_HD)
    var = _seg_sum(xc * xc, ones) * (1.0 / RWKV_HD)
    yn = xc * lax.rsqrt(var + RWKV_LN_EPS) * lng_ref[...] + lnb_ref[...]
    out = (yn + bv_ref[...].reshape(tb * bb, HALF)) * g_ref[...].reshape(tb * bb, HALF)
    if tmaj_scr is None:
        y_ref[...] = out.reshape(tb, bb, HALF)
    else:
        for t in range(tb):
            y_ref[:, t, :] = out[t * bb:(t + 1) * bb]


def rwkv_mixer(z, s0, shift0, mu, w0, w_up, a0, a_up, g_up, k_k, k_a, r_k, ln_g, ln_b,
               *, batch, seq_len, tm, bb, tb):
    m_rows = batch * seq_len
    tm_shape = (seq_len, batch, HALF)
    if seq_len >= tm:
        tiles_per_seq = seq_len // tm
        st = jnp.pad(shift0[:, None], ((0, 0), (SUBLANES - 1, 0), (0, 0)))
        st_spec = pl.BlockSpec((None, SUBLANES, RWKV_COLS), lambda i: (i // tiles_per_seq, 0, 0))
        sem = "arbitrary"
        tok = pl.BlockSpec((tm, HALF), lambda i: (i % tiles_per_seq, i // tiles_per_seq))
        tok_shape = (seq_len, batch * HALF)
    else:
        st = jnp.pad(shift0[:, None], ((0, 0), (0, seq_len - 1), (0, 0))).reshape(m_rows, RWKV_COLS)
        st_spec = pl.BlockSpec((tm, RWKV_COLS), lambda i: (i, 0))
        sem = "parallel"
        tok = pl.BlockSpec((tm, HALF), lambda i: (i, 0))
        tok_shape = (m_rows, HALF)
    row = lambda a: a.reshape(1, -1)
    const = lambda i: (0, 0)
    vec = pl.BlockSpec((1, HALF), const)
    lora_rows = RWKV_W_LORA + RWKV_A_LORA
    w_up_p = jnp.pad(w_up, ((0, RWKV_A_LORA), (0, 0))).astype(BF16)
    a_up_p = jnp.pad(a_up, ((RWKV_W_LORA, 0), (0, 0))).astype(BF16)
    ones_half = _block_ones(HALF, RWKV_HD)
    prep = pl.pallas_call(
        functools.partial(_rwkv_prep_kernel, tm=tm, seq_len=seq_len),
        grid=(m_rows // tm,),
        in_specs=[pl.BlockSpec((tm, RWKV_COLS), lambda i: (i, Z0_RWKV // RWKV_COLS)),
                  st_spec,
                  pl.BlockSpec((1, RWKV_COLS), const),
                  vec, pl.BlockSpec((lora_rows, HALF), const),
                  vec, pl.BlockSpec((lora_rows, HALF), const),
                  pl.BlockSpec((RWKV_G_LORA, HALF), const),
                  vec, vec, vec,
                  pl.BlockSpec((HALF, HALF), const)],
        out_specs=[tok] * 8,
        out_shape=[jax.ShapeDtypeStruct(tok_shape, F32)] * 8,
        scratch_shapes=[pltpu.VMEM((SUBLANES, RWKV_COLS), F32)],
        compiler_params=_cparams(sem),
        name="rwkv_prep",
    )(z, st, row(mu), row(w0), w_up_p, row(a0), a_up_p, g_up.astype(BF16), row(k_k), row(k_a), row(r_k),
      ones_half)
    pairs = RWKV_HEADS // 2
    s_pairs = (s0.reshape(batch, pairs, 2, RWKV_HD, RWKV_HD).transpose(0, 1, 3, 2, 4)
               .reshape(batch, pairs, RWKV_HD, LANES))
    scratch = [pltpu.VMEM((tb, bb, HALF), F32),
               pltpu.VMEM((bb * pairs * RWKV_HD * 3 // 2, 2 * LANES), BF16),
               pltpu.VMEM((bb * pairs * RWKV_HD // 2, 2 * LANES), BF16)]
    if seq_len >= tm:
        seqs = [a.reshape(tm_shape) for a in prep]
        blk = pl.BlockSpec((tb, bb, HALF), lambda b, t: (t, b, 0))
        y_shape = tm_shape
    else:
        assert tb == seq_len
        seqs = [a.reshape(batch, seq_len, HALF) for a in prep]
        blk = pl.BlockSpec((bb, tb, HALF), lambda b, t: (b, 0, 0))
        y_shape = (batch, seq_len, HALF)
        scratch.append(pltpu.VMEM((len(prep), tb, bb, HALF), F32))
    s_spec = pl.BlockSpec((bb, pairs, RWKV_HD, LANES), lambda b, t: (b, 0, 0, 0))
    const2 = lambda b, t: (0, 0)
    y, s_new = pl.pallas_call(
        functools.partial(_rwkv_rec_kernel, bb=bb, tb=tb),
        grid=(batch // bb, seq_len // tb),
        in_specs=[blk] * 8 + [s_spec,
                              pl.BlockSpec((1, HALF), const2), pl.BlockSpec((1, HALF), const2),
                              pl.BlockSpec((2 * LANES, 2 * LANES), const2),
                              pl.BlockSpec((HALF, HALF), const2)],
        out_specs=[blk, s_spec],
        out_shape=[jax.ShapeDtypeStruct(y_shape, F32),
                   jax.ShapeDtypeStruct((batch, pairs, RWKV_HD, LANES), F32)],
        scratch_shapes=scratch,
        compiler_params=_cparams("parallel", "arbitrary"),
        name="rwkv_rec",
    )(*seqs, s_pairs, row(ln_g), row(ln_b),
      _block_ones(2 * LANES, RWKV_HD), ones_half)
    s_new = (s_new.reshape(batch, pairs, RWKV_HD, 2, RWKV_HD).transpose(0, 1, 3, 2, 4)
             .reshape(batch, RWKV_HEADS, RWKV_HD, RWKV_HD))
    if seq_len < tm:
        y = y.reshape(m_rows, HALF)
    return y, s_new


GLA_SUB = 16


def _gla_kernel(q_ref, k_ref, v_ref, g_ref, ad_ref, aup_ref, ab_ref, lng_ref, s0_ref,
                y_ref, s_ref, bc_scr, *, chunk, n_chunks, bb):
    L = chunk
    sub = min(GLA_SUB, L)

    @pl.when(pl.program_id(1) == 0)
    def _():
        s_ref[...] = s0_ref[...]

    lane = lax.broadcasted_iota(jnp.int32, (1, LANES), 1)
    head_mask = [(lane < GLA_DK).astype(F32), (lane >= GLA_DK).astype(F32)]
    tril = (lax.broadcasted_iota(jnp.int32, (L, L), 1)
            <= lax.broadcasted_iota(jnp.int32, (L, L), 0)).astype(F32)
    eye = (lax.broadcasted_iota(jnp.int32, (LANES, LANES), 0)
           == lax.broadcasted_iota(jnp.int32, (LANES, LANES), 1))

    pre = jnp.dot(ad_ref[...].astype(BF16), aup_ref[...], preferred_element_type=F32) + ab_ref[...]
    la = _log_sigmoid(pre) * (1.0 / GLA_TAU)
    for blk in range(bb * n_chunks):
        rows = slice(blk * L, (blk + 1) * L)
        bc_scr[rows, :] = jnp.dot(tril, la[rows], preferred_element_type=F32,
                                  precision=lax.Precision.HIGHEST)

    stack = lambda x: jnp.concatenate([x * head_mask[0], x * head_mask[1]], axis=0).astype(BF16)

    def compute(items):
        units = []
        for j, c in items:
            start = (j * n_chunks + c) * L
            rows = pl.ds(start if isinstance(start, int) else pl.multiple_of(start, L), L)
            for p in range(GLA_HEADS // 2):
                kcols = slice(p * LANES, (p + 1) * LANES)
                units.append(dict(
                    j=j, p=p, rows=rows, qp=q_ref[rows, kcols] * (GLA_DK ** -0.5), kp=k_ref[rows, kcols],
                    bc=bc_scr[rows, kcols], s_old=s_ref[j, p],
                    v_pair=v_ref[rows, 2 * p * GLA_DV:2 * (p + 1) * GLA_DV].astype(BF16)))
        for d in units:
            qp, kp, bc, v_pair = d["qp"], d["kp"], d["bc"], d["v_pair"]
            d["inter"] = jnp.dot(stack(qp * jnp.exp(bc)), d["s_old"].astype(BF16), preferred_element_type=F32)
            d["att"] = []
            for i in range(L // sub):
                lo, hi = i * sub, (i + 1) * sub
                c_i = bc[lo - 1:lo] if i > 0 else jnp.zeros((1, LANES), F32)
                qe = qp[lo:hi] * jnp.exp(bc[lo:hi] - c_i)
                ke = kp[:hi] * jnp.exp(c_i - bc[:hi])
                d["att"].append(_dot_nt(stack(qe), ke.astype(BF16)))
            bl = bc[L - 1:L]
            v_stack = jnp.concatenate([v_pair[:, :GLA_DV], v_pair[:, GLA_DV:]], axis=0)
            d["s_upd"] = _dot_tn(stack(kp * jnp.exp(bl - bc)), v_stack)
            d["dec_col"] = jnp.sum(jnp.where(eye, jnp.exp(bl), 0.0), axis=1, keepdims=True)
        for d in units:
            d["intra"] = []
            for i, att in enumerate(d["att"]):
                lo, hi = i * sub, (i + 1) * sub
                t_idx = lo + lax.broadcasted_iota(jnp.int32, (2 * sub, hi), 0) % sub
                s_idx = lax.broadcasted_iota(jnp.int32, (2 * sub, hi), 1)
                att = jnp.where(s_idx <= t_idx, att, 0.0)
                d["intra"].append(jnp.dot(att.astype(BF16), d["v_pair"][:hi], preferred_element_type=F32))
        stores = []
        for d in units:
            for e in range(2):
                h = 2 * d["p"] + e
                vcols = slice(h * GLA_DV, (h + 1) * GLA_DV)
                o = d["inter"][e * L:(e + 1) * L] + jnp.concatenate(
                    [blk[e * sub:(e + 1) * sub, e * GLA_DV:(e + 1) * GLA_DV] for blk in d["intra"]], axis=0)
                stores.append((y_ref, (d["rows"], vcols), o))
            stores.append((s_ref, (d["j"], d["p"]), d["dec_col"] * d["s_old"] + d["s_upd"]))
        return stores

    def commit(stores):
        for ref, idx, val in stores:
            ref[idx] = val

    def chunk_body(j, c, carry):
        commit(compute([(j, c)]))
        return carry

    if n_chunks == 1:
        commit(compute([(j, 0) for j in range(bb)]))
    else:
        for j in range(bb):
            lax.fori_loop(0, n_chunks, functools.partial(chunk_body, j), 0)

    for h in range(GLA_HEADS):
        vcols = slice(h * GLA_DV, (h + 1) * GLA_DV)
        o = y_ref[:, vcols]
        on = o * lax.rsqrt(jnp.mean(o * o, -1, keepdims=True) + EPS) * lng_ref[:, vcols]
        gate = g_ref[:, vcols]
        y_ref[:, vcols] = on * (gate * jax.nn.sigmoid(gate))


def gla_mixer(z, s0, a_up, a_b, ln_g, *, batch, seq_len, tb, bb, time_major):
    m_rows = batch * seq_len
    L = math.gcd(seq_len, CHUNK)
    n_tb = seq_len // tb
    pairs = GLA_HEADS // 2
    assert bb == 1 or (n_tb == 1 and not time_major)
    rows = bb * tb
    if time_major:
        zblk = lambda width, off: pl.BlockSpec((tb, width), lambda b, t: (t, (b * Z1_COLS + off) // width))
    else:
        zblk = lambda width, off: pl.BlockSpec((rows, width), lambda b, t: (b * n_tb + t, off // width))
    s_spec = pl.BlockSpec((bb, pairs, 2 * GLA_DK, GLA_DV), lambda b, t: (b, 0, 0, 0))
    const = lambda b, t: (0, 0)
    a_up_p = jnp.pad(a_up, ((0, LANES - GLA_GATE_RANK), (0, 0))).astype(BF16)
    y, s_new = pl.pallas_call(
        functools.partial(_gla_kernel, chunk=L, n_chunks=tb // L, bb=bb),
        grid=(batch // bb, n_tb),
        in_specs=[zblk(GLA_QK, Z1_Q), zblk(GLA_QK, Z1_K), zblk(HALF, Z1_V), zblk(HALF, Z1_G),
                  zblk(LANES, Z1_AD),
                  pl.BlockSpec((LANES, GLA_QK), const), pl.BlockSpec((1, GLA_QK), const),
                  pl.BlockSpec((1, HALF), const), s_spec],
        out_specs=[pl.BlockSpec((rows, HALF), lambda b, t: (b * n_tb + t, 0)), s_spec],
        out_shape=[jax.ShapeDtypeStruct((m_rows, HALF), F32),
                   jax.ShapeDtypeStruct((batch, pairs, 2 * GLA_DK, GLA_DV), F32)],
        scratch_shapes=[pltpu.VMEM((rows, GLA_QK), F32)],
        compiler_params=_cparams("parallel", "arbitrary"),
        name="gla",
    )(z, z, z, z, z, a_up_p, a_b.reshape(1, GLA_QK), ln_g.reshape(1, HALF),
      s0.reshape(batch, pairs, 2 * GLA_DK, GLA_DV))
    return y, s_new.reshape(batch, GLA_HEADS, GLA_DK, GLA_DV)


S5_KT = LANES // S5_GROUP
S5_SCAN_SPLIT = 2


def _s5_param_kernel(lre_ref, lim_ref, step_ref, bre_ref, bim_ref, bbre_ref, bbim_ref, pre_ref, pim_ref):
    lre, lim = lre_ref[...], lim_ref[...]
    step = jnp.exp(step_ref[...])
    mag = jnp.exp(lre * step)
    bar_re = mag * jnp.cos(lim * step)
    bar_im = mag * jnp.sin(lim * step)
    inv = 1.0 / (lre * lre + lim * lim)
    cre = ((bar_re - 1.0) * lre + bar_im * lim) * inv
    cim = (bar_im * lre - (bar_re - 1.0) * lim) * inv
    for g in range(S5_GROUPS):
        cr, ci = cre[g:g + 1], cim[g:g + 1]
        bbre_ref[g] = cr * bre_ref[g] - ci * bim_ref[g]
        bbim_ref[g] = cr * bim_ref[g] + ci * bre_ref[g]
    pre_ref[0] = bar_re
    pim_ref[0] = bar_im


def _s5_kernel(u_ref, wbre_ref, wbim_ref, lre_ref, lim_ref, x0re_ref, x0im_ref, wcre_ref, wcim_ref,
               d_ref, wglu_ref, y_ref, xre_ref, xim_ref, sre_scr, sim_scr, *, tt, bb, row_major):
    @pl.when(pl.program_id(1) == 0)
    def _():
        xre_ref[...] = x0re_ref[...]
        xim_ref[...] = x0im_ref[...]

    rows_all = tt * bb
    if row_major:
        u = jnp.concatenate([u_ref[:, t, :] for t in range(tt)], axis=0)
    else:
        u = u_ref[...].reshape(rows_all, S5_CH)
    ub = u.astype(BF16)
    nblk = S5_CH // LANES
    wide = S5_STATE // nblk
    for kt in range(nblk):
        cols = slice(kt * wide, (kt + 1) * wide)
        ukt = ub[:, kt * LANES:(kt + 1) * LANES]
        sre_scr[:, cols] = jnp.dot(ukt, wbre_ref[kt], preferred_element_type=F32)
        sim_scr[:, cols] = jnp.dot(ukt, wbim_ref[kt], preferred_element_type=F32)

    for part in range(S5_SCAN_SPLIT):
        width = S5_STATE // S5_SCAN_SPLIT
        cols = slice(part * width, (part + 1) * width)
        lr, li = lre_ref[:, cols], lim_ref[:, cols]

        def step(t, carry, cols=cols, lr=lr, li=li):
            cr, ci = carry
            rows = pl.ds(pl.multiple_of(t * bb, bb), bb)
            nr = sre_scr[rows, cols] + (lr * cr - li * ci)
            ni = sim_scr[rows, cols] + (lr * ci + li * cr)
            sre_scr[rows, cols] = nr
            sim_scr[rows, cols] = ni
            return nr, ni

        cr, ci = lax.fori_loop(0, tt, step, (xre_ref[:, cols], xim_ref[:, cols]))
        xre_ref[:, cols] = cr
        xim_ref[:, cols] = ci

    parts = []
    for nt in range(nblk):
        cols = slice(nt * wide, (nt + 1) * wide)
        parts.append(jnp.dot(sre_scr[:, cols].astype(BF16), wcre_ref[nt], preferred_element_type=F32)
                     - jnp.dot(sim_scr[:, cols].astype(BF16), wcim_ref[nt], preferred_element_type=F32))
    y = jnp.concatenate(parts, axis=1) + d_ref[...] * u
    ys = y * (0.5 * (1.0 + jnp.tanh(math.sqrt(2.0 / math.pi) * (y + 0.044715 * (y * y * y)))))
    out = ys * jax.nn.sigmoid(jnp.dot(ys.astype(BF16), wglu_ref[...], preferred_element_type=F32))
    if row_major:
        for t in range(tt):
            y_ref[:, t, :] = out[t * bb:(t + 1) * bb]
    else:
        y_ref[...] = out.reshape(tt, bb, S5_CH)


def s5_mixer(z, x0_re, x0_im, lam_re, lam_im, log_step, b_re, b_im, c_re, c_im, d_skip,
             w_glu, *, batch, seq_len, tt, bb, time_major):
    u_col_block = Z1_U // S5_CH
    if time_major:
        z = z.reshape(seq_len, batch, Z1_COLS)
        u_spec = pl.BlockSpec((tt, bb, S5_CH), lambda b, t: (t, b, u_col_block))
        y_spec = pl.BlockSpec((tt, bb, S5_CH), lambda b, t: (t, b, 0))
        y_shape = (seq_len, batch, S5_CH)
    else:
        assert tt == seq_len
        z = z.reshape(batch, seq_len, Z1_COLS)
        u_spec = pl.BlockSpec((bb, tt, S5_CH), lambda b, t: (b, 0, u_col_block))
        y_spec = pl.BlockSpec((bb, tt, S5_CH), lambda b, t: (b, 0, 0))
        y_shape = (batch, seq_len, S5_CH)
    gpc = (S5_GROUPS, S5_GROUP, S5_P)
    bb_re, bb_im, p_re, p_im = pl.pallas_call(
        _s5_param_kernel,
        out_shape=[jax.ShapeDtypeStruct(gpc, F32), jax.ShapeDtypeStruct(gpc, F32),
                   jax.ShapeDtypeStruct((1, S5_GROUPS, S5_P), F32),
                   jax.ShapeDtypeStruct((1, S5_GROUPS, S5_P), F32)],
        name="s5_params",
    )(lam_re, lam_im, log_step.reshape(S5_GROUPS, 1), jnp.swapaxes(b_re, 1, 2), jnp.swapaxes(b_im, 1, 2))

    nblk = S5_CH // LANES
    eye = jnp.eye(S5_KT, dtype=F32)

    def in_blocks(w):
        w = w.reshape(nblk, S5_KT, S5_GROUP, 1, S5_P) * eye[None, :, None, :, None]
        return w.reshape(nblk, LANES, S5_KT * S5_P).astype(BF16)

    def out_blocks(c):
        w = jnp.swapaxes(c, 1, 2).reshape(nblk, S5_KT, S5_P, 1, S5_GROUP) * eye[None, :, None, :, None]
        return w.reshape(nblk, S5_KT * S5_P, LANES).astype(BF16)

    lam_rows = lambda p: jnp.broadcast_to(p.reshape(1, S5_STATE), (bb, S5_STATE))
    const2 = lambda b, t: (0, 0)
    const3 = lambda b, t: (0, 0, 0)
    st_spec = pl.BlockSpec((bb, S5_STATE), lambda b, t: (b, 0))
    wb_spec = pl.BlockSpec((nblk, LANES, S5_KT * S5_P), const3)
    wc_spec = pl.BlockSpec((nblk, S5_KT * S5_P, LANES), const3)
    lam_spec = pl.BlockSpec((bb, S5_STATE), const2)
    y, x_re, x_im = pl.pallas_call(
        functools.partial(_s5_kernel, tt=tt, bb=bb, row_major=not time_major),
        grid=(batch // bb, seq_len // tt),
        in_specs=[u_spec,
                  wb_spec, wb_spec, lam_spec, lam_spec, st_spec, st_spec, wc_spec, wc_spec,
                  pl.BlockSpec((1, S5_CH), const2), pl.BlockSpec((S5_CH, S5_CH), const2)],
        out_specs=[y_spec, st_spec, st_spec],
        out_shape=[jax.ShapeDtypeStruct(y_shape, F32),
                   jax.ShapeDtypeStruct((batch, S5_STATE), F32),
                   jax.ShapeDtypeStruct((batch, S5_STATE), F32)],
        scratch_shapes=[pltpu.VMEM((tt * bb, S5_STATE), F32), pltpu.VMEM((tt * bb, S5_STATE), F32)],
        compiler_params=_cparams("parallel", "arbitrary"),
        name="s5",
    )(z, in_blocks(bb_re), in_blocks(bb_im), lam_rows(p_re), lam_rows(p_im),
      x0_re.reshape(batch, S5_STATE), x0_im.reshape(batch, S5_STATE),
      out_blocks(c_re), out_blocks(c_im), d_skip.reshape(1, S5_CH), w_glu.astype(BF16))
    shape = (batch, S5_GROUPS, S5_P)
    if not time_major:
        y = y.reshape(batch * seq_len, S5_CH)
    return y, x_re.reshape(shape), x_im.reshape(shape)


def _pack_even_w_in(w_in):
    d = w_in.shape[0]
    m_cols = 4 * HALF + 2 * MLSTM_HEADS
    parts = [w_in[:, m_cols:],
             w_in[:, 4 * HALF:m_cols], jnp.zeros((d, Z0_Q - Z0_GATE - 2 * MLSTM_HEADS), w_in.dtype),
             w_in[:, :4 * HALF]]
    return jnp.concatenate(parts, axis=1).astype(BF16)


def _pack_odd_w_in(w_in):
    gla_main = 2 * GLA_QK + 2 * HALF
    gla_cols = gla_main + GLA_GATE_RANK
    parts = [w_in[:, :gla_main], w_in[:, gla_cols:], w_in[:, gla_main:gla_cols]]
    w = jnp.concatenate(parts, axis=1)
    return jnp.pad(w, ((0, 0), (0, Z1_COLS - w.shape[1]))).astype(BF16)


def _trunk(x, st, prm, *, batch, seq_len):
    (st_C, st_n, st_m, st_S, st_shift, st_gla, st_re, st_im, st_conv) = st
    (norm_mix, norm_ffn, norm_final,
     e_w_in, e_b_if, e_mu, e_w0, e_w_up, e_a0, e_a_up, e_g_up, e_k_k, e_k_a, e_r_k, e_ln_m_g, e_ln_r_g,
     e_ln_r_b, e_w_out,
     o_w_in, o_a_up, o_a_b, o_ln_g, o_lam_re, o_lam_im, o_log_step, o_b_re, o_b_im, o_c_re, o_c_im, o_d,
     o_w_glu, o_w_out,
     f_w_up, f_conv_w, f_conv_b, f_w_down) = prm
    assert len(e_w_in) == 1 and len(o_w_in) == 1 and len(f_w_up) == 2, "wired for depth 2"
    long_seq = seq_len >= 512
    assert long_seq or seq_len == SUBLANES
    m_rows = batch * seq_len
    tiles = dict(
        tm_proj=512,
        tm_tok=512 if long_seq else 256,
        tm_ffn=512 if long_seq else 128,
        tb=512 if long_seq else seq_len,
        bb_chunked=1 if long_seq else 4,
        tb_rwkv=128 if long_seq else seq_len,
        tt_s5=128 if long_seq else seq_len,
    )
    x = x.reshape(m_rows, D_MODEL)
    ffn = functools.partial(conv_ffn, batch=batch, seq_len=seq_len, tm=tiles["tm_ffn"])

    z0 = norm_matmul(x, norm_mix[0], _pack_even_w_in(e_w_in[0]), tm=tiles["tm_proj"], tn=512)
    y_m, c_new, n_new, m_new = mlstm_mixer(z0, e_b_if[0], st_C[0], st_n[0], st_m[0], e_ln_m_g[0],
                                           batch=batch, seq_len=seq_len, tb=tiles["tb"], bb=tiles["bb_chunked"])
    y_r, s_new = rwkv_mixer(z0, st_S[0], st_shift[0], e_mu[0], e_w0[0], e_w_up[0], e_a0[0], e_a_up[0],
                            e_g_up[0], e_k_k[0], e_k_a[0], e_r_k[0], e_ln_r_g[0], e_ln_r_b[0],
                            batch=batch, seq_len=seq_len, tm=tiles["tm_tok"], bb=SUBLANES,
                            tb=tiles["tb_rwkv"])
    shift_new = z0.reshape(batch, seq_len, Z0_COLS)[:, -1, Z0_RWKV:Z0_RWKV + RWKV_COLS]
    w_out = e_w_out[0].astype(BF16)
    x = out_proj(x, y_m, y_r, w_out[:HALF], w_out[HALF:], tm=tiles["tm_tok"])
    x, conv_new0 = ffn(x, norm_ffn[0], f_w_up[0].astype(BF16), f_conv_w[0], f_conv_b[0],
                       f_w_down[0].astype(BF16), st_conv[0], norm_final, final_norm=False)

    z1 = norm_matmul(x, norm_mix[1], _pack_odd_w_in(o_w_in[0]), tm=tiles["tm_proj"], tn=512,
                     time_major_seq_len=seq_len if long_seq else None)
    y_g, gla_new = gla_mixer(z1, st_gla[0], o_a_up[0], o_a_b[0], o_ln_g[0],
                             batch=batch, seq_len=seq_len, tb=tiles["tb"], bb=tiles["bb_chunked"],
                             time_major=long_seq)
    y_s, re_new, im_new = s5_mixer(z1, st_re[0], st_im[0], o_lam_re[0], o_lam_im[0],
                                   o_log_step[0], o_b_re[0], o_b_im[0], o_c_re[0], o_c_im[0], o_d[0],
                                   o_w_glu[0], batch=batch, seq_len=seq_len, tt=tiles["tt_s5"],
                                   bb=SUBLANES, time_major=long_seq)
    w_out = o_w_out[0].astype(BF16)
    x = out_proj(x, y_g, y_s, w_out[:HALF], w_out[HALF:], tm=tiles["tm_tok"])
    x, conv_new1 = ffn(x, norm_ffn[1], f_w_up[1].astype(BF16), f_conv_w[1], f_conv_b[1],
                       f_w_down[1].astype(BF16), st_conv[1], norm_final, final_norm=True)
    new_state = (c_new[None], n_new[None], m_new[None], s_new[None], shift_new[None],
                 gla_new[None], re_new[None], im_new[None], jnp.stack([conv_new0, conv_new1]))
    return x.reshape(batch, seq_len, D_MODEL), new_state


def kernel(x_prompt, x_sample, state_mlstm_C, state_mlstm_n, state_mlstm_m, state_rwkv_S, state_rwkv_shift,
           state_gla_S, state_s5_re, state_s5_im, state_ffn_conv,
           norm_mix, norm_ffn, norm_final,
           e_w_in, e_b_if, e_mu, e_w0, e_w_up, e_a0, e_a_up, e_g_up, e_k_k, e_k_a, e_r_k,
           e_ln_m_g, e_ln_r_g, e_ln_r_b, e_w_out,
           o_w_in, o_a_up, o_a_b, o_ln_g, o_lam_re, o_lam_im, o_log_step, o_b_re, o_b_im, o_c_re, o_c_im,
           o_d, o_w_glu, o_w_out,
           f_w_up, f_conv_w, f_conv_b, f_w_down):
    params = (norm_mix, norm_ffn, norm_final,
              e_w_in, e_b_if, e_mu, e_w0, e_w_up, e_a0, e_a_up, e_g_up, e_k_k, e_k_a, e_r_k,
              e_ln_m_g, e_ln_r_g, e_ln_r_b, e_w_out,
              o_w_in, o_a_up, o_a_b, o_ln_g, o_lam_re, o_lam_im, o_log_step, o_b_re, o_b_im, o_c_re, o_c_im,
              o_d, o_w_glu, o_w_out,
              f_w_up, f_conv_w, f_conv_b, f_w_down)
    bp, tp, _ = x_prompt.shape
    bs, ts, _ = x_sample.shape
    n_even, n_odd, depth = state_mlstm_C.shape[0], state_gla_S.shape[0], state_ffn_conv.shape[0]
    init = (jnp.zeros((n_even, bp, MLSTM_HEADS, MLSTM_HD, MLSTM_HD), F32),
            jnp.zeros((n_even, bp, MLSTM_HEADS, MLSTM_HD), F32),
            jnp.full((n_even, bp, MLSTM_HEADS), M_INIT, F32),
            jnp.zeros((n_even, bp, RWKV_HEADS, RWKV_HD, RWKV_HD), F32),
            jnp.zeros((n_even, bp, RWKV_COLS), F32),
            jnp.zeros((n_odd, bp, GLA_HEADS, GLA_DK, GLA_DV), F32),
            jnp.zeros((n_odd, bp, S5_GROUPS, S5_P), F32),
            jnp.zeros((n_odd, bp, S5_GROUPS, S5_P), F32),
            jnp.zeros((depth, bp, CONV_W - 1, 2 * D_FF), F32))
    past = (state_mlstm_C, state_mlstm_n, state_mlstm_m, state_rwkv_S, state_rwkv_shift,
            state_gla_S, state_s5_re, state_s5_im, state_ffn_conv)
    y_prompt, p_state = _trunk(x_prompt, init, params, batch=bp, seq_len=tp)
    y_sample, s_state = _trunk(x_sample, past, params, batch=bs, seq_len=ts)
    return (y_prompt, y_sample) + tuple(p_state) + tuple(s_state)
```

```python
import functools
import math

import jax
import jax.numpy as jnp
from jax import lax
from jax.experimental import pallas as pl
from jax.experimental.pallas import tpu as pltpu

F32 = jnp.float32
BF16 = jnp.bfloat16

D_MODEL = 1024
HALF = D_MODEL // 2
MLSTM_HEADS = 4
MLSTM_HD = HALF // MLSTM_HEADS
RWKV_HD = 64
RWKV_HEADS = HALF // RWKV_HD
RWKV_W_LORA = 64
RWKV_A_LORA = 64
RWKV_G_LORA = 128
RWKV_COLS = 3 * HALF + RWKV_W_LORA + RWKV_A_LORA + RWKV_G_LORA
GLA_HEADS = 4
GLA_DK = 64
GLA_DV = 128
GLA_QK = GLA_HEADS * GLA_DK
GLA_GATE_RANK = 16
GLA_TAU = 16.0
S5_CH = HALF
S5_GROUP = 16
S5_GROUPS = S5_CH // S5_GROUP
S5_P = 64
S5_STATE = S5_GROUPS * S5_P
D_FF = 2816
CONV_W = 3
CHUNK = 64
EPS = 1e-6
RWKV_LN_EPS = 64e-5
M_INIT = -1e30

LANES = 128
SUBLANES = 8
VMEM_LIMIT_BYTES = 56 * 1024 * 1024

Z0_RWKV = 0
Z0_GATE = RWKV_COLS
Z0_Q = Z0_GATE + 2 * LANES
Z0_K, Z0_V, Z0_O = Z0_Q + HALF, Z0_Q + 2 * HALF, Z0_Q + 3 * HALF
Z0_COLS = Z0_Q + 4 * HALF
RW_LORA = 3 * HALF
RW_G = RW_LORA + RWKV_W_LORA + RWKV_A_LORA
Z1_Q, Z1_K, Z1_V, Z1_G, Z1_U = 0, GLA_QK, 2 * GLA_QK, 2 * GLA_QK + HALF, 2 * GLA_QK + 2 * HALF
Z1_AD = Z1_U + S5_CH
Z1_COLS = 2560


def _cparams(*sem):
    return pltpu.CompilerParams(dimension_semantics=sem, vmem_limit_bytes=VMEM_LIMIT_BYTES)


def _rms(x, g):
    return x * lax.rsqrt(jnp.mean(x * x, -1, keepdims=True) + EPS) * g


def _norm_matmul_kernel(x_ref, g_ref, w_ref, o_ref, *, tn):
    h = _rms(x_ref[...], g_ref[...]).astype(BF16)
    for j in range(w_ref.shape[1] // tn):
        cols = slice(j * tn, (j + 1) * tn)
        o_ref[:, cols] = jnp.dot(h, w_ref[:, cols], preferred_element_type=F32)


def norm_matmul(x, g, w, *, tm, tn, time_major_seq_len=None):
    m, d = x.shape
    n = w.shape[1]
    if time_major_seq_len is None:
        out_shape = (m, n)
        out_spec = pl.BlockSpec((tm, n), lambda i: (i, 0))
    else:
        tiles_per_seq = time_major_seq_len // tm
        out_shape = (time_major_seq_len, (m // time_major_seq_len) * n)
        out_spec = pl.BlockSpec((tm, n), lambda i: (i % tiles_per_seq, i // tiles_per_seq))
    return pl.pallas_call(
        functools.partial(_norm_matmul_kernel, tn=tn),
        grid=(m // tm,),
        in_specs=[pl.BlockSpec((tm, d), lambda i: (i, 0)),
                  pl.BlockSpec((1, d), lambda i: (0, 0)),
                  pl.BlockSpec((d, n), lambda i: (0, 0), pipeline_mode=pl.Buffered(1))],
        out_specs=out_spec,
        out_shape=jax.ShapeDtypeStruct(out_shape, F32),
        compiler_params=_cparams("parallel"),
        name="norm_matmul",
    )(x, g.reshape(1, d), w)


def _out_proj_kernel(x_ref, ya_ref, yb_ref, wa_ref, wb_ref, o_ref):
    acc = jnp.dot(ya_ref[...].astype(BF16), wa_ref[...], preferred_element_type=F32)
    acc += jnp.dot(yb_ref[...].astype(BF16), wb_ref[...], preferred_element_type=F32)
    o_ref[...] = x_ref[...] + acc


def out_proj(x, ya, yb, wa, wb, *, tm):
    m, d = x.shape
    k = ya.shape[1]
    if yb.ndim == 3:
        tiles_per_seq = yb.shape[0] // tm
        yb = yb.reshape(yb.shape[0], -1)
        yb_spec = pl.BlockSpec((tm, k), lambda i: (i % tiles_per_seq, i // tiles_per_seq))
    else:
        yb_spec = pl.BlockSpec((tm, k), lambda i: (i, 0))
    return pl.pallas_call(
        _out_proj_kernel,
        grid=(m // tm,),
        in_specs=[pl.BlockSpec((tm, d), lambda i: (i, 0)),
                  pl.BlockSpec((tm, k), lambda i: (i, 0)),
                  yb_spec,
                  pl.BlockSpec((k, d), lambda i: (0, 0)),
                  pl.BlockSpec((k, d), lambda i: (0, 0))],
        out_specs=pl.BlockSpec((tm, d), lambda i: (i, 0)),
        out_shape=jax.ShapeDtypeStruct((m, d), F32),
        compiler_params=_cparams("parallel"),
        name="out_proj",
    )(x, ya, yb, wa, wb)


FFN_SUB = 256


def _ffn_kernel(x_ref, ya_ref, yb_ref, wa_ref, wb_ref, g_ref, wup_ref, cw_ref, cb_ref, wdn_ref, st_ref,
                gf_ref, o_ref, tail_ref, h_scr, act_scr, car_scr, x_scr, *, tm, seq_len, final_norm):
    long_seq = seq_len >= tm
    i = pl.program_id(0)
    x_scr[...] = (x_ref[...]
                  + jnp.dot(ya_ref[...].astype(BF16), wa_ref[...], preferred_element_type=F32)
                  + jnp.dot(yb_ref[...].astype(BF16), wb_ref[...], preferred_element_type=F32))
    h_scr[...] = _rms(x_scr[...], g_ref[...]).astype(BF16)
    row = lax.broadcasted_iota(jnp.int32, (tm, FFN_SUB), 0)
    if long_seq:
        @pl.when((i * tm) % seq_len == 0)
        def _():
            car_scr[...] = st_ref[...]
    else:
        t_in_seq = row % seq_len

    row8 = lax.broadcasted_iota(jnp.int32, (SUBLANES, FFN_SUB), 0)

    def conv_part(col0):
        u = jnp.dot(h_scr[...], wup_ref[:, col0:col0 + FFN_SUB], preferred_element_type=F32)
        p1 = pltpu.roll(u, 1, 0)
        p2 = pltpu.roll(u, 2, 0)
        if long_seq:
            halo = car_scr[:, col0:col0 + FFN_SUB]
            h6 = halo[SUBLANES - 2:SUBLANES - 1]
            h7 = halo[SUBLANES - 1:SUBLANES]
            p1 = jnp.concatenate([jnp.where(row8 == 0, h7, p1[:SUBLANES]), p1[SUBLANES:]], axis=0)
            p2 = jnp.concatenate([jnp.where(row8 == 0, h6, jnp.where(row8 == 1, h7, p2[:SUBLANES])),
                                  p2[SUBLANES:]], axis=0)
            car_scr[:, col0:col0 + FFN_SUB] = u[tm - SUBLANES:]
            tail_ref[:, col0:col0 + FFN_SUB] = u[tm - SUBLANES:]
        else:
            e = st_ref[:, col0:col0 + FFN_SUB]
            p1 = jnp.where(t_in_seq == 0, pltpu.roll(e, tm - 1, 0), p1)
            p2 = jnp.where(t_in_seq < 2, e, p2)
            tail_ref[:, col0:col0 + FFN_SUB] = u
        cw = cw_ref[:, col0:col0 + FFN_SUB]
        return (cb_ref[:, col0:col0 + FFN_SUB]
                + (cw[0:1] * p2 + cw[1:2] * p1 + cw[2:3] * u))

    for c in range(D_FF // FFN_SUB):
        val = conv_part(c * FFN_SUB)
        gate = conv_part(D_FF + c * FFN_SUB)
        act_scr[:, c * FFN_SUB:(c + 1) * FFN_SUB] = (val * (gate * jax.nn.sigmoid(gate))).astype(BF16)

    y = x_scr[...] + jnp.dot(act_scr[...], wdn_ref[...], preferred_element_type=F32)
    if final_norm:
        y = _rms(y, gf_ref[...])
    o_ref[...] = y


def conv_ffn(x, ya, yb, wa, wb, g, w_up, conv_w, conv_b, w_down, conv0, gf, *, batch, seq_len, tm,
             final_norm):
    m, d = x.shape
    k = ya.shape[1]
    if yb.ndim == 3:
        yb_tiles_per_seq = yb.shape[0] // tm
        yb = yb.reshape(yb.shape[0], -1)
        yb_spec = pl.BlockSpec((tm, k), lambda i: (i % yb_tiles_per_seq, i // yb_tiles_per_seq))
    else:
        yb_spec = pl.BlockSpec((tm, k), lambda i: (i, 0))
    f2 = 2 * D_FF
    long_seq = seq_len >= tm
    if long_seq:
        assert seq_len % tm == 0
        tiles_per_seq = seq_len // tm
        st = jnp.pad(conv0, ((0, 0), (SUBLANES - (CONV_W - 1), 0), (0, 0)))
        st_spec = pl.BlockSpec((None, SUBLANES, f2), lambda i: (i // tiles_per_seq, 0, 0))
        tail_shape = jax.ShapeDtypeStruct((batch, SUBLANES, f2), F32)
        tail_spec = pl.BlockSpec((None, SUBLANES, f2), lambda i: (i // tiles_per_seq, 0, 0))
        sem = "arbitrary"
    else:
        assert seq_len == SUBLANES and tm % seq_len == 0
        st = jnp.pad(conv0, ((0, 0), (0, seq_len - (CONV_W - 1)), (0, 0))).reshape(m, f2)
        st_spec = pl.BlockSpec((tm, f2), lambda i: (i, 0))
        tail_shape = jax.ShapeDtypeStruct((m, f2), F32)
        tail_spec = pl.BlockSpec((tm, f2), lambda i: (i, 0))
        sem = "parallel"
    const = lambda i: (0, 0)
    out, tail = pl.pallas_call(
        functools.partial(_ffn_kernel, tm=tm, seq_len=seq_len, final_norm=final_norm),
        grid=(m // tm,),
        in_specs=[pl.BlockSpec((tm, d), lambda i: (i, 0)),
                  pl.BlockSpec((tm, k), lambda i: (i, 0)),
                  yb_spec,
                  pl.BlockSpec((k, d), const),
                  pl.BlockSpec((k, d), const),
                  pl.BlockSpec((1, d), const),
                  pl.BlockSpec((d, f2), const, pipeline_mode=pl.Buffered(1)),
                  pl.BlockSpec((CONV_W, f2), const),
                  pl.BlockSpec((1, f2), const),
                  pl.BlockSpec((D_FF, d), const, pipeline_mode=pl.Buffered(1)),
                  st_spec,
                  pl.BlockSpec((1, d), const)],
        out_specs=[pl.BlockSpec((tm, d), lambda i: (i, 0)), tail_spec],
        out_shape=[jax.ShapeDtypeStruct((m, d), F32), tail_shape],
        scratch_shapes=[pltpu.VMEM((tm, d), BF16),
                        pltpu.VMEM((tm, D_FF), BF16),
                        pltpu.VMEM((SUBLANES, f2), F32),
                        pltpu.VMEM((tm, d), F32)],
        compiler_params=_cparams(sem),
        name="conv_ffn",
    )(x, ya, yb, wa, wb, g.reshape(1, d), w_up, conv_w, conv_b.reshape(1, f2), w_down, st, gf.reshape(1, d))
    new_state = tail.reshape(batch, -1, f2)[:, -(CONV_W - 1):]
    return out, new_state


def _log_sigmoid(x):
    return jnp.minimum(x, 0.0) - jnp.log1p(jnp.exp(-jnp.abs(x)))


def _dot_nt(a, b):
    return lax.dot_general(a, b, (((1,), (1,)), ((), ())), preferred_element_type=F32)


def _dot_tn(a, b):
    return lax.dot_general(a, b, (((0,), (0,)), ((), ())), preferred_element_type=F32)


def _mlstm_kernel(bif_ref, q_ref, k_ref, v_ref, og_ref, gz_ref, gt_ref, c0_ref, n0_ref, m0_ref,
                  lng_ref, y_ref, c_ref, n_ref, m_ref, grow_scr, gcol_scr, bcol_scr, cmax_scr,
                  *, chunk, n_chunks, bb):
    L = chunk

    @pl.when(pl.program_id(1) == 0)
    def _():
        c_ref[...] = c0_ref[...]
        n_ref[...] = n0_ref[...]
        m_ref[...] = m0_ref[...]

    t_idx = lax.broadcasted_iota(jnp.int32, (L, L), 0)
    s_idx = lax.broadcasted_iota(jnp.int32, (L, L), 1)
    causal = s_idx <= t_idx
    scale = MLSTM_HD ** -0.5

    def compute(items):
        ph = []
        for j, c in items:
            start = (j * n_chunks + c) * L
            rows = pl.ds(start if isinstance(start, int) else pl.multiple_of(start, L), L)
            for h in range(MLSTM_HEADS):
                ph.append(gate_phase(j, c, h, rows))
        for d in ph:
            d["qk_raw"] = _dot_nt(d["qb"], d["kb"])
            d["qc"] = _dot_nt(d["qb"], d["c_old"].astype(BF16))
            d["c_upd"] = _dot_tn((d["v"] * d["wl_col"]).astype(BF16), d["kb"])
        for d in ph:
            d["qk"] = d["qk_raw"] * d["w"]
            d["pv"] = jnp.dot(d["qk"].astype(BF16), d["v"].astype(BF16), preferred_element_type=F32)
        stores = []
        for d in ph:
            stores += finish_phase(d)
        return stores

    def entry(j, c, h):
        return (j * n_chunks + c) * MLSTM_HEADS + h

    lane = lax.broadcasted_iota(jnp.int32, (1, LANES), 1)
    bias_row = jnp.zeros((1, LANES), F32)
    for i in range(2 * MLSTM_HEADS):
        bias_row = jnp.where(lane == i, bif_ref[i], bias_row)
    gates_col = gz_ref[...] + bias_row
    gates_col = jnp.where(lane >= MLSTM_HEADS, _log_sigmoid(gates_col), gates_col)
    sub = lax.broadcasted_iota(jnp.int32, (2 * MLSTM_HEADS, 1), 0)
    bias_col = jnp.zeros((2 * MLSTM_HEADS, 1), F32)
    for i in range(2 * MLSTM_HEADS):
        bias_col = jnp.where(sub == i, bif_ref[i], bias_col)
    tril = causal.astype(F32)
    triu = (t_idx <= s_idx).astype(F32)
    exact = dict(preferred_element_type=F32, precision=lax.Precision.HIGHEST)
    for j in range(bb):
        for c in range(n_chunks):
            rows = slice((j * n_chunks + c) * L, (j * n_chunks + c + 1) * L)
            g_c = gates_col[rows]
            b_c = jnp.dot(tril, g_c, **exact)
            g_r = gt_ref[j, :, c, :] + bias_col
            g_r = jnp.where(sub >= MLSTM_HEADS, _log_sigmoid(g_r), g_r)
            b_r = jnp.dot(g_r, triu, **exact)
            for h in range(MLSTM_HEADS):
                g_row = g_r[h:h + 1] - b_r[MLSTM_HEADS + h:MLSTM_HEADS + h + 1]
                b_col = b_c[:, MLSTM_HEADS + h:MLSTM_HEADS + h + 1]
                e = entry(j, c, h)
                grow_scr[e, 0:1, 0:L] = g_row
                gcol_scr[e, :, 0:1] = g_c[:, h:h + 1] - b_col
                bcol_scr[e, :, 0:1] = b_col
                cmax_scr[e, :, 0:1] = jnp.max(jnp.where(causal, g_row, -jnp.inf), axis=1, keepdims=True)

    def gate_phase(j, c, h, rows):
        cols = slice(h * MLSTM_HD, (h + 1) * MLSTM_HD)
        e = entry(j, c, h)
        g_row = grow_scr[e, 0:1, 0:L]
        b_col = bcol_scr[e, :, 0:1]
        m_old = m_ref[j, h]
        mu_col = jnp.maximum(m_old, cmax_scr[e, :, 0:1])
        w = jnp.where(causal, jnp.exp(g_row - mu_col), 0.0)
        s_inter = jnp.exp(m_old - mu_col)
        q = q_ref[rows, cols]
        ks = k_ref[rows, cols] * scale
        mu_last = mu_col[L - 1:L]
        return dict(j=j, h=h, rows=rows, cols=cols, q=q, ks=ks, v=v_ref[rows, cols],
                    qb=q.astype(BF16), kb=ks.astype(BF16), c_old=c_ref[j, h], n_old=n_ref[j, h],
                    w=w, s_inter=s_inter, m_t=b_col + mu_col,
                    wl_col=jnp.exp(gcol_scr[e, :, 0:1] - mu_last),
                    dec=jnp.exp(m_old - mu_last), m_new=b_col[L - 1:L] + mu_last)

    def finish_phase(d):
        j, h, rows, cols = d["j"], d["h"], d["rows"], d["cols"]
        num = d["s_inter"] * d["qc"] + d["pv"]
        den = (d["s_inter"] * jnp.sum(d["q"] * d["n_old"], axis=1, keepdims=True)
               + jnp.sum(d["qk"], axis=1, keepdims=True))
        hh = num / jnp.maximum(jnp.abs(den), jnp.exp(-d["m_t"]))
        return [(c_ref, (j, h), d["dec"] * d["c_old"] + d["c_upd"]),
                (n_ref, (j, h), d["dec"] * d["n_old"] + jnp.sum(d["wl_col"] * d["ks"], axis=0, keepdims=True)),
                (m_ref, (j, h), d["m_new"]),
                (y_ref, (rows, cols), hh)]

    def commit(stores):
        for ref, idx, val in stores:
            ref[idx] = val

    def chunk_body(j, c, carry):
        commit(compute([(j, c)]))
        return carry

    if n_chunks == 1:
        commit(compute([(j, 0) for j in range(bb)]))
    else:
        for j in range(bb):
            lax.fori_loop(0, n_chunks, functools.partial(chunk_body, j), 0)

    for h in range(MLSTM_HEADS):
        cols = slice(h * MLSTM_HD, (h + 1) * MLSTM_HD)
        hh = y_ref[:, cols]
        xc = hh - jnp.mean(hh, -1, keepdims=True)
        hn = xc * lax.rsqrt(jnp.mean(xc * xc, -1, keepdims=True) + EPS) * lng_ref[:, cols]
        y_ref[:, cols] = jax.nn.sigmoid(og_ref[:, cols]) * hn


def mlstm_mixer(z, b_if, c0, n0, m0, ln_g, *, batch, seq_len, tb, bb):
    m_rows = batch * seq_len
    L = math.gcd(seq_len, CHUNK)
    n_tb = seq_len // tb
    n_chunks = tb // L
    assert bb == 1 or n_tb == 1
    rows = bb * tb
    n_entries = bb * n_chunks * MLSTM_HEADS
    gates = z[:, Z0_GATE:Z0_GATE + 2 * MLSTM_HEADS].reshape(batch, seq_len, 2 * MLSTM_HEADS)
    gates_t = jnp.swapaxes(gates, 1, 2).reshape(batch, 2 * MLSTM_HEADS, seq_len // L, L)
    zcol = lambda off: pl.BlockSpec((rows, HALF), lambda b, t: (b * n_tb + t, off // HALF))
    state4 = lambda a, b_: pl.BlockSpec((bb, MLSTM_HEADS, a, b_), lambda b, t: (b, 0, 0, 0))
    y, c, n, m = pl.pallas_call(
        functools.partial(_mlstm_kernel, chunk=L, n_chunks=n_chunks, bb=bb),
        grid=(batch // bb, n_tb),
        in_specs=[pl.BlockSpec(memory_space=pltpu.SMEM),
                  zcol(Z0_Q), zcol(Z0_K), zcol(Z0_V), zcol(Z0_O),
                  pl.BlockSpec((rows, LANES), lambda b, t: (b * n_tb + t, Z0_GATE // LANES)),
                  pl.BlockSpec((bb, 2 * MLSTM_HEADS, n_chunks, L), lambda b, t: (b, 0, t, 0)),
                  state4(MLSTM_HD, MLSTM_HD), state4(1, MLSTM_HD), state4(1, 1),
                  pl.BlockSpec((1, HALF), lambda b, t: (0, 0))],
        out_specs=[pl.BlockSpec((rows, HALF), lambda b, t: (b * n_tb + t, 0)),
                   state4(MLSTM_HD, MLSTM_HD), state4(1, MLSTM_HD), state4(1, 1)],
        out_shape=[jax.ShapeDtypeStruct((m_rows, HALF), F32),
                   jax.ShapeDtypeStruct((batch, MLSTM_HEADS, MLSTM_HD, MLSTM_HD), F32),
                   jax.ShapeDtypeStruct((batch, MLSTM_HEADS, 1, MLSTM_HD), F32),
                   jax.ShapeDtypeStruct((batch, MLSTM_HEADS, 1, 1), F32)],
        scratch_shapes=[pltpu.VMEM((n_entries, SUBLANES, LANES), F32)]
        + [pltpu.VMEM((n_entries, L, LANES), F32)] * 3,
        compiler_params=_cparams("parallel", "arbitrary"),
        name="mlstm",
    )(b_if, z, z, z, z, z, gates_t,
      c0, n0.reshape(batch, MLSTM_HEADS, 1, MLSTM_HD), m0.reshape(batch, MLSTM_HEADS, 1, 1),
      ln_g.reshape(1, HALF))
    return y, c, n.reshape(batch, MLSTM_HEADS, MLSTM_HD), m.reshape(batch, MLSTM_HEADS)


def _split2(x):
    hi = x.astype(BF16)
    lo = (x - hi.astype(F32)).astype(BF16)
    return hi, lo


def _seg_sum(x, ones_blk):
    hi, lo = _split2(x)
    return (jnp.dot(hi, ones_blk, preferred_element_type=F32)
            + jnp.dot(lo, ones_blk, preferred_element_type=F32))


def _softplus(x):
    return jnp.maximum(x, 0.0) + jnp.log1p(jnp.exp(-jnp.abs(x)))


def _block_ones(n, seg):
    idx = jnp.arange(n) // seg
    return (idx[:, None] == idx[None, :]).astype(BF16)


def _rwkv_prep_kernel(z_ref, st_ref, mu_ref, w0_ref, wup_ref, a0_ref, aup_ref, gup_ref, kk_ref, ka_ref,
                      rk_ref, ones_ref, r_out, w_out, k_out, v_out, kk_out, kka_out, bv_out, g_out,
                      car_scr, *, tm, seq_len):
    zr = z_ref[...]
    row = lax.broadcasted_iota(jnp.int32, zr.shape, 0)
    prev = pltpu.roll(zr, 1, 0)
    if seq_len >= tm:
        @pl.when((pl.program_id(0) * tm) % seq_len == 0)
        def _():
            car_scr[...] = st_ref[...]
        prev = jnp.where(row == 0, car_scr[SUBLANES - 1:SUBLANES], prev)
        car_scr[...] = zr[tm - SUBLANES:]
    else:
        prev = jnp.where(row % seq_len == 0, st_ref[...], prev)
    zs = zr + (prev - zr) * mu_ref[...]
    r = zs[:, :HALF]
    kr = zs[:, HALF:2 * HALF]
    vr = zs[:, 2 * HALF:3 * HALF]
    lora = zs[:, RW_LORA:RW_G]
    ones = ones_ref[...]
    w_log = -_softplus(-(w0_ref[...] + jnp.dot(jnp.tanh(lora).astype(BF16), wup_ref[...],
                                               preferred_element_type=F32))) - 0.5
    a = jax.nn.sigmoid(a0_ref[...] + jnp.dot(lora.astype(BF16), aup_ref[...], preferred_element_type=F32))
    g = jnp.dot(jax.nn.sigmoid(zs[:, RW_G:]).astype(BF16), gup_ref[...], preferred_element_type=F32)
    kk = kr * kk_ref[...]
    kk = kk * lax.rsqrt(jnp.maximum(_seg_sum(kk * kk, ones), 1e-24))
    k2 = kr * (1.0 + (a - 1.0) * ka_ref[...])
    r_out[...] = r
    w_out[...] = jnp.exp(-jnp.exp(w_log))
    k_out[...] = k2
    v_out[...] = vr
    kk_out[...] = kk
    kka_out[...] = kk * a
    bv_out[...] = _seg_sum(r * k2 * rk_ref[...], ones) * vr
    g_out[...] = g


RWKV_GROUPS = 2


def _rwkv_rec_kernel(r_ref, w_ref, k_ref, v_ref, kk_ref, kka_ref, bv_ref, g_ref, s0_ref, lng_ref, lnb_ref,
                     onesy_ref, ones_ref, y_ref, s_ref, yraw_scr, lhs_scr, ylhs_scr, tmaj_scr=None,
                     *, bb, tb):
    @pl.when(pl.program_id(1) == 0)
    def _():
        s_ref[...] = s0_ref[...]

    if tmaj_scr is not None:
        streams = (r_ref, w_ref, k_ref, v_ref, kk_ref, kka_ref, bv_ref, g_ref)
        for i, ref in enumerate(streams):
            for t in range(tb):
                tmaj_scr[i, t] = ref[:, t, :]
        r_ref, w_ref, k_ref, v_ref, kk_ref, kka_ref, bv_ref, g_ref = (
            tmaj_scr.at[i] for i in range(len(streams)))

    pairs = RWKV_HEADS // 2
    n_tiles = bb * pairs
    shape = (RWKV_HD, LANES)
    eye2 = (lax.broadcasted_iota(jnp.int32, shape, 0)
            == lax.broadcasted_iota(jnp.int32, shape, 1) % RWKV_HD)
    pair_diag = (lax.broadcasted_iota(jnp.int32, shape, 0) // 2
                 == (lax.broadcasted_iota(jnp.int32, shape, 1) % RWKV_HD) // 2)
    lane_even = lax.broadcasted_iota(jnp.int32, (1, LANES), 1) % 2 == 0
    sublane_even = lax.broadcasted_iota(jnp.int32, (SUBLANES, 1), 0) % 2 == 0
    y_rows = lambda idx: slice((idx // 2) * RWKV_HD, (idx // 2 + 1) * RWKV_HD)
    y_cols = lambda idx: slice((idx % 2) * LANES, (idx % 2 + 1) * LANES)

    per_group = n_tiles // RWKV_GROUPS

    def step(t, carry):
        def tile(idx):
            b, p = divmod(idx, pairs)
            cols = slice(p * LANES, (p + 1) * LANES)
            return b, p, (lambda ref: ref[t, b:b + 1, cols])

        groups = [range(g * per_group, (g + 1) * per_group) for g in range(RWKV_GROUPS)]
        g_rows = per_group * RWKV_HD * 3 // 2
        sums = []
        for g, grp in enumerate(groups):
            for idx in grp:
                b, p, row = tile(idx)
                rel = idx - grp[0]
                base = g * g_rows + rel * RWKV_HD
                hi, lo = _split2(s_ref[b, p] * row(kk_ref))
                lhs_scr[base:base + RWKV_HD, :LANES] = hi
                lhs_scr[base:base + RWKV_HD, LANES:] = lo
                v_row = row(v_ref)
                v_hi = v_row.astype(BF16).astype(F32)
                v_lo = v_row - v_hi
                v_lo_swapped = jnp.where(lane_even, pltpu.roll(v_lo, LANES - 1, 1), pltpu.roll(v_lo, 1, 1))
                pattern = jnp.where(sublane_even, jnp.where(lane_even, v_hi, v_lo_swapped),
                                    jnp.where(lane_even, v_lo_swapped, v_hi))
                v_base = g * g_rows + per_group * RWKV_HD + (rel // 2) * RWKV_HD
                lhs_scr[v_base:v_base + RWKV_HD, y_cols(rel)] = jnp.where(
                    pair_diag, jnp.concatenate([pattern] * (RWKV_HD // SUBLANES), axis=0), 0.0).astype(BF16)
            sums.append(jnp.dot(lhs_scr[g * g_rows:(g + 1) * g_rows, :], onesy_ref[...],
                                preferred_element_type=F32))
        ysums = []
        for grp, sm in zip(groups, sums):
            for idx in grp:
                b, p, row = tile(idx)
                rel = idx - grp[0]
                sa_rows = slice(rel * RWKV_HD, (rel + 1) * RWKV_HD)
                sa = sm[sa_rows, :LANES] + sm[sa_rows, LANES:]
                v_base = per_group * RWKV_HD + (rel // 2) * RWKV_HD
                v_tile = sm[v_base:v_base + RWKV_HD, y_cols(rel)]
                s = s_ref[b, p] * row(w_ref) - sa * row(kka_ref) + v_tile * row(k_ref)
                s_ref[b, p] = s
                ylhs_scr[y_rows(idx), y_cols(idx)] = (s * row(r_ref)).astype(BF16)
            g_rows = slice(y_rows(grp[0]).start, y_rows(grp[-1]).stop)
            ysums.append(jnp.dot(ylhs_scr[g_rows, :], onesy_ref[...], preferred_element_type=F32))
        for grp, ys in zip(groups, ysums):
            for idx in grp:
                b, p, row = tile(idx)
                rel = slice(y_rows(idx).start - y_rows(grp[0]).start, y_rows(idx).stop - y_rows(grp[0]).start)
                yraw_scr[t, b:b + 1, p * LANES:(p + 1) * LANES] = jnp.sum(
                    jnp.where(eye2, ys[rel, y_cols(idx)], 0.0), axis=0, keepdims=True)
        return carry

    lax.fori_loop(0, tb, step, 0, unroll=2)

    ones = ones_ref[...]
    y = yraw_scr[...].reshape(tb * bb, HALF)
    xc = y - _seg_sum(y, ones) * (1.0 / RWKV_HD)
    var = _seg_sum(xc * xc, ones) * (1.0 / RWKV_HD)
    yn = xc * lax.rsqrt(var + RWKV_LN_EPS) * lng_ref[...] + lnb_ref[...]
    out = (yn + bv_ref[...].reshape(tb * bb, HALF)) * g_ref[...].reshape(tb * bb, HALF)
    if tmaj_scr is None:
        y_ref[...] = out.reshape(tb, bb, HALF)
    else:
        for t in range(tb):
            y_ref[:, t, :] = out[t * bb:(t + 1) * bb]


def rwkv_mixer(z, s0, shift0, mu, w0, w_up, a0, a_up, g_up, k_k, k_a, r_k, ln_g, ln_b,
               *, batch, seq_len, tm, bb, tb):
    m_rows = batch * seq_len
    tm_shape = (seq_len, batch, HALF)
    if seq_len >= tm:
        tiles_per_seq = seq_len // tm
        st = jnp.pad(shift0[:, None], ((0, 0), (SUBLANES - 1, 0), (0, 0)))
        st_spec = pl.BlockSpec((None, SUBLANES, RWKV_COLS), lambda i: (i // tiles_per_seq, 0, 0))
        sem = "arbitrary"
        tok = pl.BlockSpec((tm, HALF), lambda i: (i % tiles_per_seq, i // tiles_per_seq))
        tok_shape = (seq_len, batch * HALF)
    else:
        st = jnp.pad(shift0[:, None], ((0, 0), (0, seq_len - 1), (0, 0))).reshape(m_rows, RWKV_COLS)
        st_spec = pl.BlockSpec((tm, RWKV_COLS), lambda i: (i, 0))
        sem = "parallel"
        tok = pl.BlockSpec((tm, HALF), lambda i: (i, 0))
        tok_shape = (m_rows, HALF)
    row = lambda a: a.reshape(1, -1)
    const = lambda i: (0, 0)
    vec = pl.BlockSpec((1, HALF), const)
    lora_rows = RWKV_W_LORA + RWKV_A_LORA
    w_up_p = jnp.pad(w_up, ((0, RWKV_A_LORA), (0, 0))).astype(BF16)
    a_up_p = jnp.pad(a_up, ((RWKV_W_LORA, 0), (0, 0))).astype(BF16)
    ones_half = _block_ones(HALF, RWKV_HD)
    prep = pl.pallas_call(
        functools.partial(_rwkv_prep_kernel, tm=tm, seq_len=seq_len),
        grid=(m_rows // tm,),
        in_specs=[pl.BlockSpec((tm, RWKV_COLS), lambda i: (i, Z0_RWKV // RWKV_COLS)),
                  st_spec,
                  pl.BlockSpec((1, RWKV_COLS), const),
                  vec, pl.BlockSpec((lora_rows, HALF), const),
                  vec, pl.BlockSpec((lora_rows, HALF), const),
                  pl.BlockSpec((RWKV_G_LORA, HALF), const),
                  vec, vec, vec,
                  pl.BlockSpec((HALF, HALF), const)],
        out_specs=[tok] * 8,
        out_shape=[jax.ShapeDtypeStruct(tok_shape, F32)] * 8,
        scratch_shapes=[pltpu.VMEM((SUBLANES, RWKV_COLS), F32)],
        compiler_params=_cparams(sem),
        name="rwkv_prep",
    )(z, st, row(mu), row(w0), w_up_p, row(a0), a_up_p, g_up.astype(BF16), row(k_k), row(k_a), row(r_k),
      ones_half)
    pairs = RWKV_HEADS // 2
    s_pairs = (s0.reshape(batch, pairs, 2, RWKV_HD, RWKV_HD).transpose(0, 1, 3, 2, 4)
               .reshape(batch, pairs, RWKV_HD, LANES))
    scratch = [pltpu.VMEM((tb, bb, HALF), F32),
               pltpu.VMEM((bb * pairs * RWKV_HD * 3 // 2, 2 * LANES), BF16),
               pltpu.VMEM((bb * pairs * RWKV_HD // 2, 2 * LANES), BF16)]
    if seq_len >= tm:
        seqs = [a.reshape(tm_shape) for a in prep]
        blk = pl.BlockSpec((tb, bb, HALF), lambda b, t: (t, b, 0))
        y_shape = tm_shape
    else:
        assert tb == seq_len
        seqs = [a.reshape(batch, seq_len, HALF) for a in prep]
        blk = pl.BlockSpec((bb, tb, HALF), lambda b, t: (b, 0, 0))
        y_shape = (batch, seq_len, HALF)
        scratch.append(pltpu.VMEM((len(prep), tb, bb, HALF), F32))
    s_spec = pl.BlockSpec((bb, pairs, RWKV_HD, LANES), lambda b, t: (b, 0, 0, 0))
    const2 = lambda b, t: (0, 0)
    y, s_new = pl.pallas_call(
        functools.partial(_rwkv_rec_kernel, bb=bb, tb=tb),
        grid=(batch // bb, seq_len // tb),
        in_specs=[blk] * 8 + [s_spec,
                              pl.BlockSpec((1, HALF), const2), pl.BlockSpec((1, HALF), const2),
                              pl.BlockSpec((2 * LANES, 2 * LANES), const2),
                              pl.BlockSpec((HALF, HALF), const2)],
        out_specs=[blk, s_spec],
        out_shape=[jax.ShapeDtypeStruct(y_shape, F32),
                   jax.ShapeDtypeStruct((batch, pairs, RWKV_HD, LANES), F32)],
        scratch_shapes=scratch,
        compiler_params=_cparams("parallel", "arbitrary"),
        name="rwkv_rec",
    )(*seqs, s_pairs, row(ln_g), row(ln_b),
      _block_ones(2 * LANES, RWKV_HD), ones_half)
    s_new = (s_new.reshape(batch, pairs, RWKV_HD, 2, RWKV_HD).transpose(0, 1, 3, 2, 4)
             .reshape(batch, RWKV_HEADS, RWKV_HD, RWKV_HD))
    if seq_len < tm:
        y = y.reshape(m_rows, HALF)
    return y, s_new


GLA_SUB = 16


def _gla_kernel(q_ref, k_ref, v_ref, g_ref, ad_ref, aup_ref, ab_ref, lng_ref, s0_ref,
                y_ref, s_ref, bc_scr, *, chunk, n_chunks, bb):
    L = chunk
    sub = min(GLA_SUB, L)

    @pl.when(pl.program_id(1) == 0)
    def _():
        s_ref[...] = s0_ref[...]

    lane = lax.broadcasted_iota(jnp.int32, (1, LANES), 1)
    head_mask = [(lane < GLA_DK).astype(F32), (lane >= GLA_DK).astype(F32)]
    tril = (lax.broadcasted_iota(jnp.int32, (L, L), 1)
            <= lax.broadcasted_iota(jnp.int32, (L, L), 0)).astype(F32)
    eye = (lax.broadcasted_iota(jnp.int32, (LANES, LANES), 0)
           == lax.broadcasted_iota(jnp.int32, (LANES, LANES), 1))

    pre = jnp.dot(ad_ref[...].astype(BF16), aup_ref[...], preferred_element_type=F32) + ab_ref[...]
    la = _log_sigmoid(pre) * (1.0 / GLA_TAU)
    for blk in range(bb * n_chunks):
        rows = slice(blk * L, (blk + 1) * L)
        bc_scr[rows, :] = jnp.dot(tril, la[rows], preferred_element_type=F32,
                                  precision=lax.Precision.HIGHEST)

    stack = lambda x: jnp.concatenate([x * head_mask[0], x * head_mask[1]], axis=0).astype(BF16)

    def compute(items):
        units = []
        for j, c in items:
            start = (j * n_chunks + c) * L
            rows = pl.ds(start if isinstance(start, int) else pl.multiple_of(start, L), L)
            for p in range(GLA_HEADS // 2):
                kcols = slice(p * LANES, (p + 1) * LANES)
                units.append(dict(
                    j=j, p=p, rows=rows, qp=q_ref[rows, kcols] * (GLA_DK ** -0.5), kp=k_ref[rows, kcols],
                    bc=bc_scr[rows, kcols], s_old=s_ref[j, p],
                    v_pair=v_ref[rows, 2 * p * GLA_DV:2 * (p + 1) * GLA_DV].astype(BF16)))
        for d in units:
            qp, kp, bc, v_pair = d["qp"], d["kp"], d["bc"], d["v_pair"]
            d["inter"] = jnp.dot(stack(qp * jnp.exp(bc)), d["s_old"].astype(BF16), preferred_element_type=F32)
            d["att"] = []
            for i in range(L // sub):
                lo, hi = i * sub, (i + 1) * sub
                c_i = bc[lo - 1:lo] if i > 0 else jnp.zeros((1, LANES), F32)
                qe = qp[lo:hi] * jnp.exp(bc[lo:hi] - c_i)
                ke = kp[:hi] * jnp.exp(c_i - bc[:hi])
                d["att"].append(_dot_nt(stack(qe), ke.astype(BF16)))
            bl = bc[L - 1:L]
            v_stack = jnp.concatenate([v_pair[:, :GLA_DV], v_pair[:, GLA_DV:]], axis=0)
            d["s_upd"] = _dot_tn(stack(kp * jnp.exp(bl - bc)), v_stack)
            d["dec_col"] = jnp.sum(jnp.where(eye, jnp.exp(bl), 0.0), axis=1, keepdims=True)
        for d in units:
            d["intra"] = []
            for i, att in enumerate(d["att"]):
                lo, hi = i * sub, (i + 1) * sub
                t_idx = lo + lax.broadcasted_iota(jnp.int32, (2 * sub, hi), 0) % sub
                s_idx = lax.broadcasted_iota(jnp.int32, (2 * sub, hi), 1)
                att = jnp.where(s_idx <= t_idx, att, 0.0)
                d["intra"].append(jnp.dot(att.astype(BF16), d["v_pair"][:hi], preferred_element_type=F32))
        stores = []
        for d in units:
            for e in range(2):
                h = 2 * d["p"] + e
                vcols = slice(h * GLA_DV, (h + 1) * GLA_DV)
                o = d["inter"][e * L:(e + 1) * L] + jnp.concatenate(
                    [blk[e * sub:(e + 1) * sub, e * GLA_DV:(e + 1) * GLA_DV] for blk in d["intra"]], axis=0)
                stores.append((y_ref, (d["rows"], vcols), o))
            stores.append((s_ref, (d["j"], d["p"]), d["dec_col"] * d["s_old"] + d["s_upd"]))
        return stores

    def commit(stores):
        for ref, idx, val in stores:
            ref[idx] = val

    def chunk_body(j, c, carry):
        commit(compute([(j, c)]))
        return carry

    if n_chunks == 1:
        commit(compute([(j, 0) for j in range(bb)]))
    else:
        for j in range(bb):
            lax.fori_loop(0, n_chunks, functools.partial(chunk_body, j), 0)

    for h in range(GLA_HEADS):
        vcols = slice(h * GLA_DV, (h + 1) * GLA_DV)
        o = y_ref[:, vcols]
        on = o * lax.rsqrt(jnp.mean(o * o, -1, keepdims=True) + EPS) * lng_ref[:, vcols]
        gate = g_ref[:, vcols]
        y_ref[:, vcols] = on * (gate * jax.nn.sigmoid(gate))


def gla_mixer(z, s0, a_up, a_b, ln_g, *, batch, seq_len, tb, bb, time_major):
    m_rows = batch * seq_len
    L = math.gcd(seq_len, CHUNK)
    n_tb = seq_len // tb
    pairs = GLA_HEADS // 2
    assert bb == 1 or (n_tb == 1 and not time_major)
    rows = bb * tb
    if time_major:
        zblk = lambda width, off: pl.BlockSpec((tb, width), lambda b, t: (t, (b * Z1_COLS + off) // width))
    else:
        zblk = lambda width, off: pl.BlockSpec((rows, width), lambda b, t: (b * n_tb + t, off // width))
    s_spec = pl.BlockSpec((bb, pairs, 2 * GLA_DK, GLA_DV), lambda b, t: (b, 0, 0, 0))
    const = lambda b, t: (0, 0)
    a_up_p = jnp.pad(a_up, ((0, LANES - GLA_GATE_RANK), (0, 0))).astype(BF16)
    y, s_new = pl.pallas_call(
        functools.partial(_gla_kernel, chunk=L, n_chunks=tb // L, bb=bb),
        grid=(batch // bb, n_tb),
        in_specs=[zblk(GLA_QK, Z1_Q), zblk(GLA_QK, Z1_K), zblk(HALF, Z1_V), zblk(HALF, Z1_G),
                  zblk(LANES, Z1_AD),
                  pl.BlockSpec((LANES, GLA_QK), const), pl.BlockSpec((1, GLA_QK), const),
                  pl.BlockSpec((1, HALF), const), s_spec],
        out_specs=[pl.BlockSpec((rows, HALF), lambda b, t: (b * n_tb + t, 0)), s_spec],
        out_shape=[jax.ShapeDtypeStruct((m_rows, HALF), F32),
                   jax.ShapeDtypeStruct((batch, pairs, 2 * GLA_DK, GLA_DV), F32)],
        scratch_shapes=[pltpu.VMEM((rows, GLA_QK), F32)],
        compiler_params=_cparams("parallel", "arbitrary"),
        name="gla",
    )(z, z, z, z, z, a_up_p, a_b.reshape(1, GLA_QK), ln_g.reshape(1, HALF),
      s0.reshape(batch, pairs, 2 * GLA_DK, GLA_DV))
    return y, s_new.reshape(batch, GLA_HEADS, GLA_DK, GLA_DV)


S5_KT = LANES // S5_GROUP
S5_SCAN_SPLIT = 2


def _s5_param_kernel(lre_ref, lim_ref, step_ref, bre_ref, bim_ref, bbre_ref, bbim_ref, pre_ref, pim_ref):
    lre, lim = lre_ref[...], lim_ref[...]
    step = jnp.exp(step_ref[...])
    mag = jnp.exp(lre * step)
    bar_re = mag * jnp.cos(lim * step)
    bar_im = mag * jnp.sin(lim * step)
    inv = 1.0 / (lre * lre + lim * lim)
    cre = ((bar_re - 1.0) * lre + bar_im * lim) * inv
    cim = (bar_im * lre - (bar_re - 1.0) * lim) * inv
    for g in range(S5_GROUPS):
        cr, ci = cre[g:g + 1], cim[g:g + 1]
        bbre_ref[g] = cr * bre_ref[g] - ci * bim_ref[g]
        bbim_ref[g] = cr * bim_ref[g] + ci * bre_ref[g]
    pre_ref[0] = bar_re
    pim_ref[0] = bar_im


def _s5_kernel(u_ref, wbre_ref, wbim_ref, lre_ref, lim_ref, x0re_ref, x0im_ref, wcre_ref, wcim_ref,
               d_ref, wglu_ref, y_ref, xre_ref, xim_ref, sre_scr, sim_scr, *, tt, bb, row_major):
    @pl.when(pl.program_id(1) == 0)
    def _():
        xre_ref[...] = x0re_ref[...]
        xim_ref[...] = x0im_ref[...]

    rows_all = tt * bb
    if row_major:
        u = jnp.concatenate([u_ref[:, t, :] for t in range(tt)], axis=0)
    else:
        u = u_ref[...].reshape(rows_all, S5_CH)
    ub = u.astype(BF16)
    nblk = S5_CH // LANES
    wide = S5_STATE // nblk
    for kt in range(nblk):
        cols = slice(kt * wide, (kt + 1) * wide)
        ukt = ub[:, kt * LANES:(kt + 1) * LANES]
        sre_scr[:, cols] = jnp.dot(ukt, wbre_ref[kt], preferred_element_type=F32)
        sim_scr[:, cols] = jnp.dot(ukt, wbim_ref[kt], preferred_element_type=F32)

    for part in range(S5_SCAN_SPLIT):
        width = S5_STATE // S5_SCAN_SPLIT
        cols = slice(part * width, (part + 1) * width)
        lr, li = lre_ref[:, cols], lim_ref[:, cols]

        def step(t, carry, cols=cols, lr=lr, li=li):
            cr, ci = carry
            rows = pl.ds(pl.multiple_of(t * bb, bb), bb)
            nr = sre_scr[rows, cols] + (lr * cr - li * ci)
            ni = sim_scr[rows, cols] + (lr * ci + li * cr)
            sre_scr[rows, cols] = nr
            sim_scr[rows, cols] = ni
            return nr, ni

        cr, ci = lax.fori_loop(0, tt, step, (xre_ref[:, cols], xim_ref[:, cols]))
        xre_ref[:, cols] = cr
        xim_ref[:, cols] = ci

    parts = []
    for nt in range(nblk):
        cols = slice(nt * wide, (nt + 1) * wide)
        parts.append(jnp.dot(sre_scr[:, cols].astype(BF16), wcre_ref[nt], preferred_element_type=F32)
                     - jnp.dot(sim_scr[:, cols].astype(BF16), wcim_ref[nt], preferred_element_type=F32))
    y = jnp.concatenate(parts, axis=1) + d_ref[...] * u
    ys = y * (0.5 * (1.0 + jnp.tanh(math.sqrt(2.0 / math.pi) * (y + 0.044715 * (y * y * y)))))
    out = ys * jax.nn.sigmoid(jnp.dot(ys.astype(BF16), wglu_ref[...], preferred_element_type=F32))
    if row_major:
        for t in range(tt):
            y_ref[:, t, :] = out[t * bb:(t + 1) * bb]
    else:
        y_ref[...] = out.reshape(tt, bb, S5_CH)


def s5_mixer(z, x0_re, x0_im, lam_re, lam_im, log_step, b_re, b_im, c_re, c_im, d_skip,
             w_glu, *, batch, seq_len, tt, bb, time_major):
    u_col_block = Z1_U // S5_CH
    if time_major:
        z = z.reshape(seq_len, batch, Z1_COLS)
        u_spec = pl.BlockSpec((tt, bb, S5_CH), lambda b, t: (t, b, u_col_block))
        y_spec = pl.BlockSpec((tt, bb, S5_CH), lambda b, t: (t, b, 0))
        y_shape = (seq_len, batch, S5_CH)
    else:
        assert tt == seq_len
        z = z.reshape(batch, seq_len, Z1_COLS)
        u_spec = pl.BlockSpec((bb, tt, S5_CH), lambda b, t: (b, 0, u_col_block))
        y_spec = pl.BlockSpec((bb, tt, S5_CH), lambda b, t: (b, 0, 0))
        y_shape = (batch, seq_len, S5_CH)
    gpc = (S5_GROUPS, S5_GROUP, S5_P)
    bb_re, bb_im, p_re, p_im = pl.pallas_call(
        _s5_param_kernel,
        out_shape=[jax.ShapeDtypeStruct(gpc, F32), jax.ShapeDtypeStruct(gpc, F32),
                   jax.ShapeDtypeStruct((1, S5_GROUPS, S5_P), F32),
                   jax.ShapeDtypeStruct((1, S5_GROUPS, S5_P), F32)],
        name="s5_params",
    )(lam_re, lam_im, log_step.reshape(S5_GROUPS, 1), jnp.swapaxes(b_re, 1, 2), jnp.swapaxes(b_im, 1, 2))

    nblk = S5_CH // LANES
    eye = jnp.eye(S5_KT, dtype=F32)

    def in_blocks(w):
        w = w.reshape(nblk, S5_KT, S5_GROUP, 1, S5_P) * eye[None, :, None, :, None]
        return w.reshape(nblk, LANES, S5_KT * S5_P).astype(BF16)

    def out_blocks(c):
        w = jnp.swapaxes(c, 1, 2).reshape(nblk, S5_KT, S5_P, 1, S5_GROUP) * eye[None, :, None, :, None]
        return w.reshape(nblk, S5_KT * S5_P, LANES).astype(BF16)

    lam_rows = lambda p: jnp.broadcast_to(p.reshape(1, S5_STATE), (bb, S5_STATE))
    const2 = lambda b, t: (0, 0)
    const3 = lambda b, t: (0, 0, 0)
    st_spec = pl.BlockSpec((bb, S5_STATE), lambda b, t: (b, 0))
    wb_spec = pl.BlockSpec((nblk, LANES, S5_KT * S5_P), const3)
    wc_spec = pl.BlockSpec((nblk, S5_KT * S5_P, LANES), const3)
    lam_spec = pl.BlockSpec((bb, S5_STATE), const2)
    y, x_re, x_im = pl.pallas_call(
        functools.partial(_s5_kernel, tt=tt, bb=bb, row_major=not time_major),
        grid=(batch // bb, seq_len // tt),
        in_specs=[u_spec,
                  wb_spec, wb_spec, lam_spec, lam_spec, st_spec, st_spec, wc_spec, wc_spec,
                  pl.BlockSpec((1, S5_CH), const2), pl.BlockSpec((S5_CH, S5_CH), const2)],
        out_specs=[y_spec, st_spec, st_spec],
        out_shape=[jax.ShapeDtypeStruct(y_shape, F32),
                   jax.ShapeDtypeStruct((batch, S5_STATE), F32),
                   jax.ShapeDtypeStruct((batch, S5_STATE), F32)],
        scratch_shapes=[pltpu.VMEM((tt * bb, S5_STATE), F32), pltpu.VMEM((tt * bb, S5_STATE), F32)],
        compiler_params=_cparams("parallel", "arbitrary"),
        name="s5",
    )(z, in_blocks(bb_re), in_blocks(bb_im), lam_rows(p_re), lam_rows(p_im),
      x0_re.reshape(batch, S5_STATE), x0_im.reshape(batch, S5_STATE),
      out_blocks(c_re), out_blocks(c_im), d_skip.reshape(1, S5_CH), w_glu.astype(BF16))
    shape = (batch, S5_GROUPS, S5_P)
    if not time_major:
        y = y.reshape(batch * seq_len, S5_CH)
    return y, x_re.reshape(shape), x_im.reshape(shape)


def _pack_even_w_in(w_in):
    d = w_in.shape[0]
    m_cols = 4 * HALF + 2 * MLSTM_HEADS
    parts = [w_in[:, m_cols:],
             w_in[:, 4 * HALF:m_cols], jnp.zeros((d, Z0_Q - Z0_GATE - 2 * MLSTM_HEADS), w_in.dtype),
             w_in[:, :4 * HALF]]
    return jnp.concatenate(parts, axis=1).astype(BF16)


def _pack_odd_w_in(w_in):
    gla_main = 2 * GLA_QK + 2 * HALF
    gla_cols = gla_main + GLA_GATE_RANK
    parts = [w_in[:, :gla_main], w_in[:, gla_cols:], w_in[:, gla_main:gla_cols]]
    w = jnp.concatenate(parts, axis=1)
    return jnp.pad(w, ((0, 0), (0, Z1_COLS - w.shape[1]))).astype(BF16)


def _trunk(x, st, prm, *, batch, seq_len):
    (st_C, st_n, st_m, st_S, st_shift, st_gla, st_re, st_im, st_conv) = st
    (norm_mix, norm_ffn, norm_final,
     e_w_in, e_b_if, e_mu, e_w0, e_w_up, e_a0, e_a_up, e_g_up, e_k_k, e_k_a, e_r_k, e_ln_m_g, e_ln_r_g,
     e_ln_r_b, e_w_out,
     o_w_in, o_a_up, o_a_b, o_ln_g, o_lam_re, o_lam_im, o_log_step, o_b_re, o_b_im, o_c_re, o_c_im, o_d,
     o_w_glu, o_w_out,
     f_w_up, f_conv_w, f_conv_b, f_w_down) = prm
    assert len(e_w_in) == 1 and len(o_w_in) == 1 and len(f_w_up) == 2, "wired for depth 2"
    long_seq = seq_len >= 512
    assert long_seq or seq_len == SUBLANES
    m_rows = batch * seq_len
    tiles = dict(
        tm_proj=512,
        tm_tok=512 if long_seq else 256,
        tm_ffn=512 if long_seq else 128,
        tb=512 if long_seq else seq_len,
        bb_chunked=1 if long_seq else 4,
        tb_rwkv=128 if long_seq else seq_len,
        tt_s5=128 if long_seq else seq_len,
    )
    x = x.reshape(m_rows, D_MODEL)
    ffn = functools.partial(conv_ffn, batch=batch, seq_len=seq_len, tm=tiles["tm_ffn"])

    z0 = norm_matmul(x, norm_mix[0], _pack_even_w_in(e_w_in[0]), tm=tiles["tm_proj"], tn=512)
    y_m, c_new, n_new, m_new = mlstm_mixer(z0, e_b_if[0], st_C[0], st_n[0], st_m[0], e_ln_m_g[0],
                                           batch=batch, seq_len=seq_len, tb=tiles["tb"], bb=tiles["bb_chunked"])
    y_r, s_new = rwkv_mixer(z0, st_S[0], st_shift[0], e_mu[0], e_w0[0], e_w_up[0], e_a0[0], e_a_up[0],
                            e_g_up[0], e_k_k[0], e_k_a[0], e_r_k[0], e_ln_r_g[0], e_ln_r_b[0],
                            batch=batch, seq_len=seq_len, tm=tiles["tm_tok"], bb=SUBLANES,
                            tb=tiles["tb_rwkv"])
    shift_new = z0.reshape(batch, seq_len, Z0_COLS)[:, -1, Z0_RWKV:Z0_RWKV + RWKV_COLS]
    w_out = e_w_out[0].astype(BF16)
    x, conv_new0 = ffn(x, y_m, y_r, w_out[:HALF], w_out[HALF:],
                       norm_ffn[0], f_w_up[0].astype(BF16), f_conv_w[0], f_conv_b[0],
                       f_w_down[0].astype(BF16), st_conv[0], norm_final, final_norm=False)

    z1 = norm_matmul(x, norm_mix[1], _pack_odd_w_in(o_w_in[0]), tm=tiles["tm_proj"], tn=512,
                     time_major_seq_len=seq_len if long_seq else None)
    y_g, gla_new = gla_mixer(z1, st_gla[0], o_a_up[0], o_a_b[0], o_ln_g[0],
                             batch=batch, seq_len=seq_len, tb=tiles["tb"], bb=tiles["bb_chunked"],
                             time_major=long_seq)
    y_s, re_new, im_new = s5_mixer(z1, st_re[0], st_im[0], o_lam_re[0], o_lam_im[0],
                                   o_log_step[0], o_b_re[0], o_b_im[0], o_c_re[0], o_c_im[0], o_d[0],
                                   o_w_glu[0], batch=batch, seq_len=seq_len, tt=tiles["tt_s5"],
                                   bb=SUBLANES, time_major=long_seq)
    w_out = o_w_out[0].astype(BF16)
    x, conv_new1 = ffn(x, y_g, y_s, w_out[:HALF], w_out[HALF:],
                       norm_ffn[1], f_w_up[1].astype(BF16), f_conv_w[1], f_conv_b[1],
                       f_w_down[1].astype(BF16), st_conv[1], norm_final, final_norm=True)
    new_state = (c_new[None], n_new[None], m_new[None], s_new[None], shift_new[None],
                 gla_new[None], re_new[None], im_new[None], jnp.stack([conv_new0, conv_new1]))
    return x.reshape(batch, seq_len, D_MODEL), new_state


def kernel(x_prompt, x_sample, state_mlstm_C, state_mlstm_n, state_mlstm_m, state_rwkv_S, state_rwkv_shift,
           state_gla_S, state_s5_re, state_s5_im, state_ffn_conv,
           norm_mix, norm_ffn, norm_final,
           e_w_in, e_b_if, e_mu, e_w0, e_w_up, e_a0, e_a_up, e_g_up, e_k_k, e_k_a, e_r_k,
           e_ln_m_g, e_ln_r_g, e_ln_r_b, e_w_out,
           o_w_in, o_a_up, o_a_b, o_ln_g, o_lam_re, o_lam_im, o_log_step, o_b_re, o_b_im, o_c_re, o_c_im,
           o_d, o_w_glu, o_w_out,
           f_w_up, f_conv_w, f_conv_b, f_w_down):
    params = (norm_mix, norm_ffn, norm_final,
              e_w_in, e_b_if, e_mu, e_w0, e_w_up, e_a0, e_a_up, e_g_up, e_k_k, e_k_a, e_r_k,
              e_ln_m_g, e_ln_r_g, e_ln_r_b, e_w_out,
              o_w_in, o_a_up, o_a_b, o_ln_g, o_lam_re, o_lam_im, o_log_step, o_b_re, o_b_im, o_c_re, o_c_im,
              o_d, o_w_glu, o_w_out,
              f_w_up, f_conv_w, f_conv_b, f_w_down)
    bp, tp, _ = x_prompt.shape
    bs, ts, _ = x_sample.shape
    n_even, n_odd, depth = state_mlstm_C.shape[0], state_gla_S.shape[0], state_ffn_conv.shape[0]
    init = (jnp.zeros((n_even, bp, MLSTM_HEADS, MLSTM_HD, MLSTM_HD), F32),
            jnp.zeros((n_even, bp, MLSTM_HEADS, MLSTM_HD), F32),
            jnp.full((n_even, bp, MLSTM_HEADS), M_INIT, F32),
            jnp.zeros((n_even, bp, RWKV_HEADS, RWKV_HD, RWKV_HD), F32),
            jnp.zeros((n_even, bp, RWKV_COLS), F32),
            jnp.zeros((n_odd, bp, GLA_HEADS, GLA_DK, GLA_DV), F32),
            jnp.zeros((n_odd, bp, S5_GROUPS, S5_P), F32),
            jnp.zeros((n_odd, bp, S5_GROUPS, S5_P), F32),
            jnp.zeros((depth, bp, CONV_W - 1, 2 * D_FF), F32))
    past = (state_mlstm_C, state_mlstm_n, state_mlstm_m, state_rwkv_S, state_rwkv_shift,
            state_gla_S, state_s5_re, state_s5_im, state_ffn_conv)
    y_prompt, p_state = _trunk(x_prompt, init, params, batch=bp, seq_len=tp)
    y_sample, s_state = _trunk(x_sample, past, params, batch=bs, seq_len=ts)
    return (y_prompt, y_sample) + tuple(p_state) + tuple(s_state)
```
